```python
import math
import jax, jax.numpy as jnp
from jax import lax
import numpy as np

D_MODEL = 1024
BATCH = 8
SEQ = 4096
DEPTH = 4

N_MIXERS = 3
N_A = (DEPTH + 2) // 3
N_B = (DEPTH + 1) // 3
N_C = DEPTH // 3

DN_ALPHA = (2.0 * DEPTH) ** 0.25
DN_BETA = (8.0 * DEPTH) ** -0.25
LN_EPS = 1e-5

A_HEADS = 16
A_HEAD_DIM = D_MODEL // A_HEADS
A_BRANCHES = ((128, 1), (512, 4), (2048, 16))
A_BLOCK = 128
A_ROT_DIM = A_HEAD_DIM // 4
ROPE_THETA = 500000.0

B_HEADS = 8
B_QK_DIM = D_MODEL // (2 * B_HEADS)
B_V_DIM = D_MODEL // B_HEADS
B_QK_W = 2 * B_HEADS * B_QK_DIM
B_CONV = 4
B_CHUNK = 64
B_IN = B_QK_W + 2 * D_MODEL + 2 * B_HEADS

C_HEADS = 4
C_QK_DIM = D_MODEL // C_HEADS
C_V_DIM = 2 * D_MODEL // C_HEADS
C_CHUNK = 64
C_THETA = 10000.0
C_IN = 2 * D_MODEL + 4 * D_MODEL

MOE_GROUPS = 8
MOE_PER_GROUP = 8
MOE_EXPERTS = MOE_GROUPS * MOE_PER_GROUP
MOE_TOPK = 2
MOE_HIDDEN = D_MODEL // 4
MOE_BLOCK = 128

kernel_name = "hybrid_dilated_mlstm_retention_hmoe"


def layer_norm(x, g, b):
    xf = x.astype(jnp.float32)
    mu = jnp.mean(xf, -1, keepdims=True)
    var = jnp.mean(jnp.square(xf - mu), -1, keepdims=True)
    return ((xf - mu) * lax.rsqrt(var + LN_EPS) * g + b).astype(x.dtype)


def head_norm(h, g):
    mu = jnp.mean(h, -1, keepdims=True)
    var = jnp.mean(jnp.square(h - mu), -1, keepdims=True)
    return (h - mu) * lax.rsqrt(var + LN_EPS) * g


def rope_tables(positions, dim, theta):
    inv = theta ** (-jnp.arange(0, dim, 2, dtype=jnp.float32) / dim)
    ang = positions.astype(jnp.float32)[:, None] * inv[None, :]
    return jnp.cos(ang), jnp.sin(ang)


def apply_rope(t, cos, sin):
    half = t.shape[-1] // 2
    c, s = cos[:, None, :], sin[:, None, :]
    t1, t2 = t[..., :half], t[..., half:]
    return jnp.concatenate([t1 * c - t2 * s, t2 * c + t1 * s], -1)


def dilated_branch(q, k, v, dilation, n_back):
    b, s, h, dh = q.shape
    L = s // dilation
    nb = -(-L // A_BLOCK)
    Lp = nb * A_BLOCK

    def split(t):
        t = t.reshape(b, L, dilation, h, dh).transpose(0, 2, 3, 1, 4)
        t = jnp.pad(t, ((0, 0), (0, 0), (0, 0), (0, Lp - L), (0, 0)))
        return t.reshape(b, dilation, h, nb, A_BLOCK, dh)

    def with_prev(tb):
        prev = jnp.concatenate([jnp.zeros_like(tb[:, :, :, :1]), tb[:, :, :, :-1]], axis=3)
        return jnp.concatenate([prev, tb], axis=4)

    qb = split(q)
    kb = with_prev(split(k))
    vb = with_prev(split(v))
    scores = jnp.einsum('brhnqd,brhnkd->brhnqk', qb, kb)
    qi = jnp.arange(A_BLOCK)[:, None]
    ki = jnp.arange(2 * A_BLOCK)[None, :]
    rel = qi + A_BLOCK - ki
    kpos = jnp.arange(nb)[:, None, None] * A_BLOCK - A_BLOCK + ki[None]
    mask = (rel >= 0) & (rel <= n_back) & (kpos >= 0)
    scores = jnp.where(mask, scores, -jnp.inf)
    m = jnp.max(scores, -1, keepdims=True)
    p = jnp.exp(scores - m)
    den = jnp.sum(p, -1, keepdims=True)
    o = jnp.einsum('brhnqk,brhnkd->brhnqd', p, vb) / den
    lse = (m + jnp.log(den))[..., 0]
    o = o.reshape(b, dilation, h, Lp, dh)[:, :, :, :L].transpose(0, 3, 1, 2, 4).reshape(b, s, h, dh)
    lse = lse.reshape(b, dilation, h, Lp)[..., :L].transpose(0, 3, 1, 2).reshape(b, s, h)
    return o, lse


def mixer_dilated(x, w_in, w_out, cos, sin):
    b, s, _ = x.shape
    qkv = (x @ w_in).astype(jnp.float32).reshape(b, s, 3, A_HEADS, A_HEAD_DIM)
    q, k, v = qkv[:, :, 0], qkv[:, :, 1], qkv[:, :, 2]
    q = jnp.concatenate([apply_rope(q[..., :A_ROT_DIM], cos, sin), q[..., A_ROT_DIM:]], -1)
    k = jnp.concatenate([apply_rope(k[..., :A_ROT_DIM], cos, sin), k[..., A_ROT_DIM:]], -1)
    q = q * (A_HEAD_DIM ** -0.5)
    outs, lses = [], []
    for window, dilation in A_BRANCHES:
        o_g, lse_g = dilated_branch(q, k, v, dilation, window // dilation)
        outs.append(o_g)
        lses.append(lse_g)
    wts = jax.nn.softmax(jnp.stack(lses, 0), axis=0)
    o = jnp.einsum('gbsh,gbshd->bshd', wts, jnp.stack(outs, 0))
    return o.reshape(b, s, D_MODEL).astype(x.dtype) @ w_out


def causal_depthwise_conv(u, w, bias):
    kw, c = w.shape
    y = lax.conv_general_dilated(u, w[:, None, :].astype(u.dtype), window_strides=(1,),
                                 padding=[(kw - 1, 0)], dimension_numbers=('NWC', 'WIO', 'NWC'),
                                 feature_group_count=c)
    return y + bias.astype(u.dtype)


def mlstm_chunkwise(q, k, v, i_pre, logf):
    b, h, s, dk = q.shape
    dv = v.shape[-1]
    L = B_CHUNK
    nc = s // L
    def chunks(t):
        return jnp.moveaxis(t.reshape(b, h, nc, L, *t.shape[3:]), 2, 0)
    causal = jnp.tril(jnp.ones((L, L), dtype=bool))

    def step(carry, inp):
        C, n, m = carry
        qj, kj, vj, ij, fj = inp
        a = jnp.cumsum(fj, axis=-1)
        dmat = jnp.where(causal, a[..., :, None] - a[..., None, :] + ij[..., None, :], -jnp.inf)
        inter = a + m[..., None]
        m_t = jnp.maximum(inter, jnp.max(dmat, -1))
        sc = jnp.einsum('bhtd,bhsd->bhts', qj, kj) * jnp.exp(dmat - m_t[..., None])
        g_inter = jnp.exp(inter - m_t)
        num = jnp.einsum('bhts,bhsv->bhtv', sc, vj) + g_inter[..., None] * jnp.einsum('bhtd,bhdv->bhtv', qj, C)
        den = jnp.sum(sc, -1) + g_inter * jnp.einsum('bhtd,bhd->bht', qj, n)
        h_out = num / jnp.maximum(jnp.abs(den), jnp.exp(-m_t))[..., None]
        a_end = a[..., -1]
        w_s = a_end[..., None] - a + ij
        m_new = jnp.maximum(a_end + m, jnp.max(w_s, -1))
        decay = jnp.exp(a_end + m - m_new)
        ws = jnp.exp(w_s - m_new[..., None])
        C_new = decay[..., None, None] * C + jnp.einsum('bhs,bhsd,bhsv->bhdv', ws, kj, vj)
        n_new = decay[..., None] * n + jnp.einsum('bhs,bhsd->bhd', ws, kj)
        return (C_new, n_new, m_new), h_out

    init = (jnp.zeros((b, h, dk, dv), jnp.float32), jnp.zeros((b, h, dk), jnp.float32),
            jnp.zeros((b, h), jnp.float32))
    _, hs = lax.scan(step, init, (chunks(q), chunks(k), chunks(v), chunks(i_pre), chunks(logf)))
    return hs.transpose(1, 0, 3, 2, 4).reshape(b, s, h, dv)


def mixer_mlstm(x, w_in, gate_bias, conv_w, conv_b, norm_g, w_out):
    b, s, _ = x.shape
    proj = x @ w_in
    qk = proj[..., :B_QK_W]
    v = proj[..., B_QK_W:B_QK_W + D_MODEL]
    o = proj[..., B_QK_W + D_MODEL:B_QK_W + 2 * D_MODEL]
    gates = proj[..., B_QK_W + 2 * D_MODEL:].astype(jnp.float32) + gate_bias
    qk = jax.nn.silu(causal_depthwise_conv(qk, conv_w, conv_b)).astype(jnp.float32)
    half = B_QK_W // 2
    q = qk[..., :half].reshape(b, s, B_HEADS, B_QK_DIM).transpose(0, 2, 1, 3)
    k = qk[..., half:].reshape(b, s, B_HEADS, B_QK_DIM).transpose(0, 2, 1, 3) * (B_QK_DIM ** -0.5)
    v = v.astype(jnp.float32).reshape(b, s, B_HEADS, B_V_DIM).transpose(0, 2, 1, 3)
    i_pre = gates[..., :B_HEADS].transpose(0, 2, 1)
    logf = jax.nn.log_sigmoid(gates[..., B_HEADS:]).transpose(0, 2, 1)
    hh = mlstm_chunkwise(q, k, v, i_pre, logf)
    hh = head_norm(hh, norm_g.reshape(B_HEADS, B_V_DIM)).reshape(b, s, D_MODEL)
    out = hh * jax.nn.sigmoid(o.astype(jnp.float32))
    return out.astype(x.dtype) @ w_out


def retention_chunkwise(q, k, v):
    b, h, s, dk = q.shape
    dv = v.shape[-1]
    L = C_CHUNK
    nc = s // L
    log_gamma = jnp.log(1.0 - 2.0 ** (-5.0 - jnp.arange(h, dtype=jnp.float32)))
    idx = jnp.arange(L, dtype=jnp.float32)
    rel = idx[:, None] - idx[None, :]
    dmask = jnp.where(rel >= 0, jnp.exp(jnp.maximum(rel, 0.0)[None] * log_gamma[:, None, None]), 0.0)
    xi = jnp.exp((idx + 1.0)[None] * log_gamma[:, None])
    zeta = jnp.exp((L - 1.0 - idx)[None] * log_gamma[:, None])
    chunk_decay = jnp.exp(L * log_gamma)
    def chunks(t):
        return jnp.moveaxis(t.reshape(b, h, nc, L, t.shape[-1]), 2, 0)

    def step(R, inp):
        qj, kj, vj = inp
        sc = jnp.einsum('bhtd,bhsd->bhts', qj, kj) * dmask
        out = jnp.einsum('bhts,bhsv->bhtv', sc, vj) + xi[..., None] * jnp.einsum('bhtd,bhdv->bhtv', qj, R)
        R = chunk_decay[:, None, None] * R + jnp.einsum('bhsd,bhsv->bhdv', kj * zeta[..., None], vj)
        return R, out

    _, outs = lax.scan(step, jnp.zeros((b, h, dk, dv), jnp.float32), (chunks(q), chunks(k), chunks(v)))
    return outs.transpose(1, 0, 3, 2, 4).reshape(b, s, h, dv)


def mixer_retention(x, w_in, norm_g, w_out, cos, sin):
    b, s, _ = x.shape
    proj = x @ w_in
    q = proj[..., :D_MODEL].astype(jnp.float32).reshape(b, s, C_HEADS, C_QK_DIM)
    k = proj[..., D_MODEL:2 * D_MODEL].astype(jnp.float32).reshape(b, s, C_HEADS, C_QK_DIM)
    v = proj[..., 2 * D_MODEL:4 * D_MODEL].astype(jnp.float32).reshape(b, s, C_HEADS, C_V_DIM)
    g = proj[..., 4 * D_MODEL:].astype(jnp.float32)
    q = apply_rope(q, cos, sin).transpose(0, 2, 1, 3)
    k = (apply_rope(k, cos, sin) * (C_QK_DIM ** -0.5)).transpose(0, 2, 1, 3)
    v = v.transpose(0, 2, 1, 3)
    o = retention_chunkwise(q, k, v)
    o = head_norm(o, norm_g.reshape(C_HEADS, C_V_DIM)).reshape(b, s, 2 * D_MODEL)
    o = o * jax.nn.silu(g)
    return o.astype(x.dtype) @ w_out


def moe_ffn(x, wg_r, bg_r, we_r, be_r, w_gate, w_up, w_down):
    b, s, d = x.shape
    xt = x.reshape(-1, d)
    n = xt.shape[0]
    g_logits = (xt @ wg_r).astype(jnp.float32) + bg_r
    grp = jnp.argmax(g_logits, -1)
    p_grp = jnp.take_along_axis(jax.nn.softmax(g_logits, -1), grp[:, None], -1)[:, 0]
    e_logits = ((xt @ we_r).astype(jnp.float32) + be_r).reshape(n, MOE_GROUPS, MOE_PER_GROUP)
    e_in = jnp.take_along_axis(e_logits, grp[:, None, None], axis=1)[:, 0]
    top_v, top_i = lax.top_k(e_in, MOE_TOPK)
    gates = jax.nn.softmax(top_v, -1) * p_grp[:, None]
    eid = (grp[:, None] * MOE_PER_GROUP + top_i).reshape(-1).astype(jnp.int32)
    tok = jnp.repeat(jnp.arange(n, dtype=jnp.int32), MOE_TOPK)
    order = jnp.argsort(eid)
    eid_s, tok_s, gate_s = eid[order], tok[order], gates.reshape(-1)[order]
    counts = jnp.bincount(eid, length=MOE_EXPERTS)
    starts = jnp.cumsum(counts) - counts
    padded = (counts + MOE_BLOCK - 1) // MOE_BLOCK * MOE_BLOCK
    pends = jnp.cumsum(padded)
    pstarts = pends - padded
    n_assign = n * MOE_TOPK
    dest = pstarts[eid_s] + (jnp.arange(n_assign) - starts[eid_s])
    P = n_assign + MOE_EXPERTS * MOE_BLOCK
    nb = P // MOE_BLOCK
    buf_tok = jnp.full((P,), n, jnp.int32).at[dest].set(tok_s)
    x_pad = jnp.concatenate([xt, jnp.zeros((1, d), xt.dtype)], 0)
    xb = x_pad[buf_tok].reshape(nb, MOE_BLOCK, d)
    blk_e = jnp.minimum(jnp.searchsorted(pends, jnp.arange(nb) * MOE_BLOCK, side='right'), MOE_EXPERTS - 1)

    def expert_block(args):
        xblk, e = args
        return (jax.nn.silu(xblk @ w_gate[e]) * (xblk @ w_up[e])) @ w_down[e]

    yb = lax.map(expert_block, (xb, blk_e)).reshape(P, d)
    y = yb[dest] * gate_s[:, None].astype(x.dtype)
    out = jnp.zeros((n, d), x.dtype).at[tok_s].add(y)
    return out.reshape(b, s, d)


def setup_inputs(seed: int = 0) -> dict:
    key = jax.random.key(seed)
    ks = jax.random.split(key, 26)
    f32 = jnp.float32
    D = D_MODEL
    def nrm(k, shape, scale):
        return jax.random.normal(k, shape, f32) * scale
    return {
        "x": nrm(ks[0], (BATCH, SEQ, D), 1.0),
        "positions": jnp.arange(SEQ, dtype=jnp.int32),
        "ln1_g": 1.0 + nrm(ks[1], (DEPTH, D), 0.02),
        "ln1_b": nrm(ks[2], (DEPTH, D), 0.02),
        "ln2_g": 1.0 + nrm(ks[3], (DEPTH, D), 0.02),
        "ln2_b": nrm(ks[4], (DEPTH, D), 0.02),
        "a_w_in": nrm(ks[5], (N_A, D, 3 * D), D ** -0.5),
        "a_w_out": nrm(ks[6], (N_A, D, D), DN_BETA * D ** -0.5),
        "b_w_in": nrm(ks[7], (N_B, D, B_IN), D ** -0.5),
        "b_gate_bias": jnp.concatenate([nrm(ks[8], (N_B, B_HEADS), 0.1),
                                        jnp.broadcast_to(jnp.linspace(3.0, 6.0, B_HEADS, dtype=f32), (N_B, B_HEADS))
                                        + nrm(ks[9], (N_B, B_HEADS), 0.1)], -1),
        "b_conv_w": nrm(ks[10], (N_B, B_CONV, B_QK_W), B_CONV ** -0.5),
        "b_conv_b": nrm(ks[11], (N_B, B_QK_W), 0.02),
        "b_norm_g": 1.0 + nrm(ks[12], (N_B, D), 0.02),
        "b_w_out": nrm(ks[13], (N_B, D, D), DN_BETA * D ** -0.5),
        "c_w_in": nrm(ks[14], (N_C, D, C_IN), D ** -0.5),
        "c_norm_g": 1.0 + nrm(ks[15], (N_C, 2 * D), 0.02),
        "c_w_out": nrm(ks[16], (N_C, 2 * D, D), DN_BETA * (2 * D) ** -0.5),
        "r_group_w": nrm(ks[17], (DEPTH, D, MOE_GROUPS), D ** -0.5),
        "r_group_b": nrm(ks[18], (DEPTH, MOE_GROUPS), 0.01),
        "r_expert_w": nrm(ks[19], (DEPTH, D, MOE_EXPERTS), D ** -0.5),
        "r_expert_b": nrm(ks[20], (DEPTH, MOE_EXPERTS), 0.01),
        "e_w_gate": nrm(ks[21], (DEPTH, MOE_EXPERTS, D, MOE_HIDDEN), D ** -0.5),
        "e_w_up": nrm(ks[22], (DEPTH, MOE_EXPERTS, D, MOE_HIDDEN), D ** -0.5),
        "e_w_down": nrm(ks[23], (DEPTH, MOE_EXPERTS, MOE_HIDDEN, D), DN_BETA * MOE_HIDDEN ** -0.5),
    }


def reference(x, positions, ln1_g, ln1_b, ln2_g, ln2_b, a_w_in, a_w_out, b_w_in, b_gate_bias,
              b_conv_w, b_conv_b, b_norm_g, b_w_out, c_w_in, c_norm_g, c_w_out, r_group_w,
              r_group_b, r_expert_w, r_expert_b, e_w_gate, e_w_up, e_w_down):
    cos_a, sin_a = rope_tables(positions, A_ROT_DIM, ROPE_THETA)
    cos_c, sin_c = rope_tables(positions, C_QK_DIM, C_THETA)
    h = x
    for i in range(DEPTH):
        kind, j = i % N_MIXERS, i // N_MIXERS
        if kind == 0:
            y = mixer_dilated(h, a_w_in[j], a_w_out[j], cos_a, sin_a)
        elif kind == 1:
            y = mixer_mlstm(h, b_w_in[j], b_gate_bias[j], b_conv_w[j], b_conv_b[j], b_norm_g[j], b_w_out[j])
        else:
            y = mixer_retention(h, c_w_in[j], c_norm_g[j], c_w_out[j], cos_c, sin_c)
        h = layer_norm(DN_ALPHA * h + y, ln1_g[i], ln1_b[i])
        y = moe_ffn(h, r_group_w[i], r_group_b[i], r_expert_w[i], r_expert_b[i],
                    e_w_gate[i], e_w_up[i], e_w_down[i])
        h = layer_norm(DN_ALPHA * h + y, ln2_g[i], ln2_b[i])
    return h
```

```python
import functools

import jax
import jax.numpy as jnp
from jax import lax
from jax.experimental import pallas as pl
from jax.experimental.pallas import tpu as pltpu

F32 = jnp.float32
BF16 = jnp.bfloat16

D_MODEL = 1024
DEPTH = 4
DN_ALPHA = (2.0 * DEPTH) ** 0.25
LN_EPS = 1e-5

A_HEADS = 16
A_HEAD_DIM = 64
A_DILATIONS = (1, 4, 16)
A_BLOCK = 128
A_ROT_DIM = 16
ROPE_THETA = 500000.0

B_HEADS = 8
B_QK_DIM = 64
B_V_DIM = 128
B_CONV = 4
B_CHUNK = 128

C_HEADS = 4
C_QK_DIM = 256
C_V_DIM = 512
C_CHUNK = 256
C_THETA = 10000.0

MOE_GROUPS = 8
MOE_PER_GROUP = 8
MOE_EXPERTS = 64
MOE_TOPK = 2
MOE_HIDDEN = 256
MOE_BLOCK = 256

LANES = 128
NEG = -1e30
VMEM_LIMIT = 48 * 1024 * 1024


def _params(*sem):
    return pltpu.CompilerParams(dimension_semantics=sem, vmem_limit_bytes=VMEM_LIMIT)


def _mm_kernel(x_ref, w_ref, o_ref):
    o_ref[...] = jnp.dot(x_ref[...], w_ref[...], preferred_element_type=F32).astype(o_ref.dtype)


def _matmul(x, w, *, tm, tn, out_dtype=F32):
    n, k = x.shape
    m = w.shape[1]
    return pl.pallas_call(
        _mm_kernel,
        grid=(n // tm, m // tn),
        in_specs=[pl.BlockSpec((tm, k), lambda i, j: (i, 0)),
                  pl.BlockSpec((k, tn), lambda i, j: (0, j))],
        out_specs=pl.BlockSpec((tm, tn), lambda i, j: (i, j)),
        out_shape=jax.ShapeDtypeStruct((n, m), out_dtype),
        compiler_params=_params("parallel", "parallel"),
        name="matmul",
    )(x, w)


def _mm_f32_kernel(x_ref, w_ref, o_ref):
    o_ref[...] = jnp.dot(x_ref[...], w_ref[...], preferred_element_type=F32,
                         precision=lax.Precision.HIGHEST)


def _matmul_f32(x, w, *, tm):
    n, k = x.shape
    m = w.shape[1]
    return pl.pallas_call(
        _mm_f32_kernel,
        grid=(n // tm,),
        in_specs=[pl.BlockSpec((tm, k), lambda i: (i, 0)),
                  pl.BlockSpec((k, m), lambda i: (0, 0))],
        out_specs=pl.BlockSpec((tm, m), lambda i: (i, 0)),
        out_shape=jax.ShapeDtypeStruct((n, m), F32),
        compiler_params=_params("parallel"),
        name="matmul_f32",
    )(x, w)


def _layer_norm_rows(z, g, b):
    mu = jnp.mean(z, -1, keepdims=True)
    zc = z - mu
    var = jnp.mean(zc * zc, -1, keepdims=True)
    return zc * lax.rsqrt(var + LN_EPS) * g + b


def _mm_res_ln_kernel(x_ref, w_ref, h_ref, g_ref, b_ref, o_ref, ob_ref):
    y = jnp.dot(x_ref[...], w_ref[...], preferred_element_type=F32)
    out = _layer_norm_rows(DN_ALPHA * h_ref[...] + y, g_ref[...], b_ref[...])
    o_ref[...] = out
    ob_ref[...] = out.astype(BF16)


def _matmul_res_ln(x, w, h, g, b, *, tm):
    n, k = x.shape
    d = w.shape[1]
    return pl.pallas_call(
        _mm_res_ln_kernel,
        grid=(n // tm,),
        in_specs=[pl.BlockSpec((tm, k), lambda i: (i, 0)),
                  pl.BlockSpec((k, d), lambda i: (0, 0)),
                  pl.BlockSpec((tm, d), lambda i: (i, 0)),
                  pl.BlockSpec((1, d), lambda i: (0, 0)),
                  pl.BlockSpec((1, d), lambda i: (0, 0))],
        out_specs=[pl.BlockSpec((tm, d), lambda i: (i, 0)),
                   pl.BlockSpec((tm, d), lambda i: (i, 0))],
        out_shape=[jax.ShapeDtypeStruct((n, d), F32), jax.ShapeDtypeStruct((n, d), BF16)],
        compiler_params=_params("parallel"),
        name="matmul_res_ln",
    )(x, w, h, g.reshape(1, d), b.reshape(1, d))


def _proj_a_kernel(x_ref, w_ref, c_ref, s1_ref, s2_ref, q_ref, k_ref, v_ref):
    x = x_ref[...]
    width = 2 * LANES
    for c, ref in ((0, q_ref), (1, k_ref), (2, v_ref)):
        for j in range(D_MODEL // width):
            col = c * D_MODEL + j * width
            y = jnp.dot(x, w_ref[:, col:col + width], preferred_element_type=F32)
            if c < 2:
                half = A_ROT_DIM // 2
                y = (y * c_ref[...] + pltpu.roll(y, width - half, 1) * s1_ref[...]
                     + pltpu.roll(y, half, 1) * s2_ref[...])
            if c == 0:
                y = y * (A_HEAD_DIM ** -0.5)
            ref[0, 2 * j] = y[:, :LANES]
            ref[0, 2 * j + 1] = y[:, LANES:]


def _proj_a(xb, w, tabs, batch, seq, *, tm):
    n, d = xb.shape
    spb = seq // tm
    hp = D_MODEL // LANES
    qkv_shape = jax.ShapeDtypeStruct((batch, hp, seq, LANES), F32)
    out_spec = pl.BlockSpec((1, hp, tm, LANES), lambda i: (i // spb, 0, i % spb, 0))
    tab_spec = pl.BlockSpec((tm, 2 * LANES), lambda i: (i % spb, 0))
    return pl.pallas_call(
        _proj_a_kernel,
        grid=(n // tm,),
        in_specs=[pl.BlockSpec((tm, d), lambda i: (i, 0)),
                  pl.BlockSpec((d, 3 * d), lambda i: (0, 0)),
                  tab_spec, tab_spec, tab_spec],
        out_specs=[out_spec, out_spec, out_spec],
        out_shape=[qkv_shape, qkv_shape, qkv_shape],
        compiler_params=_params("parallel"),
        name="proj_a",
    )(xb, w, *tabs)


def _rope_tables_a(positions):
    half = A_ROT_DIM // 2
    inv = ROPE_THETA ** (-jnp.arange(0, A_ROT_DIM, 2, dtype=F32) / A_ROT_DIM)
    ang = positions.astype(F32)[:, None] * inv[None, :]
    cos, sin = jnp.cos(ang), jnp.sin(ang)
    s = positions.shape[0]
    pad = jnp.zeros((s, A_HEAD_DIM - A_ROT_DIM), F32)
    c_head = jnp.concatenate([cos, cos, pad + 1.0], -1)
    s1_head = jnp.concatenate([-sin, jnp.zeros_like(sin), pad], -1)
    s2_head = jnp.concatenate([jnp.zeros_like(sin), sin, pad], -1)
    reps = 2 * LANES // A_HEAD_DIM
    return tuple(jnp.tile(t, (1, reps)) for t in (c_head, s1_head, s2_head))


def _attn_update(q_ref, k_ref, v_ref, acc_ref, m_ref, l_ref, bias, head0, qstart, kstart, nk, d):
    qsl = pl.ds(qstart, A_BLOCK, stride=d) if d > 1 else pl.ds(qstart, A_BLOCK)
    ksl = pl.ds(kstart, nk, stride=d) if d > 1 else pl.ds(kstart, nk)
    q = q_ref[qsl, :]
    kb = k_ref[ksl, :].astype(BF16)
    vb = v_ref[ksl, :].astype(BF16)
    acc_old = acc_ref[qsl, :]
    alphas, pvs = [], []
    for h in range(2):
        sel = head0 if h == 0 else jnp.logical_not(head0)
        qh = jnp.where(sel, q, 0.0).astype(BF16)
        s = lax.dot_general(qh, kb, (((1,), (1,)), ((), ())), preferred_element_type=F32) + bias
        m_o = m_ref[h, qsl, :]
        m_n = jnp.maximum(m_o, jnp.max(s, -1, keepdims=True))
        p = jnp.exp(s - jnp.concatenate([m_n] * (nk // LANES), axis=1))
        alpha = jnp.exp(m_o - m_n)
        l_ref[h, qsl, :] = alpha * l_ref[h, qsl, :] + jnp.sum(p, -1, keepdims=True)
        m_ref[h, qsl, :] = m_n
        alphas.append(alpha)
        pvs.append(jnp.dot(p.astype(BF16), vb, preferred_element_type=F32))
    acc_ref[qsl, :] = acc_old * jnp.where(head0, alphas[0], alphas[1]) + jnp.where(head0, pvs[0], pvs[1])


def _attn_kernel(q_ref, k_ref, v_ref, o_ref, acc_ref, m_ref, l_ref):
    seq = q_ref.shape[0]
    m_ref[...] = jnp.full(m_ref.shape, NEG, F32)
    l_ref[...] = jnp.zeros(l_ref.shape, F32)
    acc_ref[...] = jnp.zeros(acc_ref.shape, F32)
    head0 = lax.broadcasted_iota(jnp.int32, (A_BLOCK, LANES), 1) < A_HEAD_DIM
    qi = lax.broadcasted_iota(jnp.int32, (A_BLOCK, 2 * A_BLOCK), 0)
    kj = lax.broadcasted_iota(jnp.int32, (A_BLOCK, 2 * A_BLOCK), 1)
    bias2 = jnp.where((kj >= qi) & (kj <= qi + A_BLOCK), 0.0, NEG).astype(F32)
    qi1 = lax.broadcasted_iota(jnp.int32, (A_BLOCK, A_BLOCK), 0)
    kj1 = lax.broadcasted_iota(jnp.int32, (A_BLOCK, A_BLOCK), 1)
    bias1 = jnp.where(kj1 <= qi1, 0.0, NEG).astype(F32)
    for d in A_DILATIONS:
        nb = seq // (d * A_BLOCK)

        def first(r, carry, d=d):
            _attn_update(q_ref, k_ref, v_ref, acc_ref, m_ref, l_ref, bias1, head0, r, r, A_BLOCK, d)
            return carry

        def rest(it, carry, d=d, nb=nb):
            r = it % d
            n = it // d + 1
            qstart = n * (A_BLOCK * d) + r
            _attn_update(q_ref, k_ref, v_ref, acc_ref, m_ref, l_ref, bias2, head0,
                         qstart, qstart - A_BLOCK * d, 2 * A_BLOCK, d)
            return carry

        lax.fori_loop(0, d, first, 0)
        lax.fori_loop(0, d * (nb - 1), rest, 0)
    head0_row = lax.broadcasted_iota(jnp.int32, (1, LANES), 1) < A_HEAD_DIM
    o_ref[...] = (acc_ref[...] / jnp.where(head0_row, l_ref[0], l_ref[1])).astype(o_ref.dtype)


def _attention(q, k, v):
    batch, hp, seq, _ = q.shape
    in_spec = pl.BlockSpec((None, None, seq, LANES), lambda b, p: (b, p, 0, 0))
    return pl.pallas_call(
        _attn_kernel,
        grid=(batch, hp),
        in_specs=[in_spec, in_spec, in_spec],
        out_specs=pl.BlockSpec((None, seq, LANES), lambda b, p: (b, 0, p)),
        out_shape=jax.ShapeDtypeStruct((batch, seq, hp * LANES), BF16),
        scratch_shapes=[pltpu.VMEM((seq, LANES), F32), pltpu.VMEM((2, seq, LANES), F32),
                        pltpu.VMEM((2, seq, LANES), F32)],
        compiler_params=_params("parallel", "parallel"),
        name="dilated_attention",
    )(q, k, v)


def _mixer_dilated(hb, w_in, tabs, batch, seq):
    q, k, v = _proj_a(hb, w_in.astype(BF16), tabs, batch, seq, tm=512)
    o = _attention(q, k, v)
    return o.reshape(batch * seq, D_MODEL)


def _mlstm_kernel(qk_ref, v_ref, o_ref, gc_ref, gr_ref, cw_ref, cb_ref, ng_ref, out_ref,
                  ext_ref, c_ref, n_ref, m_ref):
    L = B_CHUNK
    chunk = pl.program_id(1)

    @pl.when(chunk == 0)
    def _():
        ext_ref[0:8, :] = jnp.zeros((8, D_MODEL), F32)
        c_ref[...] = jnp.zeros(c_ref.shape, F32)
        n_ref[...] = jnp.zeros(n_ref.shape, F32)
        m_ref[...] = jnp.zeros(m_ref.shape, F32)

    u = qk_ref[...]
    ext_ref[8:8 + L, :] = u
    conv = u * cw_ref[B_CONV - 1:B_CONV, :] + cb_ref[...]
    for j in range(1, B_CONV):
        conv = conv + ext_ref[pl.ds(8 - j, L), :] * cw_ref[B_CONV - 1 - j:B_CONV - j, :]
    ext_ref[0:8, :] = u[L - 8:, :]
    qk = conv * jax.nn.sigmoid(conv)
    half = D_MODEL // 2

    gc = gc_ref[...]
    gr = gr_ref[...]
    i_col, i_row = gc[:, :B_HEADS], gr[:B_HEADS, :]
    lf_col = jax.nn.log_sigmoid(gc[:, B_HEADS:])
    lf_row = jax.nn.log_sigmoid(gr[B_HEADS:, :])
    ti = lax.broadcasted_iota(jnp.int32, (L, L), 0)
    si = lax.broadcasted_iota(jnp.int32, (L, L), 1)
    causal = ti >= si
    tri = causal.astype(F32)
    a_col = jnp.dot(tri, lf_col, preferred_element_type=F32, precision=lax.Precision.HIGHEST)
    a_row = lax.dot_general(lf_row, tri, (((1,), (1,)), ((), ())), preferred_element_type=F32,
                            precision=lax.Precision.HIGHEST)
    lane = lax.broadcasted_iota(jnp.int32, (1, LANES), 1)
    lane_h0 = lane < B_QK_DIM
    col_h0 = lax.broadcasted_iota(jnp.int32, (1, 2 * B_V_DIM), 1) < B_V_DIM

    for p in range(B_HEADS // 2):
        qp = qk[:, p * LANES:(p + 1) * LANES]
        kp = qk[:, half + p * LANES:half + (p + 1) * LANES] * (B_QK_DIM ** -0.5)
        vp = v_ref[:, p * 2 * B_V_DIM:(p + 1) * 2 * B_V_DIM]
        kpb = kp.astype(BF16)
        vpb = vp.astype(BF16)
        c_old = c_ref[p]
        c_oldb = c_old.astype(BF16)
        n_old = n_ref[p]
        m_pair = m_ref[p]
        ws_cols, decays, m_news = [], [], []
        for hh in range(2):
            h = 2 * p + hh
            m_old = m_pair[:, hh * B_QK_DIM:hh * B_QK_DIM + 1]
            ac, ar = a_col[:, h:h + 1], a_row[h:h + 1, :]
            ic, ir = i_col[:, h:h + 1], i_row[h:h + 1, :]
            dmat = jnp.where(causal, ac - ar + ir, NEG)
            inter = ac + m_old
            m_t = jnp.maximum(inter, jnp.max(dmat, -1, keepdims=True))
            qm = jnp.where(lane_h0 if hh == 0 else jnp.logical_not(lane_h0), qp, 0.0)
            qmb = qm.astype(BF16)
            sc = lax.dot_general(qmb, kpb, (((1,), (1,)), ((), ())), preferred_element_type=F32)
            sc = sc * jnp.exp(dmat - m_t)
            g_inter = jnp.exp(inter - m_t)
            vh = vpb[:, hh * B_V_DIM:(hh + 1) * B_V_DIM]
            qc = jnp.dot(qmb, c_oldb, preferred_element_type=F32)[:, hh * B_V_DIM:(hh + 1) * B_V_DIM]
            num = jnp.dot(sc.astype(BF16), vh, preferred_element_type=F32) + g_inter * qc
            den = jnp.sum(sc, -1, keepdims=True) + g_inter * jnp.sum(qm * n_old, -1, keepdims=True)
            h_out = num / jnp.maximum(jnp.abs(den), jnp.exp(-m_t))
            mu = jnp.mean(h_out, -1, keepdims=True)
            hc = h_out - mu
            var = jnp.mean(hc * hc, -1, keepdims=True)
            cols = slice(h * B_V_DIM, (h + 1) * B_V_DIM)
            hn = hc * lax.rsqrt(var + LN_EPS) * ng_ref[:, cols]
            out_ref[:, cols] = (hn * jax.nn.sigmoid(o_ref[:, cols])).astype(out_ref.dtype)
            a_end = ac[L - 1:L, :]
            w_col = a_end - ac + ic
            m_new = jnp.maximum(a_end + m_old, jnp.max(w_col, 0, keepdims=True))
            decays.append(jnp.exp(a_end + m_old - m_new))
            ws_cols.append(jnp.exp(w_col - m_new))
            m_news.append(m_new)
        ws = jnp.where(lane_h0, jnp.broadcast_to(ws_cols[0], (L, LANES)), jnp.broadcast_to(ws_cols[1], (L, LANES)))
        kw = kp * ws
        dec_c = jnp.where(col_h0, jnp.broadcast_to(decays[0], (1, 2 * B_V_DIM)),
                          jnp.broadcast_to(decays[1], (1, 2 * B_V_DIM)))
        dec_n = jnp.where(lane_h0, jnp.broadcast_to(decays[0], (1, LANES)), jnp.broadcast_to(decays[1], (1, LANES)))
        c_ref[p] = dec_c * c_old + lax.dot_general(kw.astype(BF16), vpb, (((0,), (0,)), ((), ())),
                                                   preferred_element_type=F32)
        n_ref[p] = dec_n * n_old + jnp.sum(kw, 0, keepdims=True)
        m_ref[p] = jnp.where(lane_h0, jnp.broadcast_to(m_news[0], (1, LANES)), jnp.broadcast_to(m_news[1], (1, LANES)))


def _mlstm(proj, gates_col, gates_row, conv_w, conv_b, norm_g, batch, seq):
    L = B_CHUNK
    d = D_MODEL
    slab = lambda c: pl.BlockSpec((None, L, d), lambda b, s, c=c: (b, s, c))
    full = lambda shape: pl.BlockSpec(shape, lambda b, s: (0,) * len(shape))
    return pl.pallas_call(
        _mlstm_kernel,
        grid=(batch, seq // L),
        in_specs=[slab(0), slab(1), slab(2),
                  pl.BlockSpec((None, L, 2 * B_HEADS), lambda b, s: (b, s, 0)),
                  pl.BlockSpec((None, 2 * B_HEADS, L), lambda b, s: (b, 0, s)),
                  full((B_CONV, d)), full((1, d)), full((1, d))],
        out_specs=pl.BlockSpec((None, L, d), lambda b, s: (b, s, 0)),
        out_shape=jax.ShapeDtypeStruct((batch, seq, d), BF16),
        scratch_shapes=[pltpu.VMEM((L + 8, d), F32),
                        pltpu.VMEM((B_HEADS // 2, 2 * B_QK_DIM, 2 * B_V_DIM), F32),
                        pltpu.VMEM((B_HEADS // 2, 1, LANES), F32),
                        pltpu.VMEM((B_HEADS // 2, 1, LANES), F32)],
        compiler_params=_params("parallel", "arbitrary"),
        name="mlstm",
    )(proj, proj, proj, gates_col, gates_row, conv_w, conv_b.reshape(1, d), norm_g.reshape(1, d))


def _mixer_mlstm(h, hb, w_in, gate_bias, conv_w, conv_b, norm_g, batch, seq):
    n = batch * seq
    main = 3 * D_MODEL
    proj = _matmul(hb, w_in[:, :main].astype(BF16), tm=1024, tn=1024)
    w_g = jnp.pad(w_in[:, main:], ((0, 0), (0, LANES - 2 * B_HEADS)))
    gates = _matmul_f32(h, w_g, tm=1024)[:, :2 * B_HEADS] + gate_bias
    gates = gates.reshape(batch, seq, 2 * B_HEADS)
    out = _mlstm(proj.reshape(batch, seq, main), gates, gates.transpose(0, 2, 1),
                 conv_w, conv_b, norm_g, batch, seq)
    return out.reshape(n, D_MODEL)


def _retention_kernel(lg_ref, q_ref, k_ref, v_ref, g_ref, cos_ref, sin_ref, ng_ref, out_ref,
                      r_ref, dm_ref, xi_ref, zeta_ref):
    L = C_CHUNK
    head = pl.program_id(1)
    chunk = pl.program_id(2)
    lg = lg_ref[head]

    @pl.when(chunk == 0)
    def _():
        r_ref[...] = jnp.zeros(r_ref.shape, F32)
        ti = lax.broadcasted_iota(jnp.int32, (L, L), 0)
        si = lax.broadcasted_iota(jnp.int32, (L, L), 1)
        rel = (ti - si).astype(F32)
        dm_ref[...] = jnp.where(rel >= 0, jnp.exp(jnp.maximum(rel, 0.0) * lg), 0.0)
        idx = lax.broadcasted_iota(jnp.int32, (L, LANES), 0).astype(F32)
        xi_ref[...] = jnp.exp((idx + 1.0) * lg)
        zeta_ref[...] = jnp.exp((L - 1.0 - idx) * lg)

    cos, sin = cos_ref[...], sin_ref[...]
    hd = C_QK_DIM // 2

    def rope(t):
        t1, t2 = t[:, :hd], t[:, hd:]
        return jnp.concatenate([t1 * cos - t2 * sin, t2 * cos + t1 * sin], -1)

    q = rope(q_ref[...])
    k = rope(k_ref[...]) * (C_QK_DIM ** -0.5)
    qb = q.astype(BF16)
    vb = v_ref[...].astype(BF16)
    r_old = r_ref[...]
    sc = lax.dot_general(qb, k.astype(BF16), (((1,), (1,)), ((), ())), preferred_element_type=F32) * dm_ref[...]
    o = jnp.dot(sc.astype(BF16), vb, preferred_element_type=F32)
    o = o + xi_ref[:, 0:1] * jnp.dot(qb, r_old.astype(BF16), preferred_element_type=F32)
    kz = (k * zeta_ref[:, 0:1]).astype(BF16)
    cd = jnp.exp(jnp.full((1, 1), float(L), F32) * lg)
    r_ref[...] = cd * r_old + lax.dot_general(kz, vb, (((0,), (0,)), ((), ())), preferred_element_type=F32)
    mu = jnp.mean(o, -1, keepdims=True)
    oc = o - mu
    var = jnp.mean(oc * oc, -1, keepdims=True)
    on = oc * lax.rsqrt(var + LN_EPS) * ng_ref[...]
    g = g_ref[...]
    out_ref[...] = (on * (g * jax.nn.sigmoid(g))).astype(out_ref.dtype)


def _retention(proj, cos, sin, norm_g, batch, seq):
    L = C_CHUNK
    log_gamma = jnp.log(1.0 - 2.0 ** (-5.0 - jnp.arange(C_HEADS, dtype=F32)))
    qk_blocks = D_MODEL // C_QK_DIM
    v_off = 2 * D_MODEL // C_V_DIM
    g_off = 4 * D_MODEL // C_V_DIM
    return pl.pallas_call(
        _retention_kernel,
        grid=(batch, C_HEADS, seq // L),
        in_specs=[pl.BlockSpec(memory_space=pltpu.SMEM),
                  pl.BlockSpec((None, L, C_QK_DIM), lambda b, h, c: (b, c, h)),
                  pl.BlockSpec((None, L, C_QK_DIM), lambda b, h, c: (b, c, qk_blocks + h)),
                  pl.BlockSpec((None, L, C_V_DIM), lambda b, h, c: (b, c, v_off + h)),
                  pl.BlockSpec((None, L, C_V_DIM), lambda b, h, c: (b, c, g_off + h)),
                  pl.BlockSpec((L, C_QK_DIM // 2), lambda b, h, c: (c, 0)),
                  pl.BlockSpec((L, C_QK_DIM // 2), lambda b, h, c: (c, 0)),
                  pl.BlockSpec((1, C_V_DIM), lambda b, h, c: (0, h))],
        out_specs=pl.BlockSpec((None, L, C_V_DIM), lambda b, h, c: (b, c, h)),
        out_shape=jax.ShapeDtypeStruct((batch, seq, 2 * D_MODEL), BF16),
        scratch_shapes=[pltpu.VMEM((C_QK_DIM, C_V_DIM), F32),
                        pltpu.VMEM((L, L), F32),
                        pltpu.VMEM((L, LANES), F32),
                        pltpu.VMEM((L, LANES), F32)],
        compiler_params=_params("parallel", "parallel", "arbitrary"),
        name="retention",
    )(log_gamma, proj, proj, proj, proj, cos, sin, norm_g.reshape(1, 2 * D_MODEL))


def _mixer_retention(hb, w_in, norm_g, cos, sin, batch, seq):
    proj = _matmul(hb, w_in.astype(BF16), tm=1024, tn=1024)
    out = _retention(proj.reshape(batch, seq, 6 * D_MODEL), cos, sin, norm_g, batch, seq)
    return out.reshape(batch * seq, 2 * D_MODEL)


def _expert_kernel(be_ref, x_ref, wg_ref, wu_ref, wd_ref, y_ref, wgb_ref, wub_ref, wdb_ref):
    j = pl.program_id(0)
    changed = jnp.logical_or(j == 0, be_ref[j] != be_ref[jnp.maximum(j - 1, 0)])

    @pl.when(changed)
    def _():
        wgb_ref[...] = wg_ref[...].astype(BF16)
        wub_ref[...] = wu_ref[...].astype(BF16)
        wdb_ref[...] = wd_ref[...].astype(BF16)

    x = x_ref[...]
    a = jnp.dot(x, wgb_ref[...], preferred_element_type=F32)
    u = jnp.dot(x, wub_ref[...], preferred_element_type=F32)
    act = (a * jax.nn.sigmoid(a) * u).astype(BF16)
    y_ref[...] = jnp.dot(act, wdb_ref[...], preferred_element_type=F32)


def _experts(blk_e, xb, w_gate, w_up, w_down):
    p, d = xb.shape
    nb = p // MOE_BLOCK
    hid = MOE_HIDDEN
    grid_spec = pltpu.PrefetchScalarGridSpec(
        num_scalar_prefetch=1,
        grid=(nb,),
        in_specs=[pl.BlockSpec((MOE_BLOCK, d), lambda j, be: (j, 0)),
                  pl.BlockSpec((None, d, hid), lambda j, be: (be[j], 0, 0)),
                  pl.BlockSpec((None, d, hid), lambda j, be: (be[j], 0, 0)),
                  pl.BlockSpec((None, hid, d), lambda j, be: (be[j], 0, 0))],
        out_specs=pl.BlockSpec((MOE_BLOCK, d), lambda j, be: (j, 0)),
        scratch_shapes=[pltpu.VMEM((d, hid), BF16), pltpu.VMEM((d, hid), BF16), pltpu.VMEM((hid, d), BF16)],
    )
    return pl.pallas_call(
        _expert_kernel,
        grid_spec=grid_spec,
        out_shape=jax.ShapeDtypeStruct((p, d), F32),
        compiler_params=_params("arbitrary"),
        name="moe_experts",
    )(blk_e, xb, w_gate, w_up, w_down)


def _combine_ln_kernel(h_ref, y0_ref, y1_ref, gt_ref, g_ref, b_ref, o_ref, ob_ref):
    gt = gt_ref[...]
    y = y0_ref[...] * gt[:, 0:1] + y1_ref[...] * gt[:, 1:2]
    out = _layer_norm_rows(DN_ALPHA * h_ref[...] + y, g_ref[...], b_ref[...])
    o_ref[...] = out
    ob_ref[...] = out.astype(BF16)


def _combine_ln(h, y0, y1, gates, g, b, *, tm):
    n, d = h.shape
    row = pl.BlockSpec((tm, d), lambda i: (i, 0))
    vec = pl.BlockSpec((1, d), lambda i: (0, 0))
    return pl.pallas_call(
        _combine_ln_kernel,
        grid=(n // tm,),
        in_specs=[row, row, row, pl.BlockSpec((tm, MOE_TOPK), lambda i: (i, 0)), vec, vec],
        out_specs=[row, row],
        out_shape=[jax.ShapeDtypeStruct((n, d), F32), jax.ShapeDtypeStruct((n, d), BF16)],
        compiler_params=_params("parallel"),
        name="moe_combine_ln",
    )(h, y0, y1, gates, g.reshape(1, d), b.reshape(1, d))


def _moe(h, hb, wg_r, bg_r, we_r, be_r, w_gate, w_up, w_down, ln_g, ln_b):
    n, d = h.shape
    w_r = jnp.pad(jnp.concatenate([wg_r, we_r], 1), ((0, 0), (0, LANES - MOE_GROUPS - MOE_EXPERTS)))
    logits = _matmul_f32(h, w_r, tm=1024)
    g_logits = logits[:, :MOE_GROUPS] + bg_r
    e_logits = (logits[:, MOE_GROUPS:MOE_GROUPS + MOE_EXPERTS] + be_r).reshape(n, MOE_GROUPS, MOE_PER_GROUP)
    grp = jnp.argmax(g_logits, -1)
    p_grp = jnp.take_along_axis(jax.nn.softmax(g_logits, -1), grp[:, None], -1)[:, 0]
    e_in = jnp.take_along_axis(e_logits, grp[:, None, None], axis=1)[:, 0]
    top_v, top_i = lax.top_k(e_in, MOE_TOPK)
    gates = jax.nn.softmax(top_v, -1) * p_grp[:, None]
    eid = (grp[:, None] * MOE_PER_GROUP + top_i).astype(jnp.int32)
    flat = eid.reshape(-1)
    onehot = (flat[:, None] == jnp.arange(MOE_EXPERTS, dtype=jnp.int32)[None, :]).astype(jnp.int32)
    rank = jnp.take_along_axis(jnp.cumsum(onehot, 0) - onehot, flat[:, None], 1)[:, 0]
    counts = jnp.sum(onehot, 0)
    padded = (counts + MOE_BLOCK - 1) // MOE_BLOCK * MOE_BLOCK
    pends = jnp.cumsum(padded)
    dest = (pends - padded)[flat] + rank
    p_rows = n * MOE_TOPK + MOE_EXPERTS * MOE_BLOCK
    nb = p_rows // MOE_BLOCK
    tok = jnp.repeat(jnp.arange(n, dtype=jnp.int32), MOE_TOPK)
    buf_tok = jnp.zeros((p_rows,), jnp.int32).at[dest].set(tok)
    blk_e = jnp.minimum(jnp.searchsorted(pends, jnp.arange(nb, dtype=jnp.int32) * MOE_BLOCK, side='right'),
                        MOE_EXPERTS - 1).astype(jnp.int32)
    xb = hb[buf_tok]
    yb = _experts(blk_e, xb, w_gate, w_up, w_down)
    dest2 = dest.reshape(n, MOE_TOPK)
    return _combine_ln(h, yb[dest2[:, 0]], yb[dest2[:, 1]], gates, ln_g, ln_b, tm=512)


def kernel(x, positions, ln1_g, ln1_b, ln2_g, ln2_b, a_w_in, a_w_out, b_w_in, b_gate_bias, b_conv_w, b_conv_b,
           b_norm_g, b_w_out, c_w_in, c_norm_g, c_w_out, r_group_w, r_group_b, r_expert_w, r_expert_b,
           e_w_gate, e_w_up, e_w_down):
    batch, seq, d = x.shape
    n = batch * seq
    tabs_a = _rope_tables_a(positions)
    inv_c = C_THETA ** (-jnp.arange(0, C_QK_DIM, 2, dtype=F32) / C_QK_DIM)
    ang_c = positions.astype(F32)[:, None] * inv_c[None, :]
    cos_c, sin_c = jnp.cos(ang_c), jnp.sin(ang_c)
    h = x.reshape(n, d)
    hb = h.astype(BF16)
    for i in range(DEPTH):
        kind, j = i % 3, i // 3
        if kind == 0:
            y = _mixer_dilated(hb, a_w_in[j], tabs_a, batch, seq)
            w_out = a_w_out[j]
        elif kind == 1:
            y = _mixer_mlstm(h, hb, b_w_in[j], b_gate_bias[j], b_conv_w[j], b_conv_b[j], b_norm_g[j], batch, seq)
            w_out = b_w_out[j]
        else:
            y = _mixer_retention(hb, c_w_in[j], c_norm_g[j], cos_c, sin_c, batch, seq)
            w_out = c_w_out[j]
        h, hb = _matmul_res_ln(y, w_out.astype(BF16), h, ln1_g[i], ln1_b[i], tm=512)
        h, hb = _moe(h, hb, r_group_w[i], r_group_b[i], r_expert_w[i], r_expert_b[i],
                     e_w_gate[i], e_w_up[i], e_w_down[i], ln2_g[i], ln2_b[i])
    return h.reshape(batch, seq, d)
```

```python
import functools

import jax
import jax.numpy as jnp
from jax import lax
from jax.experimental import pallas as pl
from jax.experimental.pallas import tpu as pltpu

F32 = jnp.float32
BF16 = jnp.bfloat16

D_MODEL = 1024
DEPTH = 4
DN_ALPHA = (2.0 * DEPTH) ** 0.25
LN_EPS = 1e-5

A_HEADS = 16
A_HEAD_DIM = 64
A_DILATIONS = (1, 4, 16)
A_BLOCK = 128
A_UNROLL = 4
LOG2_E = 1.4426950408889634
A_ROT_DIM = 16
ROPE_THETA = 500000.0

B_HEADS = 8
B_QK_DIM = 64
B_V_DIM = 128
B_CONV = 4
B_CHUNK = 128

C_HEADS = 4
C_QK_DIM = 256
C_V_DIM = 512
C_CHUNK = 256
C_THETA = 10000.0

MOE_GROUPS = 8
MOE_PER_GROUP = 8
MOE_EXPERTS = 64
MOE_TOPK = 2
MOE_HIDDEN = 256
MOE_BLOCK = 256

LANES = 128
NEG = -1e30
VMEM_LIMIT = 48 * 1024 * 1024


def _params(*sem):
    return pltpu.CompilerParams(dimension_semantics=sem, vmem_limit_bytes=VMEM_LIMIT)


def _mm_kernel(x_ref, w_ref, o_ref):
    o_ref[...] = jnp.dot(x_ref[...], w_ref[...], preferred_element_type=F32).astype(o_ref.dtype)


def _matmul(x, w, *, tm, tn, out_dtype=F32):
    n, k = x.shape
    m = w.shape[1]
    return pl.pallas_call(
        _mm_kernel,
        grid=(n // tm, m // tn),
        in_specs=[pl.BlockSpec((tm, k), lambda i, j: (i, 0)),
                  pl.BlockSpec((k, tn), lambda i, j: (0, j))],
        out_specs=pl.BlockSpec((tm, tn), lambda i, j: (i, j)),
        out_shape=jax.ShapeDtypeStruct((n, m), out_dtype),
        compiler_params=_params("parallel", "parallel"),
        name="matmul",
    )(x, w)


def _layer_norm_rows(z, g, b):
    mu = jnp.mean(z, -1, keepdims=True)
    zc = z - mu
    var = jnp.mean(zc * zc, -1, keepdims=True)
    return zc * lax.rsqrt(var + LN_EPS) * g + b


def _mm_res_ln_kernel(x_ref, w_ref, h_ref, g_ref, b_ref, o_ref, ob_ref):
    y = jnp.dot(x_ref[...], w_ref[...], preferred_element_type=F32)
    out = _layer_norm_rows(DN_ALPHA * h_ref[...] + y, g_ref[...], b_ref[...])
    o_ref[...] = out
    ob_ref[...] = out.astype(BF16)


def _matmul_res_ln(x, w, h, g, b, *, tm):
    n, k = x.shape
    d = w.shape[1]
    return pl.pallas_call(
        _mm_res_ln_kernel,
        grid=(n // tm,),
        in_specs=[pl.BlockSpec((tm, k), lambda i: (i, 0)),
                  pl.BlockSpec((k, d), lambda i: (0, 0)),
                  pl.BlockSpec((tm, d), lambda i: (i, 0)),
                  pl.BlockSpec((1, d), lambda i: (0, 0)),
                  pl.BlockSpec((1, d), lambda i: (0, 0))],
        out_specs=[pl.BlockSpec((tm, d), lambda i: (i, 0)),
                   pl.BlockSpec((tm, d), lambda i: (i, 0))],
        out_shape=[jax.ShapeDtypeStruct((n, d), F32), jax.ShapeDtypeStruct((n, d), BF16)],
        compiler_params=_params("parallel"),
        name="matmul_res_ln",
    )(x, w, h, g.reshape(1, d), b.reshape(1, d))


def _proj_a_kernel(x_ref, w_ref, c_ref, s1_ref, s2_ref, q_ref, k_ref, v_ref):
    x = x_ref[...]
    width = 2 * LANES
    for c, ref in ((0, q_ref), (1, k_ref), (2, v_ref)):
        for j in range(D_MODEL // width):
            col = c * D_MODEL + j * width
            y = jnp.dot(x, w_ref[:, col:col + width], preferred_element_type=F32)
            if c < 2:
                half = A_ROT_DIM // 2
                y = (y * c_ref[...] + pltpu.roll(y, width - half, 1) * s1_ref[...]
                     + pltpu.roll(y, half, 1) * s2_ref[...])
            if c == 0:
                y = y * (LOG2_E * A_HEAD_DIM ** -0.5)
            ref[0, 2 * j] = y[:, :LANES]
            ref[0, 2 * j + 1] = y[:, LANES:]


def _proj_a(xb, w, tabs, batch, seq, *, tm):
    n, d = xb.shape
    spb = seq // tm
    hp = D_MODEL // LANES
    qkv_shape = jax.ShapeDtypeStruct((batch, hp, seq, LANES), F32)
    out_spec = pl.BlockSpec((1, hp, tm, LANES), lambda i: (i // spb, 0, i % spb, 0))
    tab_spec = pl.BlockSpec((tm, 2 * LANES), lambda i: (i % spb, 0))
    return pl.pallas_call(
        _proj_a_kernel,
        grid=(n // tm,),
        in_specs=[pl.BlockSpec((tm, d), lambda i: (i, 0)),
                  pl.BlockSpec((d, 3 * d), lambda i: (0, 0)),
                  tab_spec, tab_spec, tab_spec],
        out_specs=[out_spec, out_spec, out_spec],
        out_shape=[qkv_shape, qkv_shape, qkv_shape],
        compiler_params=_params("parallel"),
        name="proj_a",
    )(xb, w, *tabs)


def _rope_tables_a(positions):
    half = A_ROT_DIM // 2
    inv = ROPE_THETA ** (-jnp.arange(0, A_ROT_DIM, 2, dtype=F32) / A_ROT_DIM)
    ang = positions.astype(F32)[:, None] * inv[None, :]
    cos, sin = jnp.cos(ang), jnp.sin(ang)
    s = positions.shape[0]
    pad = jnp.zeros((s, A_HEAD_DIM - A_ROT_DIM), F32)
    c_head = jnp.concatenate([cos, cos, pad + 1.0], -1)
    s1_head = jnp.concatenate([-sin, jnp.zeros_like(sin), pad], -1)
    s2_head = jnp.concatenate([jnp.zeros_like(sin), sin, pad], -1)
    reps = 2 * LANES // A_HEAD_DIM
    return tuple(jnp.tile(t, (1, reps)) for t in (c_head, s1_head, s2_head))


def _attn_blocks(q_ref, k_ref, v_ref, acc_ref, m_ref, l_ref, bias_ref, hmask, head0, first_block, d):
    nk = 2 * A_BLOCK
    loaded = []
    for u in range(A_UNROLL):
        g = first_block + u
        r = g % d
        n = g // d
        qstart = n * (A_BLOCK * d) + r
        kstart = jnp.maximum(qstart - A_BLOCK * d, r)
        if d == 1:
            qstart = pl.multiple_of(qstart, A_BLOCK)
            kstart = pl.multiple_of(kstart, A_BLOCK)
            qsl, ksl = pl.ds(qstart, A_BLOCK), pl.ds(kstart, nk)
        else:
            qsl, ksl = pl.ds(qstart, A_BLOCK, stride=d), pl.ds(kstart, nk, stride=d)
        bias = bias_ref[jnp.minimum(n, 1)]
        loaded.append((qsl, q_ref[qsl, :].astype(BF16), k_ref[ksl, :].astype(BF16), v_ref[ksl, :].astype(BF16),
                       bias, acc_ref[qsl, :], [m_ref[h, qsl, :] for h in range(2)],
                       [l_ref[h, qsl, :] for h in range(2)]))
    results = []
    for qsl, qb, kb, vb, bias, acc_old, m_old, l_old in loaded:
        alphas, pvs, m_new, l_new = [], [], [], []
        for h in range(2):
            s = lax.dot_general(qb * hmask[h], kb, (((1,), (1,)), ((), ())), preferred_element_type=F32) + bias
            m_n = jnp.maximum(m_old[h], jnp.max(s, -1, keepdims=True))
            p = jnp.exp2(s - jnp.concatenate([m_n] * (nk // LANES), axis=1))
            alpha = jnp.exp2(m_old[h] - m_n)
            l_new.append(alpha * l_old[h] + jnp.sum(p, -1, keepdims=True))
            m_new.append(m_n)
            alphas.append(alpha)
            pvs.append(jnp.dot(p.astype(BF16), vb, preferred_element_type=F32))
        acc_new = acc_old * jnp.where(head0, alphas[0], alphas[1]) + jnp.where(head0, pvs[0], pvs[1])
        results.append((qsl, acc_new, m_new, l_new))
    for qsl, acc_new, m_new, l_new in results:
        acc_ref[qsl, :] = acc_new
        for h in range(2):
            m_ref[h, qsl, :] = m_new[h]
            l_ref[h, qsl, :] = l_new[h]


def _attn_kernel(q_ref, k_ref, v_ref, o_ref, acc_ref, m_ref, l_ref, bias_ref):
    seq = q_ref.shape[0]
    m_ref[...] = jnp.full(m_ref.shape, NEG, F32)
    l_ref[...] = jnp.zeros(l_ref.shape, F32)
    acc_ref[...] = jnp.zeros(acc_ref.shape, F32)
    head0 = lax.broadcasted_iota(jnp.int32, (A_BLOCK, LANES), 1) < A_HEAD_DIM
    hmask = [jnp.where(head0, 1.0, 0.0).astype(BF16), jnp.where(head0, 0.0, 1.0).astype(BF16)]
    qi = lax.broadcasted_iota(jnp.int32, (A_BLOCK, 2 * A_BLOCK), 0)
    kj = lax.broadcasted_iota(jnp.int32, (A_BLOCK, 2 * A_BLOCK), 1)
    bias_ref[0] = jnp.where(kj <= qi, 0.0, NEG).astype(F32)
    bias_ref[1] = jnp.where((kj >= qi) & (kj <= qi + A_BLOCK), 0.0, NEG).astype(F32)
    for d in A_DILATIONS:

        def body(it, carry, d=d):
            _attn_blocks(q_ref, k_ref, v_ref, acc_ref, m_ref, l_ref, bias_ref, hmask, head0, it * A_UNROLL, d)
            return carry

        lax.fori_loop(0, seq // (A_BLOCK * A_UNROLL), body, 0)
    head0_row = lax.broadcasted_iota(jnp.int32, (1, LANES), 1) < A_HEAD_DIM
    o_ref[...] = (acc_ref[...] / jnp.where(head0_row, l_ref[0], l_ref[1])).astype(o_ref.dtype)


def _attention(q, k, v):
    batch, hp, seq, _ = q.shape
    assert seq % (2 * A_BLOCK * max(A_DILATIONS)) == 0 and seq % (A_BLOCK * A_UNROLL) == 0
    in_spec = pl.BlockSpec((None, None, seq, LANES), lambda b, p: (b, p, 0, 0))
    return pl.pallas_call(
        _attn_kernel,
        grid=(batch, hp),
        in_specs=[in_spec, in_spec, in_spec],
        out_specs=pl.BlockSpec((None, seq, LANES), lambda b, p: (b, 0, p)),
        out_shape=jax.ShapeDtypeStruct((batch, seq, hp * LANES), BF16),
        scratch_shapes=[pltpu.VMEM((seq, LANES), F32), pltpu.VMEM((2, seq, LANES), F32),
                        pltpu.VMEM((2, seq, LANES), F32), pltpu.VMEM((2, A_BLOCK, 2 * A_BLOCK), F32)],
        compiler_params=_params("parallel", "parallel"),
        name="dilated_attention",
    )(q, k, v)


def _mixer_dilated(hb, w_in, tabs, batch, seq):
    q, k, v = _proj_a(hb, w_in.astype(BF16), tabs, batch, seq, tm=512)
    o = _attention(q, k, v)
    return o.reshape(batch * seq, D_MODEL)


def _mlstm_kernel(qk_ref, v_ref, o_ref, gc_ref, gb_ref, cw_ref, cb_ref, ng_ref, out_ref,
                  ext_ref, c_ref, n_ref, m_ref):
    L = B_CHUNK
    chunk = pl.program_id(1)

    @pl.when(chunk == 0)
    def _():
        ext_ref[0:8, :] = jnp.zeros((8, D_MODEL), F32)
        c_ref[...] = jnp.zeros(c_ref.shape, F32)
        n_ref[...] = jnp.zeros(n_ref.shape, F32)
        m_ref[...] = jnp.zeros(m_ref.shape, F32)

    u = qk_ref[...]
    ext_ref[8:8 + L, :] = u
    conv = u * cw_ref[B_CONV - 1:B_CONV, :] + cb_ref[...]
    for j in range(1, B_CONV):
        conv = conv + ext_ref[pl.ds(8 - j, L), :] * cw_ref[B_CONV - 1 - j:B_CONV - j, :]
    ext_ref[0:8, :] = u[L - 8:, :]
    qk = conv * jax.nn.sigmoid(conv)
    half = D_MODEL // 2

    gc = gc_ref[...] + gb_ref[...]
    gr = gc.T
    i_col, i_row = gc, gr[:B_HEADS, :]
    lf_col = jax.nn.log_sigmoid(gc)
    lf_row = jax.nn.log_sigmoid(gr[B_HEADS:2 * B_HEADS, :])
    ti = lax.broadcasted_iota(jnp.int32, (L, L), 0)
    si = lax.broadcasted_iota(jnp.int32, (L, L), 1)
    causal = ti >= si
    tri = causal.astype(F32)
    a_col = jnp.dot(tri, lf_col, preferred_element_type=F32, precision=lax.Precision.HIGHEST)
    a_row = lax.dot_general(lf_row, tri, (((1,), (1,)), ((), ())), preferred_element_type=F32,
                            precision=lax.Precision.HIGHEST)
    lane = lax.broadcasted_iota(jnp.int32, (1, LANES), 1)
    lane_h0 = lane < B_QK_DIM
    col_h0 = lax.broadcasted_iota(jnp.int32, (1, 2 * B_V_DIM), 1) < B_V_DIM

    for p in range(B_HEADS // 2):
        qp = qk[:, p * LANES:(p + 1) * LANES]
        kp = qk[:, half + p * LANES:half + (p + 1) * LANES] * (B_QK_DIM ** -0.5)
        vp = v_ref[:, p * 2 * B_V_DIM:(p + 1) * 2 * B_V_DIM]
        kpb = kp.astype(BF16)
        vpb = vp.astype(BF16)
        c_old = c_ref[p]
        c_oldb = c_old.astype(BF16)
        n_old = n_ref[p]
        m_pair = m_ref[p]
        ws_cols, decays, m_news = [], [], []
        for hh in range(2):
            h = 2 * p + hh
            m_old = m_pair[:, hh * B_QK_DIM:hh * B_QK_DIM + 1]
            ac, ar = a_col[:, B_HEADS + h:B_HEADS + h + 1], a_row[h:h + 1, :]
            ic, ir = i_col[:, h:h + 1], i_row[h:h + 1, :]
            dmat = jnp.where(causal, ac - ar + ir, NEG)
            inter = ac + m_old
            m_t = jnp.maximum(inter, jnp.max(dmat, -1, keepdims=True))
            qm = jnp.where(lane_h0 if hh == 0 else jnp.logical_not(lane_h0), qp, 0.0)
            qmb = qm.astype(BF16)
            sc = lax.dot_general(qmb, kpb, (((1,), (1,)), ((), ())), preferred_element_type=F32)
            sc = sc * jnp.exp(dmat - m_t)
            g_inter = jnp.exp(inter - m_t)
            vh = vpb[:, hh * B_V_DIM:(hh + 1) * B_V_DIM]
            qc = jnp.dot(qmb, c_oldb, preferred_element_type=F32)[:, hh * B_V_DIM:(hh + 1) * B_V_DIM]
            num = jnp.dot(sc.astype(BF16), vh, preferred_element_type=F32) + g_inter * qc
            den = jnp.sum(sc, -1, keepdims=True) + g_inter * jnp.sum(qm * n_old, -1, keepdims=True)
            h_out = num / jnp.maximum(jnp.abs(den), jnp.exp(-m_t))
            mu = jnp.mean(h_out, -1, keepdims=True)
            hc = h_out - mu
            var = jnp.mean(hc * hc, -1, keepdims=True)
            cols = slice(h * B_V_DIM, (h + 1) * B_V_DIM)
            hn = hc * lax.rsqrt(var + LN_EPS) * ng_ref[:, cols]
            out_ref[:, cols] = (hn * jax.nn.sigmoid(o_ref[:, cols])).astype(out_ref.dtype)
            a_end = ac[L - 1:L, :]
            w_col = a_end - ac + ic
            m_new = jnp.maximum(a_end + m_old, jnp.max(w_col, 0, keepdims=True))
            decays.append(jnp.exp(a_end + m_old - m_new))
            ws_cols.append(jnp.exp(w_col - m_new))
            m_news.append(m_new)
        ws = jnp.where(lane_h0, jnp.broadcast_to(ws_cols[0], (L, LANES)), jnp.broadcast_to(ws_cols[1], (L, LANES)))
        kw = kp * ws
        dec_c = jnp.where(col_h0, jnp.broadcast_to(decays[0], (1, 2 * B_V_DIM)),
                          jnp.broadcast_to(decays[1], (1, 2 * B_V_DIM)))
        dec_n = jnp.where(lane_h0, jnp.broadcast_to(decays[0], (1, LANES)), jnp.broadcast_to(decays[1], (1, LANES)))
        c_ref[p] = dec_c * c_old + lax.dot_general(kw.astype(BF16), vpb, (((0,), (0,)), ((), ())),
                                                   preferred_element_type=F32)
        n_ref[p] = dec_n * n_old + jnp.sum(kw, 0, keepdims=True)
        m_ref[p] = jnp.where(lane_h0, jnp.broadcast_to(m_news[0], (1, LANES)), jnp.broadcast_to(m_news[1], (1, LANES)))


def _mlstm(proj, gate_bias, conv_w, conv_b, norm_g, batch, seq):
    L = B_CHUNK
    d = D_MODEL
    slab = lambda c: pl.BlockSpec((None, L, d), lambda b, s, c=c: (b, s, c))
    full = lambda shape: pl.BlockSpec(shape, lambda b, s: (0,) * len(shape))
    gate_block = 3 * d // LANES
    return pl.pallas_call(
        _mlstm_kernel,
        grid=(batch, seq // L),
        in_specs=[slab(0), slab(1), slab(2),
                  pl.BlockSpec((None, L, LANES), lambda b, s: (b, s, gate_block)),
                  full((1, LANES)),
                  full((B_CONV, d)), full((1, d)), full((1, d))],
        out_specs=pl.BlockSpec((None, L, d), lambda b, s: (b, s, 0)),
        out_shape=jax.ShapeDtypeStruct((batch, seq, d), BF16),
        scratch_shapes=[pltpu.VMEM((L + 8, d), F32),
                        pltpu.VMEM((B_HEADS // 2, 2 * B_QK_DIM, 2 * B_V_DIM), F32),
                        pltpu.VMEM((B_HEADS // 2, 1, LANES), F32),
                        pltpu.VMEM((B_HEADS // 2, 1, LANES), F32)],
        compiler_params=_params("parallel", "arbitrary"),
        name="mlstm",
    )(proj, proj, proj, proj, jnp.pad(gate_bias, (0, LANES - 2 * B_HEADS)).reshape(1, LANES),
      conv_w, conv_b.reshape(1, d), norm_g.reshape(1, d))


def _mixer_mlstm(hb, w_in, gate_bias, conv_w, conv_b, norm_g, batch, seq):
    n = batch * seq
    width = 3 * D_MODEL + LANES
    w_pad = jnp.pad(w_in, ((0, 0), (0, width - w_in.shape[1]))).astype(BF16)
    proj = _matmul(hb, w_pad, tm=1024, tn=width // 5)
    out = _mlstm(proj.reshape(batch, seq, width), gate_bias, conv_w, conv_b, norm_g, batch, seq)
    return out.reshape(n, D_MODEL)


def _retention_kernel(lg_ref, q_ref, k_ref, v_ref, g_ref, cos_ref, sin_ref, ng_ref, out_ref,
                      r_ref, dm_ref, xi_ref, zeta_ref):
    L = C_CHUNK
    head = pl.program_id(1)
    chunk = pl.program_id(2)
    lg = lg_ref[head]

    @pl.when(chunk == 0)
    def _():
        r_ref[...] = jnp.zeros(r_ref.shape, F32)
        ti = lax.broadcasted_iota(jnp.int32, (L, L), 0)
        si = lax.broadcasted_iota(jnp.int32, (L, L), 1)
        rel = (ti - si).astype(F32)
        dm_ref[...] = jnp.where(rel >= 0, jnp.exp(jnp.maximum(rel, 0.0) * lg), 0.0)
        idx = lax.broadcasted_iota(jnp.int32, (L, LANES), 0).astype(F32)
        xi_ref[...] = jnp.exp((idx + 1.0) * lg)
        zeta_ref[...] = jnp.exp((L - 1.0 - idx) * lg)

    cos, sin = cos_ref[...], sin_ref[...]
    hd = C_QK_DIM // 2

    def rope(t):
        t1, t2 = t[:, :hd], t[:, hd:]
        return jnp.concatenate([t1 * cos - t2 * sin, t2 * cos + t1 * sin], -1)

    q = rope(q_ref[...])
    k = rope(k_ref[...]) * (C_QK_DIM ** -0.5)
    qb = q.astype(BF16)
    vb = v_ref[...].astype(BF16)
    r_old = r_ref[...]
    sc = lax.dot_general(qb, k.astype(BF16), (((1,), (1,)), ((), ())), preferred_element_type=F32) * dm_ref[...]
    o = jnp.dot(sc.astype(BF16), vb, preferred_element_type=F32)
    o = o + xi_ref[:, 0:1] * jnp.dot(qb, r_old.astype(BF16), preferred_element_type=F32)
    kz = (k * zeta_ref[:, 0:1]).astype(BF16)
    cd = jnp.exp(jnp.full((1, 1), float(L), F32) * lg)
    r_ref[...] = cd * r_old + lax.dot_general(kz, vb, (((0,), (0,)), ((), ())), preferred_element_type=F32)
    mu = jnp.mean(o, -1, keepdims=True)
    oc = o - mu
    var = jnp.mean(oc * oc, -1, keepdims=True)
    on = oc * lax.rsqrt(var + LN_EPS) * ng_ref[...]
    g = g_ref[...]
    out_ref[...] = (on * (g * jax.nn.sigmoid(g))).astype(out_ref.dtype)


def _retention(proj, cos, sin, norm_g, batch, seq):
    L = C_CHUNK
    log_gamma = jnp.log(1.0 - 2.0 ** (-5.0 - jnp.arange(C_HEADS, dtype=F32)))
    qk_blocks = D_MODEL // C_QK_DIM
    v_off = 2 * D_MODEL // C_V_DIM
    g_off = 4 * D_MODEL // C_V_DIM
    return pl.pallas_call(
        _retention_kernel,
        grid=(batch, C_HEADS, seq // L),
        in_specs=[pl.BlockSpec(memory_space=pltpu.SMEM),
                  pl.BlockSpec((None, L, C_QK_DIM), lambda b, h, c: (b, c, h)),
                  pl.BlockSpec((None, L, C_QK_DIM), lambda b, h, c: (b, c, qk_blocks + h)),
                  pl.BlockSpec((None, L, C_V_DIM), lambda b, h, c: (b, c, v_off + h)),
                  pl.BlockSpec((None, L, C_V_DIM), lambda b, h, c: (b, c, g_off + h)),
                  pl.BlockSpec((L, C_QK_DIM // 2), lambda b, h, c: (c, 0)),
                  pl.BlockSpec((L, C_QK_DIM // 2), lambda b, h, c: (c, 0)),
                  pl.BlockSpec((1, C_V_DIM), lambda b, h, c: (0, h))],
        out_specs=pl.BlockSpec((None, L, C_V_DIM), lambda b, h, c: (b, c, h)),
        out_shape=jax.ShapeDtypeStruct((batch, seq, 2 * D_MODEL), BF16),
        scratch_shapes=[pltpu.VMEM((C_QK_DIM, C_V_DIM), F32),
                        pltpu.VMEM((L, L), F32),
                        pltpu.VMEM((L, LANES), F32),
                        pltpu.VMEM((L, LANES), F32)],
        compiler_params=_params("parallel", "parallel", "arbitrary"),
        name="retention",
    )(log_gamma, proj, proj, proj, proj, cos, sin, norm_g.reshape(1, 2 * D_MODEL))


def _mixer_retention(hb, w_in, norm_g, cos, sin, batch, seq):
    proj = _matmul(hb, w_in.astype(BF16), tm=1024, tn=1024)
    out = _retention(proj.reshape(batch, seq, 6 * D_MODEL), cos, sin, norm_g, batch, seq)
    return out.reshape(batch * seq, 2 * D_MODEL)


def _expert_kernel(be_ref, x_ref, wg_ref, wu_ref, wd_ref, y_ref, wgb_ref, wub_ref, wdb_ref):
    j = pl.program_id(0)
    changed = jnp.logical_or(j == 0, be_ref[j] != be_ref[jnp.maximum(j - 1, 0)])

    @pl.when(changed)
    def _():
        wgb_ref[...] = wg_ref[...].astype(BF16)
        wub_ref[...] = wu_ref[...].astype(BF16)
        wdb_ref[...] = wd_ref[...].astype(BF16)

    x = x_ref[...].astype(BF16)
    a = jnp.dot(x, wgb_ref[...], preferred_element_type=F32)
    u = jnp.dot(x, wub_ref[...], preferred_element_type=F32)
    act = (a * jax.nn.sigmoid(a) * u).astype(BF16)
    y_ref[...] = jnp.dot(act, wdb_ref[...], preferred_element_type=F32)


def _experts(blk_e, xb, w_gate, w_up, w_down, layer):
    p, d = xb.shape
    nb = p // MOE_BLOCK
    hid = MOE_HIDDEN
    grid_spec = pltpu.PrefetchScalarGridSpec(
        num_scalar_prefetch=1,
        grid=(nb,),
        in_specs=[pl.BlockSpec((MOE_BLOCK, d), lambda j, be: (j, 0)),
                  pl.BlockSpec((None, None, d, hid), lambda j, be: (layer, be[j], 0, 0)),
                  pl.BlockSpec((None, None, d, hid), lambda j, be: (layer, be[j], 0, 0)),
                  pl.BlockSpec((None, None, hid, d), lambda j, be: (layer, be[j], 0, 0))],
        out_specs=pl.BlockSpec((MOE_BLOCK, d), lambda j, be: (j, 0)),
        scratch_shapes=[pltpu.VMEM((d, hid), BF16), pltpu.VMEM((d, hid), BF16), pltpu.VMEM((hid, d), BF16)],
    )
    return pl.pallas_call(
        _expert_kernel,
        grid_spec=grid_spec,
        out_shape=jax.ShapeDtypeStruct((p, d), F32),
        compiler_params=_params("arbitrary"),
        name="moe_experts",
    )(blk_e, xb, w_gate, w_up, w_down)


def _combine_ln_kernel(h_ref, y0_ref, y1_ref, rt_ref, g_ref, b_ref, o_ref, ob_ref):
    rt = rt_ref[...]
    y = y0_ref[...] * rt[:, R_GATE:R_GATE + 1] + y1_ref[...] * rt[:, R_GATE + 1:R_GATE + 2]
    out = _layer_norm_rows(DN_ALPHA * h_ref[...] + y, g_ref[...], b_ref[...])
    o_ref[...] = out
    ob_ref[...] = out.astype(BF16)


def _combine_ln(h, y0, y1, route, g, b, *, tm):
    n, d = h.shape
    row = pl.BlockSpec((tm, d), lambda i: (i, 0))
    vec = pl.BlockSpec((1, d), lambda i: (0, 0))
    return pl.pallas_call(
        _combine_ln_kernel,
        grid=(n // tm,),
        in_specs=[row, row, row, pl.BlockSpec((tm, LANES), lambda i: (i, 0)), vec, vec],
        out_specs=[row, row],
        out_shape=[jax.ShapeDtypeStruct((n, d), F32), jax.ShapeDtypeStruct((n, d), BF16)],
        compiler_params=_params("parallel"),
        name="moe_combine_ln",
    )(h, y0, y1, route, g.reshape(1, d), b.reshape(1, d))


R_EID, R_GATE, R_RANK = 0, 2, 4


def _router_kernel(h_ref, w_ref, b_ref, route_ref, cnt_ref, base_ref, tri_ref):
    i = pl.program_id(0)
    tm = h_ref.shape[0]

    @pl.when(i == 0)
    def _():
        base_ref[...] = jnp.zeros(base_ref.shape, F32)
        ti = lax.broadcasted_iota(jnp.int32, (tm, tm), 0)
        si = lax.broadcasted_iota(jnp.int32, (tm, tm), 1)
        tri_ref[...] = jnp.where(si < ti, 1.0, 0.0).astype(BF16)

    logits = jnp.dot(h_ref[...], w_ref[...], preferred_element_type=F32,
                     precision=lax.Precision.HIGHEST) + b_ref[...]
    lane = lax.broadcasted_iota(jnp.int32, (tm, LANES), 1).astype(F32)
    neg_inf = -jnp.inf
    big = float(4 * LANES)
    is_g = lane < MOE_GROUPS
    gl = jnp.where(is_g, logits, neg_inf)
    gmax = jnp.max(gl, -1, keepdims=True)
    grp = jnp.min(jnp.where(gl == gmax, lane, big), -1, keepdims=True)
    p_grp = 1.0 / jnp.sum(jnp.where(is_g, jnp.exp(logits - gmax), 0.0), -1, keepdims=True)
    lo = MOE_GROUPS + MOE_PER_GROUP * grp
    el = jnp.where((lane >= lo) & (lane < lo + MOE_PER_GROUP), logits, neg_inf)
    v1 = jnp.max(el, -1, keepdims=True)
    i1 = jnp.min(jnp.where(el == v1, lane, big), -1, keepdims=True)
    el2 = jnp.where(lane == i1, neg_inf, el)
    v2 = jnp.max(el2, -1, keepdims=True)
    i2 = jnp.min(jnp.where(el2 == v2, lane, big), -1, keepdims=True)
    t = jnp.exp(v2 - v1)
    g1 = p_grp / (1.0 + t)
    g2 = g1 * t
    e1 = i1 - MOE_GROUPS
    e2 = i2 - MOE_GROUPS
    oh1 = jnp.where(lane == e1, 1.0, 0.0)
    oh2 = jnp.where(lane == e2, 1.0, 0.0)
    oh = oh1 + oh2
    tot = base_ref[...] + jnp.dot(tri_ref[...], oh.astype(BF16), preferred_element_type=F32)
    r1 = jnp.sum(oh1 * tot, -1, keepdims=True)
    r2 = jnp.sum(oh2 * tot, -1, keepdims=True)
    new_base = base_ref[...] + jnp.sum(oh, 0, keepdims=True)
    base_ref[...] = new_base
    cnt_ref[...] = jnp.broadcast_to(new_base, cnt_ref.shape)
    route = jnp.zeros((tm, LANES), F32)
    for k, val in enumerate((e1, e2, g1, g2, r1, r2)):
        route = jnp.where(lane == float(k), val, route)
    route_ref[...] = route


def _router(h, w_r, b_r, *, tm):
    n, d = h.shape
    return pl.pallas_call(
        _router_kernel,
        grid=(n // tm,),
        in_specs=[pl.BlockSpec((tm, d), lambda i: (i, 0)),
                  pl.BlockSpec((d, LANES), lambda i: (0, 0)),
                  pl.BlockSpec((1, LANES), lambda i: (0, 0))],
        out_specs=[pl.BlockSpec((tm, LANES), lambda i: (i, 0)),
                   pl.BlockSpec((8, LANES), lambda i: (0, 0))],
        out_shape=[jax.ShapeDtypeStruct((n, LANES), F32), jax.ShapeDtypeStruct((8, LANES), F32)],
        scratch_shapes=[pltpu.VMEM((1, LANES), F32), pltpu.VMEM((tm, tm), BF16)],
        compiler_params=_params("arbitrary"),
        name="moe_router",
    )(h, w_r, b_r)


def _moe(h, wg_r, bg_r, we_r, be_r, w_gate, w_up, w_down, layer, ln_g, ln_b):
    n, d = h.shape
    pad = LANES - MOE_GROUPS - MOE_EXPERTS
    w_r = jnp.pad(jnp.concatenate([wg_r, we_r], 1), ((0, 0), (0, pad)))
    b_r = jnp.pad(jnp.concatenate([bg_r, be_r]), (0, pad)).reshape(1, LANES)
    route, cnt = _router(h, w_r, b_r, tm=512)
    counts = cnt[0, :MOE_EXPERTS].astype(jnp.int32)
    padded = (counts + MOE_BLOCK - 1) // MOE_BLOCK * MOE_BLOCK
    pends = jnp.cumsum(padded)
    pstarts = pends - padded
    eid = route[:, R_EID:R_EID + MOE_TOPK].astype(jnp.int32)
    rank = route[:, R_RANK:R_RANK + MOE_TOPK].astype(jnp.int32)
    experts = jnp.arange(MOE_EXPERTS, dtype=jnp.int32)
    dest = jnp.sum(jnp.where(eid[..., None] == experts, pstarts, 0), -1) + rank
    p_rows = n * MOE_TOPK + MOE_EXPERTS * MOE_BLOCK
    nb = p_rows // MOE_BLOCK
    tok = jnp.broadcast_to(jnp.arange(n, dtype=jnp.int32)[:, None], (n, MOE_TOPK))
    buf_tok = jnp.zeros((p_rows,), jnp.int32).at[dest.reshape(-1)].set(tok.reshape(-1))
    blk_start = jnp.arange(nb, dtype=jnp.int32) * MOE_BLOCK
    blk_e = jnp.minimum(jnp.sum((pends[None, :] <= blk_start[:, None]).astype(jnp.int32), -1), MOE_EXPERTS - 1)
    xb = h[buf_tok]
    yb = _experts(blk_e, xb, w_gate, w_up, w_down, layer)
    return _combine_ln(h, yb[dest[:, 0]], yb[dest[:, 1]], route, ln_g, ln_b, tm=512)


def kernel(x, positions, ln1_g, ln1_b, ln2_g, ln2_b, a_w_in, a_w_out, b_w_in, b_gate_bias, b_conv_w, b_conv_b,
           b_norm_g, b_w_out, c_w_in, c_norm_g, c_w_out, r_group_w, r_group_b, r_expert_w, r_expert_b,
           e_w_gate, e_w_up, e_w_down):
    batch, seq, d = x.shape
    n = batch * seq
    tabs_a = _rope_tables_a(positions)
    inv_c = C_THETA ** (-jnp.arange(0, C_QK_DIM, 2, dtype=F32) / C_QK_DIM)
    ang_c = positions.astype(F32)[:, None] * inv_c[None, :]
    cos_c, sin_c = jnp.cos(ang_c), jnp.sin(ang_c)
    h = x.reshape(n, d)
    hb = h.astype(BF16)
    for i in range(DEPTH):
        kind, j = i % 3, i // 3
        if kind == 0:
            y = _mixer_dilated(hb, a_w_in[j], tabs_a, batch, seq)
            w_out = a_w_out[j]
        elif kind == 1:
            y = _mixer_mlstm(hb, b_w_in[j], b_gate_bias[j], b_conv_w[j], b_conv_b[j], b_norm_g[j], batch, seq)
            w_out = b_w_out[j]
        else:
            y = _mixer_retention(hb, c_w_in[j], c_norm_g[j], cos_c, sin_c, batch, seq)
            w_out = c_w_out[j]
        h, hb = _matmul_res_ln(y, w_out.astype(BF16), h, ln1_g[i], ln1_b[i], tm=512)
        h, hb = _moe(h, r_group_w[i], r_group_b[i], r_expert_w[i], r_expert_b[i],
                     e_w_gate, e_w_up, e_w_down, i, ln2_g[i], ln2_b[i])
    return h.reshape(batch, seq, d)
```

```python
import functools

import jax
import jax.numpy as jnp
from jax import lax
from jax.experimental import pallas as pl
from jax.experimental.pallas import tpu as pltpu

F32 = jnp.float32
BF16 = jnp.bfloat16

D_MODEL = 1024
DEPTH = 4
DN_ALPHA = (2.0 * DEPTH) ** 0.25
LN_EPS = 1e-5

A_HEADS = 16
A_HEAD_DIM = 64
A_DILATIONS = (1, 4, 16)
A_BLOCK = 128
A_UNROLL = 4
A_GROUP = 16
A_PITCH = 20
LOG2_E = 1.4426950408889634
A_ROT_DIM = 16
ROPE_THETA = 500000.0

B_HEADS = 8
B_QK_DIM = 64
B_V_DIM = 128
B_CONV = 4
B_CHUNK = 128

C_HEADS = 4
C_QK_DIM = 256
C_V_DIM = 512
C_CHUNK = 256
C_THETA = 10000.0

MOE_GROUPS = 8
MOE_PER_GROUP = 8
MOE_EXPERTS = 64
MOE_TOPK = 2
MOE_HIDDEN = 256
MOE_BLOCK = 256

LANES = 128
NEG = -1e30
VMEM_LIMIT = 48 * 1024 * 1024


def _params(*sem):
    return pltpu.CompilerParams(dimension_semantics=sem, vmem_limit_bytes=VMEM_LIMIT)


def _mm_kernel(x_ref, w_ref, o_ref):
    o_ref[...] = jnp.dot(x_ref[...], w_ref[...], preferred_element_type=F32).astype(o_ref.dtype)


def _matmul(x, w, *, tm, tn, out_dtype=F32):
    n, k = x.shape
    m = w.shape[1]
    return pl.pallas_call(
        _mm_kernel,
        grid=(n // tm, m // tn),
        in_specs=[pl.BlockSpec((tm, k), lambda i, j: (i, 0)),
                  pl.BlockSpec((k, tn), lambda i, j: (0, j))],
        out_specs=pl.BlockSpec((tm, tn), lambda i, j: (i, j)),
        out_shape=jax.ShapeDtypeStruct((n, m), out_dtype),
        compiler_params=_params("parallel", "parallel"),
        name="matmul",
    )(x, w)


def _layer_norm_rows(z, g, b):
    mu = jnp.mean(z, -1, keepdims=True)
    zc = z - mu
    var = jnp.mean(zc * zc, -1, keepdims=True)
    return zc * lax.rsqrt(var + LN_EPS) * g + b


def _mm_res_ln_kernel(x_ref, w_ref, h_ref, g_ref, b_ref, o_ref, ob_ref):
    y = jnp.dot(x_ref[...], w_ref[...], preferred_element_type=F32)
    out = _layer_norm_rows(DN_ALPHA * h_ref[...] + y, g_ref[...], b_ref[...])
    o_ref[...] = out
    ob_ref[...] = out.astype(BF16)


def _matmul_res_ln(x, w, h, g, b, *, tm):
    n, k = x.shape
    d = w.shape[1]
    return pl.pallas_call(
        _mm_res_ln_kernel,
        grid=(n // tm,),
        in_specs=[pl.BlockSpec((tm, k), lambda i: (i, 0)),
                  pl.BlockSpec((k, d), lambda i: (0, 0)),
                  pl.BlockSpec((tm, d), lambda i: (i, 0)),
                  pl.BlockSpec((1, d), lambda i: (0, 0)),
                  pl.BlockSpec((1, d), lambda i: (0, 0))],
        out_specs=[pl.BlockSpec((tm, d), lambda i: (i, 0)),
                   pl.BlockSpec((tm, d), lambda i: (i, 0))],
        out_shape=[jax.ShapeDtypeStruct((n, d), F32), jax.ShapeDtypeStruct((n, d), BF16)],
        compiler_params=_params("parallel"),
        name="matmul_res_ln",
    )(x, w, h, g.reshape(1, d), b.reshape(1, d))


def _proj_a_kernel(x_ref, w_ref, c_ref, s1_ref, s2_ref, q_ref, k_ref, v_ref):
    x = x_ref[...]
    width = 2 * LANES
    for c, ref in ((0, q_ref), (1, k_ref), (2, v_ref)):
        for j in range(D_MODEL // width):
            col = c * D_MODEL + j * width
            y = jnp.dot(x, w_ref[:, col:col + width], preferred_element_type=F32)
            if c < 2:
                half = A_ROT_DIM // 2
                y = (y * c_ref[...] + pltpu.roll(y, width - half, 1) * s1_ref[...]
                     + pltpu.roll(y, half, 1) * s2_ref[...])
            if c == 0:
                y = y * (LOG2_E * A_HEAD_DIM ** -0.5)
            ref[0, 2 * j] = y[:, :LANES]
            ref[0, 2 * j + 1] = y[:, LANES:]


def _proj_a(xb, w, tabs, batch, seq, *, tm):
    n, d = xb.shape
    spb = seq // tm
    hp = D_MODEL // LANES
    qkv_shape = jax.ShapeDtypeStruct((batch, hp, seq, LANES), F32)
    out_spec = pl.BlockSpec((1, hp, tm, LANES), lambda i: (i // spb, 0, i % spb, 0))
    tab_spec = pl.BlockSpec((tm, 2 * LANES), lambda i: (i % spb, 0))
    return pl.pallas_call(
        _proj_a_kernel,
        grid=(n // tm,),
        in_specs=[pl.BlockSpec((tm, d), lambda i: (i, 0)),
                  pl.BlockSpec((d, 3 * d), lambda i: (0, 0)),
                  tab_spec, tab_spec, tab_spec],
        out_specs=[out_spec, out_spec, out_spec],
        out_shape=[qkv_shape, qkv_shape, qkv_shape],
        compiler_params=_params("parallel"),
        name="proj_a",
    )(xb, w, *tabs)


def _rope_tables_a(positions):
    half = A_ROT_DIM // 2
    inv = ROPE_THETA ** (-jnp.arange(0, A_ROT_DIM, 2, dtype=F32) / A_ROT_DIM)
    ang = positions.astype(F32)[:, None] * inv[None, :]
    cos, sin = jnp.cos(ang), jnp.sin(ang)
    s = positions.shape[0]
    pad = jnp.zeros((s, A_HEAD_DIM - A_ROT_DIM), F32)
    c_head = jnp.concatenate([cos, cos, pad + 1.0], -1)
    s1_head = jnp.concatenate([-sin, jnp.zeros_like(sin), pad], -1)
    s2_head = jnp.concatenate([jnp.zeros_like(sin), sin, pad], -1)
    reps = 2 * LANES // A_HEAD_DIM
    return tuple(jnp.tile(t, (1, reps)) for t in (c_head, s1_head, s2_head))


def _attn_blocks(q_ref, k_ref, v_ref, o_ref, lse_ref, bias_ref, hmask, head0, first_block, d, pitch):
    nk = 2 * A_BLOCK
    loaded = []
    for u in range(A_UNROLL):
        g = first_block + u
        r = g % d
        n = g // d
        qstart = n * (A_BLOCK * pitch) + r
        kstart = jnp.maximum(qstart - A_BLOCK * pitch, r)
        if pitch == 1:
            qstart = pl.multiple_of(qstart, A_BLOCK)
            kstart = pl.multiple_of(kstart, A_BLOCK)
            qsl, ksl = pl.ds(qstart, A_BLOCK), pl.ds(kstart, nk)
        else:
            qsl, ksl = pl.ds(qstart, A_BLOCK, stride=pitch), pl.ds(kstart, nk, stride=pitch)
        bias = bias_ref[jnp.minimum(n, 1)]
        loaded.append((qsl, q_ref[qsl, :].astype(BF16), k_ref[ksl, :].astype(BF16), v_ref[ksl, :].astype(BF16), bias))
    results = []
    for qsl, qb, kb, vb, bias in loaded:
        outs, lses = [], []
        for h in range(2):
            s = lax.dot_general(qb * hmask[h], kb, (((1,), (1,)), ((), ())), preferred_element_type=F32) + bias
            m = jnp.max(s, -1, keepdims=True)
            p = jnp.exp2(s - m)
            l = jnp.sum(p, -1, keepdims=True)
            outs.append(jnp.dot(p.astype(BF16), vb, preferred_element_type=F32) * (1.0 / l))
            lses.append(jnp.broadcast_to(m + jnp.log2(l), (A_BLOCK, LANES)))
        results.append((qsl, jnp.where(head0, outs[0], outs[1]), jnp.where(head0, lses[0], lses[1])))
    for qsl, out, lse in results:
        o_ref[qsl, :] = out
        lse_ref[qsl, :] = lse


def _attn_kernel(q_ref, k_ref, v_ref, o_ref, q16_ref, k16_ref, v16_ref, ob_ref, lb_ref, o16_ref, l16_ref,
                 bias_ref):
    seq = q_ref.shape[0]
    groups = seq // A_GROUP
    head0 = lax.broadcasted_iota(jnp.int32, (A_BLOCK, LANES), 1) < A_HEAD_DIM
    hmask = [jnp.where(head0, 1.0, 0.0).astype(BF16), jnp.where(head0, 0.0, 1.0).astype(BF16)]
    qi = lax.broadcasted_iota(jnp.int32, (A_BLOCK, 2 * A_BLOCK), 0)
    kj = lax.broadcasted_iota(jnp.int32, (A_BLOCK, 2 * A_BLOCK), 1)
    bias_ref[0] = jnp.where(kj <= qi, 0.0, NEG).astype(F32)
    bias_ref[1] = jnp.where((kj >= qi) & (kj <= qi + A_BLOCK), 0.0, NEG).astype(F32)

    def spread(g, carry):
        src = pl.ds(pl.multiple_of(g * A_GROUP, A_GROUP), A_GROUP)
        dst = pl.ds(pl.multiple_of(g * A_PITCH, 4), A_GROUP)
        q16_ref[dst, :] = q_ref[src, :]
        k16_ref[dst, :] = k_ref[src, :]
        v16_ref[dst, :] = v_ref[src, :]
        return carry

    lax.fori_loop(0, groups, spread, 0, unroll=8)

    branches = ((1, 1, q_ref, k_ref, v_ref, ob_ref.at[0], lb_ref.at[0]),
                (4, 4, q_ref, k_ref, v_ref, ob_ref.at[1], lb_ref.at[1]),
                (16, A_PITCH, q16_ref, k16_ref, v16_ref, o16_ref, l16_ref))
    for d, pitch, qr, kr, vr, orf, lrf in branches:

        def body(it, carry, d=d, pitch=pitch, qr=qr, kr=kr, vr=vr, orf=orf, lrf=lrf):
            _attn_blocks(qr, kr, vr, orf, lrf, bias_ref, hmask, head0, it * A_UNROLL, d, pitch)
            return carry

        lax.fori_loop(0, seq // (A_BLOCK * A_UNROLL), body, 0)

    def mix(g, carry):
        nat = pl.ds(pl.multiple_of(g * A_GROUP, A_GROUP), A_GROUP)
        pad = pl.ds(pl.multiple_of(g * A_PITCH, 4), A_GROUP)
        o0, o1, o2 = ob_ref[0, nat, :], ob_ref[1, nat, :], o16_ref[pad, :]
        l0, l1, l2 = lb_ref[0, nat, :], lb_ref[1, nat, :], l16_ref[pad, :]
        mx = jnp.maximum(jnp.maximum(l0, l1), l2)
        w0, w1, w2 = jnp.exp2(l0 - mx), jnp.exp2(l1 - mx), jnp.exp2(l2 - mx)
        o_ref[nat, :] = ((w0 * o0 + w1 * o1 + w2 * o2) / (w0 + w1 + w2)).astype(o_ref.dtype)
        return carry

    lax.fori_loop(0, groups, mix, 0, unroll=8)


def _attention(q, k, v):
    batch, hp, seq, _ = q.shape
    assert seq % (2 * A_BLOCK * max(A_DILATIONS)) == 0 and seq % (A_BLOCK * A_UNROLL) == 0
    in_spec = pl.BlockSpec((None, None, seq, LANES), lambda b, p: (b, p, 0, 0))
    padded = seq // A_GROUP * A_PITCH
    return pl.pallas_call(
        _attn_kernel,
        grid=(batch, hp),
        in_specs=[in_spec, in_spec, in_spec],
        out_specs=pl.BlockSpec((None, seq, LANES), lambda b, p: (b, 0, p)),
        out_shape=jax.ShapeDtypeStruct((batch, seq, hp * LANES), BF16),
        scratch_shapes=[pltpu.VMEM((padded, LANES), F32)] * 3
        + [pltpu.VMEM((2, seq, LANES), F32)] * 2
        + [pltpu.VMEM((padded, LANES), F32)] * 2
        + [pltpu.VMEM((2, A_BLOCK, 2 * A_BLOCK), F32)],
        compiler_params=_params("parallel", "parallel"),
        name="dilated_attention",
    )(q, k, v)


def _mixer_dilated(hb, w_in, tabs, batch, seq):
    q, k, v = _proj_a(hb, w_in.astype(BF16), tabs, batch, seq, tm=512)
    o = _attention(q, k, v)
    return o.reshape(batch * seq, D_MODEL)


def _mlstm_kernel(qk_ref, v_ref, o_ref, gc_ref, gb_ref, cw_ref, cb_ref, ng_ref, out_ref,
                  ext_ref, c_ref, n_ref, m_ref):
    L = B_CHUNK
    chunk = pl.program_id(1)

    @pl.when(chunk == 0)
    def _():
        ext_ref[0:8, :] = jnp.zeros((8, D_MODEL), F32)
        c_ref[...] = jnp.zeros(c_ref.shape, F32)
        n_ref[...] = jnp.zeros(n_ref.shape, F32)
        m_ref[...] = jnp.zeros(m_ref.shape, F32)

    u = qk_ref[...]
    ext_ref[8:8 + L, :] = u
    conv = u * cw_ref[B_CONV - 1:B_CONV, :] + cb_ref[...]
    for j in range(1, B_CONV):
        conv = conv + ext_ref[pl.ds(8 - j, L), :] * cw_ref[B_CONV - 1 - j:B_CONV - j, :]
    ext_ref[0:8, :] = u[L - 8:, :]
    qk = conv * jax.nn.sigmoid(conv)
    half = D_MODEL // 2

    gc = gc_ref[...] + gb_ref[...]
    gr = gc.T
    i_col, i_row = gc, gr[:B_HEADS, :]
    lf_col = jax.nn.log_sigmoid(gc)
    lf_row = jax.nn.log_sigmoid(gr[B_HEADS:2 * B_HEADS, :])
    ti = lax.broadcasted_iota(jnp.int32, (L, L), 0)
    si = lax.broadcasted_iota(jnp.int32, (L, L), 1)
    causal = ti >= si
    tri = causal.astype(F32)
    a_col = jnp.dot(tri, lf_col, preferred_element_type=F32, precision=lax.Precision.HIGHEST)
    a_row = lax.dot_general(lf_row, tri, (((1,), (1,)), ((), ())), preferred_element_type=F32,
                            precision=lax.Precision.HIGHEST)
    lane = lax.broadcasted_iota(jnp.int32, (1, LANES), 1)
    lane_h0 = lane < B_QK_DIM
    col_h0 = lax.broadcasted_iota(jnp.int32, (1, 2 * B_V_DIM), 1) < B_V_DIM

    for p in range(B_HEADS // 2):
        qp = qk[:, p * LANES:(p + 1) * LANES]
        kp = qk[:, half + p * LANES:half + (p + 1) * LANES] * (B_QK_DIM ** -0.5)
        vp = v_ref[:, p * 2 * B_V_DIM:(p + 1) * 2 * B_V_DIM]
        kpb = kp.astype(BF16)
        vpb = vp.astype(BF16)
        c_old = c_ref[p]
        c_oldb = c_old.astype(BF16)
        n_old = n_ref[p]
        m_pair = m_ref[p]
        ws_cols, decays, m_news = [], [], []
        for hh in range(2):
            h = 2 * p + hh
            m_old = m_pair[:, hh * B_QK_DIM:hh * B_QK_DIM + 1]
            ac, ar = a_col[:, B_HEADS + h:B_HEADS + h + 1], a_row[h:h + 1, :]
            ic, ir = i_col[:, h:h + 1], i_row[h:h + 1, :]
            dmat = jnp.where(causal, ac - ar + ir, NEG)
            inter = ac + m_old
            m_t = jnp.maximum(inter, jnp.max(dmat, -1, keepdims=True))
            qm = jnp.where(lane_h0 if hh == 0 else jnp.logical_not(lane_h0), qp, 0.0)
            qmb = qm.astype(BF16)
            sc = lax.dot_general(qmb, kpb, (((1,), (1,)), ((), ())), preferred_element_type=F32)
            sc = sc * jnp.exp(dmat - m_t)
            g_inter = jnp.exp(inter - m_t)
            vh = vpb[:, hh * B_V_DIM:(hh + 1) * B_V_DIM]
            qc = jnp.dot(qmb, c_oldb, preferred_element_type=F32)[:, hh * B_V_DIM:(hh + 1) * B_V_DIM]
            num = jnp.dot(sc.astype(BF16), vh, preferred_element_type=F32) + g_inter * qc
            den = jnp.sum(sc, -1, keepdims=True) + g_inter * jnp.sum(qm * n_old, -1, keepdims=True)
            h_out = num / jnp.maximum(jnp.abs(den), jnp.exp(-m_t))
            mu = jnp.mean(h_out, -1, keepdims=True)
            hc = h_out - mu
            var = jnp.mean(hc * hc, -1, keepdims=True)
            cols = slice(h * B_V_DIM, (h + 1) * B_V_DIM)
            hn = hc * lax.rsqrt(var + LN_EPS) * ng_ref[:, cols]
            out_ref[:, cols] = (hn * jax.nn.sigmoid(o_ref[:, cols])).astype(out_ref.dtype)
            a_end = ac[L - 1:L, :]
            w_col = a_end - ac + ic
            m_new = jnp.maximum(a_end + m_old, jnp.max(w_col, 0, keepdims=True))
            decays.append(jnp.exp(a_end + m_old - m_new))
            ws_cols.append(jnp.exp(w_col - m_new))
            m_news.append(m_new)
        ws = jnp.where(lane_h0, jnp.broadcast_to(ws_cols[0], (L, LANES)), jnp.broadcast_to(ws_cols[1], (L, LANES)))
        kw = kp * ws
        dec_c = jnp.where(col_h0, jnp.broadcast_to(decays[0], (1, 2 * B_V_DIM)),
                          jnp.broadcast_to(decays[1], (1, 2 * B_V_DIM)))
        dec_n = jnp.where(lane_h0, jnp.broadcast_to(decays[0], (1, LANES)), jnp.broadcast_to(decays[1], (1, LANES)))
        c_ref[p] = dec_c * c_old + lax.dot_general(kw.astype(BF16), vpb, (((0,), (0,)), ((), ())),
                                                   preferred_element_type=F32)
        n_ref[p] = dec_n * n_old + jnp.sum(kw, 0, keepdims=True)
        m_ref[p] = jnp.where(lane_h0, jnp.broadcast_to(m_news[0], (1, LANES)), jnp.broadcast_to(m_news[1], (1, LANES)))


def _mlstm(proj, gate_bias, conv_w, conv_b, norm_g, batch, seq):
    L = B_CHUNK
    d = D_MODEL
    slab = lambda c: pl.BlockSpec((None, L, d), lambda b, s, c=c: (b, s, c))
    full = lambda shape: pl.BlockSpec(shape, lambda b, s: (0,) * len(shape))
    gate_block = 3 * d // LANES
    return pl.pallas_call(
        _mlstm_kernel,
        grid=(batch, seq // L),
        in_specs=[slab(0), slab(1), slab(2),
                  pl.BlockSpec((None, L, LANES), lambda b, s: (b, s, gate_block)),
                  full((1, LANES)),
                  full((B_CONV, d)), full((1, d)), full((1, d))],
        out_specs=pl.BlockSpec((None, L, d), lambda b, s: (b, s, 0)),
        out_shape=jax.ShapeDtypeStruct((batch, seq, d), BF16),
        scratch_shapes=[pltpu.VMEM((L + 8, d), F32),
                        pltpu.VMEM((B_HEADS // 2, 2 * B_QK_DIM, 2 * B_V_DIM), F32),
                        pltpu.VMEM((B_HEADS // 2, 1, LANES), F32),
                        pltpu.VMEM((B_HEADS // 2, 1, LANES), F32)],
        compiler_params=_params("parallel", "arbitrary"),
        name="mlstm",
    )(proj, proj, proj, proj, jnp.pad(gate_bias, (0, LANES - 2 * B_HEADS)).reshape(1, LANES),
      conv_w, conv_b.reshape(1, d), norm_g.reshape(1, d))


def _mixer_mlstm(hb, w_in, gate_bias, conv_w, conv_b, norm_g, batch, seq):
    n = batch * seq
    width = 3 * D_MODEL + LANES
    w_pad = jnp.pad(w_in, ((0, 0), (0, width - w_in.shape[1]))).astype(BF16)
    proj = _matmul(hb, w_pad, tm=1024, tn=width // 5)
    out = _mlstm(proj.reshape(batch, seq, width), gate_bias, conv_w, conv_b, norm_g, batch, seq)
    return out.reshape(n, D_MODEL)


def _retention_kernel(lg_ref, q_ref, k_ref, v_ref, g_ref, cos_ref, sin_ref, ng_ref, out_ref,
                      r_ref, dm_ref, xi_ref, zeta_ref):
    L = C_CHUNK
    head = pl.program_id(1)
    chunk = pl.program_id(2)
    lg = lg_ref[head]

    @pl.when(chunk == 0)
    def _():
        r_ref[...] = jnp.zeros(r_ref.shape, F32)
        ti = lax.broadcasted_iota(jnp.int32, (L, L), 0)
        si = lax.broadcasted_iota(jnp.int32, (L, L), 1)
        rel = (ti - si).astype(F32)
        dm_ref[...] = jnp.where(rel >= 0, jnp.exp(jnp.maximum(rel, 0.0) * lg), 0.0)
        idx = lax.broadcasted_iota(jnp.int32, (L, LANES), 0).astype(F32)
        xi_ref[...] = jnp.exp((idx + 1.0) * lg)
        zeta_ref[...] = jnp.exp((L - 1.0 - idx) * lg)

    cos, sin = cos_ref[...], sin_ref[...]
    hd = C_QK_DIM // 2

    def rope(t):
        t1, t2 = t[:, :hd], t[:, hd:]
        return jnp.concatenate([t1 * cos - t2 * sin, t2 * cos + t1 * sin], -1)

    q = rope(q_ref[...])
    k = rope(k_ref[...]) * (C_QK_DIM ** -0.5)
    qb = q.astype(BF16)
    vb = v_ref[...].astype(BF16)
    r_old = r_ref[...]
    sc = lax.dot_general(qb, k.astype(BF16), (((1,), (1,)), ((), ())), preferred_element_type=F32) * dm_ref[...]
    o = jnp.dot(sc.astype(BF16), vb, preferred_element_type=F32)
    o = o + xi_ref[:, 0:1] * jnp.dot(qb, r_old.astype(BF16), preferred_element_type=F32)
    kz = (k * zeta_ref[:, 0:1]).astype(BF16)
    cd = jnp.exp(jnp.full((1, 1), float(L), F32) * lg)
    r_ref[...] = cd * r_old + lax.dot_general(kz, vb, (((0,), (0,)), ((), ())), preferred_element_type=F32)
    mu = jnp.mean(o, -1, keepdims=True)
    oc = o - mu
    var = jnp.mean(oc * oc, -1, keepdims=True)
    on = oc * lax.rsqrt(var + LN_EPS) * ng_ref[...]
    g = g_ref[...]
    out_ref[...] = (on * (g * jax.nn.sigmoid(g))).astype(out_ref.dtype)


def _retention(proj, cos, sin, norm_g, batch, seq):
    L = C_CHUNK
    log_gamma = jnp.log(1.0 - 2.0 ** (-5.0 - jnp.arange(C_HEADS, dtype=F32)))
    qk_blocks = D_MODEL // C_QK_DIM
    v_off = 2 * D_MODEL // C_V_DIM
    g_off = 4 * D_MODEL // C_V_DIM
    return pl.pallas_call(
        _retention_kernel,
        grid=(batch, C_HEADS, seq // L),
        in_specs=[pl.BlockSpec(memory_space=pltpu.SMEM),
                  pl.BlockSpec((None, L, C_QK_DIM), lambda b, h, c: (b, c, h)),
                  pl.BlockSpec((None, L, C_QK_DIM), lambda b, h, c: (b, c, qk_blocks + h)),
                  pl.BlockSpec((None, L, C_V_DIM), lambda b, h, c: (b, c, v_off + h)),
                  pl.BlockSpec((None, L, C_V_DIM), lambda b, h, c: (b, c, g_off + h)),
                  pl.BlockSpec((L, C_QK_DIM // 2), lambda b, h, c: (c, 0)),
                  pl.BlockSpec((L, C_QK_DIM // 2), lambda b, h, c: (c, 0)),
                  pl.BlockSpec((1, C_V_DIM), lambda b, h, c: (0, h))],
        out_specs=pl.BlockSpec((None, L, C_V_DIM), lambda b, h, c: (b, c, h)),
        out_shape=jax.ShapeDtypeStruct((batch, seq, 2 * D_MODEL), BF16),
        scratch_shapes=[pltpu.VMEM((C_QK_DIM, C_V_DIM), F32),
                        pltpu.VMEM((L, L), F32),
                        pltpu.VMEM((L, LANES), F32),
                        pltpu.VMEM((L, LANES), F32)],
        compiler_params=_params("parallel", "parallel", "arbitrary"),
        name="retention",
    )(log_gamma, proj, proj, proj, proj, cos, sin, norm_g.reshape(1, 2 * D_MODEL))


def _mixer_retention(hb, w_in, norm_g, cos, sin, batch, seq):
    proj = _matmul(hb, w_in.astype(BF16), tm=1024, tn=1024)
    out = _retention(proj.reshape(batch, seq, 6 * D_MODEL), cos, sin, norm_g, batch, seq)
    return out.reshape(batch * seq, 2 * D_MODEL)


def _expert_kernel(be_ref, x_ref, wg_ref, wu_ref, wd_ref, y_ref, wgb_ref, wub_ref, wdb_ref):
    j = pl.program_id(0)
    changed = jnp.logical_or(j == 0, be_ref[j] != be_ref[jnp.maximum(j - 1, 0)])

    @pl.when(changed)
    def _():
        wgb_ref[...] = wg_ref[...].astype(BF16)
        wub_ref[...] = wu_ref[...].astype(BF16)
        wdb_ref[...] = wd_ref[...].astype(BF16)

    x = x_ref[...].astype(BF16)
    a = jnp.dot(x, wgb_ref[...], preferred_element_type=F32)
    u = jnp.dot(x, wub_ref[...], preferred_element_type=F32)
    act = (a * jax.nn.sigmoid(a) * u).astype(BF16)
    y_ref[...] = jnp.dot(act, wdb_ref[...], preferred_element_type=F32)


def _experts(blk_e, xb, w_gate, w_up, w_down, layer):
    p, d = xb.shape
    nb = p // MOE_BLOCK
    hid = MOE_HIDDEN
    grid_spec = pltpu.PrefetchScalarGridSpec(
        num_scalar_prefetch=1,
        grid=(nb,),
        in_specs=[pl.BlockSpec((MOE_BLOCK, d), lambda j, be: (j, 0)),
                  pl.BlockSpec((None, None, d, hid), lambda j, be: (layer, be[j], 0, 0)),
                  pl.BlockSpec((None, None, d, hid), lambda j, be: (layer, be[j], 0, 0)),
                  pl.BlockSpec((None, None, hid, d), lambda j, be: (layer, be[j], 0, 0))],
        out_specs=pl.BlockSpec((MOE_BLOCK, d), lambda j, be: (j, 0)),
        scratch_shapes=[pltpu.VMEM((d, hid), BF16), pltpu.VMEM((d, hid), BF16), pltpu.VMEM((hid, d), BF16)],
    )
    return pl.pallas_call(
        _expert_kernel,
        grid_spec=grid_spec,
        out_shape=jax.ShapeDtypeStruct((p, d), F32),
        compiler_params=_params("arbitrary"),
        name="moe_experts",
    )(blk_e, xb, w_gate, w_up, w_down)


def _combine_ln_kernel(h_ref, y0_ref, y1_ref, rt_ref, g_ref, b_ref, o_ref, ob_ref):
    rt = rt_ref[...]
    y = y0_ref[...] * rt[:, R_GATE:R_GATE + 1] + y1_ref[...] * rt[:, R_GATE + 1:R_GATE + 2]
    out = _layer_norm_rows(DN_ALPHA * h_ref[...] + y, g_ref[...], b_ref[...])
    o_ref[...] = out
    ob_ref[...] = out.astype(BF16)


def _combine_ln(h, y0, y1, route, g, b, *, tm):
    n, d = h.shape
    row = pl.BlockSpec((tm, d), lambda i: (i, 0))
    vec = pl.BlockSpec((1, d), lambda i: (0, 0))
    return pl.pallas_call(
        _combine_ln_kernel,
        grid=(n // tm,),
        in_specs=[row, row, row, pl.BlockSpec((tm, LANES), lambda i: (i, 0)), vec, vec],
        out_specs=[row, row],
        out_shape=[jax.ShapeDtypeStruct((n, d), F32), jax.ShapeDtypeStruct((n, d), BF16)],
        compiler_params=_params("parallel"),
        name="moe_combine_ln",
    )(h, y0, y1, route, g.reshape(1, d), b.reshape(1, d))


R_EID, R_GATE, R_RANK = 0, 2, 4


def _router_kernel(h_ref, w_ref, b_ref, route_ref, cnt_ref, base_ref, tri_ref):
    i = pl.program_id(0)
    tm = h_ref.shape[0]

    @pl.when(i == 0)
    def _():
        base_ref[...] = jnp.zeros(base_ref.shape, F32)
        ti = lax.broadcasted_iota(jnp.int32, (tm, tm), 0)
        si = lax.broadcasted_iota(jnp.int32, (tm, tm), 1)
        tri_ref[...] = jnp.where(si < ti, 1.0, 0.0).astype(BF16)

    logits = jnp.dot(h_ref[...], w_ref[...], preferred_element_type=F32,
                     precision=lax.Precision.HIGHEST) + b_ref[...]
    lane = lax.broadcasted_iota(jnp.int32, (tm, LANES), 1).astype(F32)
    neg_inf = -jnp.inf
    big = float(4 * LANES)
    is_g = lane < MOE_GROUPS
    gl = jnp.where(is_g, logits, neg_inf)
    gmax = jnp.max(gl, -1, keepdims=True)
    grp = jnp.min(jnp.where(gl == gmax, lane, big), -1, keepdims=True)
    p_grp = 1.0 / jnp.sum(jnp.where(is_g, jnp.exp(logits - gmax), 0.0), -1, keepdims=True)
    lo = MOE_GROUPS + MOE_PER_GROUP * grp
    el = jnp.where((lane >= lo) & (lane < lo + MOE_PER_GROUP), logits, neg_inf)
    v1 = jnp.max(el, -1, keepdims=True)
    i1 = jnp.min(jnp.where(el == v1, lane, big), -1, keepdims=True)
    el2 = jnp.where(lane == i1, neg_inf, el)
    v2 = jnp.max(el2, -1, keepdims=True)
    i2 = jnp.min(jnp.where(el2 == v2, lane, big), -1, keepdims=True)
    t = jnp.exp(v2 - v1)
    g1 = p_grp / (1.0 + t)
    g2 = g1 * t
    e1 = i1 - MOE_GROUPS
    e2 = i2 - MOE_GROUPS
    oh1 = jnp.where(lane == e1, 1.0, 0.0)
    oh2 = jnp.where(lane == e2, 1.0, 0.0)
    oh = oh1 + oh2
    tot = base_ref[...] + jnp.dot(tri_ref[...], oh.astype(BF16), preferred_element_type=F32)
    r1 = jnp.sum(oh1 * tot, -1, keepdims=True)
    r2 = jnp.sum(oh2 * tot, -1, keepdims=True)
    new_base = base_ref[...] + jnp.sum(oh, 0, keepdims=True)
    base_ref[...] = new_base
    cnt_ref[...] = jnp.broadcast_to(new_base, cnt_ref.shape)
    route = jnp.zeros((tm, LANES), F32)
    for k, val in enumerate((e1, e2, g1, g2, r1, r2)):
        route = jnp.where(lane == float(k), val, route)
    route_ref[...] = route


def _router(h, w_r, b_r, *, tm):
    n, d = h.shape
    return pl.pallas_call(
        _router_kernel,
        grid=(n // tm,),
        in_specs=[pl.BlockSpec((tm, d), lambda i: (i, 0)),
                  pl.BlockSpec((d, LANES), lambda i: (0, 0)),
                  pl.BlockSpec((1, LANES), lambda i: (0, 0))],
        out_specs=[pl.BlockSpec((tm, LANES), lambda i: (i, 0)),
                   pl.BlockSpec((8, LANES), lambda i: (0, 0))],
        out_shape=[jax.ShapeDtypeStruct((n, LANES), F32), jax.ShapeDtypeStruct((8, LANES), F32)],
        scratch_shapes=[pltpu.VMEM((1, LANES), F32), pltpu.VMEM((tm, tm), BF16)],
        compiler_params=_params("arbitrary"),
        name="moe_router",
    )(h, w_r, b_r)


def _moe(h, hb, wg_r, bg_r, we_r, be_r, w_gate, w_up, w_down, layer, ln_g, ln_b):
    n, d = h.shape
    pad = LANES - MOE_GROUPS - MOE_EXPERTS
    w_r = jnp.pad(jnp.concatenate([wg_r, we_r], 1), ((0, 0), (0, pad)))
    b_r = jnp.pad(jnp.concatenate([bg_r, be_r]), (0, pad)).reshape(1, LANES)
    route, cnt = _router(h, w_r, b_r, tm=512)
    counts = cnt[0, :MOE_EXPERTS].astype(jnp.int32)
    padded = (counts + MOE_BLOCK - 1) // MOE_BLOCK * MOE_BLOCK
    pends = jnp.cumsum(padded)
    pstarts = pends - padded
    eid = route[:, R_EID:R_EID + MOE_TOPK].astype(jnp.int32)
    rank = route[:, R_RANK:R_RANK + MOE_TOPK].astype(jnp.int32)
    experts = jnp.arange(MOE_EXPERTS, dtype=jnp.int32)
    dest = jnp.sum(jnp.where(eid[..., None] == experts, pstarts, 0), -1) + rank
    p_rows = n * MOE_TOPK + MOE_EXPERTS * MOE_BLOCK
    nb = p_rows // MOE_BLOCK
    tok = jnp.broadcast_to(jnp.arange(n, dtype=jnp.int32)[:, None], (n, MOE_TOPK))
    buf_tok = (jnp.arange(p_rows, dtype=jnp.int32) % n).at[dest.reshape(-1)].set(tok.reshape(-1))
    blk_start = jnp.arange(nb, dtype=jnp.int32) * MOE_BLOCK
    blk_e = jnp.minimum(jnp.sum((pends[None, :] <= blk_start[:, None]).astype(jnp.int32), -1), MOE_EXPERTS - 1)
    xb = hb[buf_tok]
    yb = _experts(blk_e, xb, w_gate, w_up, w_down, layer)
    return _combine_ln(h, yb[dest[:, 0]], yb[dest[:, 1]], route, ln_g, ln_b, tm=512)


def kernel(x, positions, ln1_g, ln1_b, ln2_g, ln2_b, a_w_in, a_w_out, b_w_in, b_gate_bias, b_conv_w, b_conv_b,
           b_norm_g, b_w_out, c_w_in, c_norm_g, c_w_out, r_group_w, r_group_b, r_expert_w, r_expert_b,
           e_w_gate, e_w_up, e_w_down):
    batch, seq, d = x.shape
    n = batch * seq
    tabs_a = _rope_tables_a(positions)
    inv_c = C_THETA ** (-jnp.arange(0, C_QK_DIM, 2, dtype=F32) / C_QK_DIM)
    ang_c = positions.astype(F32)[:, None] * inv_c[None, :]
    cos_c, sin_c = jnp.cos(ang_c), jnp.sin(ang_c)
    h = x.reshape(n, d)
    hb = h.astype(BF16)
    for i in range(DEPTH):
        kind, j = i % 3, i // 3
        if kind == 0:
            y = _mixer_dilated(hb, a_w_in[j], tabs_a, batch, seq)
            w_out = a_w_out[j]
        elif kind == 1:
            y = _mixer_mlstm(hb, b_w_in[j], b_gate_bias[j], b_conv_w[j], b_conv_b[j], b_norm_g[j], batch, seq)
            w_out = b_w_out[j]
        else:
            y = _mixer_retention(hb, c_w_in[j], c_norm_g[j], cos_c, sin_c, batch, seq)
            w_out = c_w_out[j]
        h, hb = _matmul_res_ln(y, w_out.astype(BF16), h, ln1_g[i], ln1_b[i], tm=512)
        h, hb = _moe(h, hb, r_group_w[i], r_group_b[i], r_expert_w[i], r_expert_b[i],
                     e_w_gate, e_w_up, e_w_down, i, ln2_g[i], ln2_b[i])
    return h.reshape(batch, seq, d)
```

```python
import functools

import jax
import jax.numpy as jnp
from jax import lax
from jax.experimental import pallas as pl
from jax.experimental.pallas import tpu as pltpu

F32 = jnp.float32
BF16 = jnp.bfloat16

D_MODEL = 1024
DEPTH = 4
DN_ALPHA = (2.0 * DEPTH) ** 0.25
LN_EPS = 1e-5

A_HEADS = 16
A_HEAD_DIM = 64
A_DILATIONS = (1, 4, 16)
A_BLOCK = 128
A_UNROLL = 8
A_GROUP = 16
A_PITCH = 20
LOG2_E = 1.4426950408889634
A_ROT_DIM = 16
ROPE_THETA = 500000.0

B_HEADS = 8
B_QK_DIM = 64
B_V_DIM = 128
B_CONV = 4
B_CHUNK = 256

C_HEADS = 4
C_QK_DIM = 256
C_V_DIM = 512
C_CHUNK = 256
C_THETA = 10000.0

MOE_GROUPS = 8
MOE_PER_GROUP = 8
MOE_EXPERTS = 64
MOE_TOPK = 2
MOE_HIDDEN = 256
MOE_BLOCK = 256

LANES = 128
NEG = -1e30
VMEM_LIMIT = 48 * 1024 * 1024


def _params(*sem):
    return pltpu.CompilerParams(dimension_semantics=sem, vmem_limit_bytes=VMEM_LIMIT)


def _mm_kernel(x_ref, w_ref, o_ref):
    o_ref[...] = jnp.dot(x_ref[...], w_ref[...], preferred_element_type=F32).astype(o_ref.dtype)


def _matmul(x, w, *, tm, tn, out_dtype=F32):
    n, k = x.shape
    m = w.shape[1]
    return pl.pallas_call(
        _mm_kernel,
        grid=(n // tm, m // tn),
        in_specs=[pl.BlockSpec((tm, k), lambda i, j: (i, 0)),
                  pl.BlockSpec((k, tn), lambda i, j: (0, j))],
        out_specs=pl.BlockSpec((tm, tn), lambda i, j: (i, j)),
        out_shape=jax.ShapeDtypeStruct((n, m), out_dtype),
        compiler_params=_params("parallel", "parallel"),
        name="matmul",
    )(x, w)


def _layer_norm_rows(z, g, b):
    mu = jnp.mean(z, -1, keepdims=True)
    zc = z - mu
    var = jnp.mean(zc * zc, -1, keepdims=True)
    return zc * lax.rsqrt(var + LN_EPS) * g + b


def _mm_res_ln_kernel(x_ref, w_ref, h_ref, g_ref, b_ref, o_ref, ob_ref):
    y = jnp.dot(x_ref[...], w_ref[...], preferred_element_type=F32)
    out = _layer_norm_rows(DN_ALPHA * h_ref[...] + y, g_ref[...], b_ref[...])
    o_ref[...] = out
    ob_ref[...] = out.astype(BF16)


def _matmul_res_ln(x, w, h, g, b, *, tm):
    n, k = x.shape
    d = w.shape[1]
    return pl.pallas_call(
        _mm_res_ln_kernel,
        grid=(n // tm,),
        in_specs=[pl.BlockSpec((tm, k), lambda i: (i, 0)),
                  pl.BlockSpec((k, d), lambda i: (0, 0)),
                  pl.BlockSpec((tm, d), lambda i: (i, 0)),
                  pl.BlockSpec((1, d), lambda i: (0, 0)),
                  pl.BlockSpec((1, d), lambda i: (0, 0))],
        out_specs=[pl.BlockSpec((tm, d), lambda i: (i, 0)),
                   pl.BlockSpec((tm, d), lambda i: (i, 0))],
        out_shape=[jax.ShapeDtypeStruct((n, d), F32), jax.ShapeDtypeStruct((n, d), BF16)],
        compiler_params=_params("parallel"),
        name="matmul_res_ln",
    )(x, w, h, g.reshape(1, d), b.reshape(1, d))


def _proj_a_kernel(x_ref, w_ref, c_ref, s1_ref, s2_ref, q_ref, k_ref, v_ref):
    x = x_ref[...]
    width = 2 * LANES
    for c, ref in ((0, q_ref), (1, k_ref), (2, v_ref)):
        for j in range(D_MODEL // width):
            col = c * D_MODEL + j * width
            y = jnp.dot(x, w_ref[:, col:col + width], preferred_element_type=F32)
            if c < 2:
                half = A_ROT_DIM // 2
                y = (y * c_ref[...] + pltpu.roll(y, width - half, 1) * s1_ref[...]
                     + pltpu.roll(y, half, 1) * s2_ref[...])
            if c == 0:
                y = y * (LOG2_E * A_HEAD_DIM ** -0.5)
            ref[0, 2 * j] = y[:, :LANES]
            ref[0, 2 * j + 1] = y[:, LANES:]


def _proj_a(xb, w, tabs, batch, seq, *, tm):
    n, d = xb.shape
    spb = seq // tm
    hp = D_MODEL // LANES
    qkv_shape = jax.ShapeDtypeStruct((batch, hp, seq, LANES), F32)
    out_spec = pl.BlockSpec((1, hp, tm, LANES), lambda i: (i // spb, 0, i % spb, 0))
    tab_spec = pl.BlockSpec((tm, 2 * LANES), lambda i: (i % spb, 0))
    return pl.pallas_call(
        _proj_a_kernel,
        grid=(n // tm,),
        in_specs=[pl.BlockSpec((tm, d), lambda i: (i, 0)),
                  pl.BlockSpec((d, 3 * d), lambda i: (0, 0)),
                  tab_spec, tab_spec, tab_spec],
        out_specs=[out_spec, out_spec, out_spec],
        out_shape=[qkv_shape, qkv_shape, qkv_shape],
        compiler_params=_params("parallel"),
        name="proj_a",
    )(xb, w, *tabs)


def _rope_tables_a(positions):
    half = A_ROT_DIM // 2
    inv = ROPE_THETA ** (-jnp.arange(0, A_ROT_DIM, 2, dtype=F32) / A_ROT_DIM)
    ang = positions.astype(F32)[:, None] * inv[None, :]
    cos, sin = jnp.cos(ang), jnp.sin(ang)
    s = positions.shape[0]
    pad = jnp.zeros((s, A_HEAD_DIM - A_ROT_DIM), F32)
    c_head = jnp.concatenate([cos, cos, pad + 1.0], -1)
    s1_head = jnp.concatenate([-sin, jnp.zeros_like(sin), pad], -1)
    s2_head = jnp.concatenate([jnp.zeros_like(sin), sin, pad], -1)
    reps = 2 * LANES // A_HEAD_DIM
    return tuple(jnp.tile(t, (1, reps)) for t in (c_head, s1_head, s2_head))


def _attn_blocks(q_ref, k_ref, v_ref, o_ref, lse_ref, bias_ref, hmask, head0, first_block, d, pitch):
    nk = 2 * A_BLOCK
    loaded = []
    for u in range(A_UNROLL):
        g = first_block + u
        r = g % d
        n = g // d
        qstart = n * (A_BLOCK * pitch) + r
        kstart = jnp.maximum(qstart - A_BLOCK * pitch, r)
        if pitch == 1:
            qstart = pl.multiple_of(qstart, A_BLOCK)
            kstart = pl.multiple_of(kstart, A_BLOCK)
            qsl, ksl = pl.ds(qstart, A_BLOCK), pl.ds(kstart, nk)
        else:
            qsl, ksl = pl.ds(qstart, A_BLOCK, stride=pitch), pl.ds(kstart, nk, stride=pitch)
        bias = bias_ref[jnp.minimum(n, 1)]
        loaded.append((qsl, q_ref[qsl, :].astype(BF16), k_ref[ksl, :].astype(BF16), v_ref[ksl, :].astype(BF16), bias))
    results = []
    for qsl, qb, kb, vb, bias in loaded:
        outs, lses = [], []
        for h in range(2):
            s = lax.dot_general(qb * hmask[h], kb, (((1,), (1,)), ((), ())), preferred_element_type=F32) + bias
            m = jnp.max(s, -1, keepdims=True)
            p = jnp.exp2(s - m)
            l = jnp.sum(p, -1, keepdims=True)
            outs.append(jnp.dot(p.astype(BF16), vb, preferred_element_type=F32) * (1.0 / l))
            lses.append(jnp.broadcast_to(m + jnp.log2(l), (A_BLOCK, LANES)))
        results.append((qsl, jnp.where(head0, outs[0], outs[1]), jnp.where(head0, lses[0], lses[1])))
    for qsl, out, lse in results:
        o_ref[qsl, :] = out
        lse_ref[qsl, :] = lse


def _attn_kernel(q_ref, k_ref, v_ref, o_ref, q16_ref, k16_ref, v16_ref, ob_ref, lb_ref, o16_ref, l16_ref,
                 bias_ref):
    seq = q_ref.shape[0]
    groups = seq // A_GROUP
    head0 = lax.broadcasted_iota(jnp.int32, (A_BLOCK, LANES), 1) < A_HEAD_DIM
    hmask = [jnp.where(head0, 1.0, 0.0).astype(BF16), jnp.where(head0, 0.0, 1.0).astype(BF16)]
    qi = lax.broadcasted_iota(jnp.int32, (A_BLOCK, 2 * A_BLOCK), 0)
    kj = lax.broadcasted_iota(jnp.int32, (A_BLOCK, 2 * A_BLOCK), 1)
    bias_ref[0] = jnp.where(kj <= qi, 0.0, NEG).astype(F32)
    bias_ref[1] = jnp.where((kj >= qi) & (kj <= qi + A_BLOCK), 0.0, NEG).astype(F32)

    def spread(g, carry):
        src = pl.ds(pl.multiple_of(g * A_GROUP, A_GROUP), A_GROUP)
        dst = pl.ds(pl.multiple_of(g * A_PITCH, 4), A_GROUP)
        q16_ref[dst, :] = q_ref[src, :]
        k16_ref[dst, :] = k_ref[src, :]
        v16_ref[dst, :] = v_ref[src, :]
        return carry

    lax.fori_loop(0, groups, spread, 0, unroll=8)

    branches = ((1, 1, q_ref, k_ref, v_ref, ob_ref.at[0], lb_ref.at[0]),
                (4, 4, q_ref, k_ref, v_ref, ob_ref.at[1], lb_ref.at[1]),
                (16, A_PITCH, q16_ref, k16_ref, v16_ref, o16_ref, l16_ref))
    for d, pitch, qr, kr, vr, orf, lrf in branches:

        def body(it, carry, d=d, pitch=pitch, qr=qr, kr=kr, vr=vr, orf=orf, lrf=lrf):
            _attn_blocks(qr, kr, vr, orf, lrf, bias_ref, hmask, head0, it * A_UNROLL, d, pitch)
            return carry

        lax.fori_loop(0, seq // (A_BLOCK * A_UNROLL), body, 0)

    def mix(g, carry):
        nat = pl.ds(pl.multiple_of(g * A_GROUP, A_GROUP), A_GROUP)
        pad = pl.ds(pl.multiple_of(g * A_PITCH, 4), A_GROUP)
        o0, o1, o2 = ob_ref[0, nat, :], ob_ref[1, nat, :], o16_ref[pad, :]
        l0, l1, l2 = lb_ref[0, nat, :], lb_ref[1, nat, :], l16_ref[pad, :]
        mx = jnp.maximum(jnp.maximum(l0, l1), l2)
        w0, w1, w2 = jnp.exp2(l0 - mx), jnp.exp2(l1 - mx), jnp.exp2(l2 - mx)
        o_ref[nat, :] = ((w0 * o0 + w1 * o1 + w2 * o2) / (w0 + w1 + w2)).astype(o_ref.dtype)
        return carry

    lax.fori_loop(0, groups, mix, 0, unroll=8)


def _attention(q, k, v):
    batch, hp, seq, _ = q.shape
    assert seq % (2 * A_BLOCK * max(A_DILATIONS)) == 0 and seq % (A_BLOCK * A_UNROLL) == 0
    in_spec = pl.BlockSpec((None, None, seq, LANES), lambda b, p: (b, p, 0, 0))
    padded = seq // A_GROUP * A_PITCH
    return pl.pallas_call(
        _attn_kernel,
        grid=(batch, hp),
        in_specs=[in_spec, in_spec, in_spec],
        out_specs=pl.BlockSpec((None, seq, LANES), lambda b, p: (b, 0, p)),
        out_shape=jax.ShapeDtypeStruct((batch, seq, hp * LANES), BF16),
        scratch_shapes=[pltpu.VMEM((padded, LANES), F32)] * 3
        + [pltpu.VMEM((2, seq, LANES), F32)] * 2
        + [pltpu.VMEM((padded, LANES), F32)] * 2
        + [pltpu.VMEM((2, A_BLOCK, 2 * A_BLOCK), F32)],
        compiler_params=_params("parallel", "parallel"),
        name="dilated_attention",
    )(q, k, v)


def _mixer_dilated(hb, w_in, tabs, batch, seq):
    q, k, v = _proj_a(hb, w_in.astype(BF16), tabs, batch, seq, tm=512)
    o = _attention(q, k, v)
    return o.reshape(batch * seq, D_MODEL)


def _mlstm_kernel(qk_ref, v_ref, o_ref, gc_ref, gb_ref, cw_ref, cb_ref, ng_ref, out_ref,
                  ext_ref, c_ref, n_ref, m_ref):
    L = B_CHUNK
    chunk = pl.program_id(1)

    @pl.when(chunk == 0)
    def _():
        ext_ref[0:8, :] = jnp.zeros((8, D_MODEL), F32)
        c_ref[...] = jnp.zeros(c_ref.shape, F32)
        n_ref[...] = jnp.zeros(n_ref.shape, F32)
        m_ref[...] = jnp.zeros(m_ref.shape, F32)

    u = qk_ref[...].astype(F32)
    ext_ref[8:8 + L, :] = u
    conv = u * cw_ref[B_CONV - 1:B_CONV, :] + cb_ref[...]
    for j in range(1, B_CONV):
        conv = conv + ext_ref[pl.ds(8 - j, L), :] * cw_ref[B_CONV - 1 - j:B_CONV - j, :]
    ext_ref[0:8, :] = u[L - 8:, :]
    qk = conv * jax.nn.sigmoid(conv)
    half = D_MODEL // 2

    gc = gc_ref[...] + gb_ref[...]
    gr = gc.T
    i_col, i_row = gc, gr[:B_HEADS, :]
    lf_col = jax.nn.log_sigmoid(gc)
    lf_row = jax.nn.log_sigmoid(gr[B_HEADS:2 * B_HEADS, :])
    ti = lax.broadcasted_iota(jnp.int32, (L, L), 0)
    si = lax.broadcasted_iota(jnp.int32, (L, L), 1)
    causal = ti >= si
    tri = causal.astype(F32)
    a_col = jnp.dot(tri, lf_col, preferred_element_type=F32, precision=lax.Precision.HIGHEST)
    a_row = lax.dot_general(lf_row, tri, (((1,), (1,)), ((), ())), preferred_element_type=F32,
                            precision=lax.Precision.HIGHEST)
    lane = lax.broadcasted_iota(jnp.int32, (1, LANES), 1)
    lane_h0 = lane < B_QK_DIM
    col_h0 = lax.broadcasted_iota(jnp.int32, (1, 2 * B_V_DIM), 1) < B_V_DIM

    for p in range(B_HEADS // 2):
        qp = qk[:, p * LANES:(p + 1) * LANES]
        kp = qk[:, half + p * LANES:half + (p + 1) * LANES] * (B_QK_DIM ** -0.5)
        vp = v_ref[:, p * 2 * B_V_DIM:(p + 1) * 2 * B_V_DIM]
        kpb = kp.astype(BF16)
        vpb = vp.astype(BF16)
        c_old = c_ref[p]
        c_oldb = c_old.astype(BF16)
        n_old = n_ref[p]
        m_pair = m_ref[p]
        ws_cols, decays, m_news = [], [], []
        for hh in range(2):
            h = 2 * p + hh
            m_old = m_pair[:, hh * B_QK_DIM:hh * B_QK_DIM + 1]
            ac, ar = a_col[:, B_HEADS + h:B_HEADS + h + 1], a_row[h:h + 1, :]
            ic, ir = i_col[:, h:h + 1], i_row[h:h + 1, :]
            dmat = jnp.where(causal, ac - ar + ir, NEG)
            inter = ac + m_old
            m_t = jnp.maximum(inter, jnp.max(dmat, -1, keepdims=True))
            qm = jnp.where(lane_h0 if hh == 0 else jnp.logical_not(lane_h0), qp, 0.0)
            qmb = qm.astype(BF16)
            sc = lax.dot_general(qmb, kpb, (((1,), (1,)), ((), ())), preferred_element_type=F32)
            sc = sc * jnp.exp(dmat - m_t)
            g_inter = jnp.exp(inter - m_t)
            vh = vpb[:, hh * B_V_DIM:(hh + 1) * B_V_DIM]
            qc = jnp.dot(qmb, c_oldb, preferred_element_type=F32)[:, hh * B_V_DIM:(hh + 1) * B_V_DIM]
            num = jnp.dot(sc.astype(BF16), vh, preferred_element_type=F32) + g_inter * qc
            den = jnp.sum(sc, -1, keepdims=True) + g_inter * jnp.sum(qm * n_old, -1, keepdims=True)
            h_out = num / jnp.maximum(jnp.abs(den), jnp.exp(-m_t))
            mu = jnp.mean(h_out, -1, keepdims=True)
            hc = h_out - mu
            var = jnp.mean(hc * hc, -1, keepdims=True)
            cols = slice(h * B_V_DIM, (h + 1) * B_V_DIM)
            hn = hc * lax.rsqrt(var + LN_EPS) * ng_ref[:, cols]
            out_ref[:, cols] = (hn * jax.nn.sigmoid(o_ref[:, cols].astype(F32))).astype(out_ref.dtype)
            a_end = ac[L - 1:L, :]
            w_col = a_end - ac + ic
            m_new = jnp.maximum(a_end + m_old, jnp.max(w_col, 0, keepdims=True))
            decays.append(jnp.exp(a_end + m_old - m_new))
            ws_cols.append(jnp.exp(w_col - m_new))
            m_news.append(m_new)
        ws = jnp.where(lane_h0, jnp.broadcast_to(ws_cols[0], (L, LANES)), jnp.broadcast_to(ws_cols[1], (L, LANES)))
        kw = kp * ws
        dec_c = jnp.where(col_h0, jnp.broadcast_to(decays[0], (1, 2 * B_V_DIM)),
                          jnp.broadcast_to(decays[1], (1, 2 * B_V_DIM)))
        dec_n = jnp.where(lane_h0, jnp.broadcast_to(decays[0], (1, LANES)), jnp.broadcast_to(decays[1], (1, LANES)))
        c_ref[p] = dec_c * c_old + lax.dot_general(kw.astype(BF16), vpb, (((0,), (0,)), ((), ())),
                                                   preferred_element_type=F32)
        n_ref[p] = dec_n * n_old + jnp.sum(kw, 0, keepdims=True)
        m_ref[p] = jnp.where(lane_h0, jnp.broadcast_to(m_news[0], (1, LANES)), jnp.broadcast_to(m_news[1], (1, LANES)))


def _mlstm(proj, gates, gate_bias, conv_w, conv_b, norm_g, batch, seq):
    L = B_CHUNK
    d = D_MODEL
    slab = lambda c: pl.BlockSpec((None, L, d), lambda b, s, c=c: (b, s, c))
    full = lambda shape: pl.BlockSpec(shape, lambda b, s: (0,) * len(shape))
    return pl.pallas_call(
        _mlstm_kernel,
        grid=(batch, seq // L),
        in_specs=[slab(0), slab(1), slab(2),
                  pl.BlockSpec((None, L, LANES), lambda b, s: (b, s, 0)),
                  full((1, LANES)),
                  full((B_CONV, d)), full((1, d)), full((1, d))],
        out_specs=pl.BlockSpec((None, L, d), lambda b, s: (b, s, 0)),
        out_shape=jax.ShapeDtypeStruct((batch, seq, d), BF16),
        scratch_shapes=[pltpu.VMEM((L + 8, d), F32),
                        pltpu.VMEM((B_HEADS // 2, 2 * B_QK_DIM, 2 * B_V_DIM), F32),
                        pltpu.VMEM((B_HEADS // 2, 1, LANES), F32),
                        pltpu.VMEM((B_HEADS // 2, 1, LANES), F32)],
        compiler_params=_params("parallel", "arbitrary"),
        name="mlstm",
    )(proj, proj, proj, gates, jnp.pad(gate_bias, (0, LANES - 2 * B_HEADS)).reshape(1, LANES),
      conv_w, conv_b.reshape(1, d), norm_g.reshape(1, d))


def _mixer_mlstm(hb, w_in, gate_bias, conv_w, conv_b, norm_g, batch, seq):
    n = batch * seq
    main = 3 * D_MODEL
    proj = _matmul(hb, w_in[:, :main].astype(BF16), tm=1024, tn=1024, out_dtype=BF16)
    w_g = jnp.pad(w_in[:, main:], ((0, 0), (0, LANES - 2 * B_HEADS))).astype(BF16)
    gates = _matmul(hb, w_g, tm=2048, tn=LANES)
    out = _mlstm(proj.reshape(batch, seq, main), gates.reshape(batch, seq, LANES), gate_bias,
                 conv_w, conv_b, norm_g, batch, seq)
    return out.reshape(n, D_MODEL)


def _retention_kernel(lg_ref, q_ref, k_ref, v_ref, g_ref, cos_ref, sin_ref, ng_ref, out_ref,
                      r_ref, dm_ref, xi_ref, zeta_ref):
    L = C_CHUNK
    head = pl.program_id(1)
    chunk = pl.program_id(2)
    lg = lg_ref[head]

    @pl.when(chunk == 0)
    def _():
        r_ref[...] = jnp.zeros(r_ref.shape, F32)
        ti = lax.broadcasted_iota(jnp.int32, (L, L), 0)
        si = lax.broadcasted_iota(jnp.int32, (L, L), 1)
        rel = (ti - si).astype(F32)
        dm_ref[...] = jnp.where(rel >= 0, jnp.exp(jnp.maximum(rel, 0.0) * lg), 0.0)
        idx = lax.broadcasted_iota(jnp.int32, (L, LANES), 0).astype(F32)
        xi_ref[...] = jnp.exp((idx + 1.0) * lg)
        zeta_ref[...] = jnp.exp((L - 1.0 - idx) * lg)

    cos, sin = cos_ref[...], sin_ref[...]
    hd = C_QK_DIM // 2

    def rope(t):
        t1, t2 = t[:, :hd], t[:, hd:]
        return jnp.concatenate([t1 * cos - t2 * sin, t2 * cos + t1 * sin], -1)

    q = rope(q_ref[...].astype(F32))
    k = rope(k_ref[...].astype(F32)) * (C_QK_DIM ** -0.5)
    qb = q.astype(BF16)
    vb = v_ref[...].astype(BF16)
    r_old = r_ref[...]
    sc = lax.dot_general(qb, k.astype(BF16), (((1,), (1,)), ((), ())), preferred_element_type=F32) * dm_ref[...]
    o = jnp.dot(sc.astype(BF16), vb, preferred_element_type=F32)
    o = o + xi_ref[:, 0:1] * jnp.dot(qb, r_old.astype(BF16), preferred_element_type=F32)
    kz = (k * zeta_ref[:, 0:1]).astype(BF16)
    cd = jnp.exp(jnp.full((1, 1), float(L), F32) * lg)
    r_ref[...] = cd * r_old + lax.dot_general(kz, vb, (((0,), (0,)), ((), ())), preferred_element_type=F32)
    mu = jnp.mean(o, -1, keepdims=True)
    oc = o - mu
    var = jnp.mean(oc * oc, -1, keepdims=True)
    on = oc * lax.rsqrt(var + LN_EPS) * ng_ref[...]
    g = g_ref[...].astype(F32)
    out_ref[...] = (on * (g * jax.nn.sigmoid(g))).astype(out_ref.dtype)


def _retention(proj, cos, sin, norm_g, batch, seq):
    L = C_CHUNK
    log_gamma = jnp.log(1.0 - 2.0 ** (-5.0 - jnp.arange(C_HEADS, dtype=F32)))
    qk_blocks = D_MODEL // C_QK_DIM
    v_off = 2 * D_MODEL // C_V_DIM
    g_off = 4 * D_MODEL // C_V_DIM
    return pl.pallas_call(
        _retention_kernel,
        grid=(batch, C_HEADS, seq // L),
        in_specs=[pl.BlockSpec(memory_space=pltpu.SMEM),
                  pl.BlockSpec((None, L, C_QK_DIM), lambda b, h, c: (b, c, h)),
                  pl.BlockSpec((None, L, C_QK_DIM), lambda b, h, c: (b, c, qk_blocks + h)),
                  pl.BlockSpec((None, L, C_V_DIM), lambda b, h, c: (b, c, v_off + h)),
                  pl.BlockSpec((None, L, C_V_DIM), lambda b, h, c: (b, c, g_off + h)),
                  pl.BlockSpec((L, C_QK_DIM // 2), lambda b, h, c: (c, 0)),
                  pl.BlockSpec((L, C_QK_DIM // 2), lambda b, h, c: (c, 0)),
                  pl.BlockSpec((1, C_V_DIM), lambda b, h, c: (0, h))],
        out_specs=pl.BlockSpec((None, L, C_V_DIM), lambda b, h, c: (b, c, h)),
        out_shape=jax.ShapeDtypeStruct((batch, seq, 2 * D_MODEL), BF16),
        scratch_shapes=[pltpu.VMEM((C_QK_DIM, C_V_DIM), F32),
                        pltpu.VMEM((L, L), F32),
                        pltpu.VMEM((L, LANES), F32),
                        pltpu.VMEM((L, LANES), F32)],
        compiler_params=_params("parallel", "parallel", "arbitrary"),
        name="retention",
    )(log_gamma, proj, proj, proj, proj, cos, sin, norm_g.reshape(1, 2 * D_MODEL))


def _mixer_retention(hb, w_in, norm_g, cos, sin, batch, seq):
    proj = _matmul(hb, w_in.astype(BF16), tm=1024, tn=1024, out_dtype=BF16)
    out = _retention(proj.reshape(batch, seq, 6 * D_MODEL), cos, sin, norm_g, batch, seq)
    return out.reshape(batch * seq, 2 * D_MODEL)


def _expert_kernel(be_ref, x_ref, wg_ref, wu_ref, wd_ref, y_ref, wgb_ref, wub_ref, wdb_ref):
    j = pl.program_id(0)
    changed = jnp.logical_or(j == 0, be_ref[j] != be_ref[jnp.maximum(j - 1, 0)])

    @pl.when(changed)
    def _():
        wgb_ref[...] = wg_ref[...].astype(BF16)
        wub_ref[...] = wu_ref[...].astype(BF16)
        wdb_ref[...] = wd_ref[...].astype(BF16)

    x = x_ref[...].astype(BF16)
    a = jnp.dot(x, wgb_ref[...], preferred_element_type=F32)
    u = jnp.dot(x, wub_ref[...], preferred_element_type=F32)
    act = (a * jax.nn.sigmoid(a) * u).astype(BF16)
    y_ref[...] = jnp.dot(act, wdb_ref[...], preferred_element_type=F32).astype(y_ref.dtype)


def _experts(blk_e, xb, w_gate, w_up, w_down, layer):
    p, d = xb.shape
    nb = p // MOE_BLOCK
    hid = MOE_HIDDEN
    grid_spec = pltpu.PrefetchScalarGridSpec(
        num_scalar_prefetch=1,
        grid=(nb,),
        in_specs=[pl.BlockSpec((MOE_BLOCK, d), lambda j, be: (j, 0)),
                  pl.BlockSpec((None, None, d, hid), lambda j, be: (layer, be[j], 0, 0)),
                  pl.BlockSpec((None, None, d, hid), lambda j, be: (layer, be[j], 0, 0)),
                  pl.BlockSpec((None, None, hid, d), lambda j, be: (layer, be[j], 0, 0))],
        out_specs=pl.BlockSpec((MOE_BLOCK, d), lambda j, be: (j, 0)),
        scratch_shapes=[pltpu.VMEM((d, hid), BF16), pltpu.VMEM((d, hid), BF16), pltpu.VMEM((hid, d), BF16)],
    )
    return pl.pallas_call(
        _expert_kernel,
        grid_spec=grid_spec,
        out_shape=jax.ShapeDtypeStruct((p, d), BF16),
        compiler_params=_params("arbitrary"),
        name="moe_experts",
    )(blk_e, xb, w_gate, w_up, w_down)


def _combine_ln_kernel(h_ref, y0_ref, y1_ref, rt_ref, g_ref, b_ref, o_ref, ob_ref):
    rt = rt_ref[...]
    y = (y0_ref[...].astype(F32) * rt[:, R_GATE:R_GATE + 1]
         + y1_ref[...].astype(F32) * rt[:, R_GATE + 1:R_GATE + 2])
    out = _layer_norm_rows(DN_ALPHA * h_ref[...] + y, g_ref[...], b_ref[...])
    o_ref[...] = out
    ob_ref[...] = out.astype(BF16)


def _combine_ln(h, y0, y1, route, g, b, *, tm):
    n, d = h.shape
    row = pl.BlockSpec((tm, d), lambda i: (i, 0))
    vec = pl.BlockSpec((1, d), lambda i: (0, 0))
    return pl.pallas_call(
        _combine_ln_kernel,
        grid=(n // tm,),
        in_specs=[row, row, row, pl.BlockSpec((tm, LANES), lambda i: (i, 0)), vec, vec],
        out_specs=[row, row],
        out_shape=[jax.ShapeDtypeStruct((n, d), F32), jax.ShapeDtypeStruct((n, d), BF16)],
        compiler_params=_params("parallel"),
        name="moe_combine_ln",
    )(h, y0, y1, route, g.reshape(1, d), b.reshape(1, d))


R_EID, R_GATE, R_RANK = 0, 2, 4


def _router_kernel(h_ref, w_ref, b_ref, route_ref, cnt_ref, base_ref, tri_ref):
    i = pl.program_id(0)
    tm = h_ref.shape[0]

    @pl.when(i == 0)
    def _():
        base_ref[...] = jnp.zeros(base_ref.shape, F32)
        ti = lax.broadcasted_iota(jnp.int32, (tm, tm), 0)
        si = lax.broadcasted_iota(jnp.int32, (tm, tm), 1)
        tri_ref[...] = jnp.where(si < ti, 1.0, 0.0).astype(BF16)

    h = h_ref[...]
    h_hi = h.astype(BF16)
    h_lo = (h - h_hi.astype(F32)).astype(BF16)
    logits = (jnp.dot(h_hi, w_ref[0], preferred_element_type=F32)
              + jnp.dot(h_lo, w_ref[0], preferred_element_type=F32)
              + jnp.dot(h_hi, w_ref[1], preferred_element_type=F32)) + b_ref[...]
    lane = lax.broadcasted_iota(jnp.int32, (tm, LANES), 1).astype(F32)
    neg_inf = -jnp.inf
    big = float(4 * LANES)
    is_g = lane < MOE_GROUPS
    gl = jnp.where(is_g, logits, neg_inf)
    gmax = jnp.max(gl, -1, keepdims=True)
    grp = jnp.min(jnp.where(gl == gmax, lane, big), -1, keepdims=True)
    p_grp = 1.0 / jnp.sum(jnp.where(is_g, jnp.exp(logits - gmax), 0.0), -1, keepdims=True)
    lo = MOE_GROUPS + MOE_PER_GROUP * grp
    el = jnp.where((lane >= lo) & (lane < lo + MOE_PER_GROUP), logits, neg_inf)
    v1 = jnp.max(el, -1, keepdims=True)
    i1 = jnp.min(jnp.where(el == v1, lane, big), -1, keepdims=True)
    el2 = jnp.where(lane == i1, neg_inf, el)
    v2 = jnp.max(el2, -1, keepdims=True)
    i2 = jnp.min(jnp.where(el2 == v2, lane, big), -1, keepdims=True)
    t = jnp.exp(v2 - v1)
    g1 = p_grp / (1.0 + t)
    g2 = g1 * t
    e1 = i1 - MOE_GROUPS
    e2 = i2 - MOE_GROUPS
    oh1 = jnp.where(lane == e1, 1.0, 0.0)
    oh2 = jnp.where(lane == e2, 1.0, 0.0)
    oh = oh1 + oh2
    tot = base_ref[...] + jnp.dot(tri_ref[...], oh.astype(BF16), preferred_element_type=F32)
    r1 = jnp.sum(oh1 * tot, -1, keepdims=True)
    r2 = jnp.sum(oh2 * tot, -1, keepdims=True)
    new_base = base_ref[...] + jnp.sum(oh, 0, keepdims=True)
    base_ref[...] = new_base
    cnt_ref[...] = jnp.broadcast_to(new_base, cnt_ref.shape)
    route = jnp.zeros((tm, LANES), F32)
    for k, val in enumerate((e1, e2, g1, g2, r1, r2)):
        route = jnp.where(lane == float(k), val, route)
    route_ref[...] = route


def _router(h, w_r, b_r, *, tm):
    n, d = h.shape
    return pl.pallas_call(
        _router_kernel,
        grid=(n // tm,),
        in_specs=[pl.BlockSpec((tm, d), lambda i: (i, 0)),
                  pl.BlockSpec((2, d, LANES), lambda i: (0, 0, 0)),
                  pl.BlockSpec((1, LANES), lambda i: (0, 0))],
        out_specs=[pl.BlockSpec((tm, LANES), lambda i: (i, 0)),
                   pl.BlockSpec((8, LANES), lambda i: (0, 0))],
        out_shape=[jax.ShapeDtypeStruct((n, LANES), F32), jax.ShapeDtypeStruct((8, LANES), F32)],
        scratch_shapes=[pltpu.VMEM((1, LANES), F32), pltpu.VMEM((tm, tm), BF16)],
        compiler_params=_params("arbitrary"),
        name="moe_router",
    )(h, w_r, b_r)


def _moe(h, hb, wg_r, bg_r, we_r, be_r, w_gate, w_up, w_down, layer, ln_g, ln_b):
    n, d = h.shape
    pad = LANES - MOE_GROUPS - MOE_EXPERTS
    w_r = jnp.pad(jnp.concatenate([wg_r, we_r], 1), ((0, 0), (0, pad)))
    w_hi = w_r.astype(BF16)
    w_lo = (w_r - w_hi.astype(F32)).astype(BF16)
    b_r = jnp.pad(jnp.concatenate([bg_r, be_r]), (0, pad)).reshape(1, LANES)
    route, cnt = _router(h, jnp.stack([w_hi, w_lo]), b_r, tm=512)
    counts = cnt[0, :MOE_EXPERTS].astype(jnp.int32)
    padded = (counts + MOE_BLOCK - 1) // MOE_BLOCK * MOE_BLOCK
    pends = jnp.cumsum(padded)
    pstarts = pends - padded
    eid = route[:, R_EID:R_EID + MOE_TOPK].astype(jnp.int32)
    rank = route[:, R_RANK:R_RANK + MOE_TOPK].astype(jnp.int32)
    experts = jnp.arange(MOE_EXPERTS, dtype=jnp.int32)
    dest = jnp.sum(jnp.where(eid[..., None] == experts, pstarts, 0), -1) + rank
    p_rows = n * MOE_TOPK + MOE_EXPERTS * MOE_BLOCK
    nb = p_rows // MOE_BLOCK
    tok = jnp.broadcast_to(jnp.arange(n, dtype=jnp.int32)[:, None], (n, MOE_TOPK))
    buf_tok = (jnp.arange(p_rows, dtype=jnp.int32) % n).at[dest.reshape(-1)].set(tok.reshape(-1))
    blk_start = jnp.arange(nb, dtype=jnp.int32) * MOE_BLOCK
    blk_e = jnp.minimum(jnp.sum((pends[None, :] <= blk_start[:, None]).astype(jnp.int32), -1), MOE_EXPERTS - 1)
    xb = hb[buf_tok]
    yb = _experts(blk_e, xb, w_gate, w_up, w_down, layer)
    return _combine_ln(h, yb[dest[:, 0]], yb[dest[:, 1]], route, ln_g, ln_b, tm=512)


def kernel(x, positions, ln1_g, ln1_b, ln2_g, ln2_b, a_w_in, a_w_out, b_w_in, b_gate_bias, b_conv_w, b_conv_b,
           b_norm_g, b_w_out, c_w_in, c_norm_g, c_w_out, r_group_w, r_group_b, r_expert_w, r_expert_b,
           e_w_gate, e_w_up, e_w_down):
    batch, seq, d = x.shape
    n = batch * seq
    tabs_a = _rope_tables_a(positions)
    inv_c = C_THETA ** (-jnp.arange(0, C_QK_DIM, 2, dtype=F32) / C_QK_DIM)
    ang_c = positions.astype(F32)[:, None] * inv_c[None, :]
    cos_c, sin_c = jnp.cos(ang_c), jnp.sin(ang_c)
    h = x.reshape(n, d)
    hb = h.astype(BF16)
    for i in range(DEPTH):
        kind, j = i % 3, i // 3
        if kind == 0:
            y = _mixer_dilated(hb, a_w_in[j], tabs_a, batch, seq)
            w_out = a_w_out[j]
        elif kind == 1:
            y = _mixer_mlstm(hb, b_w_in[j], b_gate_bias[j], b_conv_w[j], b_conv_b[j], b_norm_g[j], batch, seq)
            w_out = b_w_out[j]
        else:
            y = _mixer_retention(hb, c_w_in[j], c_norm_g[j], cos_c, sin_c, batch, seq)
            w_out = c_w_out[j]
        h, hb = _matmul_res_ln(y, w_out.astype(BF16), h, ln1_g[i], ln1_b[i], tm=512)
        h, hb = _moe(h, hb, r_group_w[i], r_group_b[i], r_expert_w[i], r_expert_b[i],
                     e_w_gate, e_w_up, e_w_down, i, ln2_g[i], ln2_b[i])
    return h.reshape(batch, seq, d)
```

```python
import functools

import jax
import jax.numpy as jnp
from jax import lax
from jax.experimental import pallas as pl
from jax.experimental.pallas import tpu as pltpu
from jax.experimental.pallas import tpu_sc as plsc

F32 = jnp.float32
BF16 = jnp.bfloat16

D_MODEL = 1024
DEPTH = 4
DN_ALPHA = (2.0 * DEPTH) ** 0.25
LN_EPS = 1e-5

A_HEADS = 16
A_HEAD_DIM = 64
A_DILATIONS = (1, 4, 16)
A_BLOCK = 128
A_UNROLL = 8
A_GROUP = 16
A_PITCH = 20
LOG2_E = 1.4426950408889634
A_ROT_DIM = 16
ROPE_THETA = 500000.0

B_HEADS = 8
B_QK_DIM = 64
B_V_DIM = 128
B_CONV = 4
B_CHUNK = 256

C_HEADS = 4
C_QK_DIM = 256
C_V_DIM = 512
C_CHUNK = 256
C_THETA = 10000.0

MOE_GROUPS = 8
MOE_PER_GROUP = 8
MOE_EXPERTS = 64
MOE_TOPK = 2
MOE_HIDDEN = 256
MOE_BLOCK = 256
SC_ROWS = 32
SC_INDEX_LANES = 128

LANES = 128
NEG = -1e30
VMEM_LIMIT = 48 * 1024 * 1024


def _params(*sem):
    return pltpu.CompilerParams(dimension_semantics=sem, vmem_limit_bytes=VMEM_LIMIT)


def _mm_kernel(x_ref, w_ref, o_ref):
    o_ref[...] = jnp.dot(x_ref[...], w_ref[...], preferred_element_type=F32).astype(o_ref.dtype)


def _matmul(x, w, *, tm, tn, out_dtype=F32):
    n, k = x.shape
    m = w.shape[1]
    return pl.pallas_call(
        _mm_kernel,
        grid=(n // tm, m // tn),
        in_specs=[pl.BlockSpec((tm, k), lambda i, j: (i, 0)),
                  pl.BlockSpec((k, tn), lambda i, j: (0, j))],
        out_specs=pl.BlockSpec((tm, tn), lambda i, j: (i, j)),
        out_shape=jax.ShapeDtypeStruct((n, m), out_dtype),
        compiler_params=_params("parallel", "parallel"),
        name="matmul",
    )(x, w)


def _layer_norm_rows(z, g, b):
    mu = jnp.mean(z, -1, keepdims=True)
    zc = z - mu
    var = jnp.mean(zc * zc, -1, keepdims=True)
    return zc * lax.rsqrt(var + LN_EPS) * g + b


def _mm_res_ln_kernel(x_ref, w_ref, h_ref, g_ref, b_ref, o_ref, ob_ref):
    y = jnp.dot(x_ref[...], w_ref[...], preferred_element_type=F32)
    out = _layer_norm_rows(DN_ALPHA * h_ref[...] + y, g_ref[...], b_ref[...])
    o_ref[...] = out
    ob_ref[...] = out.astype(BF16)


def _matmul_res_ln(x, w, h, g, b, *, tm):
    n, k = x.shape
    d = w.shape[1]
    return pl.pallas_call(
        _mm_res_ln_kernel,
        grid=(n // tm,),
        in_specs=[pl.BlockSpec((tm, k), lambda i: (i, 0)),
                  pl.BlockSpec((k, d), lambda i: (0, 0)),
                  pl.BlockSpec((tm, d), lambda i: (i, 0)),
                  pl.BlockSpec((1, d), lambda i: (0, 0)),
                  pl.BlockSpec((1, d), lambda i: (0, 0))],
        out_specs=[pl.BlockSpec((tm, d), lambda i: (i, 0)),
                   pl.BlockSpec((tm, d), lambda i: (i, 0))],
        out_shape=[jax.ShapeDtypeStruct((n, d), F32), jax.ShapeDtypeStruct((n, d), BF16)],
        compiler_params=_params("parallel"),
        name="matmul_res_ln",
    )(x, w, h, g.reshape(1, d), b.reshape(1, d))


def _proj_a_kernel(x_ref, w_ref, c_ref, s1_ref, s2_ref, q_ref, k_ref, v_ref):
    x = x_ref[...]
    width = 2 * LANES
    for c, ref in ((0, q_ref), (1, k_ref), (2, v_ref)):
        for j in range(D_MODEL // width):
            col = c * D_MODEL + j * width
            y = jnp.dot(x, w_ref[:, col:col + width], preferred_element_type=F32)
            if c < 2:
                half = A_ROT_DIM // 2
                y = (y * c_ref[...] + pltpu.roll(y, width - half, 1) * s1_ref[...]
                     + pltpu.roll(y, half, 1) * s2_ref[...])
            if c == 0:
                y = y * (LOG2_E * A_HEAD_DIM ** -0.5)
            ref[0, 2 * j] = y[:, :LANES]
            ref[0, 2 * j + 1] = y[:, LANES:]


def _proj_a(xb, w, tabs, batch, seq, *, tm):
    n, d = xb.shape
    spb = seq // tm
    hp = D_MODEL // LANES
    qkv_shape = jax.ShapeDtypeStruct((batch, hp, seq, LANES), F32)
    out_spec = pl.BlockSpec((1, hp, tm, LANES), lambda i: (i // spb, 0, i % spb, 0))
    tab_spec = pl.BlockSpec((tm, 2 * LANES), lambda i: (i % spb, 0))
    return pl.pallas_call(
        _proj_a_kernel,
        grid=(n // tm,),
        in_specs=[pl.BlockSpec((tm, d), lambda i: (i, 0)),
                  pl.BlockSpec((d, 3 * d), lambda i: (0, 0)),
                  tab_spec, tab_spec, tab_spec],
        out_specs=[out_spec, out_spec, out_spec],
        out_shape=[qkv_shape, qkv_shape, qkv_shape],
        compiler_params=_params("parallel"),
        name="proj_a",
    )(xb, w, *tabs)


def _rope_tables_a(positions):
    half = A_ROT_DIM // 2
    inv = ROPE_THETA ** (-jnp.arange(0, A_ROT_DIM, 2, dtype=F32) / A_ROT_DIM)
    ang = positions.astype(F32)[:, None] * inv[None, :]
    cos, sin = jnp.cos(ang), jnp.sin(ang)
    s = positions.shape[0]
    pad = jnp.zeros((s, A_HEAD_DIM - A_ROT_DIM), F32)
    c_head = jnp.concatenate([cos, cos, pad + 1.0], -1)
    s1_head = jnp.concatenate([-sin, jnp.zeros_like(sin), pad], -1)
    s2_head = jnp.concatenate([jnp.zeros_like(sin), sin, pad], -1)
    reps = 2 * LANES // A_HEAD_DIM
    return tuple(jnp.tile(t, (1, reps)) for t in (c_head, s1_head, s2_head))


def _attn_blocks(q_ref, k_ref, v_ref, o_ref, lse_ref, bias_ref, hmask, head0, first_block, d, pitch):
    nk = 2 * A_BLOCK
    loaded = []
    for u in range(A_UNROLL):
        g = first_block + u
        r = g % d
        n = g // d
        qstart = n * (A_BLOCK * pitch) + r
        kstart = jnp.maximum(qstart - A_BLOCK * pitch, r)
        if pitch == 1:
            qstart = pl.multiple_of(qstart, A_BLOCK)
            kstart = pl.multiple_of(kstart, A_BLOCK)
            qsl, ksl = pl.ds(qstart, A_BLOCK), pl.ds(kstart, nk)
        else:
            qsl, ksl = pl.ds(qstart, A_BLOCK, stride=pitch), pl.ds(kstart, nk, stride=pitch)
        bias = bias_ref[jnp.minimum(n, 1)]
        loaded.append((qsl, q_ref[qsl, :].astype(BF16), k_ref[ksl, :].astype(BF16), v_ref[ksl, :].astype(BF16), bias))
    results = []
    for qsl, qb, kb, vb, bias in loaded:
        outs, lses = [], []
        for h in range(2):
            s = lax.dot_general(qb * hmask[h], kb, (((1,), (1,)), ((), ())), preferred_element_type=F32) + bias
            m = jnp.max(s, -1, keepdims=True)
            p = jnp.exp2(s - m)
            l = jnp.sum(p, -1, keepdims=True)
            outs.append(jnp.dot(p.astype(BF16), vb, preferred_element_type=F32) * (1.0 / l))
            lses.append(jnp.broadcast_to(m + jnp.log2(l), (A_BLOCK, LANES)))
        results.append((qsl, jnp.where(head0, outs[0], outs[1]), jnp.where(head0, lses[0], lses[1])))
    for qsl, out, lse in results:
        o_ref[qsl, :] = out
        lse_ref[qsl, :] = lse


def _attn_kernel(q_ref, k_ref, v_ref, o_ref, q16_ref, k16_ref, v16_ref, ob_ref, lb_ref, o16_ref, l16_ref,
                 bias_ref):
    seq = q_ref.shape[0]
    groups = seq // A_GROUP
    head0 = lax.broadcasted_iota(jnp.int32, (A_BLOCK, LANES), 1) < A_HEAD_DIM
    hmask = [jnp.where(head0, 1.0, 0.0).astype(BF16), jnp.where(head0, 0.0, 1.0).astype(BF16)]
    qi = lax.broadcasted_iota(jnp.int32, (A_BLOCK, 2 * A_BLOCK), 0)
    kj = lax.broadcasted_iota(jnp.int32, (A_BLOCK, 2 * A_BLOCK), 1)
    bias_ref[0] = jnp.where(kj <= qi, 0.0, NEG).astype(F32)
    bias_ref[1] = jnp.where((kj >= qi) & (kj <= qi + A_BLOCK), 0.0, NEG).astype(F32)

    def spread(g, carry):
        src = pl.ds(pl.multiple_of(g * A_GROUP, A_GROUP), A_GROUP)
        dst = pl.ds(pl.multiple_of(g * A_PITCH, 4), A_GROUP)
        q16_ref[dst, :] = q_ref[src, :]
        k16_ref[dst, :] = k_ref[src, :]
        v16_ref[dst, :] = v_ref[src, :]
        return carry

    lax.fori_loop(0, groups, spread, 0, unroll=8)

    branches = ((1, 1, q_ref, k_ref, v_ref, ob_ref.at[0], lb_ref.at[0]),
                (4, 4, q_ref, k_ref, v_ref, ob_ref.at[1], lb_ref.at[1]),
                (16, A_PITCH, q16_ref, k16_ref, v16_ref, o16_ref, l16_ref))
    for d, pitch, qr, kr, vr, orf, lrf in branches:

        def body(it, carry, d=d, pitch=pitch, qr=qr, kr=kr, vr=vr, orf=orf, lrf=lrf):
            _attn_blocks(qr, kr, vr, orf, lrf, bias_ref, hmask, head0, it * A_UNROLL, d, pitch)
            return carry

        lax.fori_loop(0, seq // (A_BLOCK * A_UNROLL), body, 0)

    def mix(g, carry):
        nat = pl.ds(pl.multiple_of(g * A_GROUP, A_GROUP), A_GROUP)
        pad = pl.ds(pl.multiple_of(g * A_PITCH, 4), A_GROUP)
        o0, o1, o2 = ob_ref[0, nat, :], ob_ref[1, nat, :], o16_ref[pad, :]
        l0, l1, l2 = lb_ref[0, nat, :], lb_ref[1, nat, :], l16_ref[pad, :]
        mx = jnp.maximum(jnp.maximum(l0, l1), l2)
        w0, w1, w2 = jnp.exp2(l0 - mx), jnp.exp2(l1 - mx), jnp.exp2(l2 - mx)
        o_ref[nat, :] = ((w0 * o0 + w1 * o1 + w2 * o2) / (w0 + w1 + w2)).astype(o_ref.dtype)
        return carry

    lax.fori_loop(0, groups, mix, 0, unroll=8)


def _attention(q, k, v):
    batch, hp, seq, _ = q.shape
    assert seq % (2 * A_BLOCK * max(A_DILATIONS)) == 0 and seq % (A_BLOCK * A_UNROLL) == 0
    in_spec = pl.BlockSpec((None, None, seq, LANES), lambda b, p: (b, p, 0, 0))
    padded = seq // A_GROUP * A_PITCH
    return pl.pallas_call(
        _attn_kernel,
        grid=(batch, hp),
        in_specs=[in_spec, in_spec, in_spec],
        out_specs=pl.BlockSpec((None, seq, LANES), lambda b, p: (b, 0, p)),
        out_shape=jax.ShapeDtypeStruct((batch, seq, hp * LANES), BF16),
        scratch_shapes=[pltpu.VMEM((padded, LANES), F32)] * 3
        + [pltpu.VMEM((2, seq, LANES), F32)] * 2
        + [pltpu.VMEM((padded, LANES), F32)] * 2
        + [pltpu.VMEM((2, A_BLOCK, 2 * A_BLOCK), F32)],
        compiler_params=_params("parallel", "parallel"),
        name="dilated_attention",
    )(q, k, v)


def _mixer_dilated(hb, w_in, tabs, batch, seq):
    q, k, v = _proj_a(hb, w_in.astype(BF16), tabs, batch, seq, tm=512)
    o = _attention(q, k, v)
    return o.reshape(batch * seq, D_MODEL)


def _mlstm_kernel(qk_ref, v_ref, o_ref, gc_ref, gb_ref, cw_ref, cb_ref, ng_ref, out_ref,
                  ext_ref, c_ref, n_ref, m_ref):
    L = B_CHUNK
    chunk = pl.program_id(1)

    @pl.when(chunk == 0)
    def _():
        ext_ref[0:8, :] = jnp.zeros((8, D_MODEL), F32)
        c_ref[...] = jnp.zeros(c_ref.shape, F32)
        n_ref[...] = jnp.zeros(n_ref.shape, F32)
        m_ref[...] = jnp.zeros(m_ref.shape, F32)

    u = qk_ref[...].astype(F32)
    ext_ref[8:8 + L, :] = u
    conv = u * cw_ref[B_CONV - 1:B_CONV, :] + cb_ref[...]
    for j in range(1, B_CONV):
        conv = conv + ext_ref[pl.ds(8 - j, L), :] * cw_ref[B_CONV - 1 - j:B_CONV - j, :]
    ext_ref[0:8, :] = u[L - 8:, :]
    qk = conv * jax.nn.sigmoid(conv)
    half = D_MODEL // 2

    gc = gc_ref[...] + gb_ref[...]
    gr = gc.T
    i_col, i_row = gc, gr[:B_HEADS, :]
    lf_col = jax.nn.log_sigmoid(gc)
    lf_row = jax.nn.log_sigmoid(gr[B_HEADS:2 * B_HEADS, :])
    ti = lax.broadcasted_iota(jnp.int32, (L, L), 0)
    si = lax.broadcasted_iota(jnp.int32, (L, L), 1)
    causal = ti >= si
    tri = causal.astype(F32)
    a_col = jnp.dot(tri, lf_col, preferred_element_type=F32, precision=lax.Precision.HIGHEST)
    a_row = lax.dot_general(lf_row, tri, (((1,), (1,)), ((), ())), preferred_element_type=F32,
                            precision=lax.Precision.HIGHEST)
    lane = lax.broadcasted_iota(jnp.int32, (1, LANES), 1)
    lane_h0 = lane < B_QK_DIM
    col_h0 = lax.broadcasted_iota(jnp.int32, (1, 2 * B_V_DIM), 1) < B_V_DIM

    for p in range(B_HEADS // 2):
        qp = qk[:, p * LANES:(p + 1) * LANES]
        kp = qk[:, half + p * LANES:half + (p + 1) * LANES] * (B_QK_DIM ** -0.5)
        vp = v_ref[:, p * 2 * B_V_DIM:(p + 1) * 2 * B_V_DIM]
        kpb = kp.astype(BF16)
        vpb = vp.astype(BF16)
        c_old = c_ref[p]
        c_oldb = c_old.astype(BF16)
        n_old = n_ref[p]
        m_pair = m_ref[p]
        ws_cols, decays, m_news = [], [], []
        for hh in range(2):
            h = 2 * p + hh
            m_old = m_pair[:, hh * B_QK_DIM:hh * B_QK_DIM + 1]
            ac, ar = a_col[:, B_HEADS + h:B_HEADS + h + 1], a_row[h:h + 1, :]
            ic, ir = i_col[:, h:h + 1], i_row[h:h + 1, :]
            dmat = jnp.where(causal, ac - ar + ir, NEG)
            inter = ac + m_old
            m_t = jnp.maximum(inter, jnp.max(dmat, -1, keepdims=True))
            qm = jnp.where(lane_h0 if hh == 0 else jnp.logical_not(lane_h0), qp, 0.0)
            qmb = qm.astype(BF16)
            sc = lax.dot_general(qmb, kpb, (((1,), (1,)), ((), ())), preferred_element_type=F32)
            sc = sc * jnp.exp(dmat - m_t)
            g_inter = jnp.exp(inter - m_t)
            vh = vpb[:, hh * B_V_DIM:(hh + 1) * B_V_DIM]
            qc = jnp.dot(qmb, c_oldb, preferred_element_type=F32)[:, hh * B_V_DIM:(hh + 1) * B_V_DIM]
            num = jnp.dot(sc.astype(BF16), vh, preferred_element_type=F32) + g_inter * qc
            den = jnp.sum(sc, -1, keepdims=True) + g_inter * jnp.sum(qm * n_old, -1, keepdims=True)
            h_out = num / jnp.maximum(jnp.abs(den), jnp.exp(-m_t))
            mu = jnp.mean(h_out, -1, keepdims=True)
            hc = h_out - mu
            var = jnp.mean(hc * hc, -1, keepdims=True)
            cols = slice(h * B_V_DIM, (h + 1) * B_V_DIM)
            hn = hc * lax.rsqrt(var + LN_EPS) * ng_ref[:, cols]
            out_ref[:, cols] = (hn * jax.nn.sigmoid(o_ref[:, cols].astype(F32))).astype(out_ref.dtype)
            a_end = ac[L - 1:L, :]
            w_col = a_end - ac + ic
            m_new = jnp.maximum(a_end + m_old, jnp.max(w_col, 0, keepdims=True))
            decays.append(jnp.exp(a_end + m_old - m_new))
            ws_cols.append(jnp.exp(w_col - m_new))
            m_news.append(m_new)
        ws = jnp.where(lane_h0, jnp.broadcast_to(ws_cols[0], (L, LANES)), jnp.broadcast_to(ws_cols[1], (L, LANES)))
        kw = kp * ws
        dec_c = jnp.where(col_h0, jnp.broadcast_to(decays[0], (1, 2 * B_V_DIM)),
                          jnp.broadcast_to(decays[1], (1, 2 * B_V_DIM)))
        dec_n = jnp.where(lane_h0, jnp.broadcast_to(decays[0], (1, LANES)), jnp.broadcast_to(decays[1], (1, LANES)))
        c_ref[p] = dec_c * c_old + lax.dot_general(kw.astype(BF16), vpb, (((0,), (0,)), ((), ())),
                                                   preferred_element_type=F32)
        n_ref[p] = dec_n * n_old + jnp.sum(kw, 0, keepdims=True)
        m_ref[p] = jnp.where(lane_h0, jnp.broadcast_to(m_news[0], (1, LANES)), jnp.broadcast_to(m_news[1], (1, LANES)))


def _mlstm(proj, gates, gate_bias, conv_w, conv_b, norm_g, batch, seq):
    L = B_CHUNK
    d = D_MODEL
    slab = lambda c: pl.BlockSpec((None, L, d), lambda b, s, c=c: (b, s, c))
    full = lambda shape: pl.BlockSpec(shape, lambda b, s: (0,) * len(shape))
    return pl.pallas_call(
        _mlstm_kernel,
        grid=(batch, seq // L),
        in_specs=[slab(0), slab(1), slab(2),
                  pl.BlockSpec((None, L, LANES), lambda b, s: (b, s, 0)),
                  full((1, LANES)),
                  full((B_CONV, d)), full((1, d)), full((1, d))],
        out_specs=pl.BlockSpec((None, L, d), lambda b, s: (b, s, 0)),
        out_shape=jax.ShapeDtypeStruct((batch, seq, d), BF16),
        scratch_shapes=[pltpu.VMEM((L + 8, d), F32),
                        pltpu.VMEM((B_HEADS // 2, 2 * B_QK_DIM, 2 * B_V_DIM), F32),
                        pltpu.VMEM((B_HEADS // 2, 1, LANES), F32),
                        pltpu.VMEM((B_HEADS // 2, 1, LANES), F32)],
        compiler_params=_params("parallel", "arbitrary"),
        name="mlstm",
    )(proj, proj, proj, gates, jnp.pad(gate_bias, (0, LANES - 2 * B_HEADS)).reshape(1, LANES),
      conv_w, conv_b.reshape(1, d), norm_g.reshape(1, d))


def _mixer_mlstm(hb, w_in, gate_bias, conv_w, conv_b, norm_g, batch, seq):
    n = batch * seq
    main = 3 * D_MODEL
    proj = _matmul(hb, w_in[:, :main].astype(BF16), tm=1024, tn=1024, out_dtype=BF16)
    w_g = jnp.pad(w_in[:, main:], ((0, 0), (0, LANES - 2 * B_HEADS))).astype(BF16)
    gates = _matmul(hb, w_g, tm=2048, tn=LANES)
    out = _mlstm(proj.reshape(batch, seq, main), gates.reshape(batch, seq, LANES), gate_bias,
                 conv_w, conv_b, norm_g, batch, seq)
    return out.reshape(n, D_MODEL)


def _retention_kernel(lg_ref, q_ref, k_ref, v_ref, g_ref, cos_ref, sin_ref, ng_ref, out_ref,
                      r_ref, dm_ref, xi_ref, zeta_ref):
    L = C_CHUNK
    head = pl.program_id(1)
    chunk = pl.program_id(2)
    lg = lg_ref[head]

    @pl.when(chunk == 0)
    def _():
        r_ref[...] = jnp.zeros(r_ref.shape, F32)
        ti = lax.broadcasted_iota(jnp.int32, (L, L), 0)
        si = lax.broadcasted_iota(jnp.int32, (L, L), 1)
        rel = (ti - si).astype(F32)
        dm_ref[...] = jnp.where(rel >= 0, jnp.exp(jnp.maximum(rel, 0.0) * lg), 0.0)
        idx = lax.broadcasted_iota(jnp.int32, (L, LANES), 0).astype(F32)
        xi_ref[...] = jnp.exp((idx + 1.0) * lg)
        zeta_ref[...] = jnp.exp((L - 1.0 - idx) * lg)

    cos, sin = cos_ref[...], sin_ref[...]
    hd = C_QK_DIM // 2

    def rope(t):
        t1, t2 = t[:, :hd], t[:, hd:]
        return jnp.concatenate([t1 * cos - t2 * sin, t2 * cos + t1 * sin], -1)

    q = rope(q_ref[...].astype(F32))
    k = rope(k_ref[...].astype(F32)) * (C_QK_DIM ** -0.5)
    qb = q.astype(BF16)
    vb = v_ref[...].astype(BF16)
    r_old = r_ref[...]
    sc = lax.dot_general(qb, k.astype(BF16), (((1,), (1,)), ((), ())), preferred_element_type=F32) * dm_ref[...]
    o = jnp.dot(sc.astype(BF16), vb, preferred_element_type=F32)
    o = o + xi_ref[:, 0:1] * jnp.dot(qb, r_old.astype(BF16), preferred_element_type=F32)
    kz = (k * zeta_ref[:, 0:1]).astype(BF16)
    cd = jnp.exp(jnp.full((1, 1), float(L), F32) * lg)
    r_ref[...] = cd * r_old + lax.dot_general(kz, vb, (((0,), (0,)), ((), ())), preferred_element_type=F32)
    mu = jnp.mean(o, -1, keepdims=True)
    oc = o - mu
    var = jnp.mean(oc * oc, -1, keepdims=True)
    on = oc * lax.rsqrt(var + LN_EPS) * ng_ref[...]
    g = g_ref[...].astype(F32)
    out_ref[...] = (on * (g * jax.nn.sigmoid(g))).astype(out_ref.dtype)


def _retention(proj, cos, sin, norm_g, batch, seq):
    L = C_CHUNK
    log_gamma = jnp.log(1.0 - 2.0 ** (-5.0 - jnp.arange(C_HEADS, dtype=F32)))
    qk_blocks = D_MODEL // C_QK_DIM
    v_off = 2 * D_MODEL // C_V_DIM
    g_off = 4 * D_MODEL // C_V_DIM
    return pl.pallas_call(
        _retention_kernel,
        grid=(batch, C_HEADS, seq // L),
        in_specs=[pl.BlockSpec(memory_space=pltpu.SMEM),
                  pl.BlockSpec((None, L, C_QK_DIM), lambda b, h, c: (b, c, h)),
                  pl.BlockSpec((None, L, C_QK_DIM), lambda b, h, c: (b, c, qk_blocks + h)),
                  pl.BlockSpec((None, L, C_V_DIM), lambda b, h, c: (b, c, v_off + h)),
                  pl.BlockSpec((None, L, C_V_DIM), lambda b, h, c: (b, c, g_off + h)),
                  pl.BlockSpec((L, C_QK_DIM // 2), lambda b, h, c: (c, 0)),
                  pl.BlockSpec((L, C_QK_DIM // 2), lambda b, h, c: (c, 0)),
                  pl.BlockSpec((1, C_V_DIM), lambda b, h, c: (0, h))],
        out_specs=pl.BlockSpec((None, L, C_V_DIM), lambda b, h, c: (b, c, h)),
        out_shape=jax.ShapeDtypeStruct((batch, seq, 2 * D_MODEL), BF16),
        scratch_shapes=[pltpu.VMEM((C_QK_DIM, C_V_DIM), F32),
                        pltpu.VMEM((L, L), F32),
                        pltpu.VMEM((L, LANES), F32),
                        pltpu.VMEM((L, LANES), F32)],
        compiler_params=_params("parallel", "parallel", "arbitrary"),
        name="retention",
    )(log_gamma, proj, proj, proj, proj, cos, sin, norm_g.reshape(1, 2 * D_MODEL))


def _mixer_retention(hb, w_in, norm_g, cos, sin, batch, seq):
    proj = _matmul(hb, w_in.astype(BF16), tm=1024, tn=1024, out_dtype=BF16)
    out = _retention(proj.reshape(batch, seq, 6 * D_MODEL), cos, sin, norm_g, batch, seq)
    return out.reshape(batch * seq, 2 * D_MODEL)


def _expert_kernel(be_ref, x_ref, wg_ref, wu_ref, wd_ref, y_ref, wgb_ref, wub_ref, wdb_ref):
    j = pl.program_id(0)
    changed = jnp.logical_or(j == 0, be_ref[j] != be_ref[jnp.maximum(j - 1, 0)])

    @pl.when(changed)
    def _():
        wgb_ref[...] = wg_ref[...].astype(BF16)
        wub_ref[...] = wu_ref[...].astype(BF16)
        wdb_ref[...] = wd_ref[...].astype(BF16)

    x = x_ref[...].astype(BF16)
    a = jnp.dot(x, wgb_ref[...], preferred_element_type=F32)
    u = jnp.dot(x, wub_ref[...], preferred_element_type=F32)
    act = (a * jax.nn.sigmoid(a) * u).astype(BF16)
    y_ref[...] = jnp.dot(act, wdb_ref[...], preferred_element_type=F32).astype(y_ref.dtype)


def _experts(blk_e, xb, w_gate, w_up, w_down, layer):
    p, d = xb.shape
    nb = p // MOE_BLOCK
    hid = MOE_HIDDEN
    grid_spec = pltpu.PrefetchScalarGridSpec(
        num_scalar_prefetch=1,
        grid=(nb,),
        in_specs=[pl.BlockSpec((MOE_BLOCK, d), lambda j, be: (j, 0)),
                  pl.BlockSpec((None, None, d, hid), lambda j, be: (layer, be[j], 0, 0)),
                  pl.BlockSpec((None, None, d, hid), lambda j, be: (layer, be[j], 0, 0)),
                  pl.BlockSpec((None, None, hid, d), lambda j, be: (layer, be[j], 0, 0))],
        out_specs=pl.BlockSpec((MOE_BLOCK, d), lambda j, be: (j, 0)),
        scratch_shapes=[pltpu.VMEM((d, hid), BF16), pltpu.VMEM((d, hid), BF16), pltpu.VMEM((hid, d), BF16)],
    )
    return pl.pallas_call(
        _expert_kernel,
        grid_spec=grid_spec,
        out_shape=jax.ShapeDtypeStruct((p, d), F32),
        compiler_params=_params("arbitrary"),
        name="moe_experts",
    )(blk_e, xb, w_gate, w_up, w_down)


def _combine_ln_kernel(h_ref, y0_ref, y1_ref, rt_ref, g_ref, b_ref, o_ref, ob_ref):
    rt = rt_ref[...]
    y = (y0_ref[...].astype(F32) * rt[:, R_GATE:R_GATE + 1]
         + y1_ref[...].astype(F32) * rt[:, R_GATE + 1:R_GATE + 2])
    out = _layer_norm_rows(DN_ALPHA * h_ref[...] + y, g_ref[...], b_ref[...])
    o_ref[...] = out
    ob_ref[...] = out.astype(BF16)


def _combine_ln(h, y0, y1, route, g, b, *, tm):
    n, d = h.shape
    row = pl.BlockSpec((tm, d), lambda i: (i, 0))
    vec = pl.BlockSpec((1, d), lambda i: (0, 0))
    return pl.pallas_call(
        _combine_ln_kernel,
        grid=(n // tm,),
        in_specs=[row, row, row, pl.BlockSpec((tm, LANES), lambda i: (i, 0)), vec, vec],
        out_specs=[row, row],
        out_shape=[jax.ShapeDtypeStruct((n, d), F32), jax.ShapeDtypeStruct((n, d), BF16)],
        compiler_params=_params("parallel"),
        name="moe_combine_ln",
    )(h, y0, y1, route, g.reshape(1, d), b.reshape(1, d))


R_EID, R_GATE, R_RANK = 0, 2, 4


def _router_kernel(h_ref, w_ref, b_ref, route_ref, cnt_ref, base_ref, tri_ref):
    i = pl.program_id(0)
    tm = h_ref.shape[0]

    @pl.when(i == 0)
    def _():
        base_ref[...] = jnp.zeros(base_ref.shape, F32)
        ti = lax.broadcasted_iota(jnp.int32, (tm, tm), 0)
        si = lax.broadcasted_iota(jnp.int32, (tm, tm), 1)
        tri_ref[...] = jnp.where(si < ti, 1.0, 0.0).astype(BF16)

    h = h_ref[...]
    h_hi = h.astype(BF16)
    h_lo = (h - h_hi.astype(F32)).astype(BF16)
    logits = (jnp.dot(h_hi, w_ref[0], preferred_element_type=F32)
              + jnp.dot(h_lo, w_ref[0], preferred_element_type=F32)
              + jnp.dot(h_hi, w_ref[1], preferred_element_type=F32)) + b_ref[...]
    lane = lax.broadcasted_iota(jnp.int32, (tm, LANES), 1).astype(F32)
    neg_inf = -jnp.inf
    big = float(4 * LANES)
    is_g = lane < MOE_GROUPS
    gl = jnp.where(is_g, logits, neg_inf)
    gmax = jnp.max(gl, -1, keepdims=True)
    grp = jnp.min(jnp.where(gl == gmax, lane, big), -1, keepdims=True)
    p_grp = 1.0 / jnp.sum(jnp.where(is_g, jnp.exp(logits - gmax), 0.0), -1, keepdims=True)
    lo = MOE_GROUPS + MOE_PER_GROUP * grp
    el = jnp.where((lane >= lo) & (lane < lo + MOE_PER_GROUP), logits, neg_inf)
    v1 = jnp.max(el, -1, keepdims=True)
    i1 = jnp.min(jnp.where(el == v1, lane, big), -1, keepdims=True)
    el2 = jnp.where(lane == i1, neg_inf, el)
    v2 = jnp.max(el2, -1, keepdims=True)
    i2 = jnp.min(jnp.where(el2 == v2, lane, big), -1, keepdims=True)
    t = jnp.exp(v2 - v1)
    g1 = p_grp / (1.0 + t)
    g2 = g1 * t
    e1 = i1 - MOE_GROUPS
    e2 = i2 - MOE_GROUPS
    oh1 = jnp.where(lane == e1, 1.0, 0.0)
    oh2 = jnp.where(lane == e2, 1.0, 0.0)
    oh = oh1 + oh2
    tot = base_ref[...] + jnp.dot(tri_ref[...], oh.astype(BF16), preferred_element_type=F32)
    r1 = jnp.sum(oh1 * tot, -1, keepdims=True)
    r2 = jnp.sum(oh2 * tot, -1, keepdims=True)
    new_base = base_ref[...] + jnp.sum(oh, 0, keepdims=True)
    base_ref[...] = new_base
    cnt_ref[...] = jnp.broadcast_to(new_base, cnt_ref.shape)
    route = jnp.zeros((tm, LANES), F32)
    for k, val in enumerate((e1, e2, g1, g2, r1, r2)):
        route = jnp.where(lane == float(k), val, route)
    route_ref[...] = route


def _router(h, w_r, b_r, *, tm):
    n, d = h.shape
    return pl.pallas_call(
        _router_kernel,
        grid=(n // tm,),
        in_specs=[pl.BlockSpec((tm, d), lambda i: (i, 0)),
                  pl.BlockSpec((2, d, LANES), lambda i: (0, 0, 0)),
                  pl.BlockSpec((1, LANES), lambda i: (0, 0))],
        out_specs=[pl.BlockSpec((tm, LANES), lambda i: (i, 0)),
                   pl.BlockSpec((8, LANES), lambda i: (0, 0))],
        out_shape=[jax.ShapeDtypeStruct((n, LANES), F32), jax.ShapeDtypeStruct((8, LANES), F32)],
        scratch_shapes=[pltpu.VMEM((1, LANES), F32), pltpu.VMEM((tm, tm), BF16)],
        compiler_params=_params("arbitrary"),
        name="moe_router",
    )(h, w_r, b_r)


def _sc_index_rows(idx):
    n = idx.shape[0]
    return jnp.pad(idx.reshape(n // SC_ROWS, SC_ROWS), ((0, 0), (0, SC_INDEX_LANES - SC_ROWS)))


def _sc_mesh():
    return plsc.VectorSubcoreMesh(core_axis_name="core", subcore_axis_name="subcore")


def _sc_gather_rows(y, idx):
    n = idx.shape[0]
    d = y.shape[1]

    @pl.kernel(out_type=jax.ShapeDtypeStruct((n, d), y.dtype), mesh=_sc_mesh(), scratch_types=[])
    def gather(y_hbm, i_hbm, o_hbm):
        def body(i_vmem, o_vmem):
            pltpu.sync_copy(y_hbm.at[i_vmem.at[0, pl.ds(0, SC_ROWS)]], o_vmem)

        pltpu.emit_pipeline(
            body, grid=(n // SC_ROWS,),
            in_specs=[pl.BlockSpec((1, SC_INDEX_LANES), lambda i: (i, 0))],
            out_specs=[pl.BlockSpec((SC_ROWS, d), lambda i: (i, 0))],
            core_axis_name=("core", "subcore"), dimension_semantics=(pltpu.PARALLEL,),
        )(i_hbm, o_hbm)

    return gather(y, _sc_index_rows(idx))


def _sc_scatter_rows(x, idx0, idx1, p_rows):
    n, d = x.shape

    @pl.kernel(out_type=jax.ShapeDtypeStruct((p_rows, d), x.dtype), mesh=_sc_mesh(), scratch_types=[])
    def scatter(x_hbm, i0_hbm, i1_hbm, o_hbm):
        def body(x_vmem, i0_vmem, i1_vmem):
            pltpu.sync_copy(x_vmem, o_hbm.at[i0_vmem.at[0, pl.ds(0, SC_ROWS)]])
            pltpu.sync_copy(x_vmem, o_hbm.at[i1_vmem.at[0, pl.ds(0, SC_ROWS)]])

        pltpu.emit_pipeline(
            body, grid=(n // SC_ROWS,),
            in_specs=[pl.BlockSpec((SC_ROWS, d), lambda i: (i, 0)),
                      pl.BlockSpec((1, SC_INDEX_LANES), lambda i: (i, 0)),
                      pl.BlockSpec((1, SC_INDEX_LANES), lambda i: (i, 0))],
            out_specs=[],
            core_axis_name=("core", "subcore"), dimension_semantics=(pltpu.PARALLEL,),
        )(x_hbm, i0_hbm, i1_hbm)

    return scatter(x, _sc_index_rows(idx0), _sc_index_rows(idx1))


def _moe(h, hb, wg_r, bg_r, we_r, be_r, w_gate, w_up, w_down, layer, ln_g, ln_b):
    n, d = h.shape
    pad = LANES - MOE_GROUPS - MOE_EXPERTS
    w_r = jnp.pad(jnp.concatenate([wg_r, we_r], 1), ((0, 0), (0, pad)))
    w_hi = w_r.astype(BF16)
    w_lo = (w_r - w_hi.astype(F32)).astype(BF16)
    b_r = jnp.pad(jnp.concatenate([bg_r, be_r]), (0, pad)).reshape(1, LANES)
    route, cnt = _router(h, jnp.stack([w_hi, w_lo]), b_r, tm=512)
    counts = cnt[0, :MOE_EXPERTS].astype(jnp.int32)
    padded = (counts + MOE_BLOCK - 1) // MOE_BLOCK * MOE_BLOCK
    pends = jnp.cumsum(padded)
    pstarts = pends - padded
    eid = route[:, R_EID:R_EID + MOE_TOPK].astype(jnp.int32)
    rank = route[:, R_RANK:R_RANK + MOE_TOPK].astype(jnp.int32)
    experts = jnp.arange(MOE_EXPERTS, dtype=jnp.int32)
    dest = jnp.sum(jnp.where(eid[..., None] == experts, pstarts, 0), -1) + rank
    p_rows = n * MOE_TOPK + MOE_EXPERTS * MOE_BLOCK
    nb = p_rows // MOE_BLOCK
    blk_start = jnp.arange(nb, dtype=jnp.int32) * MOE_BLOCK
    blk_e = jnp.minimum(jnp.sum((pends[None, :] <= blk_start[:, None]).astype(jnp.int32), -1), MOE_EXPERTS - 1)
    dest0, dest1 = dest[:, 0], dest[:, 1]
    xb = _sc_scatter_rows(h, dest0, dest1, p_rows)
    yb = _experts(blk_e, xb, w_gate, w_up, w_down, layer)
    return _combine_ln(h, _sc_gather_rows(yb, dest0), _sc_gather_rows(yb, dest1), route, ln_g, ln_b, tm=512)


def kernel(x, positions, ln1_g, ln1_b, ln2_g, ln2_b, a_w_in, a_w_out, b_w_in, b_gate_bias, b_conv_w, b_conv_b,
           b_norm_g, b_w_out, c_w_in, c_norm_g, c_w_out, r_group_w, r_group_b, r_expert_w, r_expert_b,
           e_w_gate, e_w_up, e_w_down):
    batch, seq, d = x.shape
    n = batch * seq
    tabs_a = _rope_tables_a(positions)
    inv_c = C_THETA ** (-jnp.arange(0, C_QK_DIM, 2, dtype=F32) / C_QK_DIM)
    ang_c = positions.astype(F32)[:, None] * inv_c[None, :]
    cos_c, sin_c = jnp.cos(ang_c), jnp.sin(ang_c)
    h = x.reshape(n, d)
    hb = h.astype(BF16)
    for i in range(DEPTH):
        kind, j = i % 3, i // 3
        if kind == 0:
            y = _mixer_dilated(hb, a_w_in[j], tabs_a, batch, seq)
            w_out = a_w_out[j]
        elif kind == 1:
            y = _mixer_mlstm(hb, b_w_in[j], b_gate_bias[j], b_conv_w[j], b_conv_b[j], b_norm_g[j], batch, seq)
            w_out = b_w_out[j]
        else:
            y = _mixer_retention(hb, c_w_in[j], c_norm_g[j], cos_c, sin_c, batch, seq)
            w_out = c_w_out[j]
        h, hb = _matmul_res_ln(y, w_out.astype(BF16), h, ln1_g[i], ln1_b[i], tm=512)
        h, hb = _moe(h, hb, r_group_w[i], r_group_b[i], r_expert_w[i], r_expert_b[i],
                     e_w_gate, e_w_up, e_w_down, i, ln2_g[i], ln2_b[i])
    return h.reshape(batch, seq, d)
```

```python
import functools

import jax
import jax.numpy as jnp
from jax import lax
from jax.experimental import pallas as pl
from jax.experimental.pallas import tpu as pltpu
from jax.experimental.pallas import tpu_sc as plsc

F32 = jnp.float32
BF16 = jnp.bfloat16

D_MODEL = 1024
DEPTH = 4
DN_ALPHA = (2.0 * DEPTH) ** 0.25
LN_EPS = 1e-5

A_HEADS = 16
A_HEAD_DIM = 64
A_DILATIONS = (1, 4, 16)
A_BLOCK = 128
A_UNROLL = 16
A_GROUP = 16
A_PITCH = 20
LOG2_E = 1.4426950408889634
A_ROT_DIM = 16
ROPE_THETA = 500000.0

B_HEADS = 8
B_QK_DIM = 64
B_V_DIM = 128
B_CONV = 4
B_CHUNK = 256

C_HEADS = 4
C_QK_DIM = 256
C_V_DIM = 512
C_CHUNK = 256
C_THETA = 10000.0

MOE_GROUPS = 8
MOE_PER_GROUP = 8
MOE_EXPERTS = 64
MOE_TOPK = 2
MOE_HIDDEN = 256
MOE_BLOCK = 256
SC_ROWS = 64
SC_INDEX_LANES = 128

LANES = 128
NEG = -1e30
VMEM_LIMIT = 48 * 1024 * 1024


def _params(*sem):
    return pltpu.CompilerParams(dimension_semantics=sem, vmem_limit_bytes=VMEM_LIMIT)


def _mm_kernel(x_ref, w_ref, o_ref):
    o_ref[...] = jnp.dot(x_ref[...], w_ref[...], preferred_element_type=F32).astype(o_ref.dtype)


def _matmul(x, w, *, tm, tn, out_dtype=F32):
    n, k = x.shape
    m = w.shape[1]
    return pl.pallas_call(
        _mm_kernel,
        grid=(n // tm, m // tn),
        in_specs=[pl.BlockSpec((tm, k), lambda i, j: (i, 0)),
                  pl.BlockSpec((k, tn), lambda i, j: (0, j))],
        out_specs=pl.BlockSpec((tm, tn), lambda i, j: (i, j)),
        out_shape=jax.ShapeDtypeStruct((n, m), out_dtype),
        compiler_params=_params("parallel", "parallel"),
        name="matmul",
    )(x, w)


def _layer_norm_rows(z, g, b):
    mu = jnp.mean(z, -1, keepdims=True)
    zc = z - mu
    var = jnp.mean(zc * zc, -1, keepdims=True)
    return zc * lax.rsqrt(var + LN_EPS) * g + b


def _pack_pairs(x):
    c = x.shape[1] // 2
    hi = pltpu.bitcast(x[:, :c].astype(BF16).astype(F32), jnp.uint32)
    lo = pltpu.bitcast(x[:, c:].astype(BF16).astype(F32), jnp.uint32)
    return pltpu.bitcast(hi | (lo >> 16), F32)


def _unpack_pairs(w):
    bits = pltpu.bitcast(w, jnp.uint32)
    hi = pltpu.bitcast(bits & jnp.uint32(0xFFFF0000), F32)
    lo = pltpu.bitcast(bits << 16, F32)
    return jnp.concatenate([hi, lo], axis=1)


def _mm_res_ln_kernel(x_ref, w_ref, h_ref, g_ref, b_ref, o_ref, op_ref):
    y = jnp.dot(x_ref[...], w_ref[...], preferred_element_type=F32)
    out = _layer_norm_rows(DN_ALPHA * h_ref[...] + y, g_ref[...], b_ref[...])
    o_ref[...] = out
    op_ref[...] = _pack_pairs(out)


def _matmul_res_ln(x, w, h, g, b, *, tm):
    n, k = x.shape
    d = w.shape[1]
    return pl.pallas_call(
        _mm_res_ln_kernel,
        grid=(n // tm,),
        in_specs=[pl.BlockSpec((tm, k), lambda i: (i, 0)),
                  pl.BlockSpec((k, d), lambda i: (0, 0)),
                  pl.BlockSpec((tm, d), lambda i: (i, 0)),
                  pl.BlockSpec((1, d), lambda i: (0, 0)),
                  pl.BlockSpec((1, d), lambda i: (0, 0))],
        out_specs=[pl.BlockSpec((tm, d), lambda i: (i, 0)),
                   pl.BlockSpec((tm, d // 2), lambda i: (i, 0))],
        out_shape=[jax.ShapeDtypeStruct((n, d), F32), jax.ShapeDtypeStruct((n, d // 2), F32)],
        compiler_params=_params("parallel"),
        name="matmul_res_ln",
    )(x, w, h, g.reshape(1, d), b.reshape(1, d))


def _proj_a_kernel(x_ref, w_ref, c_ref, s1_ref, s2_ref, q_ref, k_ref, v_ref):
    x = x_ref[...]
    width = 2 * LANES
    for c, ref in ((0, q_ref), (1, k_ref), (2, v_ref)):
        for j in range(D_MODEL // width):
            col = c * D_MODEL + j * width
            y = jnp.dot(x, w_ref[:, col:col + width], preferred_element_type=F32)
            if c < 2:
                half = A_ROT_DIM // 2
                y = (y * c_ref[...] + pltpu.roll(y, width - half, 1) * s1_ref[...]
                     + pltpu.roll(y, half, 1) * s2_ref[...])
            if c == 0:
                y = y * (LOG2_E * A_HEAD_DIM ** -0.5)
            ref[0, 2 * j] = y[:, :LANES]
            ref[0, 2 * j + 1] = y[:, LANES:]


def _proj_a(xb, w, tabs, batch, seq, *, tm):
    n, d = xb.shape
    spb = seq // tm
    hp = D_MODEL // LANES
    qkv_shape = jax.ShapeDtypeStruct((batch, hp, seq, LANES), F32)
    out_spec = pl.BlockSpec((1, hp, tm, LANES), lambda i: (i // spb, 0, i % spb, 0))
    tab_spec = pl.BlockSpec((tm, 2 * LANES), lambda i: (i % spb, 0))
    return pl.pallas_call(
        _proj_a_kernel,
        grid=(n // tm,),
        in_specs=[pl.BlockSpec((tm, d), lambda i: (i, 0)),
                  pl.BlockSpec((d, 3 * d), lambda i: (0, 0)),
                  tab_spec, tab_spec, tab_spec],
        out_specs=[out_spec, out_spec, out_spec],
        out_shape=[qkv_shape, qkv_shape, qkv_shape],
        compiler_params=_params("parallel"),
        name="proj_a",
    )(xb, w, *tabs)


def _rope_tables_a(positions):
    half = A_ROT_DIM // 2
    inv = ROPE_THETA ** (-jnp.arange(0, A_ROT_DIM, 2, dtype=F32) / A_ROT_DIM)
    ang = positions.astype(F32)[:, None] * inv[None, :]
    cos, sin = jnp.cos(ang), jnp.sin(ang)
    s = positions.shape[0]
    pad = jnp.zeros((s, A_HEAD_DIM - A_ROT_DIM), F32)
    c_head = jnp.concatenate([cos, cos, pad + 1.0], -1)
    s1_head = jnp.concatenate([-sin, jnp.zeros_like(sin), pad], -1)
    s2_head = jnp.concatenate([jnp.zeros_like(sin), sin, pad], -1)
    reps = 2 * LANES // A_HEAD_DIM
    return tuple(jnp.tile(t, (1, reps)) for t in (c_head, s1_head, s2_head))


def _attn_blocks(q_ref, k_ref, v_ref, o_ref, lse_ref, bias_ref, hmask, head0, first_block, d, pitch):
    nk = 2 * A_BLOCK
    loaded = []
    for u in range(A_UNROLL):
        g = first_block + u
        r = g % d
        n = g // d
        qstart = n * (A_BLOCK * pitch) + r
        kstart = jnp.maximum(qstart - A_BLOCK * pitch, r)
        if pitch == 1:
            qstart = pl.multiple_of(qstart, A_BLOCK)
            kstart = pl.multiple_of(kstart, A_BLOCK)
            qsl, ksl = pl.ds(qstart, A_BLOCK), pl.ds(kstart, nk)
        else:
            qsl, ksl = pl.ds(qstart, A_BLOCK, stride=pitch), pl.ds(kstart, nk, stride=pitch)
        bias = bias_ref[jnp.minimum(n, 1)]
        loaded.append((qsl, q_ref[qsl, :].astype(BF16), k_ref[ksl, :].astype(BF16), v_ref[ksl, :].astype(BF16), bias))
    results = []
    for qsl, qb, kb, vb, bias in loaded:
        outs, lses = [], []
        for h in range(2):
            s = lax.dot_general(qb * hmask[h], kb, (((1,), (1,)), ((), ())), preferred_element_type=F32) + bias
            m = jnp.max(s, -1, keepdims=True)
            p = jnp.exp2(s - m)
            l = jnp.sum(p, -1, keepdims=True)
            outs.append(jnp.dot(p.astype(BF16), vb, preferred_element_type=F32) * (1.0 / l))
            lses.append(jnp.broadcast_to(m + jnp.log2(l), (A_BLOCK, LANES)))
        results.append((qsl, jnp.where(head0, outs[0], outs[1]), jnp.where(head0, lses[0], lses[1])))
    for qsl, out, lse in results:
        o_ref[qsl, :] = out
        lse_ref[qsl, :] = lse


def _attn_kernel(q_ref, k_ref, v_ref, o_ref, q16_ref, k16_ref, v16_ref, ob_ref, lb_ref, o16_ref, l16_ref,
                 bias_ref):
    seq = q_ref.shape[0]
    groups = seq // A_GROUP
    head0 = lax.broadcasted_iota(jnp.int32, (A_BLOCK, LANES), 1) < A_HEAD_DIM
    hmask = [jnp.where(head0, 1.0, 0.0).astype(BF16), jnp.where(head0, 0.0, 1.0).astype(BF16)]
    qi = lax.broadcasted_iota(jnp.int32, (A_BLOCK, 2 * A_BLOCK), 0)
    kj = lax.broadcasted_iota(jnp.int32, (A_BLOCK, 2 * A_BLOCK), 1)
    bias_ref[0] = jnp.where(kj <= qi, 0.0, NEG).astype(F32)
    bias_ref[1] = jnp.where((kj >= qi) & (kj <= qi + A_BLOCK), 0.0, NEG).astype(F32)

    def spread(g, carry):
        src = pl.ds(pl.multiple_of(g * A_GROUP, A_GROUP), A_GROUP)
        dst = pl.ds(pl.multiple_of(g * A_PITCH, 4), A_GROUP)
        q16_ref[dst, :] = q_ref[src, :]
        k16_ref[dst, :] = k_ref[src, :]
        v16_ref[dst, :] = v_ref[src, :]
        return carry

    lax.fori_loop(0, groups, spread, 0, unroll=8)

    branches = ((1, 1, q_ref, k_ref, v_ref, ob_ref.at[0], lb_ref.at[0]),
                (4, 4, q_ref, k_ref, v_ref, ob_ref.at[1], lb_ref.at[1]),
                (16, A_PITCH, q16_ref, k16_ref, v16_ref, o16_ref, l16_ref))
    for d, pitch, qr, kr, vr, orf, lrf in branches:

        def body(it, carry, d=d, pitch=pitch, qr=qr, kr=kr, vr=vr, orf=orf, lrf=lrf):
            _attn_blocks(qr, kr, vr, orf, lrf, bias_ref, hmask, head0, it * A_UNROLL, d, pitch)
            return carry

        lax.fori_loop(0, seq // (A_BLOCK * A_UNROLL), body, 0)

    def mix(g, carry):
        nat = pl.ds(pl.multiple_of(g * A_GROUP, A_GROUP), A_GROUP)
        pad = pl.ds(pl.multiple_of(g * A_PITCH, 4), A_GROUP)
        o0, o1, o2 = ob_ref[0, nat, :], ob_ref[1, nat, :], o16_ref[pad, :]
        l0, l1, l2 = lb_ref[0, nat, :], lb_ref[1, nat, :], l16_ref[pad, :]
        mx = jnp.maximum(jnp.maximum(l0, l1), l2)
        w0, w1, w2 = jnp.exp2(l0 - mx), jnp.exp2(l1 - mx), jnp.exp2(l2 - mx)
        o_ref[nat, :] = ((w0 * o0 + w1 * o1 + w2 * o2) / (w0 + w1 + w2)).astype(o_ref.dtype)
        return carry

    lax.fori_loop(0, groups, mix, 0, unroll=8)


def _attention(q, k, v):
    batch, hp, seq, _ = q.shape
    assert seq % (2 * A_BLOCK * max(A_DILATIONS)) == 0 and seq % (A_BLOCK * A_UNROLL) == 0
    in_spec = pl.BlockSpec((None, None, seq, LANES), lambda b, p: (b, p, 0, 0))
    padded = seq // A_GROUP * A_PITCH
    return pl.pallas_call(
        _attn_kernel,
        grid=(batch, hp),
        in_specs=[in_spec, in_spec, in_spec],
        out_specs=pl.BlockSpec((None, seq, LANES), lambda b, p: (b, 0, p)),
        out_shape=jax.ShapeDtypeStruct((batch, seq, hp * LANES), BF16),
        scratch_shapes=[pltpu.VMEM((padded, LANES), F32)] * 3
        + [pltpu.VMEM((2, seq, LANES), F32)] * 2
        + [pltpu.VMEM((padded, LANES), F32)] * 2
        + [pltpu.VMEM((2, A_BLOCK, 2 * A_BLOCK), F32)],
        compiler_params=_params("parallel", "parallel"),
        name="dilated_attention",
    )(q, k, v)


def _mixer_dilated(hb, w_in, tabs, batch, seq):
    q, k, v = _proj_a(hb, w_in.astype(BF16), tabs, batch, seq, tm=512)
    o = _attention(q, k, v)
    return o.reshape(batch * seq, D_MODEL)


def _mlstm_kernel(qk_ref, v_ref, o_ref, gc_ref, gb_ref, cw_ref, cb_ref, ng_ref, out_ref,
                  ext_ref, c_ref, n_ref, m_ref):
    L = B_CHUNK
    chunk = pl.program_id(1)

    @pl.when(chunk == 0)
    def _():
        ext_ref[0:8, :] = jnp.zeros((8, D_MODEL), F32)
        c_ref[...] = jnp.zeros(c_ref.shape, F32)
        n_ref[...] = jnp.zeros(n_ref.shape, F32)
        m_ref[...] = jnp.zeros(m_ref.shape, F32)

    u = qk_ref[...].astype(F32)
    ext_ref[8:8 + L, :] = u
    conv = u * cw_ref[B_CONV - 1:B_CONV, :] + cb_ref[...]
    for j in range(1, B_CONV):
        conv = conv + ext_ref[pl.ds(8 - j, L), :] * cw_ref[B_CONV - 1 - j:B_CONV - j, :]
    ext_ref[0:8, :] = u[L - 8:, :]
    qk = conv * jax.nn.sigmoid(conv)
    half = D_MODEL // 2

    gc = gc_ref[...] + gb_ref[...]
    gr = gc.T
    i_col, i_row = gc, gr[:B_HEADS, :]
    lf_col = jax.nn.log_sigmoid(gc)
    lf_row = jax.nn.log_sigmoid(gr[B_HEADS:2 * B_HEADS, :])
    ti = lax.broadcasted_iota(jnp.int32, (L, L), 0)
    si = lax.broadcasted_iota(jnp.int32, (L, L), 1)
    causal = ti >= si
    tri = causal.astype(F32)
    a_col = jnp.dot(tri, lf_col, preferred_element_type=F32, precision=lax.Precision.HIGHEST)
    a_row = lax.dot_general(lf_row, tri, (((1,), (1,)), ((), ())), preferred_element_type=F32,
                            precision=lax.Precision.HIGHEST)
    lane = lax.broadcasted_iota(jnp.int32, (1, LANES), 1)
    lane_h0 = lane < B_QK_DIM
    col_h0 = lax.broadcasted_iota(jnp.int32, (1, 2 * B_V_DIM), 1) < B_V_DIM

    for p in range(B_HEADS // 2):
        qp = qk[:, p * LANES:(p + 1) * LANES]
        kp = qk[:, half + p * LANES:half + (p + 1) * LANES] * (B_QK_DIM ** -0.5)
        vp = v_ref[:, p * 2 * B_V_DIM:(p + 1) * 2 * B_V_DIM]
        kpb = kp.astype(BF16)
        vpb = vp.astype(BF16)
        c_old = c_ref[p]
        c_oldb = c_old.astype(BF16)
        n_old = n_ref[p]
        m_pair = m_ref[p]
        ws_cols, decays, m_news = [], [], []
        for hh in range(2):
            h = 2 * p + hh
            m_old = m_pair[:, hh * B_QK_DIM:hh * B_QK_DIM + 1]
            ac, ar = a_col[:, B_HEADS + h:B_HEADS + h + 1], a_row[h:h + 1, :]
            ic, ir = i_col[:, h:h + 1], i_row[h:h + 1, :]
            dmat = jnp.where(causal, ac - ar + ir, NEG)
            inter = ac + m_old
            m_t = jnp.maximum(inter, jnp.max(dmat, -1, keepdims=True))
            qm = jnp.where(lane_h0 if hh == 0 else jnp.logical_not(lane_h0), qp, 0.0)
            qmb = qm.astype(BF16)
            sc = lax.dot_general(qmb, kpb, (((1,), (1,)), ((), ())), preferred_element_type=F32)
            sc = sc * jnp.exp(dmat - m_t)
            g_inter = jnp.exp(inter - m_t)
            vh = vpb[:, hh * B_V_DIM:(hh + 1) * B_V_DIM]
            qc = jnp.dot(qmb, c_oldb, preferred_element_type=F32)[:, hh * B_V_DIM:(hh + 1) * B_V_DIM]
            num = jnp.dot(sc.astype(BF16), vh, preferred_element_type=F32) + g_inter * qc
            den = jnp.sum(sc, -1, keepdims=True) + g_inter * jnp.sum(qm * n_old, -1, keepdims=True)
            h_out = num / jnp.maximum(jnp.abs(den), jnp.exp(-m_t))
            mu = jnp.mean(h_out, -1, keepdims=True)
            hc = h_out - mu
            var = jnp.mean(hc * hc, -1, keepdims=True)
            cols = slice(h * B_V_DIM, (h + 1) * B_V_DIM)
            hn = hc * lax.rsqrt(var + LN_EPS) * ng_ref[:, cols]
            out_ref[:, cols] = (hn * jax.nn.sigmoid(o_ref[:, cols].astype(F32))).astype(out_ref.dtype)
            a_end = ac[L - 1:L, :]
            w_col = a_end - ac + ic
            m_new = jnp.maximum(a_end + m_old, jnp.max(w_col, 0, keepdims=True))
            decays.append(jnp.exp(a_end + m_old - m_new))
            ws_cols.append(jnp.exp(w_col - m_new))
            m_news.append(m_new)
        ws = jnp.where(lane_h0, jnp.broadcast_to(ws_cols[0], (L, LANES)), jnp.broadcast_to(ws_cols[1], (L, LANES)))
        kw = kp * ws
        dec_c = jnp.where(col_h0, jnp.broadcast_to(decays[0], (1, 2 * B_V_DIM)),
                          jnp.broadcast_to(decays[1], (1, 2 * B_V_DIM)))
        dec_n = jnp.where(lane_h0, jnp.broadcast_to(decays[0], (1, LANES)), jnp.broadcast_to(decays[1], (1, LANES)))
        c_ref[p] = dec_c * c_old + lax.dot_general(kw.astype(BF16), vpb, (((0,), (0,)), ((), ())),
                                                   preferred_element_type=F32)
        n_ref[p] = dec_n * n_old + jnp.sum(kw, 0, keepdims=True)
        m_ref[p] = jnp.where(lane_h0, jnp.broadcast_to(m_news[0], (1, LANES)), jnp.broadcast_to(m_news[1], (1, LANES)))


def _mlstm(proj, gates, gate_bias, conv_w, conv_b, norm_g, batch, seq):
    L = B_CHUNK
    d = D_MODEL
    slab = lambda c: pl.BlockSpec((None, L, d), lambda b, s, c=c: (b, s, c))
    full = lambda shape: pl.BlockSpec(shape, lambda b, s: (0,) * len(shape))
    return pl.pallas_call(
        _mlstm_kernel,
        grid=(batch, seq // L),
        in_specs=[slab(0), slab(1), slab(2),
                  pl.BlockSpec((None, L, LANES), lambda b, s: (b, s, 0)),
                  full((1, LANES)),
                  full((B_CONV, d)), full((1, d)), full((1, d))],
        out_specs=pl.BlockSpec((None, L, d), lambda b, s: (b, s, 0)),
        out_shape=jax.ShapeDtypeStruct((batch, seq, d), BF16),
        scratch_shapes=[pltpu.VMEM((L + 8, d), F32),
                        pltpu.VMEM((B_HEADS // 2, 2 * B_QK_DIM, 2 * B_V_DIM), F32),
                        pltpu.VMEM((B_HEADS // 2, 1, LANES), F32),
                        pltpu.VMEM((B_HEADS // 2, 1, LANES), F32)],
        compiler_params=_params("parallel", "arbitrary"),
        name="mlstm",
    )(proj, proj, proj, gates, jnp.pad(gate_bias, (0, LANES - 2 * B_HEADS)).reshape(1, LANES),
      conv_w, conv_b.reshape(1, d), norm_g.reshape(1, d))


def _mixer_mlstm(hb, w_in, gate_bias, conv_w, conv_b, norm_g, batch, seq):
    n = batch * seq
    main = 3 * D_MODEL
    proj = _matmul(hb, w_in[:, :main].astype(BF16), tm=1024, tn=1024, out_dtype=BF16)
    w_g = jnp.pad(w_in[:, main:], ((0, 0), (0, LANES - 2 * B_HEADS))).astype(BF16)
    gates = _matmul(hb, w_g, tm=2048, tn=LANES)
    out = _mlstm(proj.reshape(batch, seq, main), gates.reshape(batch, seq, LANES), gate_bias,
                 conv_w, conv_b, norm_g, batch, seq)
    return out.reshape(n, D_MODEL)


def _retention_kernel(lg_ref, q_ref, k_ref, v_ref, g_ref, cos_ref, sin_ref, ng_ref, out_ref,
                      r_ref, dm_ref, xi_ref, zeta_ref):
    L = C_CHUNK
    head = pl.program_id(1)
    chunk = pl.program_id(2)
    lg = lg_ref[head]

    @pl.when(chunk == 0)
    def _():
        r_ref[...] = jnp.zeros(r_ref.shape, F32)
        ti = lax.broadcasted_iota(jnp.int32, (L, L), 0)
        si = lax.broadcasted_iota(jnp.int32, (L, L), 1)
        rel = (ti - si).astype(F32)
        dm_ref[...] = jnp.where(rel >= 0, jnp.exp(jnp.maximum(rel, 0.0) * lg), 0.0)
        idx = lax.broadcasted_iota(jnp.int32, (L, LANES), 0).astype(F32)
        xi_ref[...] = jnp.exp((idx + 1.0) * lg)
        zeta_ref[...] = jnp.exp((L - 1.0 - idx) * lg)

    cos, sin = cos_ref[...], sin_ref[...]
    hd = C_QK_DIM // 2

    def rope(t):
        t1, t2 = t[:, :hd], t[:, hd:]
        return jnp.concatenate([t1 * cos - t2 * sin, t2 * cos + t1 * sin], -1)

    q = rope(q_ref[...].astype(F32))
    k = rope(k_ref[...].astype(F32)) * (C_QK_DIM ** -0.5)
    qb = q.astype(BF16)
    vb = v_ref[...].astype(BF16)
    r_old = r_ref[...]
    sc = lax.dot_general(qb, k.astype(BF16), (((1,), (1,)), ((), ())), preferred_element_type=F32) * dm_ref[...]
    o = jnp.dot(sc.astype(BF16), vb, preferred_element_type=F32)
    o = o + xi_ref[:, 0:1] * jnp.dot(qb, r_old.astype(BF16), preferred_element_type=F32)
    kz = (k * zeta_ref[:, 0:1]).astype(BF16)
    cd = jnp.exp(jnp.full((1, 1), float(L), F32) * lg)
    r_ref[...] = cd * r_old + lax.dot_general(kz, vb, (((0,), (0,)), ((), ())), preferred_element_type=F32)
    mu = jnp.mean(o, -1, keepdims=True)
    oc = o - mu
    var = jnp.mean(oc * oc, -1, keepdims=True)
    on = oc * lax.rsqrt(var + LN_EPS) * ng_ref[...]
    g = g_ref[...].astype(F32)
    out_ref[...] = (on * (g * jax.nn.sigmoid(g))).astype(out_ref.dtype)


def _retention(proj, cos, sin, norm_g, batch, seq):
    L = C_CHUNK
    log_gamma = jnp.log(1.0 - 2.0 ** (-5.0 - jnp.arange(C_HEADS, dtype=F32)))
    qk_blocks = D_MODEL // C_QK_DIM
    v_off = 2 * D_MODEL // C_V_DIM
    g_off = 4 * D_MODEL // C_V_DIM
    return pl.pallas_call(
        _retention_kernel,
        grid=(batch, C_HEADS, seq // L),
        in_specs=[pl.BlockSpec(memory_space=pltpu.SMEM),
                  pl.BlockSpec((None, L, C_QK_DIM), lambda b, h, c: (b, c, h)),
                  pl.BlockSpec((None, L, C_QK_DIM), lambda b, h, c: (b, c, qk_blocks + h)),
                  pl.BlockSpec((None, L, C_V_DIM), lambda b, h, c: (b, c, v_off + h)),
                  pl.BlockSpec((None, L, C_V_DIM), lambda b, h, c: (b, c, g_off + h)),
                  pl.BlockSpec((L, C_QK_DIM // 2), lambda b, h, c: (c, 0)),
                  pl.BlockSpec((L, C_QK_DIM // 2), lambda b, h, c: (c, 0)),
                  pl.BlockSpec((1, C_V_DIM), lambda b, h, c: (0, h))],
        out_specs=pl.BlockSpec((None, L, C_V_DIM), lambda b, h, c: (b, c, h)),
        out_shape=jax.ShapeDtypeStruct((batch, seq, 2 * D_MODEL), BF16),
        scratch_shapes=[pltpu.VMEM((C_QK_DIM, C_V_DIM), F32),
                        pltpu.VMEM((L, L), F32),
                        pltpu.VMEM((L, LANES), F32),
                        pltpu.VMEM((L, LANES), F32)],
        compiler_params=_params("parallel", "parallel", "arbitrary"),
        name="retention",
    )(log_gamma, proj, proj, proj, proj, cos, sin, norm_g.reshape(1, 2 * D_MODEL))


def _mixer_retention(hb, w_in, norm_g, cos, sin, batch, seq):
    proj = _matmul(hb, w_in.astype(BF16), tm=1024, tn=1024, out_dtype=BF16)
    out = _retention(proj.reshape(batch, seq, 6 * D_MODEL), cos, sin, norm_g, batch, seq)
    return out.reshape(batch * seq, 2 * D_MODEL)


def _expert_kernel(be_ref, x_ref, wg_ref, wu_ref, wd_ref, y_ref, wgb_ref, wub_ref, wdb_ref):
    j = pl.program_id(0)
    changed = jnp.logical_or(j == 0, be_ref[j] != be_ref[jnp.maximum(j - 1, 0)])

    @pl.when(changed)
    def _():
        wgb_ref[...] = wg_ref[...].astype(BF16)
        wub_ref[...] = wu_ref[...].astype(BF16)
        wdb_ref[...] = wd_ref[...].astype(BF16)

    x = _unpack_pairs(x_ref[...]).astype(BF16)
    a = jnp.dot(x, wgb_ref[...], preferred_element_type=F32)
    u = jnp.dot(x, wub_ref[...], preferred_element_type=F32)
    act = (a * jax.nn.sigmoid(a) * u).astype(BF16)
    y_ref[...] = _pack_pairs(jnp.dot(act, wdb_ref[...], preferred_element_type=F32))


def _experts(blk_e, xb, w_gate, w_up, w_down, layer):
    p, dw = xb.shape
    d = 2 * dw
    nb = p // MOE_BLOCK
    hid = MOE_HIDDEN
    grid_spec = pltpu.PrefetchScalarGridSpec(
        num_scalar_prefetch=1,
        grid=(nb,),
        in_specs=[pl.BlockSpec((MOE_BLOCK, dw), lambda j, be: (j, 0)),
                  pl.BlockSpec((None, None, d, hid), lambda j, be: (layer, be[j], 0, 0)),
                  pl.BlockSpec((None, None, d, hid), lambda j, be: (layer, be[j], 0, 0)),
                  pl.BlockSpec((None, None, hid, d), lambda j, be: (layer, be[j], 0, 0))],
        out_specs=pl.BlockSpec((MOE_BLOCK, dw), lambda j, be: (j, 0)),
        scratch_shapes=[pltpu.VMEM((d, hid), BF16), pltpu.VMEM((d, hid), BF16), pltpu.VMEM((hid, d), BF16)],
    )
    return pl.pallas_call(
        _expert_kernel,
        grid_spec=grid_spec,
        out_shape=jax.ShapeDtypeStruct((p, dw), F32),
        compiler_params=_params("arbitrary"),
        name="moe_experts",
    )(blk_e, xb, w_gate, w_up, w_down)


def _combine_ln_kernel(h_ref, y0_ref, y1_ref, rt_ref, g_ref, b_ref, o_ref, ob_ref):
    rt = rt_ref[...]
    y = (_unpack_pairs(y0_ref[...]) * rt[:, R_GATE:R_GATE + 1]
         + _unpack_pairs(y1_ref[...]) * rt[:, R_GATE + 1:R_GATE + 2])
    out = _layer_norm_rows(DN_ALPHA * h_ref[...] + y, g_ref[...], b_ref[...])
    o_ref[...] = out
    ob_ref[...] = out.astype(BF16)


def _combine_ln(h, y0, y1, route, g, b, *, tm):
    n, d = h.shape
    row = pl.BlockSpec((tm, d), lambda i: (i, 0))
    words = pl.BlockSpec((tm, d // 2), lambda i: (i, 0))
    vec = pl.BlockSpec((1, d), lambda i: (0, 0))
    return pl.pallas_call(
        _combine_ln_kernel,
        grid=(n // tm,),
        in_specs=[row, words, words, pl.BlockSpec((tm, LANES), lambda i: (i, 0)), vec, vec],
        out_specs=[row, row],
        out_shape=[jax.ShapeDtypeStruct((n, d), F32), jax.ShapeDtypeStruct((n, d), BF16)],
        compiler_params=_params("parallel"),
        name="moe_combine_ln",
    )(h, y0, y1, route, g.reshape(1, d), b.reshape(1, d))


R_EID, R_GATE, R_RANK = 0, 2, 4


def _router_kernel(h_ref, w_ref, b_ref, route_ref, cnt_ref, base_ref, tri_ref):
    i = pl.program_id(0)
    tm = h_ref.shape[0]

    @pl.when(i == 0)
    def _():
        base_ref[...] = jnp.zeros(base_ref.shape, F32)
        ti = lax.broadcasted_iota(jnp.int32, (tm, tm), 0)
        si = lax.broadcasted_iota(jnp.int32, (tm, tm), 1)
        tri_ref[...] = jnp.where(si < ti, 1.0, 0.0).astype(BF16)

    h = h_ref[...]
    h_hi = h.astype(BF16)
    h_lo = (h - h_hi.astype(F32)).astype(BF16)
    logits = (jnp.dot(h_hi, w_ref[0], preferred_element_type=F32)
              + jnp.dot(h_lo, w_ref[0], preferred_element_type=F32)
              + jnp.dot(h_hi, w_ref[1], preferred_element_type=F32)) + b_ref[...]
    lane = lax.broadcasted_iota(jnp.int32, (tm, LANES), 1).astype(F32)
    neg_inf = -jnp.inf
    big = float(4 * LANES)
    is_g = lane < MOE_GROUPS
    gl = jnp.where(is_g, logits, neg_inf)
    gmax = jnp.max(gl, -1, keepdims=True)
    grp = jnp.min(jnp.where(gl == gmax, lane, big), -1, keepdims=True)
    p_grp = 1.0 / jnp.sum(jnp.where(is_g, jnp.exp(logits - gmax), 0.0), -1, keepdims=True)
    lo = MOE_GROUPS + MOE_PER_GROUP * grp
    el = jnp.where((lane >= lo) & (lane < lo + MOE_PER_GROUP), logits, neg_inf)
    v1 = jnp.max(el, -1, keepdims=True)
    i1 = jnp.min(jnp.where(el == v1, lane, big), -1, keepdims=True)
    el2 = jnp.where(lane == i1, neg_inf, el)
    v2 = jnp.max(el2, -1, keepdims=True)
    i2 = jnp.min(jnp.where(el2 == v2, lane, big), -1, keepdims=True)
    t = jnp.exp(v2 - v1)
    g1 = p_grp / (1.0 + t)
    g2 = g1 * t
    e1 = i1 - MOE_GROUPS
    e2 = i2 - MOE_GROUPS
    oh1 = jnp.where(lane == e1, 1.0, 0.0)
    oh2 = jnp.where(lane == e2, 1.0, 0.0)
    oh = oh1 + oh2
    tot = base_ref[...] + jnp.dot(tri_ref[...], oh.astype(BF16), preferred_element_type=F32)
    r1 = jnp.sum(oh1 * tot, -1, keepdims=True)
    r2 = jnp.sum(oh2 * tot, -1, keepdims=True)
    new_base = base_ref[...] + jnp.sum(oh, 0, keepdims=True)
    base_ref[...] = new_base
    cnt_ref[...] = jnp.broadcast_to(new_base, cnt_ref.shape)
    route = jnp.zeros((tm, LANES), F32)
    for k, val in enumerate((e1, e2, g1, g2, r1, r2)):
        route = jnp.where(lane == float(k), val, route)
    route_ref[...] = route


def _router(h, w_r, b_r, *, tm):
    n, d = h.shape
    return pl.pallas_call(
        _router_kernel,
        grid=(n // tm,),
        in_specs=[pl.BlockSpec((tm, d), lambda i: (i, 0)),
                  pl.BlockSpec((2, d, LANES), lambda i: (0, 0, 0)),
                  pl.BlockSpec((1, LANES), lambda i: (0, 0))],
        out_specs=[pl.BlockSpec((tm, LANES), lambda i: (i, 0)),
                   pl.BlockSpec((8, LANES), lambda i: (0, 0))],
        out_shape=[jax.ShapeDtypeStruct((n, LANES), F32), jax.ShapeDtypeStruct((8, LANES), F32)],
        scratch_shapes=[pltpu.VMEM((1, LANES), F32), pltpu.VMEM((tm, tm), BF16)],
        compiler_params=_params("arbitrary"),
        name="moe_router",
    )(h, w_r, b_r)


def _sc_index_rows(idx):
    n = idx.shape[0]
    return jnp.pad(idx.reshape(n // SC_ROWS, SC_ROWS), ((0, 0), (0, SC_INDEX_LANES - SC_ROWS)))


def _sc_mesh():
    return plsc.VectorSubcoreMesh(core_axis_name="core", subcore_axis_name="subcore")


def _sc_gather_rows(y, idx):
    n = idx.shape[0]
    d = y.shape[1]

    @pl.kernel(out_type=jax.ShapeDtypeStruct((n, d), y.dtype), mesh=_sc_mesh(), scratch_types=[])
    def gather(y_hbm, i_hbm, o_hbm):
        def body(i_vmem, o_vmem):
            pltpu.sync_copy(y_hbm.at[i_vmem.at[0, pl.ds(0, SC_ROWS)]], o_vmem)

        pltpu.emit_pipeline(
            body, grid=(n // SC_ROWS,),
            in_specs=[pl.BlockSpec((1, SC_INDEX_LANES), lambda i: (i, 0))],
            out_specs=[pl.BlockSpec((SC_ROWS, d), lambda i: (i, 0))],
            core_axis_name=("core", "subcore"), dimension_semantics=(pltpu.PARALLEL,),
        )(i_hbm, o_hbm)

    return gather(y, _sc_index_rows(idx))


def _sc_scatter_rows(x, idx0, idx1, p_rows):
    n, d = x.shape

    @pl.kernel(out_type=jax.ShapeDtypeStruct((p_rows, d), x.dtype), mesh=_sc_mesh(), scratch_types=[])
    def scatter(x_hbm, i0_hbm, i1_hbm, o_hbm):
        def body(x_vmem, i0_vmem, i1_vmem):
            pltpu.sync_copy(x_vmem, o_hbm.at[i0_vmem.at[0, pl.ds(0, SC_ROWS)]])
            pltpu.sync_copy(x_vmem, o_hbm.at[i1_vmem.at[0, pl.ds(0, SC_ROWS)]])

        pltpu.emit_pipeline(
            body, grid=(n // SC_ROWS,),
            in_specs=[pl.BlockSpec((SC_ROWS, d), lambda i: (i, 0)),
                      pl.BlockSpec((1, SC_INDEX_LANES), lambda i: (i, 0)),
                      pl.BlockSpec((1, SC_INDEX_LANES), lambda i: (i, 0))],
            out_specs=[],
            core_axis_name=("core", "subcore"), dimension_semantics=(pltpu.PARALLEL,),
        )(x_hbm, i0_hbm, i1_hbm)

    return scatter(x, _sc_index_rows(idx0), _sc_index_rows(idx1))


def _moe(h, hp, wg_r, bg_r, we_r, be_r, w_gate, w_up, w_down, layer, ln_g, ln_b):
    n, d = h.shape
    pad = LANES - MOE_GROUPS - MOE_EXPERTS
    w_r = jnp.pad(jnp.concatenate([wg_r, we_r], 1), ((0, 0), (0, pad)))
    w_hi = w_r.astype(BF16)
    w_lo = (w_r - w_hi.astype(F32)).astype(BF16)
    b_r = jnp.pad(jnp.concatenate([bg_r, be_r]), (0, pad)).reshape(1, LANES)
    route, cnt = _router(h, jnp.stack([w_hi, w_lo]), b_r, tm=512)
    counts = cnt[0, :MOE_EXPERTS].astype(jnp.int32)
    padded = (counts + MOE_BLOCK - 1) // MOE_BLOCK * MOE_BLOCK
    pends = jnp.cumsum(padded)
    pstarts = pends - padded
    eid = route[:, R_EID:R_EID + MOE_TOPK].astype(jnp.int32)
    rank = route[:, R_RANK:R_RANK + MOE_TOPK].astype(jnp.int32)
    experts = jnp.arange(MOE_EXPERTS, dtype=jnp.int32)
    dest = jnp.sum(jnp.where(eid[..., None] == experts, pstarts, 0), -1) + rank
    p_rows = n * MOE_TOPK + MOE_EXPERTS * MOE_BLOCK
    nb = p_rows // MOE_BLOCK
    blk_start = jnp.arange(nb, dtype=jnp.int32) * MOE_BLOCK
    blk_e = jnp.minimum(jnp.sum((pends[None, :] <= blk_start[:, None]).astype(jnp.int32), -1), MOE_EXPERTS - 1)
    dest0, dest1 = dest[:, 0], dest[:, 1]
    xb = _sc_scatter_rows(hp, dest0, dest1, p_rows)
    yb = _experts(blk_e, xb, w_gate, w_up, w_down, layer)
    return _combine_ln(h, _sc_gather_rows(yb, dest0), _sc_gather_rows(yb, dest1), route, ln_g, ln_b, tm=512)


def kernel(x, positions, ln1_g, ln1_b, ln2_g, ln2_b, a_w_in, a_w_out, b_w_in, b_gate_bias, b_conv_w, b_conv_b,
           b_norm_g, b_w_out, c_w_in, c_norm_g, c_w_out, r_group_w, r_group_b, r_expert_w, r_expert_b,
           e_w_gate, e_w_up, e_w_down):
    batch, seq, d = x.shape
    n = batch * seq
    tabs_a = _rope_tables_a(positions)
    inv_c = C_THETA ** (-jnp.arange(0, C_QK_DIM, 2, dtype=F32) / C_QK_DIM)
    ang_c = positions.astype(F32)[:, None] * inv_c[None, :]
    cos_c, sin_c = jnp.cos(ang_c), jnp.sin(ang_c)
    h = x.reshape(n, d)
    hb = h.astype(BF16)
    for i in range(DEPTH):
        kind, j = i % 3, i // 3
        if kind == 0:
            y = _mixer_dilated(hb, a_w_in[j], tabs_a, batch, seq)
            w_out = a_w_out[j]
        elif kind == 1:
            y = _mixer_mlstm(hb, b_w_in[j], b_gate_bias[j], b_conv_w[j], b_conv_b[j], b_norm_g[j], batch, seq)
            w_out = b_w_out[j]
        else:
            y = _mixer_retention(hb, c_w_in[j], c_norm_g[j], cos_c, sin_c, batch, seq)
            w_out = c_w_out[j]
        h, hp = _matmul_res_ln(y, w_out.astype(BF16), h, ln1_g[i], ln1_b[i], tm=512)
        h, hb = _moe(h, hp,r_group_w[i], r_group_b[i], r_expert_w[i], r_expert_b[i],
                     e_w_gate, e_w_up, e_w_down, i, ln2_g[i], ln2_b[i])
    return h.reshape(batch, seq, d)
```

```python
import functools

import jax
import jax.numpy as jnp
from jax import lax
from jax.experimental import pallas as pl
from jax.experimental.pallas import tpu as pltpu
from jax.experimental.pallas import tpu_sc as plsc

F32 = jnp.float32
BF16 = jnp.bfloat16

D_MODEL = 1024
DEPTH = 4
DN_ALPHA = (2.0 * DEPTH) ** 0.25
LN_EPS = 1e-5

A_HEADS = 16
A_HEAD_DIM = 64
A_DILATIONS = (1, 4, 16)
A_BLOCK = 128
A_UNROLL = 16
A_GROUP = 16
A_PITCH = 20
LOG2_E = 1.4426950408889634
A_ROT_DIM = 16
ROPE_THETA = 500000.0

B_HEADS = 8
B_QK_DIM = 64
B_V_DIM = 128
B_CONV = 4
B_CHUNK = 256

C_HEADS = 4
C_QK_DIM = 256
C_V_DIM = 512
C_CHUNK = 256
C_THETA = 10000.0

MOE_GROUPS = 8
MOE_PER_GROUP = 8
MOE_EXPERTS = 64
MOE_TOPK = 2
MOE_HIDDEN = 256
MOE_BLOCK = 512
SC_ROWS = 64
SC_INDEX_LANES = 128

LANES = 128
NEG = -1e30
VMEM_LIMIT = 48 * 1024 * 1024


def _params(*sem):
    return pltpu.CompilerParams(dimension_semantics=sem, vmem_limit_bytes=VMEM_LIMIT)


def _mm_kernel(x_ref, w_ref, o_ref):
    o_ref[...] = jnp.dot(x_ref[...], w_ref[...], preferred_element_type=F32).astype(o_ref.dtype)


def _matmul(x, w, *, tm, tn, out_dtype=F32):
    n, k = x.shape
    m = w.shape[1]
    return pl.pallas_call(
        _mm_kernel,
        grid=(n // tm, m // tn),
        in_specs=[pl.BlockSpec((tm, k), lambda i, j: (i, 0)),
                  pl.BlockSpec((k, tn), lambda i, j: (0, j))],
        out_specs=pl.BlockSpec((tm, tn), lambda i, j: (i, j)),
        out_shape=jax.ShapeDtypeStruct((n, m), out_dtype),
        compiler_params=_params("parallel", "parallel"),
        name="matmul",
    )(x, w)


def _layer_norm_rows(z, g, b):
    mu = jnp.mean(z, -1, keepdims=True)
    zc = z - mu
    var = jnp.mean(zc * zc, -1, keepdims=True)
    return zc * lax.rsqrt(var + LN_EPS) * g + b


def _pack_pairs(x):
    c = x.shape[1] // 2
    hi = pltpu.bitcast(x[:, :c].astype(BF16).astype(F32), jnp.uint32)
    lo = pltpu.bitcast(x[:, c:].astype(BF16).astype(F32), jnp.uint32)
    return pltpu.bitcast(hi | (lo >> 16), F32)


def _unpack_pairs(w):
    bits = pltpu.bitcast(w, jnp.uint32)
    hi = pltpu.bitcast(bits & jnp.uint32(0xFFFF0000), F32)
    lo = pltpu.bitcast(bits << 16, F32)
    return jnp.concatenate([hi, lo], axis=1)


def _mm_res_ln_kernel(x_ref, w_ref, h_ref, g_ref, b_ref, o_ref, op_ref):
    y = jnp.dot(x_ref[...], w_ref[...], preferred_element_type=F32)
    out = _layer_norm_rows(DN_ALPHA * h_ref[...] + y, g_ref[...], b_ref[...])
    o_ref[...] = out
    op_ref[...] = _pack_pairs(out)


def _matmul_res_ln(x, w, h, g, b, *, tm):
    n, k = x.shape
    d = w.shape[1]
    return pl.pallas_call(
        _mm_res_ln_kernel,
        grid=(n // tm,),
        in_specs=[pl.BlockSpec((tm, k), lambda i: (i, 0)),
                  pl.BlockSpec((k, d), lambda i: (0, 0)),
                  pl.BlockSpec((tm, d), lambda i: (i, 0)),
                  pl.BlockSpec((1, d), lambda i: (0, 0)),
                  pl.BlockSpec((1, d), lambda i: (0, 0))],
        out_specs=[pl.BlockSpec((tm, d), lambda i: (i, 0)),
                   pl.BlockSpec((tm, d // 2), lambda i: (i, 0))],
        out_shape=[jax.ShapeDtypeStruct((n, d), F32), jax.ShapeDtypeStruct((n, d // 2), F32)],
        compiler_params=_params("parallel"),
        name="matmul_res_ln",
    )(x, w, h, g.reshape(1, d), b.reshape(1, d))


def _proj_a_kernel(x_ref, w_ref, c_ref, s1_ref, s2_ref, q_ref, k_ref, v_ref):
    x = x_ref[...]
    width = 2 * LANES
    for c, ref in ((0, q_ref), (1, k_ref), (2, v_ref)):
        for j in range(D_MODEL // width):
            col = c * D_MODEL + j * width
            y = jnp.dot(x, w_ref[:, col:col + width], preferred_element_type=F32)
            if c < 2:
                half = A_ROT_DIM // 2
                y = (y * c_ref[...] + pltpu.roll(y, width - half, 1) * s1_ref[...]
                     + pltpu.roll(y, half, 1) * s2_ref[...])
            if c == 0:
                y = y * (LOG2_E * A_HEAD_DIM ** -0.5)
            ref[0, 2 * j] = y[:, :LANES]
            ref[0, 2 * j + 1] = y[:, LANES:]


def _proj_a(xb, w, tabs, batch, seq, *, tm):
    n, d = xb.shape
    spb = seq // tm
    hp = D_MODEL // LANES
    qkv_shape = jax.ShapeDtypeStruct((batch, hp, seq, LANES), F32)
    out_spec = pl.BlockSpec((1, hp, tm, LANES), lambda i: (i // spb, 0, i % spb, 0))
    tab_spec = pl.BlockSpec((tm, 2 * LANES), lambda i: (i % spb, 0))
    return pl.pallas_call(
        _proj_a_kernel,
        grid=(n // tm,),
        in_specs=[pl.BlockSpec((tm, d), lambda i: (i, 0)),
                  pl.BlockSpec((d, 3 * d), lambda i: (0, 0)),
                  tab_spec, tab_spec, tab_spec],
        out_specs=[out_spec, out_spec, out_spec],
        out_shape=[qkv_shape, qkv_shape, qkv_shape],
        compiler_params=_params("parallel"),
        name="proj_a",
    )(xb, w, *tabs)


def _rope_tables_a(positions):
    half = A_ROT_DIM // 2
    inv = ROPE_THETA ** (-jnp.arange(0, A_ROT_DIM, 2, dtype=F32) / A_ROT_DIM)
    ang = positions.astype(F32)[:, None] * inv[None, :]
    cos, sin = jnp.cos(ang), jnp.sin(ang)
    s = positions.shape[0]
    pad = jnp.zeros((s, A_HEAD_DIM - A_ROT_DIM), F32)
    c_head = jnp.concatenate([cos, cos, pad + 1.0], -1)
    s1_head = jnp.concatenate([-sin, jnp.zeros_like(sin), pad], -1)
    s2_head = jnp.concatenate([jnp.zeros_like(sin), sin, pad], -1)
    reps = 2 * LANES // A_HEAD_DIM
    return tuple(jnp.tile(t, (1, reps)) for t in (c_head, s1_head, s2_head))


def _attn_blocks(q_ref, k_ref, v_ref, o_ref, lse_ref, bias_ref, hmask, head0, first_block, d, pitch):
    nk = 2 * A_BLOCK
    loaded = []
    for u in range(A_UNROLL):
        g = first_block + u
        r = g % d
        n = g // d
        qstart = n * (A_BLOCK * pitch) + r
        kstart = jnp.maximum(qstart - A_BLOCK * pitch, r)
        if pitch == 1:
            qstart = pl.multiple_of(qstart, A_BLOCK)
            kstart = pl.multiple_of(kstart, A_BLOCK)
            qsl, ksl = pl.ds(qstart, A_BLOCK), pl.ds(kstart, nk)
        else:
            qsl, ksl = pl.ds(qstart, A_BLOCK, stride=pitch), pl.ds(kstart, nk, stride=pitch)
        bias = bias_ref[jnp.minimum(n, 1)]
        loaded.append((qsl, q_ref[qsl, :].astype(BF16), k_ref[ksl, :].astype(BF16), v_ref[ksl, :].astype(BF16), bias))
    results = []
    for qsl, qb, kb, vb, bias in loaded:
        pvs, ms, ls = [], [], []
        for h in range(2):
            s = lax.dot_general(qb * hmask[h], kb, (((1,), (1,)), ((), ())), preferred_element_type=F32) + bias
            m = jnp.max(s, -1, keepdims=True)
            p = jnp.exp2(s - m)
            ms.append(jnp.broadcast_to(m, (A_BLOCK, LANES)))
            ls.append(jnp.broadcast_to(jnp.sum(p, -1, keepdims=True), (A_BLOCK, LANES)))
            pvs.append(jnp.dot(p.astype(BF16), vb, preferred_element_type=F32))
        l = jnp.where(head0, ls[0], ls[1])
        out = jnp.where(head0, pvs[0], pvs[1]) * (1.0 / l)
        results.append((qsl, out, jnp.where(head0, ms[0], ms[1]) + jnp.log2(l)))
    for qsl, out, lse in results:
        o_ref[qsl, :] = out
        lse_ref[qsl, :] = lse


def _attn_kernel(q_ref, k_ref, v_ref, o_ref, q16_ref, k16_ref, v16_ref, ob_ref, lb_ref, o16_ref, l16_ref,
                 bias_ref):
    seq = q_ref.shape[0]
    groups = seq // A_GROUP
    head0 = lax.broadcasted_iota(jnp.int32, (A_BLOCK, LANES), 1) < A_HEAD_DIM
    hmask = [jnp.where(head0, 1.0, 0.0).astype(BF16), jnp.where(head0, 0.0, 1.0).astype(BF16)]
    qi = lax.broadcasted_iota(jnp.int32, (A_BLOCK, 2 * A_BLOCK), 0)
    kj = lax.broadcasted_iota(jnp.int32, (A_BLOCK, 2 * A_BLOCK), 1)
    bias_ref[0] = jnp.where(kj <= qi, 0.0, NEG).astype(F32)
    bias_ref[1] = jnp.where((kj >= qi) & (kj <= qi + A_BLOCK), 0.0, NEG).astype(F32)

    def spread(g, carry):
        src = pl.ds(pl.multiple_of(g * A_GROUP, A_GROUP), A_GROUP)
        dst = pl.ds(pl.multiple_of(g * A_PITCH, 4), A_GROUP)
        q16_ref[dst, :] = q_ref[src, :]
        k16_ref[dst, :] = k_ref[src, :]
        v16_ref[dst, :] = v_ref[src, :]
        return carry

    lax.fori_loop(0, groups, spread, 0, unroll=8)

    branches = ((1, 1, q_ref, k_ref, v_ref, ob_ref.at[0], lb_ref.at[0]),
                (4, 4, q_ref, k_ref, v_ref, ob_ref.at[1], lb_ref.at[1]),
                (16, A_PITCH, q16_ref, k16_ref, v16_ref, o16_ref, l16_ref))
    for d, pitch, qr, kr, vr, orf, lrf in branches:

        def body(it, carry, d=d, pitch=pitch, qr=qr, kr=kr, vr=vr, orf=orf, lrf=lrf):
            _attn_blocks(qr, kr, vr, orf, lrf, bias_ref, hmask, head0, it * A_UNROLL, d, pitch)
            return carry

        lax.fori_loop(0, seq // (A_BLOCK * A_UNROLL), body, 0)

    def mix(g, carry):
        nat = pl.ds(pl.multiple_of(g * A_GROUP, A_GROUP), A_GROUP)
        pad = pl.ds(pl.multiple_of(g * A_PITCH, 4), A_GROUP)
        o0, o1, o2 = ob_ref[0, nat, :], ob_ref[1, nat, :], o16_ref[pad, :]
        l0, l1, l2 = lb_ref[0, nat, :], lb_ref[1, nat, :], l16_ref[pad, :]
        mx = jnp.maximum(jnp.maximum(l0, l1), l2)
        w0, w1, w2 = jnp.exp2(l0 - mx), jnp.exp2(l1 - mx), jnp.exp2(l2 - mx)
        o_ref[nat, :] = ((w0 * o0 + w1 * o1 + w2 * o2) / (w0 + w1 + w2)).astype(o_ref.dtype)
        return carry

    lax.fori_loop(0, groups, mix, 0, unroll=8)


def _attention(q, k, v):
    batch, hp, seq, _ = q.shape
    assert seq % (2 * A_BLOCK * max(A_DILATIONS)) == 0 and seq % (A_BLOCK * A_UNROLL) == 0
    in_spec = pl.BlockSpec((None, None, seq, LANES), lambda b, p: (b, p, 0, 0))
    padded = seq // A_GROUP * A_PITCH
    return pl.pallas_call(
        _attn_kernel,
        grid=(batch, hp),
        in_specs=[in_spec, in_spec, in_spec],
        out_specs=pl.BlockSpec((None, seq, LANES), lambda b, p: (b, 0, p)),
        out_shape=jax.ShapeDtypeStruct((batch, seq, hp * LANES), BF16),
        scratch_shapes=[pltpu.VMEM((padded, LANES), F32)] * 3
        + [pltpu.VMEM((2, seq, LANES), F32)] * 2
        + [pltpu.VMEM((padded, LANES), F32)] * 2
        + [pltpu.VMEM((2, A_BLOCK, 2 * A_BLOCK), F32)],
        compiler_params=_params("parallel", "parallel"),
        name="dilated_attention",
    )(q, k, v)


def _mixer_dilated(hb, w_in, tabs, batch, seq):
    q, k, v = _proj_a(hb, w_in.astype(BF16), tabs, batch, seq, tm=512)
    o = _attention(q, k, v)
    return o.reshape(batch * seq, D_MODEL)


def _mlstm_kernel(qk_ref, v_ref, o_ref, gc_ref, gb_ref, cw_ref, cb_ref, ng_ref, out_ref,
                  ext_ref, c_ref, n_ref, m_ref):
    L = B_CHUNK
    chunk = pl.program_id(1)

    @pl.when(chunk == 0)
    def _():
        ext_ref[0:8, :] = jnp.zeros((8, D_MODEL), F32)
        c_ref[...] = jnp.zeros(c_ref.shape, F32)
        n_ref[...] = jnp.zeros(n_ref.shape, F32)
        m_ref[...] = jnp.zeros(m_ref.shape, F32)

    u = qk_ref[...].astype(F32)
    ext_ref[8:8 + L, :] = u
    conv = u * cw_ref[B_CONV - 1:B_CONV, :] + cb_ref[...]
    for j in range(1, B_CONV):
        conv = conv + ext_ref[pl.ds(8 - j, L), :] * cw_ref[B_CONV - 1 - j:B_CONV - j, :]
    ext_ref[0:8, :] = u[L - 8:, :]
    qk = conv * jax.nn.sigmoid(conv)
    half = D_MODEL // 2

    gc = gc_ref[...] + gb_ref[...]
    gr = gc.T
    i_col, i_row = gc, gr[:B_HEADS, :]
    lf_col = jax.nn.log_sigmoid(gc)
    lf_row = jax.nn.log_sigmoid(gr[B_HEADS:2 * B_HEADS, :])
    ti = lax.broadcasted_iota(jnp.int32, (L, L), 0)
    si = lax.broadcasted_iota(jnp.int32, (L, L), 1)
    causal = ti >= si
    tri = causal.astype(F32)
    a_col = jnp.dot(tri, lf_col, preferred_element_type=F32, precision=lax.Precision.HIGHEST)
    a_row = lax.dot_general(lf_row, tri, (((1,), (1,)), ((), ())), preferred_element_type=F32,
                            precision=lax.Precision.HIGHEST)
    lane = lax.broadcasted_iota(jnp.int32, (1, LANES), 1)
    lane_h0 = lane < B_QK_DIM
    col_h0 = lax.broadcasted_iota(jnp.int32, (1, 2 * B_V_DIM), 1) < B_V_DIM

    for p in range(B_HEADS // 2):
        qp = qk[:, p * LANES:(p + 1) * LANES]
        kp = qk[:, half + p * LANES:half + (p + 1) * LANES] * (B_QK_DIM ** -0.5)
        vp = v_ref[:, p * 2 * B_V_DIM:(p + 1) * 2 * B_V_DIM]
        kpb = kp.astype(BF16)
        vpb = vp.astype(BF16)
        c_old = c_ref[p]
        c_oldb = c_old.astype(BF16)
        n_old = n_ref[p]
        m_pair = m_ref[p]
        ws_cols, decays, m_news = [], [], []
        for hh in range(2):
            h = 2 * p + hh
            m_old = m_pair[:, hh * B_QK_DIM:hh * B_QK_DIM + 1]
            ac, ar = a_col[:, B_HEADS + h:B_HEADS + h + 1], a_row[h:h + 1, :]
            ic, ir = i_col[:, h:h + 1], i_row[h:h + 1, :]
            dmat = jnp.where(causal, ac - ar + ir, NEG)
            inter = ac + m_old
            m_t = jnp.maximum(inter, jnp.max(dmat, -1, keepdims=True))
            qm = jnp.where(lane_h0 if hh == 0 else jnp.logical_not(lane_h0), qp, 0.0)
            qmb = qm.astype(BF16)
            sc = lax.dot_general(qmb, kpb, (((1,), (1,)), ((), ())), preferred_element_type=F32)
            sc = sc * jnp.exp(dmat - m_t)
            g_inter = jnp.exp(inter - m_t)
            vh = vpb[:, hh * B_V_DIM:(hh + 1) * B_V_DIM]
            qc = jnp.dot(qmb, c_oldb, preferred_element_type=F32)[:, hh * B_V_DIM:(hh + 1) * B_V_DIM]
            num = jnp.dot(sc.astype(BF16), vh, preferred_element_type=F32) + g_inter * qc
            den = jnp.sum(sc, -1, keepdims=True) + g_inter * jnp.sum(qm * n_old, -1, keepdims=True)
            h_out = num / jnp.maximum(jnp.abs(den), jnp.exp(-m_t))
            mu = jnp.mean(h_out, -1, keepdims=True)
            hc = h_out - mu
            var = jnp.mean(hc * hc, -1, keepdims=True)
            cols = slice(h * B_V_DIM, (h + 1) * B_V_DIM)
            hn = hc * lax.rsqrt(var + LN_EPS) * ng_ref[:, cols]
            out_ref[:, cols] = (hn * jax.nn.sigmoid(o_ref[:, cols].astype(F32))).astype(out_ref.dtype)
            a_end = ac[L - 1:L, :]
            w_col = a_end - ac + ic
            m_new = jnp.maximum(a_end + m_old, jnp.max(w_col, 0, keepdims=True))
            decays.append(jnp.exp(a_end + m_old - m_new))
            ws_cols.append(jnp.exp(w_col - m_new))
            m_news.append(m_new)
        ws = jnp.where(lane_h0, jnp.broadcast_to(ws_cols[0], (L, LANES)), jnp.broadcast_to(ws_cols[1], (L, LANES)))
        kw = kp * ws
        dec_c = jnp.where(col_h0, jnp.broadcast_to(decays[0], (1, 2 * B_V_DIM)),
                          jnp.broadcast_to(decays[1], (1, 2 * B_V_DIM)))
        dec_n = jnp.where(lane_h0, jnp.broadcast_to(decays[0], (1, LANES)), jnp.broadcast_to(decays[1], (1, LANES)))
        c_ref[p] = dec_c * c_old + lax.dot_general(kw.astype(BF16), vpb, (((0,), (0,)), ((), ())),
                                                   preferred_element_type=F32)
        n_ref[p] = dec_n * n_old + jnp.sum(kw, 0, keepdims=True)
        m_ref[p] = jnp.where(lane_h0, jnp.broadcast_to(m_news[0], (1, LANES)), jnp.broadcast_to(m_news[1], (1, LANES)))


def _mlstm(proj, gates, gate_bias, conv_w, conv_b, norm_g, batch, seq):
    L = B_CHUNK
    d = D_MODEL
    slab = lambda c: pl.BlockSpec((None, L, d), lambda b, s, c=c: (b, s, c))
    full = lambda shape: pl.BlockSpec(shape, lambda b, s: (0,) * len(shape))
    return pl.pallas_call(
        _mlstm_kernel,
        grid=(batch, seq // L),
        in_specs=[slab(0), slab(1), slab(2),
                  pl.BlockSpec((None, L, LANES), lambda b, s: (b, s, 0)),
                  full((1, LANES)),
                  full((B_CONV, d)), full((1, d)), full((1, d))],
        out_specs=pl.BlockSpec((None, L, d), lambda b, s: (b, s, 0)),
        out_shape=jax.ShapeDtypeStruct((batch, seq, d), BF16),
        scratch_shapes=[pltpu.VMEM((L + 8, d), F32),
                        pltpu.VMEM((B_HEADS // 2, 2 * B_QK_DIM, 2 * B_V_DIM), F32),
                        pltpu.VMEM((B_HEADS // 2, 1, LANES), F32),
                        pltpu.VMEM((B_HEADS // 2, 1, LANES), F32)],
        compiler_params=_params("parallel", "arbitrary"),
        name="mlstm",
    )(proj, proj, proj, gates, jnp.pad(gate_bias, (0, LANES - 2 * B_HEADS)).reshape(1, LANES),
      conv_w, conv_b.reshape(1, d), norm_g.reshape(1, d))


def _mixer_mlstm(hb, w_in, gate_bias, conv_w, conv_b, norm_g, batch, seq):
    n = batch * seq
    main = 3 * D_MODEL
    proj = _matmul(hb, w_in[:, :main].astype(BF16), tm=1024, tn=1024, out_dtype=BF16)
    w_g = jnp.pad(w_in[:, main:], ((0, 0), (0, LANES - 2 * B_HEADS))).astype(BF16)
    gates = _matmul(hb, w_g, tm=2048, tn=LANES)
    out = _mlstm(proj.reshape(batch, seq, main), gates.reshape(batch, seq, LANES), gate_bias,
                 conv_w, conv_b, norm_g, batch, seq)
    return out.reshape(n, D_MODEL)


def _retention_kernel(lg_ref, q_ref, k_ref, v_ref, g_ref, cos_ref, sin_ref, ng_ref, out_ref,
                      r_ref, dm_ref, xi_ref, zeta_ref):
    L = C_CHUNK
    chunk = pl.program_id(1)

    @pl.when(chunk == 0)
    def _():
        r_ref[...] = jnp.zeros(r_ref.shape, F32)
        ti = lax.broadcasted_iota(jnp.int32, (L, L), 0)
        si = lax.broadcasted_iota(jnp.int32, (L, L), 1)
        rel = (ti - si).astype(F32)
        idx = lax.broadcasted_iota(jnp.int32, (L, LANES), 0).astype(F32)
        for h in range(C_HEADS):
            lg = lg_ref[h]
            dm_ref[h] = jnp.where(rel >= 0, jnp.exp(jnp.maximum(rel, 0.0) * lg), 0.0)
            xi_ref[h] = jnp.exp((idx + 1.0) * lg)
            zeta_ref[h] = jnp.exp((L - 1.0 - idx) * lg)

    cos, sin = cos_ref[...], sin_ref[...]
    hd = C_QK_DIM // 2

    def rope(t):
        t1, t2 = t[:, :hd], t[:, hd:]
        return jnp.concatenate([t1 * cos - t2 * sin, t2 * cos + t1 * sin], -1)

    for h in range(C_HEADS):
        qk_cols = slice(h * C_QK_DIM, (h + 1) * C_QK_DIM)
        v_cols = slice(h * C_V_DIM, (h + 1) * C_V_DIM)
        q = rope(q_ref[:, qk_cols].astype(F32))
        k = rope(k_ref[:, qk_cols].astype(F32)) * (C_QK_DIM ** -0.5)
        qb = q.astype(BF16)
        vb = v_ref[:, v_cols]
        r_old = r_ref[h]
        sc = lax.dot_general(qb, k.astype(BF16), (((1,), (1,)), ((), ())), preferred_element_type=F32) * dm_ref[h]
        o = jnp.dot(sc.astype(BF16), vb, preferred_element_type=F32)
        o = o + xi_ref[h, :, 0:1] * jnp.dot(qb, r_old.astype(BF16), preferred_element_type=F32)
        kz = (k * zeta_ref[h, :, 0:1]).astype(BF16)
        cd = jnp.exp(jnp.full((1, 1), float(L), F32) * lg_ref[h])
        r_ref[h] = cd * r_old + lax.dot_general(kz, vb, (((0,), (0,)), ((), ())), preferred_element_type=F32)
        mu = jnp.mean(o, -1, keepdims=True)
        oc = o - mu
        var = jnp.mean(oc * oc, -1, keepdims=True)
        on = oc * lax.rsqrt(var + LN_EPS) * ng_ref[:, v_cols]
        g = g_ref[:, v_cols].astype(F32)
        out_ref[:, v_cols] = (on * (g * jax.nn.sigmoid(g))).astype(out_ref.dtype)


def _retention(proj, cos, sin, norm_g, batch, seq):
    L = C_CHUNK
    d = D_MODEL
    log_gamma = jnp.log(1.0 - 2.0 ** (-5.0 - jnp.arange(C_HEADS, dtype=F32)))
    return pl.pallas_call(
        _retention_kernel,
        grid=(batch, seq // L),
        in_specs=[pl.BlockSpec(memory_space=pltpu.SMEM),
                  pl.BlockSpec((None, L, d), lambda b, c: (b, c, 0)),
                  pl.BlockSpec((None, L, d), lambda b, c: (b, c, 1)),
                  pl.BlockSpec((None, L, 2 * d), lambda b, c: (b, c, 1)),
                  pl.BlockSpec((None, L, 2 * d), lambda b, c: (b, c, 2)),
                  pl.BlockSpec((L, C_QK_DIM // 2), lambda b, c: (c, 0)),
                  pl.BlockSpec((L, C_QK_DIM // 2), lambda b, c: (c, 0)),
                  pl.BlockSpec((1, 2 * d), lambda b, c: (0, 0))],
        out_specs=pl.BlockSpec((None, L, 2 * d), lambda b, c: (b, c, 0)),
        out_shape=jax.ShapeDtypeStruct((batch, seq, 2 * d), BF16),
        scratch_shapes=[pltpu.VMEM((C_HEADS, C_QK_DIM, C_V_DIM), F32),
                        pltpu.VMEM((C_HEADS, L, L), F32),
                        pltpu.VMEM((C_HEADS, L, LANES), F32),
                        pltpu.VMEM((C_HEADS, L, LANES), F32)],
        compiler_params=_params("parallel", "arbitrary"),
        name="retention",
    )(log_gamma, proj, proj, proj, proj, cos, sin, norm_g.reshape(1, 2 * d))


def _mixer_retention(hb, w_in, norm_g, cos, sin, batch, seq):
    proj = _matmul(hb, w_in.astype(BF16), tm=1024, tn=1024, out_dtype=BF16)
    out = _retention(proj.reshape(batch, seq, 6 * D_MODEL), cos, sin, norm_g, batch, seq)
    return out.reshape(batch * seq, 2 * D_MODEL)


def _expert_kernel(be_ref, x_ref, wg_ref, wu_ref, wd_ref, y_ref, wgb_ref, wub_ref, wdb_ref):
    j = pl.program_id(0)
    changed = jnp.logical_or(j == 0, be_ref[j] != be_ref[jnp.maximum(j - 1, 0)])

    @pl.when(changed)
    def _():
        wgb_ref[...] = wg_ref[...].astype(BF16)
        wub_ref[...] = wu_ref[...].astype(BF16)
        wdb_ref[...] = wd_ref[...].astype(BF16)

    x = _unpack_pairs(x_ref[...]).astype(BF16)
    a = jnp.dot(x, wgb_ref[...], preferred_element_type=F32)
    u = jnp.dot(x, wub_ref[...], preferred_element_type=F32)
    act = (a * jax.nn.sigmoid(a) * u).astype(BF16)
    y_ref[...] = _pack_pairs(jnp.dot(act, wdb_ref[...], preferred_element_type=F32))


def _experts(blk_e, xb, w_gate, w_up, w_down, layer):
    p, dw = xb.shape
    d = 2 * dw
    nb = p // MOE_BLOCK
    hid = MOE_HIDDEN
    grid_spec = pltpu.PrefetchScalarGridSpec(
        num_scalar_prefetch=1,
        grid=(nb,),
        in_specs=[pl.BlockSpec((MOE_BLOCK, dw), lambda j, be: (j, 0)),
                  pl.BlockSpec((None, None, d, hid), lambda j, be: (layer, be[j], 0, 0)),
                  pl.BlockSpec((None, None, d, hid), lambda j, be: (layer, be[j], 0, 0)),
                  pl.BlockSpec((None, None, hid, d), lambda j, be: (layer, be[j], 0, 0))],
        out_specs=pl.BlockSpec((MOE_BLOCK, dw), lambda j, be: (j, 0)),
        scratch_shapes=[pltpu.VMEM((d, hid), BF16), pltpu.VMEM((d, hid), BF16), pltpu.VMEM((hid, d), BF16)],
    )
    return pl.pallas_call(
        _expert_kernel,
        grid_spec=grid_spec,
        out_shape=jax.ShapeDtypeStruct((p, dw), F32),
        compiler_params=_params("arbitrary"),
        name="moe_experts",
    )(blk_e, xb, w_gate, w_up, w_down)


def _combine_ln_kernel(h_ref, y0_ref, y1_ref, rt_ref, g_ref, b_ref, o_ref, ob_ref):
    rt = rt_ref[...]
    y = (_unpack_pairs(y0_ref[...]) * rt[:, R_GATE:R_GATE + 1]
         + _unpack_pairs(y1_ref[...]) * rt[:, R_GATE + 1:R_GATE + 2])
    out = _layer_norm_rows(DN_ALPHA * h_ref[...] + y, g_ref[...], b_ref[...])
    o_ref[...] = out
    ob_ref[...] = out.astype(BF16)


def _combine_ln(h, y0, y1, route, g, b, *, tm):
    n, d = h.shape
    row = pl.BlockSpec((tm, d), lambda i: (i, 0))
    words = pl.BlockSpec((tm, d // 2), lambda i: (i, 0))
    vec = pl.BlockSpec((1, d), lambda i: (0, 0))
    return pl.pallas_call(
        _combine_ln_kernel,
        grid=(n // tm,),
        in_specs=[row, words, words, pl.BlockSpec((tm, LANES), lambda i: (i, 0)), vec, vec],
        out_specs=[row, row],
        out_shape=[jax.ShapeDtypeStruct((n, d), F32), jax.ShapeDtypeStruct((n, d), BF16)],
        compiler_params=_params("parallel"),
        name="moe_combine_ln",
    )(h, y0, y1, route, g.reshape(1, d), b.reshape(1, d))


R_EID, R_GATE, R_RANK = 0, 2, 4


def _router_kernel(h_ref, w_ref, b_ref, route_ref, cnt_ref, base_ref, tri_ref):
    i = pl.program_id(0)
    tm = h_ref.shape[0]

    @pl.when(i == 0)
    def _():
        base_ref[...] = jnp.zeros(base_ref.shape, F32)
        ti = lax.broadcasted_iota(jnp.int32, (tm, tm), 0)
        si = lax.broadcasted_iota(jnp.int32, (tm, tm), 1)
        tri_ref[...] = jnp.where(si < ti, 1.0, 0.0).astype(BF16)

    h = h_ref[...]
    h_hi = h.astype(BF16)
    h_lo = (h - h_hi.astype(F32)).astype(BF16)
    logits = (jnp.dot(h_hi, w_ref[0], preferred_element_type=F32)
              + jnp.dot(h_lo, w_ref[0], preferred_element_type=F32)
              + jnp.dot(h_hi, w_ref[1], preferred_element_type=F32)) + b_ref[...]
    lane = lax.broadcasted_iota(jnp.int32, (tm, LANES), 1).astype(F32)
    neg_inf = -jnp.inf
    big = float(4 * LANES)
    is_g = lane < MOE_GROUPS
    gl = jnp.where(is_g, logits, neg_inf)
    gmax = jnp.max(gl, -1, keepdims=True)
    grp = jnp.min(jnp.where(gl == gmax, lane, big), -1, keepdims=True)
    p_grp = 1.0 / jnp.sum(jnp.where(is_g, jnp.exp(logits - gmax), 0.0), -1, keepdims=True)
    lo = MOE_GROUPS + MOE_PER_GROUP * grp
    el = jnp.where((lane >= lo) & (lane < lo + MOE_PER_GROUP), logits, neg_inf)
    v1 = jnp.max(el, -1, keepdims=True)
    i1 = jnp.min(jnp.where(el == v1, lane, big), -1, keepdims=True)
    el2 = jnp.where(lane == i1, neg_inf, el)
    v2 = jnp.max(el2, -1, keepdims=True)
    i2 = jnp.min(jnp.where(el2 == v2, lane, big), -1, keepdims=True)
    t = jnp.exp(v2 - v1)
    g1 = p_grp / (1.0 + t)
    g2 = g1 * t
    e1 = i1 - MOE_GROUPS
    e2 = i2 - MOE_GROUPS
    oh1 = jnp.where(lane == e1, 1.0, 0.0)
    oh2 = jnp.where(lane == e2, 1.0, 0.0)
    oh = oh1 + oh2
    tot = base_ref[...] + jnp.dot(tri_ref[...], oh.astype(BF16), preferred_element_type=F32)
    r1 = jnp.sum(oh1 * tot, -1, keepdims=True)
    r2 = jnp.sum(oh2 * tot, -1, keepdims=True)
    new_base = base_ref[...] + jnp.sum(oh, 0, keepdims=True)
    base_ref[...] = new_base
    cnt_ref[...] = jnp.broadcast_to(new_base, cnt_ref.shape)
    route = jnp.zeros((tm, LANES), F32)
    for k, val in enumerate((e1, e2, g1, g2, r1, r2)):
        route = jnp.where(lane == float(k), val, route)
    route_ref[...] = route


def _router(h, w_r, b_r, *, tm):
    n, d = h.shape
    return pl.pallas_call(
        _router_kernel,
        grid=(n // tm,),
        in_specs=[pl.BlockSpec((tm, d), lambda i: (i, 0)),
                  pl.BlockSpec((2, d, LANES), lambda i: (0, 0, 0)),
                  pl.BlockSpec((1, LANES), lambda i: (0, 0))],
        out_specs=[pl.BlockSpec((tm, LANES), lambda i: (i, 0)),
                   pl.BlockSpec((8, LANES), lambda i: (0, 0))],
        out_shape=[jax.ShapeDtypeStruct((n, LANES), F32), jax.ShapeDtypeStruct((8, LANES), F32)],
        scratch_shapes=[pltpu.VMEM((1, LANES), F32), pltpu.VMEM((tm, tm), BF16)],
        compiler_params=_params("arbitrary"),
        name="moe_router",
    )(h, w_r, b_r)


def _sc_index_rows(idx):
    n = idx.shape[0]
    return jnp.pad(idx.reshape(n // SC_ROWS, SC_ROWS), ((0, 0), (0, SC_INDEX_LANES - SC_ROWS)))


def _sc_mesh():
    return plsc.VectorSubcoreMesh(core_axis_name="core", subcore_axis_name="subcore")


def _sc_gather_rows(y, idx):
    n = idx.shape[0]
    d = y.shape[1]

    @pl.kernel(out_type=jax.ShapeDtypeStruct((n, d), y.dtype), mesh=_sc_mesh(), scratch_types=[])
    def gather(y_hbm, i_hbm, o_hbm):
        def body(i_vmem, o_vmem):
            pltpu.sync_copy(y_hbm.at[i_vmem.at[0, pl.ds(0, SC_ROWS)]], o_vmem)

        pltpu.emit_pipeline(
            body, grid=(n // SC_ROWS,),
            in_specs=[pl.BlockSpec((1, SC_INDEX_LANES), lambda i: (i, 0))],
            out_specs=[pl.BlockSpec((SC_ROWS, d), lambda i: (i, 0))],
            core_axis_name=("core", "subcore"), dimension_semantics=(pltpu.PARALLEL,),
        )(i_hbm, o_hbm)

    return gather(y, _sc_index_rows(idx))


def _sc_scatter_rows(x, idx0, idx1, p_rows):
    n, d = x.shape

    @pl.kernel(out_type=jax.ShapeDtypeStruct((p_rows, d), x.dtype), mesh=_sc_mesh(), scratch_types=[])
    def scatter(x_hbm, i0_hbm, i1_hbm, o_hbm):
        def body(x_vmem, i0_vmem, i1_vmem):
            pltpu.sync_copy(x_vmem, o_hbm.at[i0_vmem.at[0, pl.ds(0, SC_ROWS)]])
            pltpu.sync_copy(x_vmem, o_hbm.at[i1_vmem.at[0, pl.ds(0, SC_ROWS)]])

        pltpu.emit_pipeline(
            body, grid=(n // SC_ROWS,),
            in_specs=[pl.BlockSpec((SC_ROWS, d), lambda i: (i, 0)),
                      pl.BlockSpec((1, SC_INDEX_LANES), lambda i: (i, 0)),
                      pl.BlockSpec((1, SC_INDEX_LANES), lambda i: (i, 0))],
            out_specs=[],
            core_axis_name=("core", "subcore"), dimension_semantics=(pltpu.PARALLEL,),
        )(x_hbm, i0_hbm, i1_hbm)

    return scatter(x, _sc_index_rows(idx0), _sc_index_rows(idx1))


def _moe(h, hp, wg_r, bg_r, we_r, be_r, w_gate, w_up, w_down, layer, ln_g, ln_b):
    n, d = h.shape
    pad = LANES - MOE_GROUPS - MOE_EXPERTS
    w_r = jnp.pad(jnp.concatenate([wg_r, we_r], 1), ((0, 0), (0, pad)))
    w_hi = w_r.astype(BF16)
    w_lo = (w_r - w_hi.astype(F32)).astype(BF16)
    b_r = jnp.pad(jnp.concatenate([bg_r, be_r]), (0, pad)).reshape(1, LANES)
    route, cnt = _router(h, jnp.stack([w_hi, w_lo]), b_r, tm=512)
    counts = cnt[0, :MOE_EXPERTS].astype(jnp.int32)
    padded = (counts + MOE_BLOCK - 1) // MOE_BLOCK * MOE_BLOCK
    pends = jnp.cumsum(padded)
    pstarts = pends - padded
    eid = route[:, R_EID:R_EID + MOE_TOPK].astype(jnp.int32)
    rank = route[:, R_RANK:R_RANK + MOE_TOPK].astype(jnp.int32)
    experts = jnp.arange(MOE_EXPERTS, dtype=jnp.int32)
    dest = jnp.sum(jnp.where(eid[..., None] == experts, pstarts, 0), -1) + rank
    p_rows = n * MOE_TOPK + MOE_EXPERTS * MOE_BLOCK
    nb = p_rows // MOE_BLOCK
    blk_start = jnp.arange(nb, dtype=jnp.int32) * MOE_BLOCK
    blk_e = jnp.minimum(jnp.sum((pends[None, :] <= blk_start[:, None]).astype(jnp.int32), -1), MOE_EXPERTS - 1)
    dest0, dest1 = dest[:, 0], dest[:, 1]
    xb = _sc_scatter_rows(hp, dest0, dest1, p_rows)
    yb = _experts(blk_e, xb, w_gate, w_up, w_down, layer)
    return _combine_ln(h, _sc_gather_rows(yb, dest0), _sc_gather_rows(yb, dest1), route, ln_g, ln_b, tm=512)


def kernel(x, positions, ln1_g, ln1_b, ln2_g, ln2_b, a_w_in, a_w_out, b_w_in, b_gate_bias, b_conv_w, b_conv_b,
           b_norm_g, b_w_out, c_w_in, c_norm_g, c_w_out, r_group_w, r_group_b, r_expert_w, r_expert_b,
           e_w_gate, e_w_up, e_w_down):
    batch, seq, d = x.shape
    n = batch * seq
    tabs_a = _rope_tables_a(positions)
    inv_c = C_THETA ** (-jnp.arange(0, C_QK_DIM, 2, dtype=F32) / C_QK_DIM)
    ang_c = positions.astype(F32)[:, None] * inv_c[None, :]
    cos_c, sin_c = jnp.cos(ang_c), jnp.sin(ang_c)
    h = x.reshape(n, d)
    hb = h.astype(BF16)
    for i in range(DEPTH):
        kind, j = i % 3, i // 3
        if kind == 0:
            y = _mixer_dilated(hb, a_w_in[j], tabs_a, batch, seq)
            w_out = a_w_out[j]
        elif kind == 1:
            y = _mixer_mlstm(hb, b_w_in[j], b_gate_bias[j], b_conv_w[j], b_conv_b[j], b_norm_g[j], batch, seq)
            w_out = b_w_out[j]
        else:
            y = _mixer_retention(hb, c_w_in[j], c_norm_g[j], cos_c, sin_c, batch, seq)
            w_out = c_w_out[j]
        h, hp = _matmul_res_ln(y, w_out.astype(BF16), h, ln1_g[i], ln1_b[i], tm=512)
        h, hb = _moe(h, hp,r_group_w[i], r_group_b[i], r_expert_w[i], r_expert_b[i],
                     e_w_gate, e_w_up, e_w_down, i, ln2_g[i], ln2_b[i])
    return h.reshape(batch, seq, d)
```

```python
import functools

import jax
import jax.numpy as jnp
from jax import lax
from jax.experimental import pallas as pl
from jax.experimental.pallas import tpu as pltpu
from jax.experimental.pallas import tpu_sc as plsc

F32 = jnp.float32
BF16 = jnp.bfloat16

D_MODEL = 1024
DEPTH = 4
DN_ALPHA = (2.0 * DEPTH) ** 0.25
LN_EPS = 1e-5

A_HEADS = 16
A_HEAD_DIM = 64
A_DILATIONS = (1, 4, 16)
A_BLOCK = 128
A_UNROLL = 16
A_GROUP = 16
A_PITCH = 20
LOG2_E = 1.4426950408889634
A_ROT_DIM = 16
ROPE_THETA = 500000.0

B_HEADS = 8
B_QK_DIM = 64
B_V_DIM = 128
B_CONV = 4
B_CHUNK = 256
C_HEADS = 4
C_QK_DIM = 256
C_V_DIM = 512
C_CHUNK = 256
C_THETA = 10000.0

MOE_GROUPS = 8
MOE_PER_GROUP = 8
MOE_EXPERTS = 64
MOE_TOPK = 2
MOE_HIDDEN = 256
MOE_BLOCK = 512
SC_ROWS = 64
SC_INDEX_LANES = 128

LANES = 128
NEG = -1e30
VMEM_LIMIT = 48 * 1024 * 1024


def _params(*sem):
    return pltpu.CompilerParams(dimension_semantics=sem, vmem_limit_bytes=VMEM_LIMIT)


def _mm_kernel(x_ref, w_ref, o_ref):
    o_ref[...] = jnp.dot(x_ref[...], w_ref[...], preferred_element_type=F32).astype(o_ref.dtype)


def _matmul(x, w, *, tm, tn, out_dtype=F32):
    n, k = x.shape
    m = w.shape[1]
    return pl.pallas_call(
        _mm_kernel,
        grid=(n // tm, m // tn),
        in_specs=[pl.BlockSpec((tm, k), lambda i, j: (i, 0)),
                  pl.BlockSpec((k, tn), lambda i, j: (0, j))],
        out_specs=pl.BlockSpec((tm, tn), lambda i, j: (i, j)),
        out_shape=jax.ShapeDtypeStruct((n, m), out_dtype),
        compiler_params=_params("parallel", "parallel"),
        name="matmul",
    )(x, w)


def _layer_norm_rows(z, g, b):
    mu = jnp.mean(z, -1, keepdims=True)
    zc = z - mu
    var = jnp.mean(zc * zc, -1, keepdims=True)
    return zc * lax.rsqrt(var + LN_EPS) * g + b


def _pack_pairs(x):
    c = x.shape[1] // 2
    hi = pltpu.bitcast(x[:, :c].astype(BF16).astype(F32), jnp.uint32)
    lo = pltpu.bitcast(x[:, c:].astype(BF16).astype(F32), jnp.uint32)
    return pltpu.bitcast(hi | (lo >> 16), F32)


def _unpack_pairs(w):
    bits = pltpu.bitcast(w, jnp.uint32)
    hi = pltpu.bitcast(bits & jnp.uint32(0xFFFF0000), F32)
    lo = pltpu.bitcast(bits << 16, F32)
    return jnp.concatenate([hi, lo], axis=1)


def _mm_res_ln_kernel(x_ref, w_ref, h_ref, g_ref, b_ref, o_ref, op_ref):
    y = jnp.dot(x_ref[...], w_ref[...], preferred_element_type=F32)
    out = _layer_norm_rows(DN_ALPHA * h_ref[...] + y, g_ref[...], b_ref[...])
    o_ref[...] = out
    op_ref[...] = _pack_pairs(out)


def _matmul_res_ln(x, w, h, g, b, *, tm):
    n, k = x.shape
    d = w.shape[1]
    return pl.pallas_call(
        _mm_res_ln_kernel,
        grid=(n // tm,),
        in_specs=[pl.BlockSpec((tm, k), lambda i: (i, 0)),
                  pl.BlockSpec((k, d), lambda i: (0, 0)),
                  pl.BlockSpec((tm, d), lambda i: (i, 0)),
                  pl.BlockSpec((1, d), lambda i: (0, 0)),
                  pl.BlockSpec((1, d), lambda i: (0, 0))],
        out_specs=[pl.BlockSpec((tm, d), lambda i: (i, 0)),
                   pl.BlockSpec((tm, d // 2), lambda i: (i, 0))],
        out_shape=[jax.ShapeDtypeStruct((n, d), F32), jax.ShapeDtypeStruct((n, d // 2), F32)],
        compiler_params=_params("parallel"),
        name="matmul_res_ln",
    )(x, w, h, g.reshape(1, d), b.reshape(1, d))


def _proj_a_kernel(x_ref, w_ref, c_ref, s1_ref, s2_ref, q_ref, k_ref, v_ref):
    x = x_ref[...].astype(BF16)
    width = 2 * LANES
    for c, ref in ((0, q_ref), (1, k_ref), (2, v_ref)):
        for j in range(D_MODEL // width):
            col = c * D_MODEL + j * width
            y = jnp.dot(x, w_ref[:, col:col + width], preferred_element_type=F32)
            if c < 2:
                half = A_ROT_DIM // 2
                y = (y * c_ref[...] + pltpu.roll(y, width - half, 1) * s1_ref[...]
                     + pltpu.roll(y, half, 1) * s2_ref[...])
            if c == 0:
                y = y * (LOG2_E * A_HEAD_DIM ** -0.5)
            ref[0, 2 * j] = y[:, :LANES]
            ref[0, 2 * j + 1] = y[:, LANES:]


def _proj_a(xb, w, tabs, batch, seq, *, tm):
    n, d = xb.shape
    spb = seq // tm
    hp = D_MODEL // LANES
    qkv_shape = jax.ShapeDtypeStruct((batch, hp, seq, LANES), F32)
    out_spec = pl.BlockSpec((1, hp, tm, LANES), lambda i: (i // spb, 0, i % spb, 0))
    tab_spec = pl.BlockSpec((tm, 2 * LANES), lambda i: (i % spb, 0))
    return pl.pallas_call(
        _proj_a_kernel,
        grid=(n // tm,),
        in_specs=[pl.BlockSpec((tm, d), lambda i: (i, 0)),
                  pl.BlockSpec((d, 3 * d), lambda i: (0, 0)),
                  tab_spec, tab_spec, tab_spec],
        out_specs=[out_spec, out_spec, out_spec],
        out_shape=[qkv_shape, qkv_shape, qkv_shape],
        compiler_params=_params("parallel"),
        name="proj_a",
    )(xb, w, *tabs)


def _rope_tables_a(positions):
    half = A_ROT_DIM // 2
    inv = ROPE_THETA ** (-jnp.arange(0, A_ROT_DIM, 2, dtype=F32) / A_ROT_DIM)
    ang = positions.astype(F32)[:, None] * inv[None, :]
    cos, sin = jnp.cos(ang), jnp.sin(ang)
    s = positions.shape[0]
    pad = jnp.zeros((s, A_HEAD_DIM - A_ROT_DIM), F32)
    c_head = jnp.concatenate([cos, cos, pad + 1.0], -1)
    s1_head = jnp.concatenate([-sin, jnp.zeros_like(sin), pad], -1)
    s2_head = jnp.concatenate([jnp.zeros_like(sin), sin, pad], -1)
    reps = 2 * LANES // A_HEAD_DIM
    return tuple(jnp.tile(t, (1, reps)) for t in (c_head, s1_head, s2_head))


def _attn_blocks(q_ref, k_ref, v_ref, o_ref, lse_ref, bias_ref, hmask, head0, first_block, d, pitch):
    nk = 2 * A_BLOCK
    loaded = []
    for u in range(A_UNROLL):
        g = first_block + u
        r = g % d
        n = g // d
        qstart = n * (A_BLOCK * pitch) + r
        kstart = jnp.maximum(qstart - A_BLOCK * pitch, r)
        if pitch == 1:
            qstart = pl.multiple_of(qstart, A_BLOCK)
            kstart = pl.multiple_of(kstart, A_BLOCK)
            qsl, ksl = pl.ds(qstart, A_BLOCK), pl.ds(kstart, nk)
        else:
            qsl, ksl = pl.ds(qstart, A_BLOCK, stride=pitch), pl.ds(kstart, nk, stride=pitch)
        bias = bias_ref[jnp.minimum(n, 1)]
        loaded.append((qsl, q_ref[qsl, :].astype(BF16), k_ref[ksl, :].astype(BF16), v_ref[ksl, :].astype(BF16), bias))
    results = []
    for qsl, qb, kb, vb, bias in loaded:
        pvs, ms, ls = [], [], []
        for h in range(2):
            s = lax.dot_general(qb * hmask[h], kb, (((1,), (1,)), ((), ())), preferred_element_type=F32) + bias
            m = jnp.max(s, -1, keepdims=True)
            p = jnp.exp2(s - m)
            ms.append(jnp.broadcast_to(m, (A_BLOCK, LANES)))
            ls.append(jnp.broadcast_to(jnp.sum(p, -1, keepdims=True), (A_BLOCK, LANES)))
            pvs.append(jnp.dot(p.astype(BF16), vb, preferred_element_type=F32))
        l = jnp.where(head0, ls[0], ls[1])
        out = jnp.where(head0, pvs[0], pvs[1]) * (1.0 / l)
        results.append((qsl, out, jnp.where(head0, ms[0], ms[1]) + jnp.log2(l)))
    for qsl, out, lse in results:
        o_ref[qsl, :] = out
        lse_ref[qsl, :] = lse


def _attn_kernel(q_ref, k_ref, v_ref, o_ref, q16_ref, k16_ref, v16_ref, ob_ref, lb_ref, o16_ref, l16_ref,
                 bias_ref):
    seq = q_ref.shape[0]
    groups = seq // A_GROUP
    head0 = lax.broadcasted_iota(jnp.int32, (A_BLOCK, LANES), 1) < A_HEAD_DIM
    hmask = [jnp.where(head0, 1.0, 0.0).astype(BF16), jnp.where(head0, 0.0, 1.0).astype(BF16)]
    qi = lax.broadcasted_iota(jnp.int32, (A_BLOCK, 2 * A_BLOCK), 0)
    kj = lax.broadcasted_iota(jnp.int32, (A_BLOCK, 2 * A_BLOCK), 1)
    bias_ref[0] = jnp.where(kj <= qi, 0.0, NEG).astype(F32)
    bias_ref[1] = jnp.where((kj >= qi) & (kj <= qi + A_BLOCK), 0.0, NEG).astype(F32)

    def spread(g, carry):
        src = pl.ds(pl.multiple_of(g * A_GROUP, A_GROUP), A_GROUP)
        dst = pl.ds(pl.multiple_of(g * A_PITCH, 4), A_GROUP)
        q16_ref[dst, :] = q_ref[src, :]
        k16_ref[dst, :] = k_ref[src, :]
        v16_ref[dst, :] = v_ref[src, :]
        return carry

    lax.fori_loop(0, groups, spread, 0, unroll=8)

    branches = ((1, 1, q_ref, k_ref, v_ref, ob_ref.at[0], lb_ref.at[0]),
                (4, 4, q_ref, k_ref, v_ref, ob_ref.at[1], lb_ref.at[1]),
                (16, A_PITCH, q16_ref, k16_ref, v16_ref, o16_ref, l16_ref))
    for d, pitch, qr, kr, vr, orf, lrf in branches:

        def body(it, carry, d=d, pitch=pitch, qr=qr, kr=kr, vr=vr, orf=orf, lrf=lrf):
            _attn_blocks(qr, kr, vr, orf, lrf, bias_ref, hmask, head0, it * A_UNROLL, d, pitch)
            return carry

        lax.fori_loop(0, seq // (A_BLOCK * A_UNROLL), body, 0)

    def mix(g, carry):
        nat = pl.ds(pl.multiple_of(g * A_GROUP, A_GROUP), A_GROUP)
        pad = pl.ds(pl.multiple_of(g * A_PITCH, 4), A_GROUP)
        o0, o1, o2 = ob_ref[0, nat, :], ob_ref[1, nat, :], o16_ref[pad, :]
        l0, l1, l2 = lb_ref[0, nat, :], lb_ref[1, nat, :], l16_ref[pad, :]
        mx = jnp.maximum(jnp.maximum(l0, l1), l2)
        w0, w1, w2 = jnp.exp2(l0 - mx), jnp.exp2(l1 - mx), jnp.exp2(l2 - mx)
        o_ref[nat, :] = ((w0 * o0 + w1 * o1 + w2 * o2) / (w0 + w1 + w2)).astype(o_ref.dtype)
        return carry

    lax.fori_loop(0, groups, mix, 0, unroll=8)


def _attention(q, k, v):
    batch, hp, seq, _ = q.shape
    assert seq % (2 * A_BLOCK * max(A_DILATIONS)) == 0 and seq % (A_BLOCK * A_UNROLL) == 0
    in_spec = pl.BlockSpec((None, None, seq, LANES), lambda b, p: (b, p, 0, 0))
    padded = seq // A_GROUP * A_PITCH
    return pl.pallas_call(
        _attn_kernel,
        grid=(batch, hp),
        in_specs=[in_spec, in_spec, in_spec],
        out_specs=pl.BlockSpec((None, seq, LANES), lambda b, p: (b, 0, p)),
        out_shape=jax.ShapeDtypeStruct((batch, seq, hp * LANES), BF16),
        scratch_shapes=[pltpu.VMEM((padded, LANES), F32)] * 3
        + [pltpu.VMEM((2, seq, LANES), F32)] * 2
        + [pltpu.VMEM((padded, LANES), F32)] * 2
        + [pltpu.VMEM((2, A_BLOCK, 2 * A_BLOCK), F32)],
        compiler_params=_params("parallel", "parallel"),
        name="dilated_attention",
    )(q, k, v)


def _mixer_dilated(hb, w_in, tabs, batch, seq):
    q, k, v = _proj_a(hb, w_in.astype(BF16), tabs, batch, seq, tm=512)
    o = _attention(q, k, v)
    return o.reshape(batch * seq, D_MODEL)


def _mlstm_kernel(qk_ref, v_ref, o_ref, gc_ref, gb_ref, cw_ref, cb_ref, ng_ref, out_ref,
                  ext_ref, c_ref, n_ref, m_ref):
    L = B_CHUNK
    chunk = pl.program_id(1)

    @pl.when(chunk == 0)
    def _():
        ext_ref[0:8, :] = jnp.zeros((8, D_MODEL), F32)
        c_ref[...] = jnp.zeros(c_ref.shape, F32)
        n_ref[...] = jnp.zeros(n_ref.shape, F32)
        m_ref[...] = jnp.zeros(m_ref.shape, F32)

    u = qk_ref[...].astype(F32)
    ext_ref[8:8 + L, :] = u
    conv = u * cw_ref[B_CONV - 1:B_CONV, :] + cb_ref[...]
    for j in range(1, B_CONV):
        conv = conv + ext_ref[pl.ds(8 - j, L), :] * cw_ref[B_CONV - 1 - j:B_CONV - j, :]
    ext_ref[0:8, :] = u[L - 8:, :]
    qk = conv * jax.nn.sigmoid(conv)
    half = D_MODEL // 2

    gc = gc_ref[...] + gb_ref[...]
    gr = gc.T
    i_col, i_row = gc, gr[:B_HEADS, :]
    lf_col = jax.nn.log_sigmoid(gc)
    lf_row = jax.nn.log_sigmoid(gr[B_HEADS:2 * B_HEADS, :])
    ti = lax.broadcasted_iota(jnp.int32, (L, L), 0)
    si = lax.broadcasted_iota(jnp.int32, (L, L), 1)
    causal = ti >= si
    tri = causal.astype(F32)
    a_col = jnp.dot(tri, lf_col, preferred_element_type=F32, precision=lax.Precision.HIGHEST)
    a_row = lax.dot_general(lf_row, tri, (((1,), (1,)), ((), ())), preferred_element_type=F32,
                            precision=lax.Precision.HIGHEST)
    lane = lax.broadcasted_iota(jnp.int32, (1, LANES), 1)
    lane_h0 = lane < B_QK_DIM
    col_h0 = lax.broadcasted_iota(jnp.int32, (1, 2 * B_V_DIM), 1) < B_V_DIM

    for p in range(B_HEADS // 2):
        qp = qk[:, p * LANES:(p + 1) * LANES]
        kp = qk[:, half + p * LANES:half + (p + 1) * LANES] * (B_QK_DIM ** -0.5)
        vp = v_ref[:, p * 2 * B_V_DIM:(p + 1) * 2 * B_V_DIM]
        kpb = kp.astype(BF16)
        vpb = vp.astype(BF16)
        c_old = c_ref[p]
        c_oldb = c_old.astype(BF16)
        n_old = n_ref[p]
        m_pair = m_ref[p]
        ws_cols, decays, m_news = [], [], []
        for hh in range(2):
            h = 2 * p + hh
            m_old = m_pair[:, hh * B_QK_DIM:hh * B_QK_DIM + 1]
            ac, ar = a_col[:, B_HEADS + h:B_HEADS + h + 1], a_row[h:h + 1, :]
            ic, ir = i_col[:, h:h + 1], i_row[h:h + 1, :]
            dmat = jnp.where(causal, ac - ar + ir, NEG)
            inter = ac + m_old
            m_t = jnp.maximum(inter, jnp.max(dmat, -1, keepdims=True))
            qm = jnp.where(lane_h0 if hh == 0 else jnp.logical_not(lane_h0), qp, 0.0)
            qmb = qm.astype(BF16)
            sc = lax.dot_general(qmb, kpb, (((1,), (1,)), ((), ())), preferred_element_type=F32)
            sc = sc * jnp.exp(dmat - m_t)
            g_inter = jnp.exp(inter - m_t)
            vh = vpb[:, hh * B_V_DIM:(hh + 1) * B_V_DIM]
            qc = jnp.dot(qmb, c_oldb, preferred_element_type=F32)[:, hh * B_V_DIM:(hh + 1) * B_V_DIM]
            num = jnp.dot(sc.astype(BF16), vh, preferred_element_type=F32) + g_inter * qc
            den = jnp.sum(sc, -1, keepdims=True) + g_inter * jnp.sum(qm * n_old, -1, keepdims=True)
            h_out = num / jnp.maximum(jnp.abs(den), jnp.exp(-m_t))
            mu = jnp.mean(h_out, -1, keepdims=True)
            hc = h_out - mu
            var = jnp.mean(hc * hc, -1, keepdims=True)
            cols = slice(h * B_V_DIM, (h + 1) * B_V_DIM)
            hn = hc * lax.rsqrt(var + LN_EPS) * ng_ref[:, cols]
            out_ref[:, cols] = (hn * jax.nn.sigmoid(o_ref[:, cols].astype(F32))).astype(out_ref.dtype)
            a_end = ac[L - 1:L, :]
            w_col = a_end - ac + ic
            m_new = jnp.maximum(a_end + m_old, jnp.max(w_col, 0, keepdims=True))
            decays.append(jnp.exp(a_end + m_old - m_new))
            ws_cols.append(jnp.exp(w_col - m_new))
            m_news.append(m_new)
        ws = jnp.where(lane_h0, jnp.broadcast_to(ws_cols[0], (L, LANES)), jnp.broadcast_to(ws_cols[1], (L, LANES)))
        kw = kp * ws
        dec_c = jnp.where(col_h0, jnp.broadcast_to(decays[0], (1, 2 * B_V_DIM)),
                          jnp.broadcast_to(decays[1], (1, 2 * B_V_DIM)))
        dec_n = jnp.where(lane_h0, jnp.broadcast_to(decays[0], (1, LANES)), jnp.broadcast_to(decays[1], (1, LANES)))
        c_ref[p] = dec_c * c_old + lax.dot_general(kw.astype(BF16), vpb, (((0,), (0,)), ((), ())),
                                                   preferred_element_type=F32)
        n_ref[p] = dec_n * n_old + jnp.sum(kw, 0, keepdims=True)
        m_ref[p] = jnp.where(lane_h0, jnp.broadcast_to(m_news[0], (1, LANES)), jnp.broadcast_to(m_news[1], (1, LANES)))


def _mlstm(proj, gates, gate_bias, conv_w, conv_b, norm_g, batch, seq):
    L = B_CHUNK
    d = D_MODEL
    slab = lambda c: pl.BlockSpec((None, L, d), lambda b, s, c=c: (b, s, c))
    full = lambda shape: pl.BlockSpec(shape, lambda b, s: (0,) * len(shape))
    return pl.pallas_call(
        _mlstm_kernel,
        grid=(batch, seq // L),
        in_specs=[slab(0), slab(1), slab(2),
                  pl.BlockSpec((None, L, LANES), lambda b, s: (b, s, 0)),
                  full((1, LANES)),
                  full((B_CONV, d)), full((1, d)), full((1, d))],
        out_specs=pl.BlockSpec((None, L, d), lambda b, s: (b, s, 0)),
        out_shape=jax.ShapeDtypeStruct((batch, seq, d), BF16),
        scratch_shapes=[pltpu.VMEM((L + 8, d), F32),
                        pltpu.VMEM((B_HEADS // 2, 2 * B_QK_DIM, 2 * B_V_DIM), F32),
                        pltpu.VMEM((B_HEADS // 2, 1, LANES), F32),
                        pltpu.VMEM((B_HEADS // 2, 1, LANES), F32)],
        compiler_params=_params("parallel", "arbitrary"),
        name="mlstm",
    )(proj, proj, proj, gates, jnp.pad(gate_bias, (0, LANES - 2 * B_HEADS)).reshape(1, LANES),
      conv_w, conv_b.reshape(1, d), norm_g.reshape(1, d))


def _mixer_mlstm(hb, w_in, gate_bias, conv_w, conv_b, norm_g, batch, seq):
    n = batch * seq
    main = 3 * D_MODEL
    proj = _matmul(hb, w_in[:, :main].astype(BF16), tm=1024, tn=1024, out_dtype=BF16)
    w_g = jnp.pad(w_in[:, main:], ((0, 0), (0, LANES - 2 * B_HEADS))).astype(BF16)
    gates = _matmul(hb, w_g, tm=2048, tn=LANES)
    out = _mlstm(proj.reshape(batch, seq, main), gates.reshape(batch, seq, LANES), gate_bias,
                 conv_w, conv_b, norm_g, batch, seq)
    return out.reshape(n, D_MODEL)


def _retention_kernel(lg_ref, q_ref, k_ref, v_ref, g_ref, cos_ref, sin_ref, ng_ref, out_ref,
                      r_ref, dm_ref, xi_ref, zeta_ref):
    L = C_CHUNK
    chunk = pl.program_id(1)

    @pl.when(chunk == 0)
    def _():
        r_ref[...] = jnp.zeros(r_ref.shape, F32)
        ti = lax.broadcasted_iota(jnp.int32, (L, L), 0)
        si = lax.broadcasted_iota(jnp.int32, (L, L), 1)
        rel = (ti - si).astype(F32)
        idx = lax.broadcasted_iota(jnp.int32, (L, LANES), 0).astype(F32)
        for h in range(C_HEADS):
            lg = lg_ref[h]
            dm_ref[h] = jnp.where(rel >= 0, jnp.exp(jnp.maximum(rel, 0.0) * lg), 0.0)
            xi_ref[h] = jnp.exp((idx + 1.0) * lg)
            zeta_ref[h] = jnp.exp((L - 1.0 - idx) * lg)

    cos, sin = cos_ref[...], sin_ref[...]
    hd = C_QK_DIM // 2

    def rope(t):
        t1, t2 = t[:, :hd], t[:, hd:]
        return jnp.concatenate([t1 * cos - t2 * sin, t2 * cos + t1 * sin], -1)

    for h in range(C_HEADS):
        qk_cols = slice(h * C_QK_DIM, (h + 1) * C_QK_DIM)
        v_cols = slice(h * C_V_DIM, (h + 1) * C_V_DIM)
        q = rope(q_ref[:, qk_cols].astype(F32))
        k = rope(k_ref[:, qk_cols].astype(F32)) * (C_QK_DIM ** -0.5)
        qb = q.astype(BF16)
        vb = v_ref[:, v_cols]
        r_old = r_ref[h]
        sc = lax.dot_general(qb, k.astype(BF16), (((1,), (1,)), ((), ())), preferred_element_type=F32) * dm_ref[h]
        o = jnp.dot(sc.astype(BF16), vb, preferred_element_type=F32)
        o = o + xi_ref[h, :, 0:1] * jnp.dot(qb, r_old.astype(BF16), preferred_element_type=F32)
        kz = (k * zeta_ref[h, :, 0:1]).astype(BF16)
        cd = jnp.exp(jnp.full((1, 1), float(L), F32) * lg_ref[h])
        r_ref[h] = cd * r_old + lax.dot_general(kz, vb, (((0,), (0,)), ((), ())), preferred_element_type=F32)
        mu = jnp.mean(o, -1, keepdims=True)
        oc = o - mu
        var = jnp.mean(oc * oc, -1, keepdims=True)
        on = oc * lax.rsqrt(var + LN_EPS) * ng_ref[:, v_cols]
        g = g_ref[:, v_cols].astype(F32)
        out_ref[:, v_cols] = (on * (g * jax.nn.sigmoid(g))).astype(out_ref.dtype)


def _retention(proj, cos, sin, norm_g, batch, seq):
    L = C_CHUNK
    d = D_MODEL
    log_gamma = jnp.log(1.0 - 2.0 ** (-5.0 - jnp.arange(C_HEADS, dtype=F32)))
    return pl.pallas_call(
        _retention_kernel,
        grid=(batch, seq // L),
        in_specs=[pl.BlockSpec(memory_space=pltpu.SMEM),
                  pl.BlockSpec((None, L, d), lambda b, c: (b, c, 0)),
                  pl.BlockSpec((None, L, d), lambda b, c: (b, c, 1)),
                  pl.BlockSpec((None, L, 2 * d), lambda b, c: (b, c, 1)),
                  pl.BlockSpec((None, L, 2 * d), lambda b, c: (b, c, 2)),
                  pl.BlockSpec((L, C_QK_DIM // 2), lambda b, c: (c, 0)),
                  pl.BlockSpec((L, C_QK_DIM // 2), lambda b, c: (c, 0)),
                  pl.BlockSpec((1, 2 * d), lambda b, c: (0, 0))],
        out_specs=pl.BlockSpec((None, L, 2 * d), lambda b, c: (b, c, 0)),
        out_shape=jax.ShapeDtypeStruct((batch, seq, 2 * d), BF16),
        scratch_shapes=[pltpu.VMEM((C_HEADS, C_QK_DIM, C_V_DIM), F32),
                        pltpu.VMEM((C_HEADS, L, L), F32),
                        pltpu.VMEM((C_HEADS, L, LANES), F32),
                        pltpu.VMEM((C_HEADS, L, LANES), F32)],
        compiler_params=_params("parallel", "arbitrary"),
        name="retention",
    )(log_gamma, proj, proj, proj, proj, cos, sin, norm_g.reshape(1, 2 * d))


def _mixer_retention(hb, w_in, norm_g, cos, sin, batch, seq):
    proj = _matmul(hb, w_in.astype(BF16), tm=1024, tn=1024, out_dtype=BF16)
    out = _retention(proj.reshape(batch, seq, 6 * D_MODEL), cos, sin, norm_g, batch, seq)
    return out.reshape(batch * seq, 2 * D_MODEL)


def _expert_kernel(be_ref, x_ref, wg_ref, wu_ref, wd_ref, y_ref, wgb_ref, wub_ref, wdb_ref):
    j = pl.program_id(0)
    changed = jnp.logical_or(j == 0, be_ref[j] != be_ref[jnp.maximum(j - 1, 0)])

    @pl.when(changed)
    def _():
        wgb_ref[...] = wg_ref[...].astype(BF16)
        wub_ref[...] = wu_ref[...].astype(BF16)
        wdb_ref[...] = wd_ref[...].astype(BF16)

    x = _unpack_pairs(x_ref[...]).astype(BF16)
    a = jnp.dot(x, wgb_ref[...], preferred_element_type=F32)
    u = jnp.dot(x, wub_ref[...], preferred_element_type=F32)
    act = (a * jax.nn.sigmoid(a) * u).astype(BF16)
    y_ref[...] = _pack_pairs(jnp.dot(act, wdb_ref[...], preferred_element_type=F32))


def _experts(blk_e, xb, w_gate, w_up, w_down, layer):
    p, dw = xb.shape
    d = 2 * dw
    nb = p // MOE_BLOCK
    hid = MOE_HIDDEN
    grid_spec = pltpu.PrefetchScalarGridSpec(
        num_scalar_prefetch=1,
        grid=(nb,),
        in_specs=[pl.BlockSpec((MOE_BLOCK, dw), lambda j, be: (j, 0)),
                  pl.BlockSpec((None, None, d, hid), lambda j, be: (layer, be[j], 0, 0)),
                  pl.BlockSpec((None, None, d, hid), lambda j, be: (layer, be[j], 0, 0)),
                  pl.BlockSpec((None, None, hid, d), lambda j, be: (layer, be[j], 0, 0))],
        out_specs=pl.BlockSpec((MOE_BLOCK, dw), lambda j, be: (j, 0)),
        scratch_shapes=[pltpu.VMEM((d, hid), BF16), pltpu.VMEM((d, hid), BF16), pltpu.VMEM((hid, d), BF16)],
    )
    return pl.pallas_call(
        _expert_kernel,
        grid_spec=grid_spec,
        out_shape=jax.ShapeDtypeStruct((p, dw), F32),
        compiler_params=_params("arbitrary"),
        name="moe_experts",
    )(blk_e, xb, w_gate, w_up, w_down)


def _combine_ln_kernel(h_ref, y0_ref, y1_ref, rt_ref, g_ref, b_ref, o_ref, ob_ref):
    rt = rt_ref[...]
    y = (_unpack_pairs(y0_ref[...]) * rt[:, R_GATE:R_GATE + 1]
         + _unpack_pairs(y1_ref[...]) * rt[:, R_GATE + 1:R_GATE + 2])
    out = _layer_norm_rows(DN_ALPHA * h_ref[...] + y, g_ref[...], b_ref[...])
    o_ref[...] = out
    ob_ref[...] = out.astype(BF16)


def _combine_ln(h, y0, y1, route, g, b, *, tm):
    n, d = h.shape
    row = pl.BlockSpec((tm, d), lambda i: (i, 0))
    words = pl.BlockSpec((tm, d // 2), lambda i: (i, 0))
    vec = pl.BlockSpec((1, d), lambda i: (0, 0))
    return pl.pallas_call(
        _combine_ln_kernel,
        grid=(n // tm,),
        in_specs=[row, words, words, pl.BlockSpec((tm, LANES), lambda i: (i, 0)), vec, vec],
        out_specs=[row, row],
        out_shape=[jax.ShapeDtypeStruct((n, d), F32), jax.ShapeDtypeStruct((n, d), BF16)],
        compiler_params=_params("parallel"),
        name="moe_combine_ln",
    )(h, y0, y1, route, g.reshape(1, d), b.reshape(1, d))


R_EID, R_GATE, R_RANK = 0, 2, 4


def _router_kernel(h_ref, w_ref, b_ref, route_ref, cnt_ref, base_ref, tri_ref):
    i = pl.program_id(0)
    tm = h_ref.shape[0]

    @pl.when(i == 0)
    def _():
        base_ref[...] = jnp.zeros(base_ref.shape, F32)
        ti = lax.broadcasted_iota(jnp.int32, (tm, tm), 0)
        si = lax.broadcasted_iota(jnp.int32, (tm, tm), 1)
        tri_ref[...] = jnp.where(si < ti, 1.0, 0.0).astype(BF16)

    h = h_ref[...]
    h_hi = h.astype(BF16)
    h_lo = (h - h_hi.astype(F32)).astype(BF16)
    hh = jnp.dot(h_hi, w_ref[...], preferred_element_type=F32)
    logits = (hh[:, :LANES] + hh[:, LANES:]
              + jnp.dot(h_lo, w_ref[:, :LANES], preferred_element_type=F32)) + b_ref[...]
    lane = lax.broadcasted_iota(jnp.int32, (tm, LANES), 1).astype(F32)
    neg_inf = -jnp.inf
    big = float(4 * LANES)
    is_g = lane < MOE_GROUPS
    gl = jnp.where(is_g, logits, neg_inf)
    gmax = jnp.max(gl, -1, keepdims=True)
    grp = jnp.min(jnp.where(gl == gmax, lane, big), -1, keepdims=True)
    p_grp = 1.0 / jnp.sum(jnp.where(is_g, jnp.exp(logits - gmax), 0.0), -1, keepdims=True)
    lo = MOE_GROUPS + MOE_PER_GROUP * grp
    el = jnp.where((lane >= lo) & (lane < lo + MOE_PER_GROUP), logits, neg_inf)
    v1 = jnp.max(el, -1, keepdims=True)
    i1 = jnp.min(jnp.where(el == v1, lane, big), -1, keepdims=True)
    el2 = jnp.where(lane == i1, neg_inf, el)
    v2 = jnp.max(el2, -1, keepdims=True)
    i2 = jnp.min(jnp.where(el2 == v2, lane, big), -1, keepdims=True)
    t = jnp.exp(v2 - v1)
    g1 = p_grp / (1.0 + t)
    g2 = g1 * t
    e1 = i1 - MOE_GROUPS
    e2 = i2 - MOE_GROUPS
    oh1 = jnp.where(lane == e1, 1.0, 0.0)
    oh2 = jnp.where(lane == e2, 1.0, 0.0)
    oh = oh1 + oh2
    tot = base_ref[...] + jnp.dot(tri_ref[...], oh.astype(BF16), preferred_element_type=F32)
    r1 = jnp.sum(oh1 * tot, -1, keepdims=True)
    r2 = jnp.sum(oh2 * tot, -1, keepdims=True)
    new_base = base_ref[...] + jnp.sum(oh, 0, keepdims=True)
    base_ref[...] = new_base
    cnt_ref[...] = jnp.broadcast_to(new_base, cnt_ref.shape)
    route = jnp.zeros((tm, LANES), F32)
    for k, val in enumerate((e1, e2, g1, g2, r1, r2)):
        route = jnp.where(lane == float(k), val, route)
    route_ref[...] = route


def _router(h, w_r, b_r, *, tm):
    n, d = h.shape
    return pl.pallas_call(
        _router_kernel,
        grid=(n // tm,),
        in_specs=[pl.BlockSpec((tm, d), lambda i: (i, 0)),
                  pl.BlockSpec((d, 2 * LANES), lambda i: (0, 0)),
                  pl.BlockSpec((1, LANES), lambda i: (0, 0))],
        out_specs=[pl.BlockSpec((tm, LANES), lambda i: (i, 0)),
                   pl.BlockSpec((8, LANES), lambda i: (0, 0))],
        out_shape=[jax.ShapeDtypeStruct((n, LANES), F32), jax.ShapeDtypeStruct((8, LANES), F32)],
        scratch_shapes=[pltpu.VMEM((1, LANES), F32), pltpu.VMEM((tm, tm), BF16)],
        compiler_params=_params("arbitrary"),
        name="moe_router",
    )(h, w_r, b_r)


def _sc_index_rows(idx):
    n = idx.shape[0]
    return jnp.pad(idx.reshape(n // SC_ROWS, SC_ROWS), ((0, 0), (0, SC_INDEX_LANES - SC_ROWS)))


def _sc_mesh():
    return plsc.VectorSubcoreMesh(core_axis_name="core", subcore_axis_name="subcore")


def _sc_gather_rows(y, idx):
    n = idx.shape[0]
    d = y.shape[1]

    @pl.kernel(out_type=jax.ShapeDtypeStruct((n, d), y.dtype), mesh=_sc_mesh(), scratch_types=[])
    def gather(y_hbm, i_hbm, o_hbm):
        def body(i_vmem, o_vmem):
            pltpu.sync_copy(y_hbm.at[i_vmem.at[0, pl.ds(0, SC_ROWS)]], o_vmem)

        pltpu.emit_pipeline(
            body, grid=(n // SC_ROWS,),
            in_specs=[pl.BlockSpec((1, SC_INDEX_LANES), lambda i: (i, 0))],
            out_specs=[pl.BlockSpec((SC_ROWS, d), lambda i: (i, 0))],
            core_axis_name=("core", "subcore"), dimension_semantics=(pltpu.PARALLEL,),
        )(i_hbm, o_hbm)

    return gather(y, _sc_index_rows(idx))


def _sc_scatter_rows(x, idx0, idx1, p_rows):
    n, d = x.shape

    @pl.kernel(out_type=jax.ShapeDtypeStruct((p_rows, d), x.dtype), mesh=_sc_mesh(), scratch_types=[])
    def scatter(x_hbm, i0_hbm, i1_hbm, o_hbm):
        def body(x_vmem, i0_vmem, i1_vmem):
            pltpu.sync_copy(x_vmem, o_hbm.at[i0_vmem.at[0, pl.ds(0, SC_ROWS)]])
            pltpu.sync_copy(x_vmem, o_hbm.at[i1_vmem.at[0, pl.ds(0, SC_ROWS)]])

        pltpu.emit_pipeline(
            body, grid=(n // SC_ROWS,),
            in_specs=[pl.BlockSpec((SC_ROWS, d), lambda i: (i, 0)),
                      pl.BlockSpec((1, SC_INDEX_LANES), lambda i: (i, 0)),
                      pl.BlockSpec((1, SC_INDEX_LANES), lambda i: (i, 0))],
            out_specs=[],
            core_axis_name=("core", "subcore"), dimension_semantics=(pltpu.PARALLEL,),
        )(x_hbm, i0_hbm, i1_hbm)

    return scatter(x, _sc_index_rows(idx0), _sc_index_rows(idx1))


def _moe(h, hp, wg_r, bg_r, we_r, be_r, w_gate, w_up, w_down, layer, ln_g, ln_b):
    n, d = h.shape
    pad = LANES - MOE_GROUPS - MOE_EXPERTS
    w_r = jnp.pad(jnp.concatenate([wg_r, we_r], 1), ((0, 0), (0, pad)))
    w_hi = w_r.astype(BF16)
    w_lo = (w_r - w_hi.astype(F32)).astype(BF16)
    b_r = jnp.pad(jnp.concatenate([bg_r, be_r]), (0, pad)).reshape(1, LANES)
    route, cnt = _router(h, jnp.concatenate([w_hi, w_lo], 1), b_r, tm=512)
    counts = cnt[0, :MOE_EXPERTS].astype(jnp.int32)
    padded = (counts + MOE_BLOCK - 1) // MOE_BLOCK * MOE_BLOCK
    pends = jnp.cumsum(padded)
    pstarts = pends - padded
    eid = route[:, R_EID:R_EID + MOE_TOPK].astype(jnp.int32)
    rank = route[:, R_RANK:R_RANK + MOE_TOPK].astype(jnp.int32)
    experts = jnp.arange(MOE_EXPERTS, dtype=jnp.int32)
    dest = jnp.sum(jnp.where(eid[..., None] == experts, pstarts, 0), -1) + rank
    p_rows = n * MOE_TOPK + MOE_EXPERTS * MOE_BLOCK
    nb = p_rows // MOE_BLOCK
    blk_start = jnp.arange(nb, dtype=jnp.int32) * MOE_BLOCK
    blk_e = jnp.minimum(jnp.sum((pends[None, :] <= blk_start[:, None]).astype(jnp.int32), -1), MOE_EXPERTS - 1)
    dest0, dest1 = dest[:, 0], dest[:, 1]
    xb = _sc_scatter_rows(hp, dest0, dest1, p_rows)
    yb = _experts(blk_e, xb, w_gate, w_up, w_down, layer)
    return _combine_ln(h, _sc_gather_rows(yb, dest0), _sc_gather_rows(yb, dest1), route, ln_g, ln_b, tm=512)


def kernel(x, positions, ln1_g, ln1_b, ln2_g, ln2_b, a_w_in, a_w_out, b_w_in, b_gate_bias, b_conv_w, b_conv_b,
           b_norm_g, b_w_out, c_w_in, c_norm_g, c_w_out, r_group_w, r_group_b, r_expert_w, r_expert_b,
           e_w_gate, e_w_up, e_w_down):
    batch, seq, d = x.shape
    n = batch * seq
    tabs_a = _rope_tables_a(positions)
    inv_c = C_THETA ** (-jnp.arange(0, C_QK_DIM, 2, dtype=F32) / C_QK_DIM)
    ang_c = positions.astype(F32)[:, None] * inv_c[None, :]
    cos_c, sin_c = jnp.cos(ang_c), jnp.sin(ang_c)
    h = x.reshape(n, d)
    hb = h
    for i in range(DEPTH):
        kind, j = i % 3, i // 3
        if kind == 0:
            y = _mixer_dilated(hb, a_w_in[j], tabs_a, batch, seq)
            w_out = a_w_out[j]
        elif kind == 1:
            y = _mixer_mlstm(hb, b_w_in[j], b_gate_bias[j], b_conv_w[j], b_conv_b[j], b_norm_g[j], batch, seq)
            w_out = b_w_out[j]
        else:
            y = _mixer_retention(hb, c_w_in[j], c_norm_g[j], cos_c, sin_c, batch, seq)
            w_out = c_w_out[j]
        h, hp = _matmul_res_ln(y, w_out.astype(BF16), h, ln1_g[i], ln1_b[i], tm=512)
        h, hb = _moe(h, hp, r_group_w[i], r_group_b[i], r_expert_w[i], r_expert_b[i],
                     e_w_gate, e_w_up, e_w_down, i, ln2_g[i], ln2_b[i])
    return h.reshape(batch, seq, d)
```

```python
import functools

import jax
import jax.numpy as jnp
from jax import lax
from jax.experimental import pallas as pl
from jax.experimental.pallas import tpu as pltpu
from jax.experimental.pallas import tpu_sc as plsc

F32 = jnp.float32
BF16 = jnp.bfloat16

D_MODEL = 1024
DEPTH = 4
DN_ALPHA = (2.0 * DEPTH) ** 0.25
LN_EPS = 1e-5

A_HEADS = 16
A_HEAD_DIM = 64
A_DILATIONS = (1, 4, 16)
A_BLOCK = 128
A_UNROLL = 16
A_GROUP = 16
A_PITCH = 20
LOG2_E = 1.4426950408889634
A_ROT_DIM = 16
ROPE_THETA = 500000.0

B_HEADS = 8
B_QK_DIM = 64
B_V_DIM = 128
B_CONV = 4
B_CHUNK = 256
C_HEADS = 4
C_QK_DIM = 256
C_V_DIM = 512
C_CHUNK = 256
C_THETA = 10000.0

MOE_GROUPS = 8
MOE_PER_GROUP = 8
MOE_EXPERTS = 64
MOE_TOPK = 2
MOE_HIDDEN = 256
MOE_BLOCK = 512
SC_ROWS = 64
SC_INDEX_LANES = 128

LANES = 128
NEG = -1e30
VMEM_LIMIT = 48 * 1024 * 1024


def _params(*sem):
    return pltpu.CompilerParams(dimension_semantics=sem, vmem_limit_bytes=VMEM_LIMIT)


def _mm_kernel(x_ref, w_ref, o_ref):
    o_ref[...] = jnp.dot(x_ref[...], w_ref[...], preferred_element_type=F32).astype(o_ref.dtype)


def _matmul(x, w, *, tm, tn, out_dtype=F32):
    n, k = x.shape
    m = w.shape[1]
    return pl.pallas_call(
        _mm_kernel,
        grid=(n // tm, m // tn),
        in_specs=[pl.BlockSpec((tm, k), lambda i, j: (i, 0)),
                  pl.BlockSpec((k, tn), lambda i, j: (0, j))],
        out_specs=pl.BlockSpec((tm, tn), lambda i, j: (i, j)),
        out_shape=jax.ShapeDtypeStruct((n, m), out_dtype),
        compiler_params=_params("parallel", "parallel"),
        name="matmul",
    )(x, w)


def _layer_norm_rows(z, g, b):
    mu = jnp.mean(z, -1, keepdims=True)
    zc = z - mu
    var = jnp.mean(zc * zc, -1, keepdims=True)
    return zc * lax.rsqrt(var + LN_EPS) * g + b


def _pack_pairs(x):
    c = x.shape[1] // 2
    hi = pltpu.bitcast(x[:, :c].astype(BF16).astype(F32), jnp.uint32)
    lo = pltpu.bitcast(x[:, c:].astype(BF16).astype(F32), jnp.uint32)
    return pltpu.bitcast(hi | (lo >> 16), F32)


def _unpack_pairs(w):
    bits = pltpu.bitcast(w, jnp.uint32)
    hi = pltpu.bitcast(bits & jnp.uint32(0xFFFF0000), F32)
    lo = pltpu.bitcast(bits << 16, F32)
    return jnp.concatenate([hi, lo], axis=1)


def _proj_a_kernel(x_ref, w_ref, c_ref, s1_ref, s2_ref, q_ref, k_ref, v_ref):
    x = x_ref[...].astype(BF16)
    width = 2 * LANES
    for c, ref in ((0, q_ref), (1, k_ref), (2, v_ref)):
        for j in range(D_MODEL // width):
            col = c * D_MODEL + j * width
            y = jnp.dot(x, w_ref[:, col:col + width], preferred_element_type=F32)
            if c < 2:
                half = A_ROT_DIM // 2
                y = (y * c_ref[...] + pltpu.roll(y, width - half, 1) * s1_ref[...]
                     + pltpu.roll(y, half, 1) * s2_ref[...])
            if c == 0:
                y = y * (LOG2_E * A_HEAD_DIM ** -0.5)
            ref[0, 2 * j] = y[:, :LANES]
            ref[0, 2 * j + 1] = y[:, LANES:]


def _proj_a(xb, w, tabs, batch, seq, *, tm):
    n, d = xb.shape
    spb = seq // tm
    hp = D_MODEL // LANES
    qkv_shape = jax.ShapeDtypeStruct((batch, hp, seq, LANES), F32)
    out_spec = pl.BlockSpec((1, hp, tm, LANES), lambda i: (i // spb, 0, i % spb, 0))
    tab_spec = pl.BlockSpec((tm, 2 * LANES), lambda i: (i % spb, 0))
    return pl.pallas_call(
        _proj_a_kernel,
        grid=(n // tm,),
        in_specs=[pl.BlockSpec((tm, d), lambda i: (i, 0)),
                  pl.BlockSpec((d, 3 * d), lambda i: (0, 0)),
                  tab_spec, tab_spec, tab_spec],
        out_specs=[out_spec, out_spec, out_spec],
        out_shape=[qkv_shape, qkv_shape, qkv_shape],
        compiler_params=_params("parallel"),
        name="proj_a",
    )(xb, w, *tabs)


def _rope_tables_a(positions):
    half = A_ROT_DIM // 2
    inv = ROPE_THETA ** (-jnp.arange(0, A_ROT_DIM, 2, dtype=F32) / A_ROT_DIM)
    ang = positions.astype(F32)[:, None] * inv[None, :]
    cos, sin = jnp.cos(ang), jnp.sin(ang)
    s = positions.shape[0]
    pad = jnp.zeros((s, A_HEAD_DIM - A_ROT_DIM), F32)
    c_head = jnp.concatenate([cos, cos, pad + 1.0], -1)
    s1_head = jnp.concatenate([-sin, jnp.zeros_like(sin), pad], -1)
    s2_head = jnp.concatenate([jnp.zeros_like(sin), sin, pad], -1)
    reps = 2 * LANES // A_HEAD_DIM
    return tuple(jnp.tile(t, (1, reps)) for t in (c_head, s1_head, s2_head))


def _attn_blocks(q_ref, k_ref, v_ref, o_ref, lse_ref, bias_ref, hmask, head0, first_block, d, pitch):
    nk = 2 * A_BLOCK
    loaded = []
    for u in range(A_UNROLL):
        g = first_block + u
        r = g % d
        n = g // d
        qstart = n * (A_BLOCK * pitch) + r
        kstart = jnp.maximum(qstart - A_BLOCK * pitch, r)
        if pitch == 1:
            qstart = pl.multiple_of(qstart, A_BLOCK)
            kstart = pl.multiple_of(kstart, A_BLOCK)
            qsl, ksl = pl.ds(qstart, A_BLOCK), pl.ds(kstart, nk)
        else:
            qsl, ksl = pl.ds(qstart, A_BLOCK, stride=pitch), pl.ds(kstart, nk, stride=pitch)
        bias = bias_ref[jnp.minimum(n, 1)]
        loaded.append((qsl, q_ref[qsl, :].astype(BF16), k_ref[ksl, :].astype(BF16), v_ref[ksl, :].astype(BF16), bias))
    results = []
    for qsl, qb, kb, vb, bias in loaded:
        pvs, ms, ls = [], [], []
        for h in range(2):
            s = lax.dot_general(qb * hmask[h], kb, (((1,), (1,)), ((), ())), preferred_element_type=F32) + bias
            m = jnp.max(s, -1, keepdims=True)
            p = jnp.exp2(s - m)
            ms.append(jnp.broadcast_to(m, (A_BLOCK, LANES)))
            ls.append(jnp.broadcast_to(jnp.sum(p, -1, keepdims=True), (A_BLOCK, LANES)))
            pvs.append(jnp.dot(p.astype(BF16), vb, preferred_element_type=F32))
        l = jnp.where(head0, ls[0], ls[1])
        out = jnp.where(head0, pvs[0], pvs[1]) * (1.0 / l)
        results.append((qsl, out, jnp.where(head0, ms[0], ms[1]) + jnp.log2(l)))
    for qsl, out, lse in results:
        o_ref[qsl, :] = out
        lse_ref[qsl, :] = lse


def _attn_kernel(q_ref, k_ref, v_ref, o_ref, q16_ref, k16_ref, v16_ref, ob_ref, lb_ref, o16_ref, l16_ref,
                 bias_ref):
    seq = q_ref.shape[0]
    groups = seq // A_GROUP
    head0 = lax.broadcasted_iota(jnp.int32, (A_BLOCK, LANES), 1) < A_HEAD_DIM
    hmask = [jnp.where(head0, 1.0, 0.0).astype(BF16), jnp.where(head0, 0.0, 1.0).astype(BF16)]
    qi = lax.broadcasted_iota(jnp.int32, (A_BLOCK, 2 * A_BLOCK), 0)
    kj = lax.broadcasted_iota(jnp.int32, (A_BLOCK, 2 * A_BLOCK), 1)
    bias_ref[0] = jnp.where(kj <= qi, 0.0, NEG).astype(F32)
    bias_ref[1] = jnp.where((kj >= qi) & (kj <= qi + A_BLOCK), 0.0, NEG).astype(F32)

    def spread(g, carry):
        src = pl.ds(pl.multiple_of(g * A_GROUP, A_GROUP), A_GROUP)
        dst = pl.ds(pl.multiple_of(g * A_PITCH, 4), A_GROUP)
        q16_ref[dst, :] = q_ref[src, :]
        k16_ref[dst, :] = k_ref[src, :]
        v16_ref[dst, :] = v_ref[src, :]
        return carry

    lax.fori_loop(0, groups, spread, 0, unroll=8)

    branches = ((1, 1, q_ref, k_ref, v_ref, ob_ref.at[0], lb_ref.at[0]),
                (4, 4, q_ref, k_ref, v_ref, ob_ref.at[1], lb_ref.at[1]),
                (16, A_PITCH, q16_ref, k16_ref, v16_ref, o16_ref, l16_ref))
    for d, pitch, qr, kr, vr, orf, lrf in branches:

        def body(it, carry, d=d, pitch=pitch, qr=qr, kr=kr, vr=vr, orf=orf, lrf=lrf):
            _attn_blocks(qr, kr, vr, orf, lrf, bias_ref, hmask, head0, it * A_UNROLL, d, pitch)
            return carry

        lax.fori_loop(0, seq // (A_BLOCK * A_UNROLL), body, 0)

    def mix(g, carry):
        nat = pl.ds(pl.multiple_of(g * A_GROUP, A_GROUP), A_GROUP)
        pad = pl.ds(pl.multiple_of(g * A_PITCH, 4), A_GROUP)
        o0, o1, o2 = ob_ref[0, nat, :], ob_ref[1, nat, :], o16_ref[pad, :]
        l0, l1, l2 = lb_ref[0, nat, :], lb_ref[1, nat, :], l16_ref[pad, :]
        mx = jnp.maximum(jnp.maximum(l0, l1), l2)
        w0, w1, w2 = jnp.exp2(l0 - mx), jnp.exp2(l1 - mx), jnp.exp2(l2 - mx)
        o_ref[nat, :] = ((w0 * o0 + w1 * o1 + w2 * o2) / (w0 + w1 + w2)).astype(o_ref.dtype)
        return carry

    lax.fori_loop(0, groups, mix, 0, unroll=8)


def _attention(q, k, v):
    batch, hp, seq, _ = q.shape
    assert seq % (2 * A_BLOCK * max(A_DILATIONS)) == 0 and seq % (A_BLOCK * A_UNROLL) == 0
    in_spec = pl.BlockSpec((None, None, seq, LANES), lambda b, p: (b, p, 0, 0))
    padded = seq // A_GROUP * A_PITCH
    return pl.pallas_call(
        _attn_kernel,
        grid=(batch, hp),
        in_specs=[in_spec, in_spec, in_spec],
        out_specs=pl.BlockSpec((None, seq, LANES), lambda b, p: (b, 0, p)),
        out_shape=jax.ShapeDtypeStruct((batch, seq, hp * LANES), BF16),
        scratch_shapes=[pltpu.VMEM((padded, LANES), F32)] * 3
        + [pltpu.VMEM((2, seq, LANES), F32)] * 2
        + [pltpu.VMEM((padded, LANES), F32)] * 2
        + [pltpu.VMEM((2, A_BLOCK, 2 * A_BLOCK), F32)],
        compiler_params=_params("parallel", "parallel"),
        name="dilated_attention",
    )(q, k, v)


def _mixer_dilated(hb, w_in, tabs, batch, seq):
    q, k, v = _proj_a(hb, w_in.astype(BF16), tabs, batch, seq, tm=512)
    o = _attention(q, k, v)
    return o.reshape(batch * seq, D_MODEL)


def _mlstm_kernel(qk_ref, v_ref, o_ref, gc_ref, gb_ref, cw_ref, cb_ref, ng_ref, out_ref,
                  ext_ref, c_ref, n_ref, m_ref):
    L = B_CHUNK
    chunk = pl.program_id(1)

    @pl.when(chunk == 0)
    def _():
        ext_ref[0:8, :] = jnp.zeros((8, D_MODEL), F32)
        c_ref[...] = jnp.zeros(c_ref.shape, F32)
        n_ref[...] = jnp.zeros(n_ref.shape, F32)
        m_ref[...] = jnp.zeros(m_ref.shape, F32)

    u = qk_ref[...].astype(F32)
    ext_ref[8:8 + L, :] = u
    conv = u * cw_ref[B_CONV - 1:B_CONV, :] + cb_ref[...]
    for j in range(1, B_CONV):
        conv = conv + ext_ref[pl.ds(8 - j, L), :] * cw_ref[B_CONV - 1 - j:B_CONV - j, :]
    ext_ref[0:8, :] = u[L - 8:, :]
    qk = conv * jax.nn.sigmoid(conv)
    half = D_MODEL // 2

    gc = gc_ref[...] + gb_ref[...]
    gr = gc.T
    i_col, i_row = gc, gr[:B_HEADS, :]
    lf_col = jax.nn.log_sigmoid(gc)
    lf_row = jax.nn.log_sigmoid(gr[B_HEADS:2 * B_HEADS, :])
    ti = lax.broadcasted_iota(jnp.int32, (L, L), 0)
    si = lax.broadcasted_iota(jnp.int32, (L, L), 1)
    causal = ti >= si
    tri = causal.astype(F32)
    a_col = jnp.dot(tri, lf_col, preferred_element_type=F32, precision=lax.Precision.HIGHEST)
    a_row = lax.dot_general(lf_row, tri, (((1,), (1,)), ((), ())), preferred_element_type=F32,
                            precision=lax.Precision.HIGHEST)
    lane = lax.broadcasted_iota(jnp.int32, (1, LANES), 1)
    lane_h0 = lane < B_QK_DIM
    col_h0 = lax.broadcasted_iota(jnp.int32, (1, 2 * B_V_DIM), 1) < B_V_DIM

    for p in range(B_HEADS // 2):
        qp = qk[:, p * LANES:(p + 1) * LANES]
        kp = qk[:, half + p * LANES:half + (p + 1) * LANES] * (B_QK_DIM ** -0.5)
        vp = v_ref[:, p * 2 * B_V_DIM:(p + 1) * 2 * B_V_DIM]
        kpb = kp.astype(BF16)
        vpb = vp.astype(BF16)
        c_old = c_ref[p]
        c_oldb = c_old.astype(BF16)
        n_old = n_ref[p]
        m_pair = m_ref[p]
        ws_cols, decays, m_news = [], [], []
        for hh in range(2):
            h = 2 * p + hh
            m_old = m_pair[:, hh * B_QK_DIM:hh * B_QK_DIM + 1]
            ac, ar = a_col[:, B_HEADS + h:B_HEADS + h + 1], a_row[h:h + 1, :]
            ic, ir = i_col[:, h:h + 1], i_row[h:h + 1, :]
            dmat = jnp.where(causal, ac - ar + ir, NEG)
            inter = ac + m_old
            m_t = jnp.maximum(inter, jnp.max(dmat, -1, keepdims=True))
            qm = jnp.where(lane_h0 if hh == 0 else jnp.logical_not(lane_h0), qp, 0.0)
            qmb = qm.astype(BF16)
            sc = lax.dot_general(qmb, kpb, (((1,), (1,)), ((), ())), preferred_element_type=F32)
            sc = sc * jnp.exp(dmat - m_t)
            g_inter = jnp.exp(inter - m_t)
            vh = vpb[:, hh * B_V_DIM:(hh + 1) * B_V_DIM]
            qc = jnp.dot(qmb, c_oldb, preferred_element_type=F32)[:, hh * B_V_DIM:(hh + 1) * B_V_DIM]
            num = jnp.dot(sc.astype(BF16), vh, preferred_element_type=F32) + g_inter * qc
            den = jnp.sum(sc, -1, keepdims=True) + g_inter * jnp.sum(qm * n_old, -1, keepdims=True)
            h_out = num / jnp.maximum(jnp.abs(den), jnp.exp(-m_t))
            mu = jnp.mean(h_out, -1, keepdims=True)
            hc = h_out - mu
            var = jnp.mean(hc * hc, -1, keepdims=True)
            cols = slice(h * B_V_DIM, (h + 1) * B_V_DIM)
            hn = hc * lax.rsqrt(var + LN_EPS) * ng_ref[:, cols]
            out_ref[:, cols] = (hn * jax.nn.sigmoid(o_ref[:, cols].astype(F32))).astype(out_ref.dtype)
            a_end = ac[L - 1:L, :]
            w_col = a_end - ac + ic
            m_new = jnp.maximum(a_end + m_old, jnp.max(w_col, 0, keepdims=True))
            decays.append(jnp.exp(a_end + m_old - m_new))
            ws_cols.append(jnp.exp(w_col - m_new))
            m_news.append(m_new)
        ws = jnp.where(lane_h0, jnp.broadcast_to(ws_cols[0], (L, LANES)), jnp.broadcast_to(ws_cols[1], (L, LANES)))
        kw = kp * ws
        dec_c = jnp.where(col_h0, jnp.broadcast_to(decays[0], (1, 2 * B_V_DIM)),
                          jnp.broadcast_to(decays[1], (1, 2 * B_V_DIM)))
        dec_n = jnp.where(lane_h0, jnp.broadcast_to(decays[0], (1, LANES)), jnp.broadcast_to(decays[1], (1, LANES)))
        c_ref[p] = dec_c * c_old + lax.dot_general(kw.astype(BF16), vpb, (((0,), (0,)), ((), ())),
                                                   preferred_element_type=F32)
        n_ref[p] = dec_n * n_old + jnp.sum(kw, 0, keepdims=True)
        m_ref[p] = jnp.where(lane_h0, jnp.broadcast_to(m_news[0], (1, LANES)), jnp.broadcast_to(m_news[1], (1, LANES)))


def _mlstm(proj, gates, gate_bias, conv_w, conv_b, norm_g, batch, seq):
    L = B_CHUNK
    d = D_MODEL
    slab = lambda c: pl.BlockSpec((None, L, d), lambda b, s, c=c: (b, s, c))
    full = lambda shape: pl.BlockSpec(shape, lambda b, s: (0,) * len(shape))
    return pl.pallas_call(
        _mlstm_kernel,
        grid=(batch, seq // L),
        in_specs=[slab(0), slab(1), slab(2),
                  pl.BlockSpec((None, L, LANES), lambda b, s: (b, s, 0)),
                  full((1, LANES)),
                  full((B_CONV, d)), full((1, d)), full((1, d))],
        out_specs=pl.BlockSpec((None, L, d), lambda b, s: (b, s, 0)),
        out_shape=jax.ShapeDtypeStruct((batch, seq, d), BF16),
        scratch_shapes=[pltpu.VMEM((L + 8, d), F32),
                        pltpu.VMEM((B_HEADS // 2, 2 * B_QK_DIM, 2 * B_V_DIM), F32),
                        pltpu.VMEM((B_HEADS // 2, 1, LANES), F32),
                        pltpu.VMEM((B_HEADS // 2, 1, LANES), F32)],
        compiler_params=_params("parallel", "arbitrary"),
        name="mlstm",
    )(proj, proj, proj, gates, jnp.pad(gate_bias, (0, LANES - 2 * B_HEADS)).reshape(1, LANES),
      conv_w, conv_b.reshape(1, d), norm_g.reshape(1, d))


def _mixer_mlstm(hb, w_in, gate_bias, conv_w, conv_b, norm_g, batch, seq):
    n = batch * seq
    main = 3 * D_MODEL
    proj = _matmul(hb, w_in[:, :main].astype(BF16), tm=1024, tn=1024, out_dtype=BF16)
    w_g = jnp.pad(w_in[:, main:], ((0, 0), (0, LANES - 2 * B_HEADS))).astype(BF16)
    gates = _matmul(hb, w_g, tm=2048, tn=LANES)
    out = _mlstm(proj.reshape(batch, seq, main), gates.reshape(batch, seq, LANES), gate_bias,
                 conv_w, conv_b, norm_g, batch, seq)
    return out.reshape(n, D_MODEL)


def _retention_kernel(lg_ref, q_ref, k_ref, v_ref, g_ref, cos_ref, sin_ref, ng_ref, out_ref,
                      r_ref, dm_ref, xi_ref, zeta_ref):
    L = C_CHUNK
    chunk = pl.program_id(1)

    @pl.when(chunk == 0)
    def _():
        r_ref[...] = jnp.zeros(r_ref.shape, F32)
        ti = lax.broadcasted_iota(jnp.int32, (L, L), 0)
        si = lax.broadcasted_iota(jnp.int32, (L, L), 1)
        rel = (ti - si).astype(F32)
        idx = lax.broadcasted_iota(jnp.int32, (L, LANES), 0).astype(F32)
        for h in range(C_HEADS):
            lg = lg_ref[h]
            dm_ref[h] = jnp.where(rel >= 0, jnp.exp(jnp.maximum(rel, 0.0) * lg), 0.0)
            xi_ref[h] = jnp.exp((idx + 1.0) * lg)
            zeta_ref[h] = jnp.exp((L - 1.0 - idx) * lg)

    cos, sin = cos_ref[...], sin_ref[...]
    hd = C_QK_DIM // 2

    def rope(t):
        t1, t2 = t[:, :hd], t[:, hd:]
        return jnp.concatenate([t1 * cos - t2 * sin, t2 * cos + t1 * sin], -1)

    for h in range(C_HEADS):
        qk_cols = slice(h * C_QK_DIM, (h + 1) * C_QK_DIM)
        v_cols = slice(h * C_V_DIM, (h + 1) * C_V_DIM)
        q = rope(q_ref[:, qk_cols].astype(F32))
        k = rope(k_ref[:, qk_cols].astype(F32)) * (C_QK_DIM ** -0.5)
        qb = q.astype(BF16)
        vb = v_ref[:, v_cols]
        r_old = r_ref[h]
        sc = lax.dot_general(qb, k.astype(BF16), (((1,), (1,)), ((), ())), preferred_element_type=F32) * dm_ref[h]
        o = jnp.dot(sc.astype(BF16), vb, preferred_element_type=F32)
        o = o + xi_ref[h, :, 0:1] * jnp.dot(qb, r_old.astype(BF16), preferred_element_type=F32)
        kz = (k * zeta_ref[h, :, 0:1]).astype(BF16)
        cd = jnp.exp(jnp.full((1, 1), float(L), F32) * lg_ref[h])
        r_ref[h] = cd * r_old + lax.dot_general(kz, vb, (((0,), (0,)), ((), ())), preferred_element_type=F32)
        mu = jnp.mean(o, -1, keepdims=True)
        oc = o - mu
        var = jnp.mean(oc * oc, -1, keepdims=True)
        on = oc * lax.rsqrt(var + LN_EPS) * ng_ref[:, v_cols]
        g = g_ref[:, v_cols].astype(F32)
        out_ref[:, v_cols] = (on * (g * jax.nn.sigmoid(g))).astype(out_ref.dtype)


def _retention(proj, cos, sin, norm_g, batch, seq):
    L = C_CHUNK
    d = D_MODEL
    log_gamma = jnp.log(1.0 - 2.0 ** (-5.0 - jnp.arange(C_HEADS, dtype=F32)))
    return pl.pallas_call(
        _retention_kernel,
        grid=(batch, seq // L),
        in_specs=[pl.BlockSpec(memory_space=pltpu.SMEM),
                  pl.BlockSpec((None, L, d), lambda b, c: (b, c, 0)),
                  pl.BlockSpec((None, L, d), lambda b, c: (b, c, 1)),
                  pl.BlockSpec((None, L, 2 * d), lambda b, c: (b, c, 1)),
                  pl.BlockSpec((None, L, 2 * d), lambda b, c: (b, c, 2)),
                  pl.BlockSpec((L, C_QK_DIM // 2), lambda b, c: (c, 0)),
                  pl.BlockSpec((L, C_QK_DIM // 2), lambda b, c: (c, 0)),
                  pl.BlockSpec((1, 2 * d), lambda b, c: (0, 0))],
        out_specs=pl.BlockSpec((None, L, 2 * d), lambda b, c: (b, c, 0)),
        out_shape=jax.ShapeDtypeStruct((batch, seq, 2 * d), BF16),
        scratch_shapes=[pltpu.VMEM((C_HEADS, C_QK_DIM, C_V_DIM), F32),
                        pltpu.VMEM((C_HEADS, L, L), F32),
                        pltpu.VMEM((C_HEADS, L, LANES), F32),
                        pltpu.VMEM((C_HEADS, L, LANES), F32)],
        compiler_params=_params("parallel", "arbitrary"),
        name="retention",
    )(log_gamma, proj, proj, proj, proj, cos, sin, norm_g.reshape(1, 2 * d))


def _mixer_retention(hb, w_in, norm_g, cos, sin, batch, seq):
    proj = _matmul(hb, w_in.astype(BF16), tm=1024, tn=1024, out_dtype=BF16)
    out = _retention(proj.reshape(batch, seq, 6 * D_MODEL), cos, sin, norm_g, batch, seq)
    return out.reshape(batch * seq, 2 * D_MODEL)


def _expert_kernel(be_ref, x_ref, wg_ref, wu_ref, wd_ref, y_ref, wgb_ref, wub_ref, wdb_ref):
    j = pl.program_id(0)
    changed = jnp.logical_or(j == 0, be_ref[j] != be_ref[jnp.maximum(j - 1, 0)])

    @pl.when(changed)
    def _():
        wgb_ref[...] = wg_ref[...].astype(BF16)
        wub_ref[...] = wu_ref[...].astype(BF16)
        wdb_ref[...] = wd_ref[...].astype(BF16)

    x = _unpack_pairs(x_ref[...]).astype(BF16)
    a = jnp.dot(x, wgb_ref[...], preferred_element_type=F32)
    u = jnp.dot(x, wub_ref[...], preferred_element_type=F32)
    act = (a * jax.nn.sigmoid(a) * u).astype(BF16)
    y_ref[...] = _pack_pairs(jnp.dot(act, wdb_ref[...], preferred_element_type=F32))


def _experts(blk_e, xb, w_gate, w_up, w_down, layer):
    p, dw = xb.shape
    d = 2 * dw
    nb = p // MOE_BLOCK
    hid = MOE_HIDDEN
    grid_spec = pltpu.PrefetchScalarGridSpec(
        num_scalar_prefetch=1,
        grid=(nb,),
        in_specs=[pl.BlockSpec((MOE_BLOCK, dw), lambda j, be: (j, 0)),
                  pl.BlockSpec((None, None, d, hid), lambda j, be: (layer, be[j], 0, 0)),
                  pl.BlockSpec((None, None, d, hid), lambda j, be: (layer, be[j], 0, 0)),
                  pl.BlockSpec((None, None, hid, d), lambda j, be: (layer, be[j], 0, 0))],
        out_specs=pl.BlockSpec((MOE_BLOCK, dw), lambda j, be: (j, 0)),
        scratch_shapes=[pltpu.VMEM((d, hid), BF16), pltpu.VMEM((d, hid), BF16), pltpu.VMEM((hid, d), BF16)],
    )
    return pl.pallas_call(
        _expert_kernel,
        grid_spec=grid_spec,
        out_shape=jax.ShapeDtypeStruct((p, dw), F32),
        compiler_params=_params("arbitrary"),
        name="moe_experts",
    )(blk_e, xb, w_gate, w_up, w_down)


def _combine_ln_kernel(h_ref, y0_ref, y1_ref, rt_ref, g_ref, b_ref, o_ref, ob_ref):
    rt = rt_ref[...]
    y = (_unpack_pairs(y0_ref[...]) * rt[:, R_GATE:R_GATE + 1]
         + _unpack_pairs(y1_ref[...]) * rt[:, R_GATE + 1:R_GATE + 2])
    out = _layer_norm_rows(DN_ALPHA * h_ref[...] + y, g_ref[...], b_ref[...])
    o_ref[...] = out
    ob_ref[...] = out.astype(BF16)


def _combine_ln(h, y0, y1, route, g, b, *, tm):
    n, d = h.shape
    row = pl.BlockSpec((tm, d), lambda i: (i, 0))
    words = pl.BlockSpec((tm, d // 2), lambda i: (i, 0))
    vec = pl.BlockSpec((1, d), lambda i: (0, 0))
    return pl.pallas_call(
        _combine_ln_kernel,
        grid=(n // tm,),
        in_specs=[row, words, words, pl.BlockSpec((tm, LANES), lambda i: (i, 0)), vec, vec],
        out_specs=[row, row],
        out_shape=[jax.ShapeDtypeStruct((n, d), F32), jax.ShapeDtypeStruct((n, d), BF16)],
        compiler_params=_params("parallel"),
        name="moe_combine_ln",
    )(h, y0, y1, route, g.reshape(1, d), b.reshape(1, d))


R_EID, R_GATE, R_RANK = 0, 2, 4


def _route_tile(h, w_ref, b_ref, route_ref, cnt_ref, base_ref, tri_ref):
    i = pl.program_id(0)
    tm = h.shape[0]

    @pl.when(i == 0)
    def _():
        base_ref[...] = jnp.zeros(base_ref.shape, F32)
        ti = lax.broadcasted_iota(jnp.int32, (tm, tm), 0)
        si = lax.broadcasted_iota(jnp.int32, (tm, tm), 1)
        tri_ref[...] = jnp.where(si < ti, 1.0, 0.0).astype(BF16)

    h_hi = h.astype(BF16)
    h_lo = (h - h_hi.astype(F32)).astype(BF16)
    hh = jnp.dot(h_hi, w_ref[...], preferred_element_type=F32)
    logits = (hh[:, :LANES] + hh[:, LANES:]
              + jnp.dot(h_lo, w_ref[:, :LANES], preferred_element_type=F32)) + b_ref[...]
    lane = lax.broadcasted_iota(jnp.int32, (tm, LANES), 1).astype(F32)
    neg_inf = -jnp.inf
    big = float(4 * LANES)
    is_g = lane < MOE_GROUPS
    gl = jnp.where(is_g, logits, neg_inf)
    gmax = jnp.max(gl, -1, keepdims=True)
    grp = jnp.min(jnp.where(gl == gmax, lane, big), -1, keepdims=True)
    p_grp = 1.0 / jnp.sum(jnp.where(is_g, jnp.exp(logits - gmax), 0.0), -1, keepdims=True)
    lo = MOE_GROUPS + MOE_PER_GROUP * grp
    el = jnp.where((lane >= lo) & (lane < lo + MOE_PER_GROUP), logits, neg_inf)
    v1 = jnp.max(el, -1, keepdims=True)
    i1 = jnp.min(jnp.where(el == v1, lane, big), -1, keepdims=True)
    el2 = jnp.where(lane == i1, neg_inf, el)
    v2 = jnp.max(el2, -1, keepdims=True)
    i2 = jnp.min(jnp.where(el2 == v2, lane, big), -1, keepdims=True)
    t = jnp.exp(v2 - v1)
    g1 = p_grp / (1.0 + t)
    g2 = g1 * t
    e1 = i1 - MOE_GROUPS
    e2 = i2 - MOE_GROUPS
    oh1 = jnp.where(lane == e1, 1.0, 0.0)
    oh2 = jnp.where(lane == e2, 1.0, 0.0)
    oh = oh1 + oh2
    tot = base_ref[...] + jnp.dot(tri_ref[...], oh.astype(BF16), preferred_element_type=F32)
    r1 = jnp.sum(oh1 * tot, -1, keepdims=True)
    r2 = jnp.sum(oh2 * tot, -1, keepdims=True)
    new_base = base_ref[...] + jnp.sum(oh, 0, keepdims=True)
    base_ref[...] = new_base
    cnt_ref[...] = jnp.broadcast_to(new_base, cnt_ref.shape)
    route = jnp.zeros((tm, LANES), F32)
    for k, val in enumerate((e1, e2, g1, g2, r1, r2)):
        route = jnp.where(lane == float(k), val, route)
    route_ref[...] = route


def _mm_res_ln_route_kernel(x_ref, w_ref, h_ref, g_ref, b_ref, rw_ref, rb_ref,
                            o_ref, op_ref, route_ref, cnt_ref, base_ref, tri_ref):
    y = jnp.dot(x_ref[...], w_ref[...], preferred_element_type=F32)
    out = _layer_norm_rows(DN_ALPHA * h_ref[...] + y, g_ref[...], b_ref[...])
    o_ref[...] = out
    op_ref[...] = _pack_pairs(out)
    _route_tile(out, rw_ref, rb_ref, route_ref, cnt_ref, base_ref, tri_ref)


def _matmul_res_ln_route(x, w, h, g, b, w_r, b_r, *, tm):
    n, k = x.shape
    d = w.shape[1]
    rows = lambda width: pl.BlockSpec((tm, width), lambda i: (i, 0))
    const = lambda shape: pl.BlockSpec(shape, lambda i: (0, 0))
    return pl.pallas_call(
        _mm_res_ln_route_kernel,
        grid=(n // tm,),
        in_specs=[rows(k), const((k, d)), rows(d), const((1, d)), const((1, d)),
                  const((d, 2 * LANES)), const((1, LANES))],
        out_specs=[rows(d), rows(d // 2), rows(LANES), const((8, LANES))],
        out_shape=[jax.ShapeDtypeStruct((n, d), F32), jax.ShapeDtypeStruct((n, d // 2), F32),
                   jax.ShapeDtypeStruct((n, LANES), F32), jax.ShapeDtypeStruct((8, LANES), F32)],
        scratch_shapes=[pltpu.VMEM((1, LANES), F32), pltpu.VMEM((tm, tm), BF16)],
        compiler_params=_params("arbitrary"),
        name="matmul_res_ln_route",
    )(x, w, h, g.reshape(1, d), b.reshape(1, d), w_r, b_r)


def _sc_index_rows(idx):
    n = idx.shape[0]
    return jnp.pad(idx.reshape(n // SC_ROWS, SC_ROWS), ((0, 0), (0, SC_INDEX_LANES - SC_ROWS)))


def _sc_mesh():
    return plsc.VectorSubcoreMesh(core_axis_name="core", subcore_axis_name="subcore")


def _sc_gather_rows(y, idx):
    n = idx.shape[0]
    d = y.shape[1]

    @pl.kernel(out_type=jax.ShapeDtypeStruct((n, d), y.dtype), mesh=_sc_mesh(), scratch_types=[])
    def gather(y_hbm, i_hbm, o_hbm):
        def body(i_vmem, o_vmem):
            pltpu.sync_copy(y_hbm.at[i_vmem.at[0, pl.ds(0, SC_ROWS)]], o_vmem)

        pltpu.emit_pipeline(
            body, grid=(n // SC_ROWS,),
            in_specs=[pl.BlockSpec((1, SC_INDEX_LANES), lambda i: (i, 0))],
            out_specs=[pl.BlockSpec((SC_ROWS, d), lambda i: (i, 0))],
            core_axis_name=("core", "subcore"), dimension_semantics=(pltpu.PARALLEL,),
        )(i_hbm, o_hbm)

    return gather(y, _sc_index_rows(idx))


def _sc_scatter_rows(x, idx0, idx1, p_rows):
    n, d = x.shape

    @pl.kernel(out_type=jax.ShapeDtypeStruct((p_rows, d), x.dtype), mesh=_sc_mesh(), scratch_types=[])
    def scatter(x_hbm, i0_hbm, i1_hbm, o_hbm):
        def body(x_vmem, i0_vmem, i1_vmem):
            pltpu.sync_copy(x_vmem, o_hbm.at[i0_vmem.at[0, pl.ds(0, SC_ROWS)]])
            pltpu.sync_copy(x_vmem, o_hbm.at[i1_vmem.at[0, pl.ds(0, SC_ROWS)]])

        pltpu.emit_pipeline(
            body, grid=(n // SC_ROWS,),
            in_specs=[pl.BlockSpec((SC_ROWS, d), lambda i: (i, 0)),
                      pl.BlockSpec((1, SC_INDEX_LANES), lambda i: (i, 0)),
                      pl.BlockSpec((1, SC_INDEX_LANES), lambda i: (i, 0))],
            out_specs=[],
            core_axis_name=("core", "subcore"), dimension_semantics=(pltpu.PARALLEL,),
        )(x_hbm, i0_hbm, i1_hbm)

    return scatter(x, _sc_index_rows(idx0), _sc_index_rows(idx1))


def _router_params(wg_r, bg_r, we_r, be_r):
    pad = LANES - MOE_GROUPS - MOE_EXPERTS
    w_r = jnp.pad(jnp.concatenate([wg_r, we_r], 1), ((0, 0), (0, pad)))
    w_hi = w_r.astype(BF16)
    w_lo = (w_r - w_hi.astype(F32)).astype(BF16)
    b_r = jnp.pad(jnp.concatenate([bg_r, be_r]), (0, pad)).reshape(1, LANES)
    return jnp.concatenate([w_hi, w_lo], 1), b_r


def _moe(h, hp, route, cnt, w_gate, w_up, w_down, layer, ln_g, ln_b):
    n, d = h.shape
    counts = cnt[0, :MOE_EXPERTS].astype(jnp.int32)
    padded = (counts + MOE_BLOCK - 1) // MOE_BLOCK * MOE_BLOCK
    pends = jnp.cumsum(padded)
    pstarts = pends - padded
    eid = route[:, R_EID:R_EID + MOE_TOPK].astype(jnp.int32)
    rank = route[:, R_RANK:R_RANK + MOE_TOPK].astype(jnp.int32)
    experts = jnp.arange(MOE_EXPERTS, dtype=jnp.int32)
    dest = jnp.sum(jnp.where(eid[..., None] == experts, pstarts, 0), -1) + rank
    p_rows = n * MOE_TOPK + MOE_EXPERTS * MOE_BLOCK
    nb = p_rows // MOE_BLOCK
    blk_start = jnp.arange(nb, dtype=jnp.int32) * MOE_BLOCK
    blk_e = jnp.minimum(jnp.sum((pends[None, :] <= blk_start[:, None]).astype(jnp.int32), -1), MOE_EXPERTS - 1)
    dest0, dest1 = dest[:, 0], dest[:, 1]
    xb = _sc_scatter_rows(hp, dest0, dest1, p_rows)
    yb = _experts(blk_e, xb, w_gate, w_up, w_down, layer)
    return _combine_ln(h, _sc_gather_rows(yb, dest0), _sc_gather_rows(yb, dest1), route, ln_g, ln_b, tm=512)


def kernel(x, positions, ln1_g, ln1_b, ln2_g, ln2_b, a_w_in, a_w_out, b_w_in, b_gate_bias, b_conv_w, b_conv_b,
           b_norm_g, b_w_out, c_w_in, c_norm_g, c_w_out, r_group_w, r_group_b, r_expert_w, r_expert_b,
           e_w_gate, e_w_up, e_w_down):
    batch, seq, d = x.shape
    n = batch * seq
    tabs_a = _rope_tables_a(positions)
    inv_c = C_THETA ** (-jnp.arange(0, C_QK_DIM, 2, dtype=F32) / C_QK_DIM)
    ang_c = positions.astype(F32)[:, None] * inv_c[None, :]
    cos_c, sin_c = jnp.cos(ang_c), jnp.sin(ang_c)
    h = x.reshape(n, d)
    hb = h
    for i in range(DEPTH):
        kind, j = i % 3, i // 3
        if kind == 0:
            y = _mixer_dilated(hb, a_w_in[j], tabs_a, batch, seq)
            w_out = a_w_out[j]
        elif kind == 1:
            y = _mixer_mlstm(hb, b_w_in[j], b_gate_bias[j], b_conv_w[j], b_conv_b[j], b_norm_g[j], batch, seq)
            w_out = b_w_out[j]
        else:
            y = _mixer_retention(hb, c_w_in[j], c_norm_g[j], cos_c, sin_c, batch, seq)
            w_out = c_w_out[j]
        w_r, b_r = _router_params(r_group_w[i], r_group_b[i], r_expert_w[i], r_expert_b[i])
        h, hp, route, cnt = _matmul_res_ln_route(y, w_out.astype(BF16), h, ln1_g[i], ln1_b[i], w_r, b_r, tm=512)
        h, hb = _moe(h, hp, route, cnt, e_w_gate, e_w_up, e_w_down, i, ln2_g[i], ln2_b[i])
    return h.reshape(batch, seq, d)
```

```python
import jax
import jax.numpy as jnp
from jax import lax
from jax.experimental import pallas as pl
from jax.experimental.pallas import tpu as pltpu
from jax.experimental.pallas import tpu_sc as plsc

F32 = jnp.float32
BF16 = jnp.bfloat16

D_MODEL = 1024
DEPTH = 4
DN_ALPHA = (2.0 * DEPTH) ** 0.25
LN_EPS = 1e-5

A_HEADS = 16
A_HEAD_DIM = 64
A_DILATIONS = (1, 4, 16)
A_BLOCK = 128
A_UNROLL = 16
A_GROUP = 16
A_PITCH = 20
LOG2_E = 1.4426950408889634
A_ROT_DIM = 16
ROPE_THETA = 500000.0

B_HEADS = 8
B_QK_DIM = 64
B_V_DIM = 128
B_CONV = 4
B_CHUNK = 256

C_HEADS = 4
C_QK_DIM = 256
C_V_DIM = 512
C_CHUNK = 256
C_THETA = 10000.0

MOE_GROUPS = 8
MOE_PER_GROUP = 8
MOE_EXPERTS = 64
MOE_TOPK = 2
MOE_HIDDEN = 256
MOE_BLOCK = 512

LANES = 128
SC_ROWS = 64
SC_INDEX_LANES = LANES
NEG = -1e30
V7X_VMEM_BYTES = 64 * 1024 * 1024
VMEM_LIMIT = V7X_VMEM_BYTES * 3 // 4

TILE_PROJ_A = 512
TILE_MLSTM_PROJ = (1024, 1536)
TILE_MLSTM_GATES = (2048, LANES)
TILE_RETENTION_PROJ = (1024, 2048)
TILE_TOKEN_ROWS = 512


def _params(*sem):
    return pltpu.CompilerParams(dimension_semantics=sem, vmem_limit_bytes=VMEM_LIMIT)


def _mm_kernel(x_ref, w_ref, o_ref):
    o_ref[...] = jnp.dot(x_ref[...], w_ref[...], preferred_element_type=F32).astype(o_ref.dtype)


def _matmul(x, w, *, tm, tn, out_dtype=F32):
    n, k = x.shape
    m = w.shape[1]
    return pl.pallas_call(
        _mm_kernel,
        grid=(n // tm, m // tn),
        in_specs=[pl.BlockSpec((tm, k), lambda i, j: (i, 0)),
                  pl.BlockSpec((k, tn), lambda i, j: (0, j))],
        out_specs=pl.BlockSpec((tm, tn), lambda i, j: (i, j)),
        out_shape=jax.ShapeDtypeStruct((n, m), out_dtype),
        compiler_params=_params("parallel", "parallel"),
        name="matmul",
    )(x, w)


def _layer_norm_rows(z, g, b):
    mu = jnp.mean(z, -1, keepdims=True)
    zc = z - mu
    var = jnp.mean(zc * zc, -1, keepdims=True)
    return zc * lax.rsqrt(var + LN_EPS) * g + b


def _pack_pairs(x):
    c = x.shape[1] // 2
    hi = pltpu.bitcast(x[:, :c].astype(BF16).astype(F32), jnp.uint32)
    lo = pltpu.bitcast(x[:, c:].astype(BF16).astype(F32), jnp.uint32)
    return pltpu.bitcast(hi | (lo >> 16), F32)


def _unpack_pairs(w):
    bits = pltpu.bitcast(w, jnp.uint32)
    hi = pltpu.bitcast(bits & jnp.uint32(0xFFFF0000), F32)
    lo = pltpu.bitcast(bits << 16, F32)
    return jnp.concatenate([hi, lo], axis=1)


def _proj_a_kernel(x_ref, w_ref, c_ref, s1_ref, s2_ref, q_ref, k_ref, v_ref):
    x = x_ref[...].astype(BF16)
    width = 2 * LANES
    for c, ref in ((0, q_ref), (1, k_ref), (2, v_ref)):
        for j in range(D_MODEL // width):
            col = c * D_MODEL + j * width
            y = jnp.dot(x, w_ref[:, col:col + width], preferred_element_type=F32)
            if c < 2:
                half = A_ROT_DIM // 2
                y = (y * c_ref[...] + pltpu.roll(y, width - half, 1) * s1_ref[...]
                     + pltpu.roll(y, half, 1) * s2_ref[...])
            if c == 0:
                y = y * (LOG2_E * A_HEAD_DIM ** -0.5)
            ref[0, 2 * j] = y[:, :LANES]
            ref[0, 2 * j + 1] = y[:, LANES:]


def _proj_a(xb, w, tabs, batch, seq, *, tm):
    n, d = xb.shape
    spb = seq // tm
    hp = D_MODEL // LANES
    qkv_shape = jax.ShapeDtypeStruct((batch, hp, seq, LANES), F32)
    out_spec = pl.BlockSpec((1, hp, tm, LANES), lambda i: (i // spb, 0, i % spb, 0))
    tab_spec = pl.BlockSpec((tm, 2 * LANES), lambda i: (i % spb, 0))
    return pl.pallas_call(
        _proj_a_kernel,
        grid=(n // tm,),
        in_specs=[pl.BlockSpec((tm, d), lambda i: (i, 0)),
                  pl.BlockSpec((d, 3 * d), lambda i: (0, 0)),
                  tab_spec, tab_spec, tab_spec],
        out_specs=[out_spec, out_spec, out_spec],
        out_shape=[qkv_shape, qkv_shape, qkv_shape],
        compiler_params=_params("parallel"),
        name="proj_a",
    )(xb, w, *tabs)


def _rope_tables_a(positions):
    half = A_ROT_DIM // 2
    inv = ROPE_THETA ** (-jnp.arange(0, A_ROT_DIM, 2, dtype=F32) / A_ROT_DIM)
    ang = positions.astype(F32)[:, None] * inv[None, :]
    cos, sin = jnp.cos(ang), jnp.sin(ang)
    s = positions.shape[0]
    pad = jnp.zeros((s, A_HEAD_DIM - A_ROT_DIM), F32)
    c_head = jnp.concatenate([cos, cos, pad + 1.0], -1)
    s1_head = jnp.concatenate([-sin, jnp.zeros_like(sin), pad], -1)
    s2_head = jnp.concatenate([jnp.zeros_like(sin), sin, pad], -1)
    reps = 2 * LANES // A_HEAD_DIM
    return tuple(jnp.tile(t, (1, reps)) for t in (c_head, s1_head, s2_head))


def _attn_blocks(q_ref, k_ref, v_ref, o_ref, lse_ref, bias_ref, hmask, head0, first_block, d, pitch):
    nk = 2 * A_BLOCK
    loaded = []
    for u in range(A_UNROLL):
        g = first_block + u
        r = g % d
        n = g // d
        qstart = n * (A_BLOCK * pitch) + r
        kstart = jnp.maximum(qstart - A_BLOCK * pitch, r)
        if pitch == 1:
            qstart = pl.multiple_of(qstart, A_BLOCK)
            kstart = pl.multiple_of(kstart, A_BLOCK)
            qsl, ksl = pl.ds(qstart, A_BLOCK), pl.ds(kstart, nk)
        else:
            qsl, ksl = pl.ds(qstart, A_BLOCK, stride=pitch), pl.ds(kstart, nk, stride=pitch)
        bias = bias_ref[jnp.minimum(n, 1)]
        loaded.append((qsl, q_ref[qsl, :].astype(BF16), k_ref[ksl, :].astype(BF16), v_ref[ksl, :].astype(BF16), bias))
    results = []
    for qsl, qb, kb, vb, bias in loaded:
        pvs, ms, ls = [], [], []
        for h in range(2):
            s = lax.dot_general(qb * hmask[h], kb, (((1,), (1,)), ((), ())), preferred_element_type=F32) + bias
            m = jnp.max(s, -1, keepdims=True)
            p = jnp.exp2(s - m)
            ms.append(jnp.broadcast_to(m, (A_BLOCK, LANES)))
            ls.append(jnp.broadcast_to(jnp.sum(p, -1, keepdims=True), (A_BLOCK, LANES)))
            pvs.append(jnp.dot(p.astype(BF16), vb, preferred_element_type=F32))
        l = jnp.where(head0, ls[0], ls[1])
        out = jnp.where(head0, pvs[0], pvs[1]) * (1.0 / l)
        results.append((qsl, out, jnp.where(head0, ms[0], ms[1]) + jnp.log2(l)))
    for qsl, out, lse in results:
        o_ref[qsl, :] = out
        lse_ref[qsl, :] = lse


def _attn_kernel(q_ref, k_ref, v_ref, o_ref, q16_ref, k16_ref, v16_ref, ob_ref, lb_ref, o16_ref, l16_ref,
                 bias_ref):
    seq = q_ref.shape[0]
    groups = seq // A_GROUP
    head0 = lax.broadcasted_iota(jnp.int32, (A_BLOCK, LANES), 1) < A_HEAD_DIM
    hmask = [jnp.where(head0, 1.0, 0.0).astype(BF16), jnp.where(head0, 0.0, 1.0).astype(BF16)]
    qi = lax.broadcasted_iota(jnp.int32, (A_BLOCK, 2 * A_BLOCK), 0)
    kj = lax.broadcasted_iota(jnp.int32, (A_BLOCK, 2 * A_BLOCK), 1)
    bias_ref[0] = jnp.where(kj <= qi, 0.0, NEG).astype(F32)
    bias_ref[1] = jnp.where((kj >= qi) & (kj <= qi + A_BLOCK), 0.0, NEG).astype(F32)

    def spread(g, carry):
        src = pl.ds(pl.multiple_of(g * A_GROUP, A_GROUP), A_GROUP)
        dst = pl.ds(pl.multiple_of(g * A_PITCH, 4), A_GROUP)
        q16_ref[dst, :] = q_ref[src, :]
        k16_ref[dst, :] = k_ref[src, :]
        v16_ref[dst, :] = v_ref[src, :]
        return carry

    lax.fori_loop(0, groups, spread, 0, unroll=8)

    branches = ((1, 1, q_ref, k_ref, v_ref, ob_ref.at[0], lb_ref.at[0]),
                (4, 4, q_ref, k_ref, v_ref, ob_ref.at[1], lb_ref.at[1]),
                (16, A_PITCH, q16_ref, k16_ref, v16_ref, o16_ref, l16_ref))
    for d, pitch, qr, kr, vr, orf, lrf in branches:

        def body(it, carry, d=d, pitch=pitch, qr=qr, kr=kr, vr=vr, orf=orf, lrf=lrf):
            _attn_blocks(qr, kr, vr, orf, lrf, bias_ref, hmask, head0, it * A_UNROLL, d, pitch)
            return carry

        lax.fori_loop(0, seq // (A_BLOCK * A_UNROLL), body, 0)

    def mix(g, carry):
        nat = pl.ds(pl.multiple_of(g * A_GROUP, A_GROUP), A_GROUP)
        pad = pl.ds(pl.multiple_of(g * A_PITCH, 4), A_GROUP)
        o0, o1, o2 = ob_ref[0, nat, :], ob_ref[1, nat, :], o16_ref[pad, :]
        l0, l1, l2 = lb_ref[0, nat, :], lb_ref[1, nat, :], l16_ref[pad, :]
        mx = jnp.maximum(jnp.maximum(l0, l1), l2)
        w0, w1, w2 = jnp.exp2(l0 - mx), jnp.exp2(l1 - mx), jnp.exp2(l2 - mx)
        o_ref[nat, :] = ((w0 * o0 + w1 * o1 + w2 * o2) / (w0 + w1 + w2)).astype(o_ref.dtype)
        return carry

    lax.fori_loop(0, groups, mix, 0, unroll=8)


def _attention(q, k, v):
    batch, hp, seq, _ = q.shape
    assert seq % (2 * A_BLOCK * max(A_DILATIONS)) == 0 and seq % (A_BLOCK * A_UNROLL) == 0
    in_spec = pl.BlockSpec((None, None, seq, LANES), lambda b, p: (b, p, 0, 0))
    padded = seq // A_GROUP * A_PITCH
    return pl.pallas_call(
        _attn_kernel,
        grid=(batch, hp),
        in_specs=[in_spec, in_spec, in_spec],
        out_specs=pl.BlockSpec((None, seq, LANES), lambda b, p: (b, 0, p)),
        out_shape=jax.ShapeDtypeStruct((batch, seq, hp * LANES), BF16),
        scratch_shapes=[pltpu.VMEM((padded, LANES), F32)] * 3
        + [pltpu.VMEM((2, seq, LANES), F32)] * 2
        + [pltpu.VMEM((padded, LANES), F32)] * 2
        + [pltpu.VMEM((2, A_BLOCK, 2 * A_BLOCK), F32)],
        compiler_params=_params("parallel", "parallel"),
        name="dilated_attention",
    )(q, k, v)


def _mixer_dilated(hb, w_in, tabs, batch, seq):
    q, k, v = _proj_a(hb, w_in.astype(BF16), tabs, batch, seq, tm=TILE_PROJ_A)
    o = _attention(q, k, v)
    return o.reshape(batch * seq, D_MODEL)


def _mlstm_kernel(qk_ref, v_ref, o_ref, gc_ref, gb_ref, cw_ref, cb_ref, ng_ref, out_ref,
                  ext_ref, c_ref, n_ref, m_ref):
    L = B_CHUNK
    chunk = pl.program_id(1)

    @pl.when(chunk == 0)
    def _():
        ext_ref[0:8, :] = jnp.zeros((8, D_MODEL), F32)
        c_ref[...] = jnp.zeros(c_ref.shape, F32)
        n_ref[...] = jnp.zeros(n_ref.shape, F32)
        m_ref[...] = jnp.zeros(m_ref.shape, F32)

    u = qk_ref[...].astype(F32)
    ext_ref[8:8 + L, :] = u
    conv = u * cw_ref[B_CONV - 1:B_CONV, :] + cb_ref[...]
    for j in range(1, B_CONV):
        conv = conv + ext_ref[pl.ds(8 - j, L), :] * cw_ref[B_CONV - 1 - j:B_CONV - j, :]
    ext_ref[0:8, :] = u[L - 8:, :]
    qk = conv * jax.nn.sigmoid(conv)
    half = D_MODEL // 2

    gc = gc_ref[...] + gb_ref[...]
    gr = gc.T
    i_col, i_row = gc, gr[:B_HEADS, :]
    lf_col = jax.nn.log_sigmoid(gc)
    lf_row = jax.nn.log_sigmoid(gr[B_HEADS:2 * B_HEADS, :])
    ti = lax.broadcasted_iota(jnp.int32, (L, L), 0)
    si = lax.broadcasted_iota(jnp.int32, (L, L), 1)
    causal = ti >= si
    tri = causal.astype(F32)
    a_col = jnp.dot(tri, lf_col, preferred_element_type=F32, precision=lax.Precision.HIGHEST)
    a_row = lax.dot_general(lf_row, tri, (((1,), (1,)), ((), ())), preferred_element_type=F32,
                            precision=lax.Precision.HIGHEST)
    lane = lax.broadcasted_iota(jnp.int32, (1, LANES), 1)
    lane_h0 = lane < B_QK_DIM
    col_h0 = lax.broadcasted_iota(jnp.int32, (1, 2 * B_V_DIM), 1) < B_V_DIM

    for p in range(B_HEADS // 2):
        qp = qk[:, p * LANES:(p + 1) * LANES]
        kp = qk[:, half + p * LANES:half + (p + 1) * LANES] * (B_QK_DIM ** -0.5)
        vp = v_ref[:, p * 2 * B_V_DIM:(p + 1) * 2 * B_V_DIM]
        kpb = kp.astype(BF16)
        vpb = vp.astype(BF16)
        c_old = c_ref[p]
        c_oldb = c_old.astype(BF16)
        n_old = n_ref[p]
        m_pair = m_ref[p]
        ws_cols, decays, m_news = [], [], []
        for hh in range(2):
            h = 2 * p + hh
            m_old = m_pair[:, hh * B_QK_DIM:hh * B_QK_DIM + 1]
            ac, ar = a_col[:, B_HEADS + h:B_HEADS + h + 1], a_row[h:h + 1, :]
            ic, ir = i_col[:, h:h + 1], i_row[h:h + 1, :]
            dmat = jnp.where(causal, ac - ar + ir, NEG)
            inter = ac + m_old
            m_t = jnp.maximum(inter, jnp.max(dmat, -1, keepdims=True))
            qm = jnp.where(lane_h0 if hh == 0 else jnp.logical_not(lane_h0), qp, 0.0)
            qmb = qm.astype(BF16)
            sc = lax.dot_general(qmb, kpb, (((1,), (1,)), ((), ())), preferred_element_type=F32)
            sc = sc * jnp.exp(dmat - m_t)
            g_inter = jnp.exp(inter - m_t)
            vh = vpb[:, hh * B_V_DIM:(hh + 1) * B_V_DIM]
            qc = jnp.dot(qmb, c_oldb, preferred_element_type=F32)[:, hh * B_V_DIM:(hh + 1) * B_V_DIM]
            num = jnp.dot(sc.astype(BF16), vh, preferred_element_type=F32) + g_inter * qc
            den = jnp.sum(sc, -1, keepdims=True) + g_inter * jnp.sum(qm * n_old, -1, keepdims=True)
            h_out = num / jnp.maximum(jnp.abs(den), jnp.exp(-m_t))
            mu = jnp.mean(h_out, -1, keepdims=True)
            hc = h_out - mu
            var = jnp.mean(hc * hc, -1, keepdims=True)
            cols = slice(h * B_V_DIM, (h + 1) * B_V_DIM)
            hn = hc * lax.rsqrt(var + LN_EPS) * ng_ref[:, cols]
            out_ref[:, cols] = (hn * jax.nn.sigmoid(o_ref[:, cols].astype(F32))).astype(out_ref.dtype)
            a_end = ac[L - 1:L, :]
            w_col = a_end - ac + ic
            m_new = jnp.maximum(a_end + m_old, jnp.max(w_col, 0, keepdims=True))
            decays.append(jnp.exp(a_end + m_old - m_new))
            ws_cols.append(jnp.exp(w_col - m_new))
            m_news.append(m_new)
        ws = jnp.where(lane_h0, jnp.broadcast_to(ws_cols[0], (L, LANES)), jnp.broadcast_to(ws_cols[1], (L, LANES)))
        kw = kp * ws
        dec_c = jnp.where(col_h0, jnp.broadcast_to(decays[0], (1, 2 * B_V_DIM)),
                          jnp.broadcast_to(decays[1], (1, 2 * B_V_DIM)))
        dec_n = jnp.where(lane_h0, jnp.broadcast_to(decays[0], (1, LANES)), jnp.broadcast_to(decays[1], (1, LANES)))
        c_ref[p] = dec_c * c_old + lax.dot_general(kw.astype(BF16), vpb, (((0,), (0,)), ((), ())),
                                                   preferred_element_type=F32)
        n_ref[p] = dec_n * n_old + jnp.sum(kw, 0, keepdims=True)
        m_ref[p] = jnp.where(lane_h0, jnp.broadcast_to(m_news[0], (1, LANES)), jnp.broadcast_to(m_news[1], (1, LANES)))


def _mlstm(proj, gates, gate_bias, conv_w, conv_b, norm_g, batch, seq):
    L = B_CHUNK
    d = D_MODEL
    slab = lambda c: pl.BlockSpec((None, L, d), lambda b, s, c=c: (b, s, c))
    full = lambda shape: pl.BlockSpec(shape, lambda b, s: (0,) * len(shape))
    return pl.pallas_call(
        _mlstm_kernel,
        grid=(batch, seq // L),
        in_specs=[slab(0), slab(1), slab(2),
                  pl.BlockSpec((None, L, LANES), lambda b, s: (b, s, 0)),
                  full((1, LANES)),
                  full((B_CONV, d)), full((1, d)), full((1, d))],
        out_specs=pl.BlockSpec((None, L, d), lambda b, s: (b, s, 0)),
        out_shape=jax.ShapeDtypeStruct((batch, seq, d), BF16),
        scratch_shapes=[pltpu.VMEM((L + 8, d), F32),
                        pltpu.VMEM((B_HEADS // 2, 2 * B_QK_DIM, 2 * B_V_DIM), F32),
                        pltpu.VMEM((B_HEADS // 2, 1, LANES), F32),
                        pltpu.VMEM((B_HEADS // 2, 1, LANES), F32)],
        compiler_params=_params("parallel", "arbitrary"),
        name="mlstm",
    )(proj, proj, proj, gates, jnp.pad(gate_bias, (0, LANES - 2 * B_HEADS)).reshape(1, LANES),
      conv_w, conv_b.reshape(1, d), norm_g.reshape(1, d))


def _mixer_mlstm(hb, w_in, gate_bias, conv_w, conv_b, norm_g, batch, seq):
    n = batch * seq
    main = 3 * D_MODEL
    tm, tn = TILE_MLSTM_PROJ
    proj = _matmul(hb, w_in[:, :main].astype(BF16), tm=tm, tn=tn, out_dtype=BF16)
    w_g = jnp.pad(w_in[:, main:], ((0, 0), (0, LANES - 2 * B_HEADS))).astype(BF16)
    gates = _matmul(hb, w_g, tm=TILE_MLSTM_GATES[0], tn=TILE_MLSTM_GATES[1])
    out = _mlstm(proj.reshape(batch, seq, main), gates.reshape(batch, seq, LANES), gate_bias,
                 conv_w, conv_b, norm_g, batch, seq)
    return out.reshape(n, D_MODEL)


def _retention_kernel(lg_ref, q_ref, k_ref, v_ref, g_ref, cos_ref, sin_ref, ng_ref, out_ref,
                      r_ref, dm_ref, xi_ref, zeta_ref):
    L = C_CHUNK
    chunk = pl.program_id(1)

    @pl.when(chunk == 0)
    def _():
        r_ref[...] = jnp.zeros(r_ref.shape, F32)
        ti = lax.broadcasted_iota(jnp.int32, (L, L), 0)
        si = lax.broadcasted_iota(jnp.int32, (L, L), 1)
        rel = (ti - si).astype(F32)
        idx = lax.broadcasted_iota(jnp.int32, (L, LANES), 0).astype(F32)
        for h in range(C_HEADS):
            lg = lg_ref[h]
            dm_ref[h] = jnp.where(rel >= 0, jnp.exp(jnp.maximum(rel, 0.0) * lg), 0.0)
            xi_ref[h] = jnp.exp((idx + 1.0) * lg)
            zeta_ref[h] = jnp.exp((L - 1.0 - idx) * lg)

    cos, sin = cos_ref[...], sin_ref[...]
    hd = C_QK_DIM // 2

    def rope(t):
        t1, t2 = t[:, :hd], t[:, hd:]
        return jnp.concatenate([t1 * cos - t2 * sin, t2 * cos + t1 * sin], -1)

    for h in range(C_HEADS):
        qk_cols = slice(h * C_QK_DIM, (h + 1) * C_QK_DIM)
        v_cols = slice(h * C_V_DIM, (h + 1) * C_V_DIM)
        q = rope(q_ref[:, qk_cols].astype(F32))
        k = rope(k_ref[:, qk_cols].astype(F32)) * (C_QK_DIM ** -0.5)
        qb = q.astype(BF16)
        vb = v_ref[:, v_cols]
        r_old = r_ref[h]
        sc = lax.dot_general(qb, k.astype(BF16), (((1,), (1,)), ((), ())), preferred_element_type=F32) * dm_ref[h]
        o = jnp.dot(sc.astype(BF16), vb, preferred_element_type=F32)
        o = o + xi_ref[h, :, 0:1] * jnp.dot(qb, r_old.astype(BF16), preferred_element_type=F32)
        kz = (k * zeta_ref[h, :, 0:1]).astype(BF16)
        cd = jnp.exp(jnp.full((1, 1), float(L), F32) * lg_ref[h])
        r_ref[h] = cd * r_old + lax.dot_general(kz, vb, (((0,), (0,)), ((), ())), preferred_element_type=F32)
        mu = jnp.mean(o, -1, keepdims=True)
        oc = o - mu
        var = jnp.mean(oc * oc, -1, keepdims=True)
        on = oc * lax.rsqrt(var + LN_EPS) * ng_ref[:, v_cols]
        g = g_ref[:, v_cols].astype(F32)
        out_ref[:, v_cols] = (on * (g * jax.nn.sigmoid(g))).astype(out_ref.dtype)


def _retention(proj, cos, sin, norm_g, batch, seq):
    L = C_CHUNK
    d = D_MODEL
    log_gamma = jnp.log(1.0 - 2.0 ** (-5.0 - jnp.arange(C_HEADS, dtype=F32)))
    return pl.pallas_call(
        _retention_kernel,
        grid=(batch, seq // L),
        in_specs=[pl.BlockSpec(memory_space=pltpu.SMEM),
                  pl.BlockSpec((None, L, d), lambda b, c: (b, c, 0)),
                  pl.BlockSpec((None, L, d), lambda b, c: (b, c, 1)),
                  pl.BlockSpec((None, L, 2 * d), lambda b, c: (b, c, 1)),
                  pl.BlockSpec((None, L, 2 * d), lambda b, c: (b, c, 2)),
                  pl.BlockSpec((L, C_QK_DIM // 2), lambda b, c: (c, 0)),
                  pl.BlockSpec((L, C_QK_DIM // 2), lambda b, c: (c, 0)),
                  pl.BlockSpec((1, 2 * d), lambda b, c: (0, 0))],
        out_specs=pl.BlockSpec((None, L, 2 * d), lambda b, c: (b, c, 0)),
        out_shape=jax.ShapeDtypeStruct((batch, seq, 2 * d), BF16),
        scratch_shapes=[pltpu.VMEM((C_HEADS, C_QK_DIM, C_V_DIM), F32),
                        pltpu.VMEM((C_HEADS, L, L), F32),
                        pltpu.VMEM((C_HEADS, L, LANES), F32),
                        pltpu.VMEM((C_HEADS, L, LANES), F32)],
        compiler_params=_params("parallel", "arbitrary"),
        name="retention",
    )(log_gamma, proj, proj, proj, proj, cos, sin, norm_g.reshape(1, 2 * d))


def _mixer_retention(hb, w_in, norm_g, cos, sin, batch, seq):
    tm, tn = TILE_RETENTION_PROJ
    proj = _matmul(hb, w_in.astype(BF16), tm=tm, tn=tn, out_dtype=BF16)
    out = _retention(proj.reshape(batch, seq, 6 * D_MODEL), cos, sin, norm_g, batch, seq)
    return out.reshape(batch * seq, 2 * D_MODEL)


def _expert_kernel(be_ref, x_ref, wg_ref, wu_ref, wd_ref, y_ref, wgb_ref, wub_ref, wdb_ref):
    j = pl.program_id(0)
    changed = jnp.logical_or(j == 0, be_ref[j] != be_ref[jnp.maximum(j - 1, 0)])

    @pl.when(changed)
    def _():
        wgb_ref[...] = wg_ref[...].astype(BF16)
        wub_ref[...] = wu_ref[...].astype(BF16)
        wdb_ref[...] = wd_ref[...].astype(BF16)

    x = _unpack_pairs(x_ref[...]).astype(BF16)
    a = jnp.dot(x, wgb_ref[...], preferred_element_type=F32)
    u = jnp.dot(x, wub_ref[...], preferred_element_type=F32)
    act = (a * jax.nn.sigmoid(a) * u).astype(BF16)
    y_ref[...] = _pack_pairs(jnp.dot(act, wdb_ref[...], preferred_element_type=F32))


def _experts(blk_e, xb, w_gate, w_up, w_down, layer):
    p, dw = xb.shape
    d = 2 * dw
    nb = p // MOE_BLOCK
    hid = MOE_HIDDEN
    grid_spec = pltpu.PrefetchScalarGridSpec(
        num_scalar_prefetch=1,
        grid=(nb,),
        in_specs=[pl.BlockSpec((MOE_BLOCK, dw), lambda j, be: (j, 0)),
                  pl.BlockSpec((None, None, d, hid), lambda j, be: (layer, be[j], 0, 0)),
                  pl.BlockSpec((None, None, d, hid), lambda j, be: (layer, be[j], 0, 0)),
                  pl.BlockSpec((None, None, hid, d), lambda j, be: (layer, be[j], 0, 0))],
        out_specs=pl.BlockSpec((MOE_BLOCK, dw), lambda j, be: (j, 0)),
        scratch_shapes=[pltpu.VMEM((d, hid), BF16), pltpu.VMEM((d, hid), BF16), pltpu.VMEM((hid, d), BF16)],
    )
    return pl.pallas_call(
        _expert_kernel,
        grid_spec=grid_spec,
        out_shape=jax.ShapeDtypeStruct((p, dw), F32),
        compiler_params=_params("arbitrary"),
        name="moe_experts",
    )(blk_e, xb, w_gate, w_up, w_down)


def _combine_ln_kernel(h_ref, y0_ref, y1_ref, rt_ref, g_ref, b_ref, o_ref, ob_ref):
    rt = rt_ref[...]
    y = (_unpack_pairs(y0_ref[...]) * rt[:, R_GATE:R_GATE + 1]
         + _unpack_pairs(y1_ref[...]) * rt[:, R_GATE + 1:R_GATE + 2])
    out = _layer_norm_rows(DN_ALPHA * h_ref[...] + y, g_ref[...], b_ref[...])
    o_ref[...] = out
    ob_ref[...] = out.astype(BF16)


def _combine_ln(h, y0, y1, route, g, b, *, tm):
    n, d = h.shape
    row = pl.BlockSpec((tm, d), lambda i: (i, 0))
    words = pl.BlockSpec((tm, d // 2), lambda i: (i, 0))
    vec = pl.BlockSpec((1, d), lambda i: (0, 0))
    return pl.pallas_call(
        _combine_ln_kernel,
        grid=(n // tm,),
        in_specs=[row, words, words, pl.BlockSpec((tm, LANES), lambda i: (i, 0)), vec, vec],
        out_specs=[row, row],
        out_shape=[jax.ShapeDtypeStruct((n, d), F32), jax.ShapeDtypeStruct((n, d), BF16)],
        compiler_params=_params("parallel"),
        name="moe_combine_ln",
    )(h, y0, y1, route, g.reshape(1, d), b.reshape(1, d))


R_EID, R_GATE, R_RANK = 0, 2, 4


def _route_tile(h, w_ref, b_ref, route_ref, cnt_ref, base_ref, tri_ref):
    i = pl.program_id(0)
    tm = h.shape[0]

    @pl.when(i == 0)
    def _():
        base_ref[...] = jnp.zeros(base_ref.shape, F32)
        ti = lax.broadcasted_iota(jnp.int32, (tm, tm), 0)
        si = lax.broadcasted_iota(jnp.int32, (tm, tm), 1)
        tri_ref[...] = jnp.where(si < ti, 1.0, 0.0).astype(BF16)

    h_hi = h.astype(BF16)
    h_lo = (h - h_hi.astype(F32)).astype(BF16)
    hh = jnp.dot(h_hi, w_ref[...], preferred_element_type=F32)
    logits = (hh[:, :LANES] + hh[:, LANES:]
              + jnp.dot(h_lo, w_ref[:, :LANES], preferred_element_type=F32)) + b_ref[...]
    lane = lax.broadcasted_iota(jnp.int32, (tm, LANES), 1).astype(F32)
    neg_inf = -jnp.inf
    big = float(4 * LANES)
    is_g = lane < MOE_GROUPS
    gl = jnp.where(is_g, logits, neg_inf)
    gmax = jnp.max(gl, -1, keepdims=True)
    grp = jnp.min(jnp.where(gl == gmax, lane, big), -1, keepdims=True)
    p_grp = 1.0 / jnp.sum(jnp.where(is_g, jnp.exp(logits - gmax), 0.0), -1, keepdims=True)
    lo = MOE_GROUPS + MOE_PER_GROUP * grp
    el = jnp.where((lane >= lo) & (lane < lo + MOE_PER_GROUP), logits, neg_inf)
    v1 = jnp.max(el, -1, keepdims=True)
    i1 = jnp.min(jnp.where(el == v1, lane, big), -1, keepdims=True)
    el2 = jnp.where(lane == i1, neg_inf, el)
    v2 = jnp.max(el2, -1, keepdims=True)
    i2 = jnp.min(jnp.where(el2 == v2, lane, big), -1, keepdims=True)
    t = jnp.exp(v2 - v1)
    g1 = p_grp / (1.0 + t)
    g2 = g1 * t
    e1 = i1 - MOE_GROUPS
    e2 = i2 - MOE_GROUPS
    oh1 = jnp.where(lane == e1, 1.0, 0.0)
    oh2 = jnp.where(lane == e2, 1.0, 0.0)
    oh = oh1 + oh2
    tot = base_ref[...] + jnp.dot(tri_ref[...], oh.astype(BF16), preferred_element_type=F32)
    r1 = jnp.sum(oh1 * tot, -1, keepdims=True)
    r2 = jnp.sum(oh2 * tot, -1, keepdims=True)
    new_base = base_ref[...] + jnp.sum(oh, 0, keepdims=True)
    base_ref[...] = new_base
    cnt_ref[...] = jnp.broadcast_to(new_base, cnt_ref.shape)
    route = jnp.zeros((tm, LANES), F32)
    for k, val in enumerate((e1, e2, g1, g2, r1, r2)):
        route = jnp.where(lane == float(k), val, route)
    route_ref[...] = route


def _mm_res_ln_route_kernel(x_ref, w_ref, h_ref, g_ref, b_ref, rw_ref, rb_ref,
                            o_ref, op_ref, route_ref, cnt_ref, base_ref, tri_ref):
    y = jnp.dot(x_ref[...], w_ref[...], preferred_element_type=F32)
    out = _layer_norm_rows(DN_ALPHA * h_ref[...] + y, g_ref[...], b_ref[...])
    o_ref[...] = out
    op_ref[...] = _pack_pairs(out)
    _route_tile(out, rw_ref, rb_ref, route_ref, cnt_ref, base_ref, tri_ref)


def _matmul_res_ln_route(x, w, h, g, b, w_r, b_r, *, tm):
    n, k = x.shape
    d = w.shape[1]
    rows = lambda width: pl.BlockSpec((tm, width), lambda i: (i, 0))
    const = lambda shape: pl.BlockSpec(shape, lambda i: (0, 0))
    return pl.pallas_call(
        _mm_res_ln_route_kernel,
        grid=(n // tm,),
        in_specs=[rows(k), const((k, d)), rows(d), const((1, d)), const((1, d)),
                  const((d, 2 * LANES)), const((1, LANES))],
        out_specs=[rows(d), rows(d // 2), rows(LANES), const((8, LANES))],
        out_shape=[jax.ShapeDtypeStruct((n, d), F32), jax.ShapeDtypeStruct((n, d // 2), F32),
                   jax.ShapeDtypeStruct((n, LANES), F32), jax.ShapeDtypeStruct((8, LANES), F32)],
        scratch_shapes=[pltpu.VMEM((1, LANES), F32), pltpu.VMEM((tm, tm), BF16)],
        compiler_params=_params("arbitrary"),
        name="matmul_res_ln_route",
    )(x, w, h, g.reshape(1, d), b.reshape(1, d), w_r, b_r)


def _sc_index_rows(idx):
    n = idx.shape[0]
    return jnp.pad(idx.reshape(n // SC_ROWS, SC_ROWS), ((0, 0), (0, SC_INDEX_LANES - SC_ROWS)))


def _sc_mesh():
    return plsc.VectorSubcoreMesh(core_axis_name="core", subcore_axis_name="subcore")


def _sc_gather_rows(y, idx):
    n = idx.shape[0]
    d = y.shape[1]

    @pl.kernel(out_type=jax.ShapeDtypeStruct((n, d), y.dtype), mesh=_sc_mesh(), scratch_types=[])
    def gather(y_hbm, i_hbm, o_hbm):
        def body(i_vmem, o_vmem):
            pltpu.sync_copy(y_hbm.at[i_vmem.at[0, pl.ds(0, SC_ROWS)]], o_vmem)

        pltpu.emit_pipeline(
            body, grid=(n // SC_ROWS,),
            in_specs=[pl.BlockSpec((1, SC_INDEX_LANES), lambda i: (i, 0))],
            out_specs=[pl.BlockSpec((SC_ROWS, d), lambda i: (i, 0))],
            core_axis_name=("core", "subcore"), dimension_semantics=(pltpu.PARALLEL,),
        )(i_hbm, o_hbm)

    return gather(y, _sc_index_rows(idx))


def _sc_scatter_rows(x, idx0, idx1, p_rows):
    n, d = x.shape

    @pl.kernel(out_type=jax.ShapeDtypeStruct((p_rows, d), x.dtype), mesh=_sc_mesh(), scratch_types=[])
    def scatter(x_hbm, i0_hbm, i1_hbm, o_hbm):
        def body(x_vmem, i0_vmem, i1_vmem):
            pltpu.sync_copy(x_vmem, o_hbm.at[i0_vmem.at[0, pl.ds(0, SC_ROWS)]])
            pltpu.sync_copy(x_vmem, o_hbm.at[i1_vmem.at[0, pl.ds(0, SC_ROWS)]])

        pltpu.emit_pipeline(
            body, grid=(n // SC_ROWS,),
            in_specs=[pl.BlockSpec((SC_ROWS, d), lambda i: (i, 0)),
                      pl.BlockSpec((1, SC_INDEX_LANES), lambda i: (i, 0)),
                      pl.BlockSpec((1, SC_INDEX_LANES), lambda i: (i, 0))],
            out_specs=[],
            core_axis_name=("core", "subcore"), dimension_semantics=(pltpu.PARALLEL,),
        )(x_hbm, i0_hbm, i1_hbm)

    return scatter(x, _sc_index_rows(idx0), _sc_index_rows(idx1))


def _router_params(wg_r, bg_r, we_r, be_r):
    pad = LANES - MOE_GROUPS - MOE_EXPERTS
    w_r = jnp.pad(jnp.concatenate([wg_r, we_r], 1), ((0, 0), (0, pad)))
    w_hi = w_r.astype(BF16)
    w_lo = (w_r - w_hi.astype(F32)).astype(BF16)
    b_r = jnp.pad(jnp.concatenate([bg_r, be_r]), (0, pad)).reshape(1, LANES)
    return jnp.concatenate([w_hi, w_lo], 1), b_r


def _moe(h, hp, route, cnt, w_gate, w_up, w_down, layer, ln_g, ln_b):
    n, d = h.shape
    counts = cnt[0, :MOE_EXPERTS].astype(jnp.int32)
    padded = (counts + MOE_BLOCK - 1) // MOE_BLOCK * MOE_BLOCK
    pends = jnp.cumsum(padded)
    pstarts = pends - padded
    eid = route[:, R_EID:R_EID + MOE_TOPK].astype(jnp.int32)
    rank = route[:, R_RANK:R_RANK + MOE_TOPK].astype(jnp.int32)
    experts = jnp.arange(MOE_EXPERTS, dtype=jnp.int32)
    dest = jnp.sum(jnp.where(eid[..., None] == experts, pstarts, 0), -1) + rank
    p_rows = n * MOE_TOPK + MOE_EXPERTS * MOE_BLOCK
    nb = p_rows // MOE_BLOCK
    blk_start = jnp.arange(nb, dtype=jnp.int32) * MOE_BLOCK
    blk_e = jnp.minimum(jnp.sum((pends[None, :] <= blk_start[:, None]).astype(jnp.int32), -1), MOE_EXPERTS - 1)
    dest0, dest1 = dest[:, 0], dest[:, 1]
    xb = _sc_scatter_rows(hp, dest0, dest1, p_rows)
    yb = _experts(blk_e, xb, w_gate, w_up, w_down, layer)
    return _combine_ln(h, _sc_gather_rows(yb, dest0), _sc_gather_rows(yb, dest1), route, ln_g, ln_b,
                       tm=TILE_TOKEN_ROWS)


def kernel(x, positions, ln1_g, ln1_b, ln2_g, ln2_b, a_w_in, a_w_out, b_w_in, b_gate_bias, b_conv_w, b_conv_b,
           b_norm_g, b_w_out, c_w_in, c_norm_g, c_w_out, r_group_w, r_group_b, r_expert_w, r_expert_b,
           e_w_gate, e_w_up, e_w_down):
    batch, seq, d = x.shape
    n = batch * seq
    tabs_a = _rope_tables_a(positions)
    inv_c = C_THETA ** (-jnp.arange(0, C_QK_DIM, 2, dtype=F32) / C_QK_DIM)
    ang_c = positions.astype(F32)[:, None] * inv_c[None, :]
    cos_c, sin_c = jnp.cos(ang_c), jnp.sin(ang_c)
    h = x.reshape(n, d)
    hb = h
    for i in range(DEPTH):
        kind, j = i % 3, i // 3
        if kind == 0:
            y = _mixer_dilated(hb, a_w_in[j], tabs_a, batch, seq)
            w_out = a_w_out[j]
        elif kind == 1:
            y = _mixer_mlstm(hb, b_w_in[j], b_gate_bias[j], b_conv_w[j], b_conv_b[j], b_norm_g[j], batch, seq)
            w_out = b_w_out[j]
        else:
            y = _mixer_retention(hb, c_w_in[j], c_norm_g[j], cos_c, sin_c, batch, seq)
            w_out = c_w_out[j]
        w_r, b_r = _router_params(r_group_w[i], r_group_b[i], r_expert_w[i], r_expert_b[i])
        h, hp, route, cnt = _matmul_res_ln_route(y, w_out.astype(BF16), h, ln1_g[i], ln1_b[i], w_r, b_r,
                                                 tm=TILE_TOKEN_ROWS)
        h, hb = _moe(h, hp, route, cnt, e_w_gate, e_w_up, e_w_down, i, ln2_g[i], ln2_b[i])
    return h.reshape(batch, seq, d)
```

```python
import jax
import jax.numpy as jnp
from jax import lax
from jax.experimental import pallas as pl
from jax.experimental.pallas import tpu as pltpu
from jax.experimental.pallas import tpu_sc as plsc

F32 = jnp.float32
BF16 = jnp.bfloat16

D_MODEL = 1024
DEPTH = 4
DN_ALPHA = (2.0 * DEPTH) ** 0.25
LN_EPS = 1e-5

A_HEADS = 16
A_HEAD_DIM = 64
A_DILATIONS = (1, 4, 16)
A_BLOCK = 128
A_UNROLL = 16
A_GROUP = 16
A_PITCH = 20
LOG2_E = 1.4426950408889634
A_ROT_DIM = 16
ROPE_THETA = 500000.0

B_HEADS = 8
B_QK_DIM = 64
B_V_DIM = 128
B_CONV = 4
B_CHUNK = 256

C_HEADS = 4
C_QK_DIM = 256
C_V_DIM = 512
C_CHUNK = 256
C_THETA = 10000.0

MOE_GROUPS = 8
MOE_PER_GROUP = 8
MOE_EXPERTS = 64
MOE_TOPK = 2
MOE_HIDDEN = 256
MOE_BLOCK = 512

LANES = 128
SC_ROWS = 64
SC_INDEX_LANES = LANES
NEG = -1e30
V7X_VMEM_BYTES = 64 * 1024 * 1024
VMEM_LIMIT = V7X_VMEM_BYTES * 3 // 4

TILE_PROJ_A = 512
TILE_MLSTM_PROJ = (1024, 1536)
TILE_MLSTM_GATES = (2048, LANES)
TILE_RETENTION_PROJ = (1024, 2048)
TILE_TOKEN_ROWS = 512


def _params(*sem):
    return pltpu.CompilerParams(dimension_semantics=sem, vmem_limit_bytes=VMEM_LIMIT)


def _mm_kernel(x_ref, w_ref, o_ref):
    o_ref[...] = jnp.dot(x_ref[...], w_ref[...], preferred_element_type=F32).astype(o_ref.dtype)


def _matmul(x, w, *, tm, tn, out_dtype=F32):
    n, k = x.shape
    m = w.shape[1]
    return pl.pallas_call(
        _mm_kernel,
        grid=(n // tm, m // tn),
        in_specs=[pl.BlockSpec((tm, k), lambda i, j: (i, 0)),
                  pl.BlockSpec((k, tn), lambda i, j: (0, j))],
        out_specs=pl.BlockSpec((tm, tn), lambda i, j: (i, j)),
        out_shape=jax.ShapeDtypeStruct((n, m), out_dtype),
        compiler_params=_params("parallel", "parallel"),
        name="matmul",
    )(x, w)


def _layer_norm_rows(z, g, b):
    mu = jnp.mean(z, -1, keepdims=True)
    zc = z - mu
    var = jnp.mean(zc * zc, -1, keepdims=True)
    return zc * lax.rsqrt(var + LN_EPS) * g + b


def _pack_pairs(x):
    c = x.shape[1] // 2
    hi = pltpu.bitcast(x[:, :c].astype(BF16).astype(F32), jnp.uint32)
    lo = pltpu.bitcast(x[:, c:].astype(BF16).astype(F32), jnp.uint32)
    return pltpu.bitcast(hi | (lo >> 16), F32)


def _unpack_pairs(w):
    bits = pltpu.bitcast(w, jnp.uint32)
    hi = pltpu.bitcast(bits & jnp.uint32(0xFFFF0000), F32)
    lo = pltpu.bitcast(bits << 16, F32)
    return jnp.concatenate([hi, lo], axis=1)


def _proj_a_kernel(x_ref, w_ref, c_ref, s1_ref, s2_ref, q_ref, k_ref, v_ref):
    x = x_ref[...].astype(BF16)
    width = 2 * LANES
    for c, ref in ((0, q_ref), (1, k_ref), (2, v_ref)):
        for j in range(D_MODEL // width):
            col = c * D_MODEL + j * width
            y = jnp.dot(x, w_ref[:, col:col + width], preferred_element_type=F32)
            if c < 2:
                half = A_ROT_DIM // 2
                y = (y * c_ref[...] + pltpu.roll(y, width - half, 1) * s1_ref[...]
                     + pltpu.roll(y, half, 1) * s2_ref[...])
            if c == 0:
                y = y * (LOG2_E * A_HEAD_DIM ** -0.5)
            ref[0, 2 * j] = y[:, :LANES]
            ref[0, 2 * j + 1] = y[:, LANES:]


def _proj_a(xb, w, tabs, batch, seq, *, tm):
    n, d = xb.shape
    spb = seq // tm
    hp = D_MODEL // LANES
    qkv_shape = jax.ShapeDtypeStruct((batch, hp, seq, LANES), F32)
    out_spec = pl.BlockSpec((1, hp, tm, LANES), lambda i: (i // spb, 0, i % spb, 0))
    tab_spec = pl.BlockSpec((tm, 2 * LANES), lambda i: (i % spb, 0))
    return pl.pallas_call(
        _proj_a_kernel,
        grid=(n // tm,),
        in_specs=[pl.BlockSpec((tm, d), lambda i: (i, 0)),
                  pl.BlockSpec((d, 3 * d), lambda i: (0, 0)),
                  tab_spec, tab_spec, tab_spec],
        out_specs=[out_spec, out_spec, out_spec],
        out_shape=[qkv_shape, qkv_shape, qkv_shape],
        compiler_params=_params("parallel"),
        name="proj_a",
    )(xb, w, *tabs)


def _rope_tables_a(positions):
    half = A_ROT_DIM // 2
    inv = ROPE_THETA ** (-jnp.arange(0, A_ROT_DIM, 2, dtype=F32) / A_ROT_DIM)
    ang = positions.astype(F32)[:, None] * inv[None, :]
    cos, sin = jnp.cos(ang), jnp.sin(ang)
    s = positions.shape[0]
    pad = jnp.zeros((s, A_HEAD_DIM - A_ROT_DIM), F32)
    c_head = jnp.concatenate([cos, cos, pad + 1.0], -1)
    s1_head = jnp.concatenate([-sin, jnp.zeros_like(sin), pad], -1)
    s2_head = jnp.concatenate([jnp.zeros_like(sin), sin, pad], -1)
    reps = 2 * LANES // A_HEAD_DIM
    return tuple(jnp.tile(t, (1, reps)) for t in (c_head, s1_head, s2_head))


def _attn_blocks(q_ref, k_ref, v_ref, o_ref, lse_ref, bias_ref, hmask, head0, first_block, d, pitch):
    nk = 2 * A_BLOCK
    loaded = []
    for u in range(A_UNROLL):
        g = first_block + u
        r = g % d
        n = g // d
        qstart = n * (A_BLOCK * pitch) + r
        kstart = jnp.maximum(qstart - A_BLOCK * pitch, r)
        if pitch == 1:
            qstart = pl.multiple_of(qstart, A_BLOCK)
            kstart = pl.multiple_of(kstart, A_BLOCK)
            qsl, ksl = pl.ds(qstart, A_BLOCK), pl.ds(kstart, nk)
        else:
            qsl, ksl = pl.ds(qstart, A_BLOCK, stride=pitch), pl.ds(kstart, nk, stride=pitch)
        bias = bias_ref[jnp.minimum(n, 1)]
        loaded.append((qsl, q_ref[qsl, :].astype(BF16), k_ref[ksl, :].astype(BF16), v_ref[ksl, :].astype(BF16), bias))
    results = []
    for qsl, qb, kb, vb, bias in loaded:
        pvs, ms, ls = [], [], []
        for h in range(2):
            s = lax.dot_general(qb * hmask[h], kb, (((1,), (1,)), ((), ())), preferred_element_type=F32) + bias
            m = jnp.max(s, -1, keepdims=True)
            p = jnp.exp2(s - m)
            ms.append(jnp.broadcast_to(m, (A_BLOCK, LANES)))
            ls.append(jnp.broadcast_to(jnp.sum(p, -1, keepdims=True), (A_BLOCK, LANES)))
            pvs.append(jnp.dot(p.astype(BF16), vb, preferred_element_type=F32))
        l = jnp.where(head0, ls[0], ls[1])
        out = jnp.where(head0, pvs[0], pvs[1]) * (1.0 / l)
        results.append((qsl, out, jnp.where(head0, ms[0], ms[1]) + jnp.log2(l)))
    for qsl, out, lse in results:
        o_ref[qsl, :] = out
        lse_ref[qsl, :] = lse


def _attn_kernel(q_ref, k_ref, v_ref, o_ref, q16_ref, k16_ref, v16_ref, ob_ref, lb_ref, o16_ref, l16_ref,
                 bias_ref):
    seq = q_ref.shape[0]
    groups = seq // A_GROUP
    head0 = lax.broadcasted_iota(jnp.int32, (A_BLOCK, LANES), 1) < A_HEAD_DIM
    hmask = [jnp.where(head0, 1.0, 0.0).astype(BF16), jnp.where(head0, 0.0, 1.0).astype(BF16)]
    qi = lax.broadcasted_iota(jnp.int32, (A_BLOCK, 2 * A_BLOCK), 0)
    kj = lax.broadcasted_iota(jnp.int32, (A_BLOCK, 2 * A_BLOCK), 1)
    bias_ref[0] = jnp.where(kj <= qi, 0.0, NEG).astype(F32)
    bias_ref[1] = jnp.where((kj >= qi) & (kj <= qi + A_BLOCK), 0.0, NEG).astype(F32)

    def spread(g, carry):
        src = pl.ds(pl.multiple_of(g * A_GROUP, A_GROUP), A_GROUP)
        dst = pl.ds(pl.multiple_of(g * A_PITCH, 4), A_GROUP)
        q16_ref[dst, :] = q_ref[src, :]
        k16_ref[dst, :] = k_ref[src, :]
        v16_ref[dst, :] = v_ref[src, :]
        return carry

    lax.fori_loop(0, groups, spread, 0, unroll=8)

    branches = ((1, 1, q_ref, k_ref, v_ref, ob_ref.at[0], lb_ref.at[0]),
                (4, 4, q_ref, k_ref, v_ref, ob_ref.at[1], lb_ref.at[1]),
                (16, A_PITCH, q16_ref, k16_ref, v16_ref, o16_ref, l16_ref))
    for d, pitch, qr, kr, vr, orf, lrf in branches:

        def body(it, carry, d=d, pitch=pitch, qr=qr, kr=kr, vr=vr, orf=orf, lrf=lrf):
            _attn_blocks(qr, kr, vr, orf, lrf, bias_ref, hmask, head0, it * A_UNROLL, d, pitch)
            return carry

        lax.fori_loop(0, seq // (A_BLOCK * A_UNROLL), body, 0)

    def mix(g, carry):
        nat = pl.ds(pl.multiple_of(g * A_GROUP, A_GROUP), A_GROUP)
        pad = pl.ds(pl.multiple_of(g * A_PITCH, 4), A_GROUP)
        o0, o1, o2 = ob_ref[0, nat, :], ob_ref[1, nat, :], o16_ref[pad, :]
        l0, l1, l2 = lb_ref[0, nat, :], lb_ref[1, nat, :], l16_ref[pad, :]
        mx = jnp.maximum(jnp.maximum(l0, l1), l2)
        w0, w1, w2 = jnp.exp2(l0 - mx), jnp.exp2(l1 - mx), jnp.exp2(l2 - mx)
        o_ref[nat, :] = ((w0 * o0 + w1 * o1 + w2 * o2) / (w0 + w1 + w2)).astype(o_ref.dtype)
        return carry

    lax.fori_loop(0, groups, mix, 0, unroll=8)


def _attention(q, k, v):
    batch, hp, seq, _ = q.shape
    assert seq % (2 * A_BLOCK * max(A_DILATIONS)) == 0 and seq % (A_BLOCK * A_UNROLL) == 0
    in_spec = pl.BlockSpec((None, None, seq, LANES), lambda b, p: (b, p, 0, 0))
    padded = seq // A_GROUP * A_PITCH
    return pl.pallas_call(
        _attn_kernel,
        grid=(batch, hp),
        in_specs=[in_spec, in_spec, in_spec],
        out_specs=pl.BlockSpec((None, seq, LANES), lambda b, p: (b, 0, p)),
        out_shape=jax.ShapeDtypeStruct((batch, seq, hp * LANES), BF16),
        scratch_shapes=[pltpu.VMEM((padded, LANES), F32)] * 3
        + [pltpu.VMEM((2, seq, LANES), F32)] * 2
        + [pltpu.VMEM((padded, LANES), F32)] * 2
        + [pltpu.VMEM((2, A_BLOCK, 2 * A_BLOCK), F32)],
        compiler_params=_params("parallel", "parallel"),
        name="dilated_attention",
    )(q, k, v)


def _mixer_dilated(hb, w_in, tabs, batch, seq):
    q, k, v = _proj_a(hb, w_in.astype(BF16), tabs, batch, seq, tm=TILE_PROJ_A)
    o = _attention(q, k, v)
    return o.reshape(batch * seq, D_MODEL)


def _mlstm_kernel(qk_ref, v_ref, o_ref, gc_ref, gb_ref, cw_ref, cb_ref, ng_ref, out_ref,
                  ext_ref, c_ref, n_ref, m_ref):
    L = B_CHUNK
    chunk = pl.program_id(1)

    @pl.when(chunk == 0)
    def _():
        ext_ref[0:8, :] = jnp.zeros((8, D_MODEL), F32)
        c_ref[...] = jnp.zeros(c_ref.shape, F32)
        n_ref[...] = jnp.zeros(n_ref.shape, F32)
        m_ref[...] = jnp.zeros(m_ref.shape, F32)

    u = qk_ref[...].astype(F32)
    ext_ref[8:8 + L, :] = u
    conv = u * cw_ref[B_CONV - 1:B_CONV, :] + cb_ref[...]
    for j in range(1, B_CONV):
        conv = conv + ext_ref[pl.ds(8 - j, L), :] * cw_ref[B_CONV - 1 - j:B_CONV - j, :]
    ext_ref[0:8, :] = u[L - 8:, :]
    qk = conv * jax.nn.sigmoid(conv)
    half = D_MODEL // 2

    gc = gc_ref[...] + gb_ref[...]
    gr = gc.T
    i_col, i_row = gc, gr[:B_HEADS, :]
    lf_col = jax.nn.log_sigmoid(gc)
    lf_row = jax.nn.log_sigmoid(gr[B_HEADS:2 * B_HEADS, :])
    ti = lax.broadcasted_iota(jnp.int32, (L, L), 0)
    si = lax.broadcasted_iota(jnp.int32, (L, L), 1)
    causal = ti >= si
    tri = causal.astype(F32)
    a_col = jnp.dot(tri, lf_col, preferred_element_type=F32, precision=lax.Precision.HIGHEST)
    a_row = lax.dot_general(lf_row, tri, (((1,), (1,)), ((), ())), preferred_element_type=F32,
                            precision=lax.Precision.HIGHEST)
    lane = lax.broadcasted_iota(jnp.int32, (1, LANES), 1)
    lane_h0 = lane < B_QK_DIM
    col_h0 = lax.broadcasted_iota(jnp.int32, (1, 2 * B_V_DIM), 1) < B_V_DIM

    for p in range(B_HEADS // 2):
        qp = qk[:, p * LANES:(p + 1) * LANES]
        kp = qk[:, half + p * LANES:half + (p + 1) * LANES] * (B_QK_DIM ** -0.5)
        vp = v_ref[:, p * 2 * B_V_DIM:(p + 1) * 2 * B_V_DIM]
        kpb = kp.astype(BF16)
        vpb = vp.astype(BF16)
        c_old = c_ref[p]
        c_oldb = c_old.astype(BF16)
        n_old = n_ref[p]
        m_pair = m_ref[p]
        ws_cols, decays, m_news = [], [], []
        for hh in range(2):
            h = 2 * p + hh
            m_old = m_pair[:, hh * B_QK_DIM:hh * B_QK_DIM + 1]
            ac, ar = a_col[:, B_HEADS + h:B_HEADS + h + 1], a_row[h:h + 1, :]
            ic, ir = i_col[:, h:h + 1], i_row[h:h + 1, :]
            dmat = jnp.where(causal, ac + (ir - ar), NEG)
            inter = ac + m_old
            m_t = jnp.maximum(inter, jnp.max(dmat, -1, keepdims=True))
            qm = jnp.where(lane_h0 if hh == 0 else jnp.logical_not(lane_h0), qp, 0.0)
            qmb = qm.astype(BF16)
            sc = lax.dot_general(qmb, kpb, (((1,), (1,)), ((), ())), preferred_element_type=F32)
            sc = sc * jnp.exp(dmat - m_t)
            g_inter = jnp.exp(inter - m_t)
            vh = vpb[:, hh * B_V_DIM:(hh + 1) * B_V_DIM]
            qc = jnp.dot(qmb, c_oldb, preferred_element_type=F32)[:, hh * B_V_DIM:(hh + 1) * B_V_DIM]
            num = jnp.dot(sc.astype(BF16), vh, preferred_element_type=F32) + g_inter * qc
            den = jnp.sum(sc, -1, keepdims=True) + g_inter * jnp.sum(qm * n_old, -1, keepdims=True)
            h_out = num / jnp.maximum(jnp.abs(den), jnp.exp(-m_t))
            mu = jnp.mean(h_out, -1, keepdims=True)
            hc = h_out - mu
            var = jnp.mean(hc * hc, -1, keepdims=True)
            cols = slice(h * B_V_DIM, (h + 1) * B_V_DIM)
            hn = hc * lax.rsqrt(var + LN_EPS) * ng_ref[:, cols]
            out_ref[:, cols] = (hn * jax.nn.sigmoid(o_ref[:, cols].astype(F32))).astype(out_ref.dtype)
            a_end = ac[L - 1:L, :]
            w_col = a_end - ac + ic
            m_new = jnp.maximum(a_end + m_old, jnp.max(w_col, 0, keepdims=True))
            decays.append(jnp.exp(a_end + m_old - m_new))
            ws_cols.append(jnp.exp(w_col - m_new))
            m_news.append(m_new)
        ws = jnp.where(lane_h0, jnp.broadcast_to(ws_cols[0], (L, LANES)), jnp.broadcast_to(ws_cols[1], (L, LANES)))
        kw = kp * ws
        dec_c = jnp.where(col_h0, jnp.broadcast_to(decays[0], (1, 2 * B_V_DIM)),
                          jnp.broadcast_to(decays[1], (1, 2 * B_V_DIM)))
        dec_n = jnp.where(lane_h0, jnp.broadcast_to(decays[0], (1, LANES)), jnp.broadcast_to(decays[1], (1, LANES)))
        c_ref[p] = dec_c * c_old + lax.dot_general(kw.astype(BF16), vpb, (((0,), (0,)), ((), ())),
                                                   preferred_element_type=F32)
        n_ref[p] = dec_n * n_old + jnp.sum(kw, 0, keepdims=True)
        m_ref[p] = jnp.where(lane_h0, jnp.broadcast_to(m_news[0], (1, LANES)), jnp.broadcast_to(m_news[1], (1, LANES)))


def _mlstm(proj, gates, gate_bias, conv_w, conv_b, norm_g, batch, seq):
    L = B_CHUNK
    d = D_MODEL
    slab = lambda c: pl.BlockSpec((None, L, d), lambda b, s, c=c: (b, s, c))
    full = lambda shape: pl.BlockSpec(shape, lambda b, s: (0,) * len(shape))
    return pl.pallas_call(
        _mlstm_kernel,
        grid=(batch, seq // L),
        in_specs=[slab(0), slab(1), slab(2),
                  pl.BlockSpec((None, L, LANES), lambda b, s: (b, s, 0)),
                  full((1, LANES)),
                  full((B_CONV, d)), full((1, d)), full((1, d))],
        out_specs=pl.BlockSpec((None, L, d), lambda b, s: (b, s, 0)),
        out_shape=jax.ShapeDtypeStruct((batch, seq, d), BF16),
        scratch_shapes=[pltpu.VMEM((L + 8, d), F32),
                        pltpu.VMEM((B_HEADS // 2, 2 * B_QK_DIM, 2 * B_V_DIM), F32),
                        pltpu.VMEM((B_HEADS // 2, 1, LANES), F32),
                        pltpu.VMEM((B_HEADS // 2, 1, LANES), F32)],
        compiler_params=_params("parallel", "arbitrary"),
        name="mlstm",
    )(proj, proj, proj, gates, jnp.pad(gate_bias, (0, LANES - 2 * B_HEADS)).reshape(1, LANES),
      conv_w, conv_b.reshape(1, d), norm_g.reshape(1, d))


def _mixer_mlstm(hb, w_in, gate_bias, conv_w, conv_b, norm_g, batch, seq):
    n = batch * seq
    main = 3 * D_MODEL
    tm, tn = TILE_MLSTM_PROJ
    proj = _matmul(hb, w_in[:, :main].astype(BF16), tm=tm, tn=tn, out_dtype=BF16)
    w_g = jnp.pad(w_in[:, main:], ((0, 0), (0, LANES - 2 * B_HEADS))).astype(BF16)
    gates = _matmul(hb, w_g, tm=TILE_MLSTM_GATES[0], tn=TILE_MLSTM_GATES[1])
    out = _mlstm(proj.reshape(batch, seq, main), gates.reshape(batch, seq, LANES), gate_bias,
                 conv_w, conv_b, norm_g, batch, seq)
    return out.reshape(n, D_MODEL)


def _retention_kernel(lg_ref, q_ref, k_ref, v_ref, g_ref, cos_ref, sin_ref, ng_ref, out_ref,
                      r_ref, dm_ref, xi_ref, zeta_ref):
    L = C_CHUNK
    chunk = pl.program_id(1)

    @pl.when(chunk == 0)
    def _():
        r_ref[...] = jnp.zeros(r_ref.shape, F32)
        ti = lax.broadcasted_iota(jnp.int32, (L, L), 0)
        si = lax.broadcasted_iota(jnp.int32, (L, L), 1)
        rel = (ti - si).astype(F32)
        idx = lax.broadcasted_iota(jnp.int32, (L, LANES), 0).astype(F32)
        for h in range(C_HEADS):
            lg = lg_ref[h]
            dm_ref[h] = jnp.where(rel >= 0, jnp.exp(jnp.maximum(rel, 0.0) * lg), 0.0)
            xi_ref[h] = jnp.exp((idx + 1.0) * lg)
            zeta_ref[h] = jnp.exp((L - 1.0 - idx) * lg)

    cos, sin = cos_ref[...], sin_ref[...]
    hd = C_QK_DIM // 2

    def rope(t):
        t1, t2 = t[:, :hd], t[:, hd:]
        return jnp.concatenate([t1 * cos - t2 * sin, t2 * cos + t1 * sin], -1)

    for h in range(C_HEADS):
        qk_cols = slice(h * C_QK_DIM, (h + 1) * C_QK_DIM)
        v_cols = slice(h * C_V_DIM, (h + 1) * C_V_DIM)
        q = rope(q_ref[:, qk_cols].astype(F32))
        k = rope(k_ref[:, qk_cols].astype(F32)) * (C_QK_DIM ** -0.5)
        qb = q.astype(BF16)
        vb = v_ref[:, v_cols]
        r_old = r_ref[h]
        sc = lax.dot_general(qb, k.astype(BF16), (((1,), (1,)), ((), ())), preferred_element_type=F32) * dm_ref[h]
        o = jnp.dot(sc.astype(BF16), vb, preferred_element_type=F32)
        o = o + xi_ref[h, :, 0:1] * jnp.dot(qb, r_old.astype(BF16), preferred_element_type=F32)
        kz = (k * zeta_ref[h, :, 0:1]).astype(BF16)
        cd = jnp.exp(jnp.full((1, 1), float(L), F32) * lg_ref[h])
        r_ref[h] = cd * r_old + lax.dot_general(kz, vb, (((0,), (0,)), ((), ())), preferred_element_type=F32)
        mu = jnp.mean(o, -1, keepdims=True)
        oc = o - mu
        var = jnp.mean(oc * oc, -1, keepdims=True)
        on = oc * lax.rsqrt(var + LN_EPS) * ng_ref[:, v_cols]
        g = g_ref[:, v_cols].astype(F32)
        out_ref[:, v_cols] = (on * (g * jax.nn.sigmoid(g))).astype(out_ref.dtype)


def _retention(proj, cos, sin, norm_g, batch, seq):
    L = C_CHUNK
    d = D_MODEL
    log_gamma = jnp.log(1.0 - 2.0 ** (-5.0 - jnp.arange(C_HEADS, dtype=F32)))
    return pl.pallas_call(
        _retention_kernel,
        grid=(batch, seq // L),
        in_specs=[pl.BlockSpec(memory_space=pltpu.SMEM),
                  pl.BlockSpec((None, L, d), lambda b, c: (b, c, 0)),
                  pl.BlockSpec((None, L, d), lambda b, c: (b, c, 1)),
                  pl.BlockSpec((None, L, 2 * d), lambda b, c: (b, c, 1)),
                  pl.BlockSpec((None, L, 2 * d), lambda b, c: (b, c, 2)),
                  pl.BlockSpec((L, C_QK_DIM // 2), lambda b, c: (c, 0)),
                  pl.BlockSpec((L, C_QK_DIM // 2), lambda b, c: (c, 0)),
                  pl.BlockSpec((1, 2 * d), lambda b, c: (0, 0))],
        out_specs=pl.BlockSpec((None, L, 2 * d), lambda b, c: (b, c, 0)),
        out_shape=jax.ShapeDtypeStruct((batch, seq, 2 * d), BF16),
        scratch_shapes=[pltpu.VMEM((C_HEADS, C_QK_DIM, C_V_DIM), F32),
                        pltpu.VMEM((C_HEADS, L, L), F32),
                        pltpu.VMEM((C_HEADS, L, LANES), F32),
                        pltpu.VMEM((C_HEADS, L, LANES), F32)],
        compiler_params=_params("parallel", "arbitrary"),
        name="retention",
    )(log_gamma, proj, proj, proj, proj, cos, sin, norm_g.reshape(1, 2 * d))


def _mixer_retention(hb, w_in, norm_g, cos, sin, batch, seq):
    tm, tn = TILE_RETENTION_PROJ
    proj = _matmul(hb, w_in.astype(BF16), tm=tm, tn=tn, out_dtype=BF16)
    out = _retention(proj.reshape(batch, seq, 6 * D_MODEL), cos, sin, norm_g, batch, seq)
    return out.reshape(batch * seq, 2 * D_MODEL)


def _expert_kernel(be_ref, bv_ref, x_ref, wg_ref, wu_ref, wd_ref, y_ref, wgb_ref, wub_ref, wdb_ref):
    j = pl.program_id(0)
    changed = jnp.logical_or(j == 0, be_ref[j] != be_ref[jnp.maximum(j - 1, 0)])

    @pl.when(changed)
    def _():
        wgb_ref[...] = wg_ref[...].astype(BF16)
        wub_ref[...] = wu_ref[...].astype(BF16)
        wdb_ref[...] = wd_ref[...].astype(BF16)

    real = lax.broadcasted_iota(jnp.int32, (x_ref.shape[0], 1), 0) < bv_ref[j]
    x = _unpack_pairs(jnp.where(real, x_ref[...], 0.0)).astype(BF16)
    a = jnp.dot(x, wgb_ref[...], preferred_element_type=F32)
    u = jnp.dot(x, wub_ref[...], preferred_element_type=F32)
    act = (a * jax.nn.sigmoid(a) * u).astype(BF16)
    y_ref[...] = _pack_pairs(jnp.dot(act, wdb_ref[...], preferred_element_type=F32))


def _experts(blk_e, blk_rows, xb, w_gate, w_up, w_down, layer):
    p, dw = xb.shape
    d = 2 * dw
    nb = p // MOE_BLOCK
    hid = MOE_HIDDEN
    grid_spec = pltpu.PrefetchScalarGridSpec(
        num_scalar_prefetch=2,
        grid=(nb,),
        in_specs=[pl.BlockSpec((MOE_BLOCK, dw), lambda j, be, bv: (j, 0)),
                  pl.BlockSpec((None, None, d, hid), lambda j, be, bv: (layer, be[j], 0, 0)),
                  pl.BlockSpec((None, None, d, hid), lambda j, be, bv: (layer, be[j], 0, 0)),
                  pl.BlockSpec((None, None, hid, d), lambda j, be, bv: (layer, be[j], 0, 0))],
        out_specs=pl.BlockSpec((MOE_BLOCK, dw), lambda j, be, bv: (j, 0)),
        scratch_shapes=[pltpu.VMEM((d, hid), BF16), pltpu.VMEM((d, hid), BF16), pltpu.VMEM((hid, d), BF16)],
    )
    return pl.pallas_call(
        _expert_kernel,
        grid_spec=grid_spec,
        out_shape=jax.ShapeDtypeStruct((p, dw), F32),
        compiler_params=_params("arbitrary"),
        name="moe_experts",
    )(blk_e, blk_rows, xb, w_gate, w_up, w_down)


def _combine_ln_kernel(h_ref, y0_ref, y1_ref, rt_ref, g_ref, b_ref, o_ref, ob_ref):
    rt = rt_ref[...]
    y = (_unpack_pairs(y0_ref[...]) * rt[:, R_GATE:R_GATE + 1]
         + _unpack_pairs(y1_ref[...]) * rt[:, R_GATE + 1:R_GATE + 2])
    out = _layer_norm_rows(DN_ALPHA * h_ref[...] + y, g_ref[...], b_ref[...])
    o_ref[...] = out
    ob_ref[...] = out.astype(BF16)


def _combine_ln(h, y0, y1, route, g, b, *, tm):
    n, d = h.shape
    row = pl.BlockSpec((tm, d), lambda i: (i, 0))
    words = pl.BlockSpec((tm, d // 2), lambda i: (i, 0))
    vec = pl.BlockSpec((1, d), lambda i: (0, 0))
    return pl.pallas_call(
        _combine_ln_kernel,
        grid=(n // tm,),
        in_specs=[row, words, words, pl.BlockSpec((tm, LANES), lambda i: (i, 0)), vec, vec],
        out_specs=[row, row],
        out_shape=[jax.ShapeDtypeStruct((n, d), F32), jax.ShapeDtypeStruct((n, d), BF16)],
        compiler_params=_params("parallel"),
        name="moe_combine_ln",
    )(h, y0, y1, route, g.reshape(1, d), b.reshape(1, d))


R_EID, R_GATE, R_RANK = 0, 2, 4


def _route_tile(h, w_ref, b_ref, route_ref, cnt_ref, base_ref, tri_ref):
    i = pl.program_id(0)
    tm = h.shape[0]

    @pl.when(i == 0)
    def _():
        base_ref[...] = jnp.zeros(base_ref.shape, F32)
        ti = lax.broadcasted_iota(jnp.int32, (tm, tm), 0)
        si = lax.broadcasted_iota(jnp.int32, (tm, tm), 1)
        tri_ref[...] = jnp.where(si < ti, 1.0, 0.0).astype(BF16)

    h_hi = h.astype(BF16)
    h_lo = (h - h_hi.astype(F32)).astype(BF16)
    hh = jnp.dot(h_hi, w_ref[...], preferred_element_type=F32)
    logits = (hh[:, :LANES] + hh[:, LANES:]
              + jnp.dot(h_lo, w_ref[:, :LANES], preferred_element_type=F32)) + b_ref[...]
    lane = lax.broadcasted_iota(jnp.int32, (tm, LANES), 1).astype(F32)
    neg_inf = -jnp.inf
    big = float(4 * LANES)
    is_g = lane < MOE_GROUPS
    gl = jnp.where(is_g, logits, neg_inf)
    gmax = jnp.max(gl, -1, keepdims=True)
    grp = jnp.min(jnp.where(gl == gmax, lane, big), -1, keepdims=True)
    p_grp = 1.0 / jnp.sum(jnp.where(is_g, jnp.exp(logits - gmax), 0.0), -1, keepdims=True)
    lo = MOE_GROUPS + MOE_PER_GROUP * grp
    el = jnp.where((lane >= lo) & (lane < lo + MOE_PER_GROUP), logits, neg_inf)
    v1 = jnp.max(el, -1, keepdims=True)
    i1 = jnp.min(jnp.where(el == v1, lane, big), -1, keepdims=True)
    el2 = jnp.where(lane == i1, neg_inf, el)
    v2 = jnp.max(el2, -1, keepdims=True)
    i2 = jnp.min(jnp.where(el2 == v2, lane, big), -1, keepdims=True)
    t = jnp.exp(v2 - v1)
    g1 = p_grp / (1.0 + t)
    g2 = g1 * t
    e1 = i1 - MOE_GROUPS
    e2 = i2 - MOE_GROUPS
    oh1 = jnp.where(lane == e1, 1.0, 0.0)
    oh2 = jnp.where(lane == e2, 1.0, 0.0)
    oh = oh1 + oh2
    tot = base_ref[...] + jnp.dot(tri_ref[...], oh.astype(BF16), preferred_element_type=F32)
    r1 = jnp.sum(oh1 * tot, -1, keepdims=True)
    r2 = jnp.sum(oh2 * tot, -1, keepdims=True)
    new_base = base_ref[...] + jnp.sum(oh, 0, keepdims=True)
    base_ref[...] = new_base
    cnt_ref[...] = jnp.broadcast_to(new_base, cnt_ref.shape)
    route = jnp.zeros((tm, LANES), F32)
    for k, val in enumerate((e1, e2, g1, g2, r1, r2)):
        route = jnp.where(lane == float(k), val, route)
    route_ref[...] = route


def _mm_res_ln_route_kernel(x_ref, w_ref, h_ref, g_ref, b_ref, rw_ref, rb_ref,
                            o_ref, op_ref, route_ref, cnt_ref, base_ref, tri_ref):
    y = jnp.dot(x_ref[...], w_ref[...], preferred_element_type=F32)
    out = _layer_norm_rows(DN_ALPHA * h_ref[...] + y, g_ref[...], b_ref[...])
    o_ref[...] = out
    op_ref[...] = _pack_pairs(out)
    _route_tile(out, rw_ref, rb_ref, route_ref, cnt_ref, base_ref, tri_ref)


def _matmul_res_ln_route(x, w, h, g, b, w_r, b_r, *, tm):
    n, k = x.shape
    d = w.shape[1]
    rows = lambda width: pl.BlockSpec((tm, width), lambda i: (i, 0))
    const = lambda shape: pl.BlockSpec(shape, lambda i: (0, 0))
    return pl.pallas_call(
        _mm_res_ln_route_kernel,
        grid=(n // tm,),
        in_specs=[rows(k), const((k, d)), rows(d), const((1, d)), const((1, d)),
                  const((d, 2 * LANES)), const((1, LANES))],
        out_specs=[rows(d), rows(d // 2), rows(LANES), const((8, LANES))],
        out_shape=[jax.ShapeDtypeStruct((n, d), F32), jax.ShapeDtypeStruct((n, d // 2), F32),
                   jax.ShapeDtypeStruct((n, LANES), F32), jax.ShapeDtypeStruct((8, LANES), F32)],
        scratch_shapes=[pltpu.VMEM((1, LANES), F32), pltpu.VMEM((tm, tm), BF16)],
        compiler_params=_params("arbitrary"),
        name="matmul_res_ln_route",
    )(x, w, h, g.reshape(1, d), b.reshape(1, d), w_r, b_r)


def _sc_index_rows(idx):
    n = idx.shape[0]
    return jnp.pad(idx.reshape(n // SC_ROWS, SC_ROWS), ((0, 0), (0, SC_INDEX_LANES - SC_ROWS)))


def _sc_mesh():
    return plsc.VectorSubcoreMesh(core_axis_name="core", subcore_axis_name="subcore")


def _sc_gather_rows(y, idx):
    n = idx.shape[0]
    d = y.shape[1]

    @pl.kernel(out_type=jax.ShapeDtypeStruct((n, d), y.dtype), mesh=_sc_mesh(), scratch_types=[])
    def gather(y_hbm, i_hbm, o_hbm):
        def body(i_vmem, o_vmem):
            pltpu.sync_copy(y_hbm.at[i_vmem.at[0, pl.ds(0, SC_ROWS)]], o_vmem)

        pltpu.emit_pipeline(
            body, grid=(n // SC_ROWS,),
            in_specs=[pl.BlockSpec((1, SC_INDEX_LANES), lambda i: (i, 0))],
            out_specs=[pl.BlockSpec((SC_ROWS, d), lambda i: (i, 0))],
            core_axis_name=("core", "subcore"), dimension_semantics=(pltpu.PARALLEL,),
        )(i_hbm, o_hbm)

    return gather(y, _sc_index_rows(idx))


def _sc_scatter_rows(x, idx0, idx1, p_rows):
    n, d = x.shape

    @pl.kernel(out_type=jax.ShapeDtypeStruct((p_rows, d), x.dtype), mesh=_sc_mesh(), scratch_types=[])
    def scatter(x_hbm, i0_hbm, i1_hbm, o_hbm):
        def body(x_vmem, i0_vmem, i1_vmem):
            pltpu.sync_copy(x_vmem, o_hbm.at[i0_vmem.at[0, pl.ds(0, SC_ROWS)]])
            pltpu.sync_copy(x_vmem, o_hbm.at[i1_vmem.at[0, pl.ds(0, SC_ROWS)]])

        pltpu.emit_pipeline(
            body, grid=(n // SC_ROWS,),
            in_specs=[pl.BlockSpec((SC_ROWS, d), lambda i: (i, 0)),
                      pl.BlockSpec((1, SC_INDEX_LANES), lambda i: (i, 0)),
                      pl.BlockSpec((1, SC_INDEX_LANES), lambda i: (i, 0))],
            out_specs=[],
            core_axis_name=("core", "subcore"), dimension_semantics=(pltpu.PARALLEL,),
        )(x_hbm, i0_hbm, i1_hbm)

    return scatter(x, _sc_index_rows(idx0), _sc_index_rows(idx1))


def _router_params(wg_r, bg_r, we_r, be_r):
    pad = LANES - MOE_GROUPS - MOE_EXPERTS
    w_r = jnp.pad(jnp.concatenate([wg_r, we_r], 1), ((0, 0), (0, pad)))
    w_hi = w_r.astype(BF16)
    w_lo = (w_r - w_hi.astype(F32)).astype(BF16)
    b_r = jnp.pad(jnp.concatenate([bg_r, be_r]), (0, pad)).reshape(1, LANES)
    return jnp.concatenate([w_hi, w_lo], 1), b_r


def _moe(h, hp, route, cnt, w_gate, w_up, w_down, layer, ln_g, ln_b):
    n, d = h.shape
    counts = cnt[0, :MOE_EXPERTS].astype(jnp.int32)
    padded = (counts + MOE_BLOCK - 1) // MOE_BLOCK * MOE_BLOCK
    pends = jnp.cumsum(padded)
    pstarts = pends - padded
    eid = route[:, R_EID:R_EID + MOE_TOPK].astype(jnp.int32)
    rank = route[:, R_RANK:R_RANK + MOE_TOPK].astype(jnp.int32)
    experts = jnp.arange(MOE_EXPERTS, dtype=jnp.int32)
    dest = jnp.sum(jnp.where(eid[..., None] == experts, pstarts, 0), -1) + rank
    p_rows = n * MOE_TOPK + MOE_EXPERTS * MOE_BLOCK
    nb = p_rows // MOE_BLOCK
    blk_start = jnp.arange(nb, dtype=jnp.int32) * MOE_BLOCK
    blk_e = jnp.minimum(jnp.sum((pends[None, :] <= blk_start[:, None]).astype(jnp.int32), -1), MOE_EXPERTS - 1)
    real_ends = pstarts + counts
    blk_end = jnp.sum(jnp.where(blk_e[:, None] == experts[None, :], real_ends[None, :], 0), -1)
    blk_rows = jnp.clip(blk_end - blk_start, 0, MOE_BLOCK).astype(jnp.int32)
    dest0, dest1 = dest[:, 0], dest[:, 1]
    xb = _sc_scatter_rows(hp, dest0, dest1, p_rows)
    yb = _experts(blk_e, blk_rows, xb, w_gate, w_up, w_down, layer)
    return _combine_ln(h, _sc_gather_rows(yb, dest0), _sc_gather_rows(yb, dest1), route, ln_g, ln_b,
                       tm=TILE_TOKEN_ROWS)


def kernel(x, positions, ln1_g, ln1_b, ln2_g, ln2_b, a_w_in, a_w_out, b_w_in, b_gate_bias, b_conv_w, b_conv_b,
           b_norm_g, b_w_out, c_w_in, c_norm_g, c_w_out, r_group_w, r_group_b, r_expert_w, r_expert_b,
           e_w_gate, e_w_up, e_w_down):
    batch, seq, d = x.shape
    n = batch * seq
    tabs_a = _rope_tables_a(positions)
    inv_c = C_THETA ** (-jnp.arange(0, C_QK_DIM, 2, dtype=F32) / C_QK_DIM)
    ang_c = positions.astype(F32)[:, None] * inv_c[None, :]
    cos_c, sin_c = jnp.cos(ang_c), jnp.sin(ang_c)
    h = x.reshape(n, d)
    hb = h
    for i in range(DEPTH):
        kind, j = i % 3, i // 3
        if kind == 0:
            y = _mixer_dilated(hb, a_w_in[j], tabs_a, batch, seq)
            w_out = a_w_out[j]
        elif kind == 1:
            y = _mixer_mlstm(hb, b_w_in[j], b_gate_bias[j], b_conv_w[j], b_conv_b[j], b_norm_g[j], batch, seq)
            w_out = b_w_out[j]
        else:
            y = _mixer_retention(hb, c_w_in[j], c_norm_g[j], cos_c, sin_c, batch, seq)
            w_out = c_w_out[j]
        w_r, b_r = _router_params(r_group_w[i], r_group_b[i], r_expert_w[i], r_expert_b[i])
        h, hp, route, cnt = _matmul_res_ln_route(y, w_out.astype(BF16), h, ln1_g[i], ln1_b[i], w_r, b_r,
                                                 tm=TILE_TOKEN_ROWS)
        h, hb = _moe(h, hp, route, cnt, e_w_gate, e_w_up, e_w_down, i, ln2_g[i], ln2_b[i])
    return h.reshape(batch, seq, d)
```

```python
import jax
import jax.numpy as jnp
from jax import lax
from jax.experimental import pallas as pl
from jax.experimental.pallas import tpu as pltpu
from jax.experimental.pallas import tpu_sc as plsc

F32 = jnp.float32
BF16 = jnp.bfloat16

D_MODEL = 1024
DEPTH = 4
DN_ALPHA = (2.0 * DEPTH) ** 0.25
LN_EPS = 1e-5

A_HEADS = 16
A_HEAD_DIM = 64
A_DILATIONS = (1, 4, 16)
A_BLOCK = 128
A_UNROLL = 16
A_GROUP = 16
A_PITCH = 20
LOG2_E = 1.4426950408889634
A_ROT_DIM = 16
ROPE_THETA = 500000.0

B_HEADS = 8
B_QK_DIM = 64
B_V_DIM = 128
B_CONV = 4
B_CHUNK = 256

C_HEADS = 4
C_QK_DIM = 256
C_V_DIM = 512
C_CHUNK = 256
C_THETA = 10000.0

MOE_GROUPS = 8
MOE_PER_GROUP = 8
MOE_EXPERTS = 64
MOE_TOPK = 2
MOE_HIDDEN = 256
MOE_BLOCK = 512

LANES = 128
SC_ROWS = 64
SC_INDEX_LANES = LANES
NEG = -1e30
V7X_VMEM_BYTES = 64 * 1024 * 1024
VMEM_LIMIT = V7X_VMEM_BYTES * 3 // 4

TILE_PROJ_A = 512
TILE_MLSTM_PROJ = (1024, 1536)
TILE_MLSTM_GATES = (2048, LANES)
TILE_RETENTION_PROJ = (1024, 2048)
TILE_TOKEN_ROWS = 512


def _params(*sem):
    return pltpu.CompilerParams(dimension_semantics=sem, vmem_limit_bytes=VMEM_LIMIT)


def _mm_kernel(x_ref, w_ref, o_ref):
    o_ref[...] = jnp.dot(x_ref[...], w_ref[...], preferred_element_type=F32).astype(o_ref.dtype)


def _matmul(x, w, *, tm, tn, out_dtype=F32):
    n, k = x.shape
    m = w.shape[1]
    return pl.pallas_call(
        _mm_kernel,
        grid=(n // tm, m // tn),
        in_specs=[pl.BlockSpec((tm, k), lambda i, j: (i, 0)),
                  pl.BlockSpec((k, tn), lambda i, j: (0, j))],
        out_specs=pl.BlockSpec((tm, tn), lambda i, j: (i, j)),
        out_shape=jax.ShapeDtypeStruct((n, m), out_dtype),
        compiler_params=_params("parallel", "parallel"),
        name="matmul",
    )(x, w)


def _layer_norm_rows(z, g, b):
    mu = jnp.mean(z, -1, keepdims=True)
    zc = z - mu
    var = jnp.mean(zc * zc, -1, keepdims=True)
    return zc * lax.rsqrt(var + LN_EPS) * g + b


def _pack_pairs(x):
    c = x.shape[1] // 2
    hi = pltpu.bitcast(x[:, :c].astype(BF16).astype(F32), jnp.uint32)
    lo = pltpu.bitcast(x[:, c:].astype(BF16).astype(F32), jnp.uint32)
    return pltpu.bitcast(hi | (lo >> 16), F32)


def _unpack_pairs(w):
    bits = pltpu.bitcast(w, jnp.uint32)
    hi = pltpu.bitcast(bits & jnp.uint32(0xFFFF0000), F32)
    lo = pltpu.bitcast(bits << 16, F32)
    return jnp.concatenate([hi, lo], axis=1)


def _proj_a_kernel(x_ref, w_ref, c_ref, s1_ref, s2_ref, q_ref, k_ref, v_ref):
    x = x_ref[...].astype(BF16)
    width = 2 * LANES
    for c, ref in ((0, q_ref), (1, k_ref), (2, v_ref)):
        for j in range(D_MODEL // width):
            col = c * D_MODEL + j * width
            y = jnp.dot(x, w_ref[:, col:col + width], preferred_element_type=F32)
            if c < 2:
                half = A_ROT_DIM // 2
                y = (y * c_ref[...] + pltpu.roll(y, width - half, 1) * s1_ref[...]
                     + pltpu.roll(y, half, 1) * s2_ref[...])
            if c == 0:
                y = y * (LOG2_E * A_HEAD_DIM ** -0.5)
            ref[0, 2 * j] = y[:, :LANES]
            ref[0, 2 * j + 1] = y[:, LANES:]


def _proj_a(xb, w, tabs, batch, seq, *, tm):
    n, d = xb.shape
    spb = seq // tm
    hp = D_MODEL // LANES
    qkv_shape = jax.ShapeDtypeStruct((batch, hp, seq, LANES), F32)
    out_spec = pl.BlockSpec((1, hp, tm, LANES), lambda i: (i // spb, 0, i % spb, 0))
    tab_spec = pl.BlockSpec((tm, 2 * LANES), lambda i: (i % spb, 0))
    return pl.pallas_call(
        _proj_a_kernel,
        grid=(n // tm,),
        in_specs=[pl.BlockSpec((tm, d), lambda i: (i, 0)),
                  pl.BlockSpec((d, 3 * d), lambda i: (0, 0)),
                  tab_spec, tab_spec, tab_spec],
        out_specs=[out_spec, out_spec, out_spec],
        out_shape=[qkv_shape, qkv_shape, qkv_shape],
        compiler_params=_params("parallel"),
        name="proj_a",
    )(xb, w, *tabs)


def _rope_tables_a(positions):
    half = A_ROT_DIM // 2
    inv = ROPE_THETA ** (-jnp.arange(0, A_ROT_DIM, 2, dtype=F32) / A_ROT_DIM)
    ang = positions.astype(F32)[:, None] * inv[None, :]
    cos, sin = jnp.cos(ang), jnp.sin(ang)
    s = positions.shape[0]
    pad = jnp.zeros((s, A_HEAD_DIM - A_ROT_DIM), F32)
    c_head = jnp.concatenate([cos, cos, pad + 1.0], -1)
    s1_head = jnp.concatenate([-sin, jnp.zeros_like(sin), pad], -1)
    s2_head = jnp.concatenate([jnp.zeros_like(sin), sin, pad], -1)
    reps = 2 * LANES // A_HEAD_DIM
    return tuple(jnp.tile(t, (1, reps)) for t in (c_head, s1_head, s2_head))


def _attn_blocks(q_ref, k_ref, v_ref, o_ref, lse_ref, bias_ref, hmask, head0, first_block, d, pitch):
    nk = 2 * A_BLOCK
    loaded = []
    for u in range(A_UNROLL):
        g = first_block + u
        r = g % d
        n = g // d
        qstart = n * (A_BLOCK * pitch) + r
        kstart = jnp.maximum(qstart - A_BLOCK * pitch, r)
        if pitch == 1:
            qstart = pl.multiple_of(qstart, A_BLOCK)
            kstart = pl.multiple_of(kstart, A_BLOCK)
            qsl, ksl = pl.ds(qstart, A_BLOCK), pl.ds(kstart, nk)
        else:
            qsl, ksl = pl.ds(qstart, A_BLOCK, stride=pitch), pl.ds(kstart, nk, stride=pitch)
        bias = bias_ref[jnp.minimum(n, 1)]
        loaded.append((qsl, q_ref[qsl, :].astype(BF16), k_ref[ksl, :].astype(BF16), v_ref[ksl, :].astype(BF16), bias))
    results = []
    for qsl, qb, kb, vb, bias in loaded:
        pvs, ms, ls = [], [], []
        for h in range(2):
            s = lax.dot_general(qb * hmask[h], kb, (((1,), (1,)), ((), ())), preferred_element_type=F32) + bias
            m = jnp.max(s, -1, keepdims=True)
            p = jnp.exp2(s - m)
            ms.append(jnp.broadcast_to(m, (A_BLOCK, LANES)))
            ls.append(jnp.broadcast_to(jnp.sum(p, -1, keepdims=True), (A_BLOCK, LANES)))
            pvs.append(jnp.dot(p.astype(BF16), vb, preferred_element_type=F32))
        l = jnp.where(head0, ls[0], ls[1])
        out = jnp.where(head0, pvs[0], pvs[1]) * (1.0 / l)
        results.append((qsl, out, jnp.where(head0, ms[0], ms[1]) + jnp.log2(l)))
    for qsl, out, lse in results:
        o_ref[qsl, :] = out
        lse_ref[qsl, :] = lse


def _attn_kernel(q_ref, k_ref, v_ref, o_ref, q16_ref, k16_ref, v16_ref, ob_ref, lb_ref, o16_ref, l16_ref,
                 bias_ref):
    seq = q_ref.shape[0]
    groups = seq // A_GROUP
    head0 = lax.broadcasted_iota(jnp.int32, (A_BLOCK, LANES), 1) < A_HEAD_DIM
    hmask = [jnp.where(head0, 1.0, 0.0).astype(BF16), jnp.where(head0, 0.0, 1.0).astype(BF16)]
    qi = lax.broadcasted_iota(jnp.int32, (A_BLOCK, 2 * A_BLOCK), 0)
    kj = lax.broadcasted_iota(jnp.int32, (A_BLOCK, 2 * A_BLOCK), 1)
    bias_ref[0] = jnp.where(kj <= qi, 0.0, NEG).astype(F32)
    bias_ref[1] = jnp.where((kj >= qi) & (kj <= qi + A_BLOCK), 0.0, NEG).astype(F32)

    def spread(g, carry):
        src = pl.ds(pl.multiple_of(g * A_GROUP, A_GROUP), A_GROUP)
        dst = pl.ds(pl.multiple_of(g * A_PITCH, 4), A_GROUP)
        q16_ref[dst, :] = q_ref[src, :]
        k16_ref[dst, :] = k_ref[src, :]
        v16_ref[dst, :] = v_ref[src, :]
        return carry

    lax.fori_loop(0, groups, spread, 0, unroll=8)

    branches = ((1, 1, q_ref, k_ref, v_ref, ob_ref.at[0], lb_ref.at[0]),
                (4, 4, q_ref, k_ref, v_ref, ob_ref.at[1], lb_ref.at[1]),
                (16, A_PITCH, q16_ref, k16_ref, v16_ref, o16_ref, l16_ref))
    for d, pitch, qr, kr, vr, orf, lrf in branches:

        def body(it, carry, d=d, pitch=pitch, qr=qr, kr=kr, vr=vr, orf=orf, lrf=lrf):
            _attn_blocks(qr, kr, vr, orf, lrf, bias_ref, hmask, head0, it * A_UNROLL, d, pitch)
            return carry

        lax.fori_loop(0, seq // (A_BLOCK * A_UNROLL), body, 0)

    def mix(g, carry):
        nat = pl.ds(pl.multiple_of(g * A_GROUP, A_GROUP), A_GROUP)
        pad = pl.ds(pl.multiple_of(g * A_PITCH, 4), A_GROUP)
        o0, o1, o2 = ob_ref[0, nat, :], ob_ref[1, nat, :], o16_ref[pad, :]
        l0, l1, l2 = lb_ref[0, nat, :], lb_ref[1, nat, :], l16_ref[pad, :]
        mx = jnp.maximum(jnp.maximum(l0, l1), l2)
        w0, w1, w2 = jnp.exp2(l0 - mx), jnp.exp2(l1 - mx), jnp.exp2(l2 - mx)
        o_ref[nat, :] = ((w0 * o0 + w1 * o1 + w2 * o2) / (w0 + w1 + w2)).astype(o_ref.dtype)
        return carry

    lax.fori_loop(0, groups, mix, 0, unroll=8)


def _attention(q, k, v):
    batch, hp, seq, _ = q.shape
    assert seq % (2 * A_BLOCK * max(A_DILATIONS)) == 0 and seq % (A_BLOCK * A_UNROLL) == 0
    in_spec = pl.BlockSpec((None, None, seq, LANES), lambda b, p: (b, p, 0, 0))
    padded = seq // A_GROUP * A_PITCH
    return pl.pallas_call(
        _attn_kernel,
        grid=(batch, hp),
        in_specs=[in_spec, in_spec, in_spec],
        out_specs=pl.BlockSpec((None, seq, LANES), lambda b, p: (b, 0, p)),
        out_shape=jax.ShapeDtypeStruct((batch, seq, hp * LANES), BF16),
        scratch_shapes=[pltpu.VMEM((padded, LANES), F32)] * 3
        + [pltpu.VMEM((2, seq, LANES), F32)] * 2
        + [pltpu.VMEM((padded, LANES), F32)] * 2
        + [pltpu.VMEM((2, A_BLOCK, 2 * A_BLOCK), F32)],
        compiler_params=_params("parallel", "parallel"),
        name="dilated_attention",
    )(q, k, v)


def _mixer_dilated(hb, w_in, tabs, batch, seq):
    q, k, v = _proj_a(hb, w_in.astype(BF16), tabs, batch, seq, tm=TILE_PROJ_A)
    o = _attention(q, k, v)
    return o.reshape(batch * seq, D_MODEL)


def _mlstm_kernel(qk_ref, v_ref, o_ref, gc_ref, gb_ref, cw_ref, cb_ref, ng_ref, out_ref,
                  ext_ref, c_ref, n_ref, m_ref):
    L = B_CHUNK
    chunk = pl.program_id(1)

    @pl.when(chunk == 0)
    def _():
        ext_ref[0:8, :] = jnp.zeros((8, D_MODEL), F32)
        c_ref[...] = jnp.zeros(c_ref.shape, F32)
        n_ref[...] = jnp.zeros(n_ref.shape, F32)
        m_ref[...] = jnp.zeros(m_ref.shape, F32)

    u = qk_ref[...].astype(F32)
    ext_ref[8:8 + L, :] = u
    conv = u * cw_ref[B_CONV - 1:B_CONV, :] + cb_ref[...]
    for j in range(1, B_CONV):
        conv = conv + ext_ref[pl.ds(8 - j, L), :] * cw_ref[B_CONV - 1 - j:B_CONV - j, :]
    ext_ref[0:8, :] = u[L - 8:, :]
    qk = conv * jax.nn.sigmoid(conv)
    half = D_MODEL // 2

    gc = gc_ref[...] + gb_ref[...]
    gr = gc.T
    i_col, i_row = gc, gr[:B_HEADS, :]
    lf_col = jax.nn.log_sigmoid(gc)
    lf_row = jax.nn.log_sigmoid(gr[B_HEADS:2 * B_HEADS, :])
    ti = lax.broadcasted_iota(jnp.int32, (L, L), 0)
    si = lax.broadcasted_iota(jnp.int32, (L, L), 1)
    causal = ti >= si
    tri = causal.astype(F32)
    a_col = jnp.dot(tri, lf_col, preferred_element_type=F32, precision=lax.Precision.HIGHEST)
    a_row = lax.dot_general(lf_row, tri, (((1,), (1,)), ((), ())), preferred_element_type=F32,
                            precision=lax.Precision.HIGHEST)
    lane = lax.broadcasted_iota(jnp.int32, (1, LANES), 1)
    lane_h0 = lane < B_QK_DIM
    col_h0 = lax.broadcasted_iota(jnp.int32, (1, 2 * B_V_DIM), 1) < B_V_DIM

    for p in range(B_HEADS // 2):
        qp = qk[:, p * LANES:(p + 1) * LANES]
        kp = qk[:, half + p * LANES:half + (p + 1) * LANES] * (B_QK_DIM ** -0.5)
        vp = v_ref[:, p * 2 * B_V_DIM:(p + 1) * 2 * B_V_DIM]
        kpb = kp.astype(BF16)
        vpb = vp.astype(BF16)
        c_old = c_ref[p]
        c_oldb = c_old.astype(BF16)
        n_old = n_ref[p]
        m_pair = m_ref[p]
        ws_cols, decays, m_news = [], [], []
        for hh in range(2):
            h = 2 * p + hh
            m_old = m_pair[:, hh * B_QK_DIM:hh * B_QK_DIM + 1]
            ac, ar = a_col[:, B_HEADS + h:B_HEADS + h + 1], a_row[h:h + 1, :]
            ic, ir = i_col[:, h:h + 1], i_row[h:h + 1, :]
            dmat = jnp.where(causal, ac + (ir - ar), NEG)
            inter = ac + m_old
            m_t = jnp.maximum(inter, jnp.max(dmat, -1, keepdims=True))
            qm = jnp.where(lane_h0 if hh == 0 else jnp.logical_not(lane_h0), qp, 0.0)
            qmb = qm.astype(BF16)
            sc = lax.dot_general(qmb, kpb, (((1,), (1,)), ((), ())), preferred_element_type=F32)
            sc = sc * jnp.exp(dmat - m_t)
            g_inter = jnp.exp(inter - m_t)
            vh = vpb[:, hh * B_V_DIM:(hh + 1) * B_V_DIM]
            qc = jnp.dot(qmb, c_oldb, preferred_element_type=F32)[:, hh * B_V_DIM:(hh + 1) * B_V_DIM]
            num = jnp.dot(sc.astype(BF16), vh, preferred_element_type=F32) + g_inter * qc
            den = jnp.sum(sc, -1, keepdims=True) + g_inter * jnp.sum(qm * n_old, -1, keepdims=True)
            h_out = num / jnp.maximum(jnp.abs(den), jnp.exp(-m_t))
            mu = jnp.mean(h_out, -1, keepdims=True)
            hc = h_out - mu
            var = jnp.mean(hc * hc, -1, keepdims=True)
            cols = slice(h * B_V_DIM, (h + 1) * B_V_DIM)
            hn = hc * lax.rsqrt(var + LN_EPS) * ng_ref[:, cols]
            out_ref[:, cols] = (hn * jax.nn.sigmoid(o_ref[:, cols].astype(F32))).astype(out_ref.dtype)
            a_end = ac[L - 1:L, :]
            w_col = a_end - ac + ic
            m_new = jnp.maximum(a_end + m_old, jnp.max(w_col, 0, keepdims=True))
            decays.append(jnp.exp(a_end + m_old - m_new))
            ws_cols.append(jnp.exp(w_col - m_new))
            m_news.append(m_new)
        ws = jnp.where(lane_h0, jnp.broadcast_to(ws_cols[0], (L, LANES)), jnp.broadcast_to(ws_cols[1], (L, LANES)))
        kw = kp * ws
        dec_c = jnp.where(col_h0, jnp.broadcast_to(decays[0], (1, 2 * B_V_DIM)),
                          jnp.broadcast_to(decays[1], (1, 2 * B_V_DIM)))
        dec_n = jnp.where(lane_h0, jnp.broadcast_to(decays[0], (1, LANES)), jnp.broadcast_to(decays[1], (1, LANES)))
        c_ref[p] = dec_c * c_old + lax.dot_general(kw.astype(BF16), vpb, (((0,), (0,)), ((), ())),
                                                   preferred_element_type=F32)
        n_ref[p] = dec_n * n_old + jnp.sum(kw, 0, keepdims=True)
        m_ref[p] = jnp.where(lane_h0, jnp.broadcast_to(m_news[0], (1, LANES)), jnp.broadcast_to(m_news[1], (1, LANES)))


def _mlstm(proj, gates, gate_bias, conv_w, conv_b, norm_g, batch, seq):
    L = B_CHUNK
    d = D_MODEL
    slab = lambda c: pl.BlockSpec((None, L, d), lambda b, s, c=c: (b, s, c))
    full = lambda shape: pl.BlockSpec(shape, lambda b, s: (0,) * len(shape))
    return pl.pallas_call(
        _mlstm_kernel,
        grid=(batch, seq // L),
        in_specs=[slab(0), slab(1), slab(2),
                  pl.BlockSpec((None, L, LANES), lambda b, s: (b, s, 0)),
                  full((1, LANES)),
                  full((B_CONV, d)), full((1, d)), full((1, d))],
        out_specs=pl.BlockSpec((None, L, d), lambda b, s: (b, s, 0)),
        out_shape=jax.ShapeDtypeStruct((batch, seq, d), BF16),
        scratch_shapes=[pltpu.VMEM((L + 8, d), F32),
                        pltpu.VMEM((B_HEADS // 2, 2 * B_QK_DIM, 2 * B_V_DIM), F32),
                        pltpu.VMEM((B_HEADS // 2, 1, LANES), F32),
                        pltpu.VMEM((B_HEADS // 2, 1, LANES), F32)],
        compiler_params=_params("parallel", "arbitrary"),
        name="mlstm",
    )(proj, proj, proj, gates, jnp.pad(gate_bias, (0, LANES - 2 * B_HEADS)).reshape(1, LANES),
      conv_w, conv_b.reshape(1, d), norm_g.reshape(1, d))


def _mixer_mlstm(hb, w_in, gate_bias, conv_w, conv_b, norm_g, batch, seq):
    n = batch * seq
    main = 3 * D_MODEL
    tm, tn = TILE_MLSTM_PROJ
    proj = _matmul(hb, w_in[:, :main].astype(BF16), tm=tm, tn=tn, out_dtype=BF16)
    w_g = jnp.pad(w_in[:, main:], ((0, 0), (0, LANES - 2 * B_HEADS))).astype(BF16)
    gates = _matmul(hb, w_g, tm=TILE_MLSTM_GATES[0], tn=TILE_MLSTM_GATES[1])
    out = _mlstm(proj.reshape(batch, seq, main), gates.reshape(batch, seq, LANES), gate_bias,
                 conv_w, conv_b, norm_g, batch, seq)
    return out.reshape(n, D_MODEL)


def _retention_kernel(lg_ref, q_ref, k_ref, v_ref, g_ref, cos_ref, sin_ref, ng_ref, out_ref,
                      r_ref, dm_ref, xi_ref, zeta_ref):
    L = C_CHUNK
    chunk = pl.program_id(1)

    @pl.when(chunk == 0)
    def _():
        r_ref[...] = jnp.zeros(r_ref.shape, F32)
        ti = lax.broadcasted_iota(jnp.int32, (L, L), 0)
        si = lax.broadcasted_iota(jnp.int32, (L, L), 1)
        rel = (ti - si).astype(F32)
        idx = lax.broadcasted_iota(jnp.int32, (L, LANES), 0).astype(F32)
        for h in range(C_HEADS):
            lg = lg_ref[h]
            dm_ref[h] = jnp.where(rel >= 0, jnp.exp(jnp.maximum(rel, 0.0) * lg), 0.0)
            xi_ref[h] = jnp.exp((idx + 1.0) * lg)
            zeta_ref[h] = jnp.exp((L - 1.0 - idx) * lg)

    cos, sin = cos_ref[...], sin_ref[...]
    hd = C_QK_DIM // 2

    def rope(t):
        t1, t2 = t[:, :hd], t[:, hd:]
        return jnp.concatenate([t1 * cos - t2 * sin, t2 * cos + t1 * sin], -1)

    for h in range(C_HEADS):
        qk_cols = slice(h * C_QK_DIM, (h + 1) * C_QK_DIM)
        v_cols = slice(h * C_V_DIM, (h + 1) * C_V_DIM)
        q = rope(q_ref[:, qk_cols].astype(F32))
        k = rope(k_ref[:, qk_cols].astype(F32)) * (C_QK_DIM ** -0.5)
        qb = q.astype(BF16)
        vb = v_ref[:, v_cols]
        r_old = r_ref[h]
        sc = lax.dot_general(qb, k.astype(BF16), (((1,), (1,)), ((), ())), preferred_element_type=F32) * dm_ref[h]
        o = jnp.dot(sc.astype(BF16), vb, preferred_element_type=F32)
        o = o + xi_ref[h, :, 0:1] * jnp.dot(qb, r_old.astype(BF16), preferred_element_type=F32)
        kz = (k * zeta_ref[h, :, 0:1]).astype(BF16)
        cd = jnp.exp(jnp.full((1, 1), float(L), F32) * lg_ref[h])
        r_ref[h] = cd * r_old + lax.dot_general(kz, vb, (((0,), (0,)), ((), ())), preferred_element_type=F32)
        mu = jnp.mean(o, -1, keepdims=True)
        oc = o - mu
        var = jnp.mean(oc * oc, -1, keepdims=True)
        on = oc * lax.rsqrt(var + LN_EPS) * ng_ref[:, v_cols]
        g = g_ref[:, v_cols].astype(F32)
        out_ref[:, v_cols] = (on * (g * jax.nn.sigmoid(g))).astype(out_ref.dtype)


def _retention(proj, cos, sin, norm_g, batch, seq):
    L = C_CHUNK
    d = D_MODEL
    log_gamma = jnp.log(1.0 - 2.0 ** (-5.0 - jnp.arange(C_HEADS, dtype=F32)))
    return pl.pallas_call(
        _retention_kernel,
        grid=(batch, seq // L),
        in_specs=[pl.BlockSpec(memory_space=pltpu.SMEM),
                  pl.BlockSpec((None, L, d), lambda b, c: (b, c, 0)),
                  pl.BlockSpec((None, L, d), lambda b, c: (b, c, 1)),
                  pl.BlockSpec((None, L, 2 * d), lambda b, c: (b, c, 1)),
                  pl.BlockSpec((None, L, 2 * d), lambda b, c: (b, c, 2)),
                  pl.BlockSpec((L, C_QK_DIM // 2), lambda b, c: (c, 0)),
                  pl.BlockSpec((L, C_QK_DIM // 2), lambda b, c: (c, 0)),
                  pl.BlockSpec((1, 2 * d), lambda b, c: (0, 0))],
        out_specs=pl.BlockSpec((None, L, 2 * d), lambda b, c: (b, c, 0)),
        out_shape=jax.ShapeDtypeStruct((batch, seq, 2 * d), BF16),
        scratch_shapes=[pltpu.VMEM((C_HEADS, C_QK_DIM, C_V_DIM), F32),
                        pltpu.VMEM((C_HEADS, L, L), F32),
                        pltpu.VMEM((C_HEADS, L, LANES), F32),
                        pltpu.VMEM((C_HEADS, L, LANES), F32)],
        compiler_params=_params("parallel", "arbitrary"),
        name="retention",
    )(log_gamma, proj, proj, proj, proj, cos, sin, norm_g.reshape(1, 2 * d))


def _mixer_retention(hb, w_in, norm_g, cos, sin, batch, seq):
    tm, tn = TILE_RETENTION_PROJ
    proj = _matmul(hb, w_in.astype(BF16), tm=tm, tn=tn, out_dtype=BF16)
    out = _retention(proj.reshape(batch, seq, 6 * D_MODEL), cos, sin, norm_g, batch, seq)
    return out.reshape(batch * seq, 2 * D_MODEL)


def _expert_kernel(be_ref, bv_ref, x_ref, wg_ref, wu_ref, wd_ref, y_ref, wgb_ref, wub_ref, wdb_ref):
    j = pl.program_id(0)
    changed = jnp.logical_or(j == 0, be_ref[j] != be_ref[jnp.maximum(j - 1, 0)])

    @pl.when(changed)
    def _():
        wgb_ref[...] = wg_ref[...].astype(BF16)
        wub_ref[...] = wu_ref[...].astype(BF16)
        wdb_ref[...] = wd_ref[...].astype(BF16)

    real = lax.broadcasted_iota(jnp.int32, (x_ref.shape[0], 1), 0) < bv_ref[j]
    x = _unpack_pairs(jnp.where(real, x_ref[...], 0.0)).astype(BF16)
    a = jnp.dot(x, wgb_ref[...], preferred_element_type=F32)
    u = jnp.dot(x, wub_ref[...], preferred_element_type=F32)
    act = (a * jax.nn.sigmoid(a) * u).astype(BF16)
    y_ref[...] = _pack_pairs(jnp.dot(act, wdb_ref[...], preferred_element_type=F32))


def _experts(blk_e, blk_rows, xb, w_gate, w_up, w_down, layer):
    p, dw = xb.shape
    d = 2 * dw
    nb = p // MOE_BLOCK
    hid = MOE_HIDDEN
    grid_spec = pltpu.PrefetchScalarGridSpec(
        num_scalar_prefetch=2,
        grid=(nb,),
        in_specs=[pl.BlockSpec((MOE_BLOCK, dw), lambda j, be, bv: (j, 0)),
                  pl.BlockSpec((None, None, d, hid), lambda j, be, bv: (layer, be[j], 0, 0)),
                  pl.BlockSpec((None, None, d, hid), lambda j, be, bv: (layer, be[j], 0, 0)),
                  pl.BlockSpec((None, None, hid, d), lambda j, be, bv: (layer, be[j], 0, 0))],
        out_specs=pl.BlockSpec((MOE_BLOCK, dw), lambda j, be, bv: (j, 0)),
        scratch_shapes=[pltpu.VMEM((d, hid), BF16), pltpu.VMEM((d, hid), BF16), pltpu.VMEM((hid, d), BF16)],
    )
    return pl.pallas_call(
        _expert_kernel,
        grid_spec=grid_spec,
        out_shape=jax.ShapeDtypeStruct((p, dw), F32),
        compiler_params=_params("arbitrary"),
        name="moe_experts",
    )(blk_e, blk_rows, xb, w_gate, w_up, w_down)


def _combine_ln_kernel(h_ref, y0_ref, y1_ref, rt_ref, g_ref, b_ref, o_ref, ob_ref):
    rt = rt_ref[...]
    y = (_unpack_pairs(y0_ref[...]) * rt[:, R_GATE:R_GATE + 1]
         + _unpack_pairs(y1_ref[...]) * rt[:, R_GATE + 1:R_GATE + 2])
    out = _layer_norm_rows(DN_ALPHA * h_ref[...] + y, g_ref[...], b_ref[...])
    o_ref[...] = out
    ob_ref[...] = out.astype(BF16)


def _combine_ln(h, y0, y1, route, g, b, *, tm):
    n, d = h.shape
    row = pl.BlockSpec((tm, d), lambda i: (i, 0))
    words = pl.BlockSpec((tm, d // 2), lambda i: (i, 0))
    vec = pl.BlockSpec((1, d), lambda i: (0, 0))
    return pl.pallas_call(
        _combine_ln_kernel,
        grid=(n // tm,),
        in_specs=[row, words, words, pl.BlockSpec((tm, LANES), lambda i: (i, 0)), vec, vec],
        out_specs=[row, row],
        out_shape=[jax.ShapeDtypeStruct((n, d), F32), jax.ShapeDtypeStruct((n, d), BF16)],
        compiler_params=_params("parallel"),
        name="moe_combine_ln",
    )(h, y0, y1, route, g.reshape(1, d), b.reshape(1, d))


R_EID, R_GATE, R_RANK = 0, 2, 4


def _route_tile(h, w_ref, b_ref, route_ref, cnt_ref, base_ref, tri_ref):
    i = pl.program_id(0)
    tm = h.shape[0]

    @pl.when(i == 0)
    def _():
        base_ref[...] = jnp.zeros(base_ref.shape, F32)
        ti = lax.broadcasted_iota(jnp.int32, (tm, tm), 0)
        si = lax.broadcasted_iota(jnp.int32, (tm, tm), 1)
        tri_ref[...] = jnp.where(si < ti, 1.0, 0.0).astype(BF16)

    h_hi = h.astype(BF16)
    h_lo = (h - h_hi.astype(F32)).astype(BF16)
    hh = jnp.dot(h_hi, w_ref[...], preferred_element_type=F32)
    logits = (hh[:, :LANES] + hh[:, LANES:]
              + jnp.dot(h_lo, w_ref[:, :LANES], preferred_element_type=F32)) + b_ref[...]
    lane = lax.broadcasted_iota(jnp.int32, (tm, LANES), 1).astype(F32)
    neg_inf = -jnp.inf
    big = float(4 * LANES)
    is_g = lane < MOE_GROUPS
    gl = jnp.where(is_g, logits, neg_inf)
    gmax = jnp.max(gl, -1, keepdims=True)
    grp = jnp.min(jnp.where(gl == gmax, lane, big), -1, keepdims=True)
    p_grp = 1.0 / jnp.sum(jnp.where(is_g, jnp.exp(logits - gmax), 0.0), -1, keepdims=True)
    lo = MOE_GROUPS + MOE_PER_GROUP * grp
    el = jnp.where((lane >= lo) & (lane < lo + MOE_PER_GROUP), logits, neg_inf)
    v1 = jnp.max(el, -1, keepdims=True)
    i1 = jnp.min(jnp.where(el == v1, lane, big), -1, keepdims=True)
    el2 = jnp.where(lane == i1, neg_inf, el)
    v2 = jnp.max(el2, -1, keepdims=True)
    i2 = jnp.min(jnp.where(el2 == v2, lane, big), -1, keepdims=True)
    t = jnp.exp(v2 - v1)
    g1 = p_grp / (1.0 + t)
    g2 = g1 * t
    e1 = i1 - MOE_GROUPS
    e2 = i2 - MOE_GROUPS
    oh1 = jnp.where(lane == e1, 1.0, 0.0)
    oh2 = jnp.where(lane == e2, 1.0, 0.0)
    oh = oh1 + oh2
    tot = base_ref[...] + jnp.dot(tri_ref[...], oh.astype(BF16), preferred_element_type=F32)
    r1 = jnp.sum(oh1 * tot, -1, keepdims=True)
    r2 = jnp.sum(oh2 * tot, -1, keepdims=True)
    new_base = base_ref[...] + jnp.sum(oh, 0, keepdims=True)
    base_ref[...] = new_base
    cnt_ref[...] = jnp.broadcast_to(new_base, cnt_ref.shape)
    route = jnp.zeros((tm, LANES), F32)
    for k, val in enumerate((e1, e2, g1, g2, r1, r2)):
        route = jnp.where(lane == float(k), val, route)
    route_ref[...] = route


def _mm_res_ln_route_kernel(x_ref, w_ref, h_ref, g_ref, b_ref, rw_ref, rb_ref,
                            o_ref, op_ref, route_ref, cnt_ref, base_ref, tri_ref):
    y = jnp.dot(x_ref[...], w_ref[...], preferred_element_type=F32)
    out = _layer_norm_rows(DN_ALPHA * h_ref[...] + y, g_ref[...], b_ref[...])
    o_ref[...] = out
    op_ref[...] = _pack_pairs(out)
    _route_tile(out, rw_ref, rb_ref, route_ref, cnt_ref, base_ref, tri_ref)


def _matmul_res_ln_route(x, w, h, g, b, w_r, b_r, *, tm):
    n, k = x.shape
    d = w.shape[1]
    rows = lambda width: pl.BlockSpec((tm, width), lambda i: (i, 0))
    const = lambda shape: pl.BlockSpec(shape, lambda i: (0, 0))
    return pl.pallas_call(
        _mm_res_ln_route_kernel,
        grid=(n // tm,),
        in_specs=[rows(k), const((k, d)), rows(d), const((1, d)), const((1, d)),
                  const((d, 2 * LANES)), const((1, LANES))],
        out_specs=[rows(d), rows(d // 2), rows(LANES), const((8, LANES))],
        out_shape=[jax.ShapeDtypeStruct((n, d), F32), jax.ShapeDtypeStruct((n, d // 2), F32),
                   jax.ShapeDtypeStruct((n, LANES), F32), jax.ShapeDtypeStruct((8, LANES), F32)],
        scratch_shapes=[pltpu.VMEM((1, LANES), F32), pltpu.VMEM((tm, tm), BF16)],
        compiler_params=_params("arbitrary"),
        name="matmul_res_ln_route",
    )(x, w, h, g.reshape(1, d), b.reshape(1, d), w_r, b_r)


def _dest_kernel(route_ref, ps_ref, i0_ref, i1_ref):
    tm = route_ref.shape[0]
    rt = route_ref[...]
    lane = lax.broadcasted_iota(jnp.int32, (tm, LANES), 1)
    row = lax.broadcasted_iota(jnp.int32, (tm, LANES), 0)
    own_lane = lane == row % SC_ROWS
    group = (lax.broadcasted_iota(jnp.int32, (tm // SC_ROWS, tm), 1) // SC_ROWS
             == lax.broadcasted_iota(jnp.int32, (tm // SC_ROWS, tm), 0)).astype(F32)
    ps = ps_ref[0:1, :]
    for slot, out_ref in ((0, i0_ref), (1, i1_ref)):
        e = rt[:, R_EID + slot:R_EID + slot + 1]
        r = rt[:, R_RANK + slot:R_RANK + slot + 1]
        dest = jnp.sum(jnp.where(lane.astype(F32) == e, ps, 0.0), -1, keepdims=True) + r
        spread = jnp.where(own_lane, dest, 0.0)
        out_ref[...] = jnp.dot(group, spread, preferred_element_type=F32,
                               precision=lax.Precision.HIGHEST).astype(jnp.int32)


def _dest_rows(route, pstarts, *, tm):
    n = route.shape[0]
    ps = jnp.zeros((8, LANES), F32).at[0, :MOE_EXPERTS].set(pstarts.astype(F32))
    idx = jax.ShapeDtypeStruct((n // SC_ROWS, SC_INDEX_LANES), jnp.int32)
    idx_spec = pl.BlockSpec((tm // SC_ROWS, SC_INDEX_LANES), lambda i: (i, 0))
    return pl.pallas_call(
        _dest_kernel,
        grid=(n // tm,),
        in_specs=[pl.BlockSpec((tm, LANES), lambda i: (i, 0)), pl.BlockSpec((8, LANES), lambda i: (0, 0))],
        out_specs=[idx_spec, idx_spec],
        out_shape=[idx, idx],
        compiler_params=_params("parallel"),
        name="moe_dest_rows",
    )(route, ps)


def _sc_mesh():
    return plsc.VectorSubcoreMesh(core_axis_name="core", subcore_axis_name="subcore")


def _sc_gather_rows(y, idx_rows):
    n = idx_rows.shape[0] * SC_ROWS
    d = y.shape[1]

    @pl.kernel(out_type=jax.ShapeDtypeStruct((n, d), y.dtype), mesh=_sc_mesh(), scratch_types=[])
    def gather(y_hbm, i_hbm, o_hbm):
        def body(i_vmem, o_vmem):
            pltpu.sync_copy(y_hbm.at[i_vmem.at[0, pl.ds(0, SC_ROWS)]], o_vmem)

        pltpu.emit_pipeline(
            body, grid=(n // SC_ROWS,),
            in_specs=[pl.BlockSpec((1, SC_INDEX_LANES), lambda i: (i, 0))],
            out_specs=[pl.BlockSpec((SC_ROWS, d), lambda i: (i, 0))],
            core_axis_name=("core", "subcore"), dimension_semantics=(pltpu.PARALLEL,),
        )(i_hbm, o_hbm)

    return gather(y, idx_rows)


def _sc_scatter_rows(x, idx0_rows, idx1_rows, p_rows):
    n, d = x.shape

    @pl.kernel(out_type=jax.ShapeDtypeStruct((p_rows, d), x.dtype), mesh=_sc_mesh(), scratch_types=[])
    def scatter(x_hbm, i0_hbm, i1_hbm, o_hbm):
        def body(x_vmem, i0_vmem, i1_vmem):
            pltpu.sync_copy(x_vmem, o_hbm.at[i0_vmem.at[0, pl.ds(0, SC_ROWS)]])
            pltpu.sync_copy(x_vmem, o_hbm.at[i1_vmem.at[0, pl.ds(0, SC_ROWS)]])

        pltpu.emit_pipeline(
            body, grid=(n // SC_ROWS,),
            in_specs=[pl.BlockSpec((SC_ROWS, d), lambda i: (i, 0)),
                      pl.BlockSpec((1, SC_INDEX_LANES), lambda i: (i, 0)),
                      pl.BlockSpec((1, SC_INDEX_LANES), lambda i: (i, 0))],
            out_specs=[],
            core_axis_name=("core", "subcore"), dimension_semantics=(pltpu.PARALLEL,),
        )(x_hbm, i0_hbm, i1_hbm)

    return scatter(x, idx0_rows, idx1_rows)


def _router_params(wg_r, bg_r, we_r, be_r):
    pad = LANES - MOE_GROUPS - MOE_EXPERTS
    w_r = jnp.pad(jnp.concatenate([wg_r, we_r], 1), ((0, 0), (0, pad)))
    w_hi = w_r.astype(BF16)
    w_lo = (w_r - w_hi.astype(F32)).astype(BF16)
    b_r = jnp.pad(jnp.concatenate([bg_r, be_r]), (0, pad)).reshape(1, LANES)
    return jnp.concatenate([w_hi, w_lo], 1), b_r


def _moe(h, hp, route, cnt, w_gate, w_up, w_down, layer, ln_g, ln_b):
    n, d = h.shape
    counts = cnt[0, :MOE_EXPERTS].astype(jnp.int32)
    padded = (counts + MOE_BLOCK - 1) // MOE_BLOCK * MOE_BLOCK
    pends = jnp.cumsum(padded)
    pstarts = pends - padded
    experts = jnp.arange(MOE_EXPERTS, dtype=jnp.int32)
    dest0, dest1 = _dest_rows(route, pstarts, tm=2 * TILE_TOKEN_ROWS)
    p_rows = n * MOE_TOPK + MOE_EXPERTS * MOE_BLOCK
    nb = p_rows // MOE_BLOCK
    blk_start = jnp.arange(nb, dtype=jnp.int32) * MOE_BLOCK
    blk_e = jnp.minimum(jnp.sum((pends[None, :] <= blk_start[:, None]).astype(jnp.int32), -1), MOE_EXPERTS - 1)
    real_ends = pstarts + counts
    blk_end = jnp.sum(jnp.where(blk_e[:, None] == experts[None, :], real_ends[None, :], 0), -1)
    blk_rows = jnp.clip(blk_end - blk_start, 0, MOE_BLOCK).astype(jnp.int32)
    xb = _sc_scatter_rows(hp, dest0, dest1, p_rows)
    yb = _experts(blk_e, blk_rows, xb, w_gate, w_up, w_down, layer)
    return _combine_ln(h, _sc_gather_rows(yb, dest0), _sc_gather_rows(yb, dest1), route, ln_g, ln_b,
                       tm=TILE_TOKEN_ROWS)


def kernel(x, positions, ln1_g, ln1_b, ln2_g, ln2_b, a_w_in, a_w_out, b_w_in, b_gate_bias, b_conv_w, b_conv_b,
           b_norm_g, b_w_out, c_w_in, c_norm_g, c_w_out, r_group_w, r_group_b, r_expert_w, r_expert_b,
           e_w_gate, e_w_up, e_w_down):
    batch, seq, d = x.shape
    n = batch * seq
    tabs_a = _rope_tables_a(positions)
    inv_c = C_THETA ** (-jnp.arange(0, C_QK_DIM, 2, dtype=F32) / C_QK_DIM)
    ang_c = positions.astype(F32)[:, None] * inv_c[None, :]
    cos_c, sin_c = jnp.cos(ang_c), jnp.sin(ang_c)
    h = x.reshape(n, d)
    hb = h
    for i in range(DEPTH):
        kind, j = i % 3, i // 3
        if kind == 0:
            y = _mixer_dilated(hb, a_w_in[j], tabs_a, batch, seq)
            w_out = a_w_out[j]
        elif kind == 1:
            y = _mixer_mlstm(hb, b_w_in[j], b_gate_bias[j], b_conv_w[j], b_conv_b[j], b_norm_g[j], batch, seq)
            w_out = b_w_out[j]
        else:
            y = _mixer_retention(hb, c_w_in[j], c_norm_g[j], cos_c, sin_c, batch, seq)
            w_out = c_w_out[j]
        w_r, b_r = _router_params(r_group_w[i], r_group_b[i], r_expert_w[i], r_expert_b[i])
        h, hp, route, cnt = _matmul_res_ln_route(y, w_out.astype(BF16), h, ln1_g[i], ln1_b[i], w_r, b_r,
                                                 tm=TILE_TOKEN_ROWS)
        h, hb = _moe(h, hp, route, cnt, e_w_gate, e_w_up, e_w_down, i, ln2_g[i], ln2_b[i])
    return h.reshape(batch, seq, d)
```

```python
import jax
import jax.numpy as jnp
from jax import lax
from jax.experimental import pallas as pl
from jax.experimental.pallas import tpu as pltpu
from jax.experimental.pallas import tpu_sc as plsc

F32 = jnp.float32
BF16 = jnp.bfloat16

D_MODEL = 1024
DEPTH = 4
DN_ALPHA = (2.0 * DEPTH) ** 0.25
LN_EPS = 1e-5

A_HEADS = 16
A_HEAD_DIM = 64
A_DILATIONS = (1, 4, 16)
A_BLOCK = 128
A_UNROLL = 32
A_GROUP = 16
A_PITCH = 20
LOG2_E = 1.4426950408889634
A_ROT_DIM = 16
ROPE_THETA = 500000.0

B_HEADS = 8
B_QK_DIM = 64
B_V_DIM = 128
B_CONV = 4
B_CHUNK = 256

C_HEADS = 4
C_QK_DIM = 256
C_V_DIM = 512
C_CHUNK = 256
C_THETA = 10000.0

MOE_GROUPS = 8
MOE_PER_GROUP = 8
MOE_EXPERTS = 64
MOE_TOPK = 2
MOE_HIDDEN = 256
MOE_BLOCK = 512

LANES = 128
SC_ROWS = 64
SC_INDEX_LANES = LANES
NEG = -1e30
V7X_VMEM_BYTES = 64 * 1024 * 1024
VMEM_LIMIT = V7X_VMEM_BYTES * 3 // 4

TILE_PROJ_A = 512
TILE_MLSTM_PROJ = (1024, 1536)
TILE_MLSTM_GATES = (2048, LANES)
TILE_RETENTION_PROJ = (1024, 2048)
TILE_TOKEN_ROWS = 512


def _params(*sem):
    return pltpu.CompilerParams(dimension_semantics=sem, vmem_limit_bytes=VMEM_LIMIT)


def _mm_kernel(x_ref, w_ref, o_ref):
    o_ref[...] = jnp.dot(x_ref[...], w_ref[...], preferred_element_type=F32).astype(o_ref.dtype)


def _matmul(x, w, *, tm, tn, out_dtype=F32):
    n, k = x.shape
    m = w.shape[1]
    return pl.pallas_call(
        _mm_kernel,
        grid=(n // tm, m // tn),
        in_specs=[pl.BlockSpec((tm, k), lambda i, j: (i, 0)),
                  pl.BlockSpec((k, tn), lambda i, j: (0, j))],
        out_specs=pl.BlockSpec((tm, tn), lambda i, j: (i, j)),
        out_shape=jax.ShapeDtypeStruct((n, m), out_dtype),
        compiler_params=_params("parallel", "parallel"),
        name="matmul",
    )(x, w)


def _layer_norm_rows(z, g, b):
    mu = jnp.mean(z, -1, keepdims=True)
    zc = z - mu
    var = jnp.mean(zc * zc, -1, keepdims=True)
    return zc * lax.rsqrt(var + LN_EPS) * g + b


def _pack_pairs(x):
    c = x.shape[1] // 2
    hi = pltpu.bitcast(x[:, :c].astype(BF16).astype(F32), jnp.uint32)
    lo = pltpu.bitcast(x[:, c:].astype(BF16).astype(F32), jnp.uint32)
    return pltpu.bitcast(hi | (lo >> 16), F32)


def _unpack_pairs(w):
    bits = pltpu.bitcast(w, jnp.uint32)
    hi = pltpu.bitcast(bits & jnp.uint32(0xFFFF0000), F32)
    lo = pltpu.bitcast(bits << 16, F32)
    return jnp.concatenate([hi, lo], axis=1)


def _proj_a_kernel(x_ref, w_ref, c_ref, s1_ref, s2_ref, q_ref, k_ref, v_ref):
    x = x_ref[...].astype(BF16)
    width = 2 * LANES
    for c, ref in ((0, q_ref), (1, k_ref), (2, v_ref)):
        for j in range(D_MODEL // width):
            col = c * D_MODEL + j * width
            y = jnp.dot(x, w_ref[:, col:col + width], preferred_element_type=F32)
            if c < 2:
                half = A_ROT_DIM // 2
                y = (y * c_ref[...] + pltpu.roll(y, width - half, 1) * s1_ref[...]
                     + pltpu.roll(y, half, 1) * s2_ref[...])
            if c == 0:
                y = y * (LOG2_E * A_HEAD_DIM ** -0.5)
            ref[0, 2 * j] = y[:, :LANES]
            ref[0, 2 * j + 1] = y[:, LANES:]


def _proj_a(xb, w, tabs, batch, seq, *, tm):
    n, d = xb.shape
    spb = seq // tm
    hp = D_MODEL // LANES
    qkv_shape = jax.ShapeDtypeStruct((batch, hp, seq, LANES), F32)
    out_spec = pl.BlockSpec((1, hp, tm, LANES), lambda i: (i // spb, 0, i % spb, 0))
    tab_spec = pl.BlockSpec((tm, 2 * LANES), lambda i: (i % spb, 0))
    return pl.pallas_call(
        _proj_a_kernel,
        grid=(n // tm,),
        in_specs=[pl.BlockSpec((tm, d), lambda i: (i, 0)),
                  pl.BlockSpec((d, 3 * d), lambda i: (0, 0)),
                  tab_spec, tab_spec, tab_spec],
        out_specs=[out_spec, out_spec, out_spec],
        out_shape=[qkv_shape, qkv_shape, qkv_shape],
        compiler_params=_params("parallel"),
        name="proj_a",
    )(xb, w, *tabs)


def _rope_tables_a(positions):
    half = A_ROT_DIM // 2
    inv = ROPE_THETA ** (-jnp.arange(0, A_ROT_DIM, 2, dtype=F32) / A_ROT_DIM)
    ang = positions.astype(F32)[:, None] * inv[None, :]
    cos, sin = jnp.cos(ang), jnp.sin(ang)
    s = positions.shape[0]
    pad = jnp.zeros((s, A_HEAD_DIM - A_ROT_DIM), F32)
    c_head = jnp.concatenate([cos, cos, pad + 1.0], -1)
    s1_head = jnp.concatenate([-sin, jnp.zeros_like(sin), pad], -1)
    s2_head = jnp.concatenate([jnp.zeros_like(sin), sin, pad], -1)
    reps = 2 * LANES // A_HEAD_DIM
    return tuple(jnp.tile(t, (1, reps)) for t in (c_head, s1_head, s2_head))


def _attn_blocks(q_ref, k_ref, v_ref, o_ref, lse_ref, bias_ref, hmask, head0, first_block, d, pitch):
    nk = 2 * A_BLOCK
    loaded = []
    for u in range(A_UNROLL):
        g = first_block + u
        r = g % d
        n = g // d
        qstart = n * (A_BLOCK * pitch) + r
        kstart = jnp.maximum(qstart - A_BLOCK * pitch, r)
        if pitch == 1:
            qstart = pl.multiple_of(qstart, A_BLOCK)
            kstart = pl.multiple_of(kstart, A_BLOCK)
            qsl, ksl = pl.ds(qstart, A_BLOCK), pl.ds(kstart, nk)
        else:
            qsl, ksl = pl.ds(qstart, A_BLOCK, stride=pitch), pl.ds(kstart, nk, stride=pitch)
        bias = bias_ref[jnp.minimum(n, 1)]
        loaded.append((qsl, q_ref[qsl, :].astype(BF16), k_ref[ksl, :].astype(BF16), v_ref[ksl, :].astype(BF16), bias))
    results = []
    for qsl, qb, kb, vb, bias in loaded:
        pvs, ms, ls = [], [], []
        for h in range(2):
            s = lax.dot_general(qb * hmask[h], kb, (((1,), (1,)), ((), ())), preferred_element_type=F32) + bias
            m = jnp.max(s, -1, keepdims=True)
            p = jnp.exp2(s - m)
            ms.append(jnp.broadcast_to(m, (A_BLOCK, LANES)))
            ls.append(jnp.broadcast_to(jnp.sum(p, -1, keepdims=True), (A_BLOCK, LANES)))
            pvs.append(jnp.dot(p.astype(BF16), vb, preferred_element_type=F32))
        l = jnp.where(head0, ls[0], ls[1])
        out = jnp.where(head0, pvs[0], pvs[1]) * (1.0 / l)
        results.append((qsl, out, jnp.where(head0, ms[0], ms[1]) + jnp.log2(l)))
    for qsl, out, lse in results:
        o_ref[qsl, :] = out
        lse_ref[qsl, :] = lse


def _attn_kernel(q_ref, k_ref, v_ref, o_ref, q16_ref, k16_ref, v16_ref, ob_ref, lb_ref, o16_ref, l16_ref,
                 bias_ref):
    seq = q_ref.shape[0]
    groups = seq // A_GROUP
    head0 = lax.broadcasted_iota(jnp.int32, (A_BLOCK, LANES), 1) < A_HEAD_DIM
    hmask = [jnp.where(head0, 1.0, 0.0).astype(BF16), jnp.where(head0, 0.0, 1.0).astype(BF16)]
    qi = lax.broadcasted_iota(jnp.int32, (A_BLOCK, 2 * A_BLOCK), 0)
    kj = lax.broadcasted_iota(jnp.int32, (A_BLOCK, 2 * A_BLOCK), 1)
    bias_ref[0] = jnp.where(kj <= qi, 0.0, NEG).astype(F32)
    bias_ref[1] = jnp.where((kj >= qi) & (kj <= qi + A_BLOCK), 0.0, NEG).astype(F32)

    def spread(g, carry):
        src = pl.ds(pl.multiple_of(g * A_GROUP, A_GROUP), A_GROUP)
        dst = pl.ds(pl.multiple_of(g * A_PITCH, 4), A_GROUP)
        q16_ref[dst, :] = q_ref[src, :]
        k16_ref[dst, :] = k_ref[src, :]
        v16_ref[dst, :] = v_ref[src, :]
        return carry

    lax.fori_loop(0, groups, spread, 0, unroll=8)

    branches = ((1, 1, q_ref, k_ref, v_ref, ob_ref.at[0], lb_ref.at[0]),
                (4, 4, q_ref, k_ref, v_ref, ob_ref.at[1], lb_ref.at[1]),
                (16, A_PITCH, q16_ref, k16_ref, v16_ref, o16_ref, l16_ref))
    zero = jnp.minimum(pl.program_id(0), 0)
    for d, pitch, qr, kr, vr, orf, lrf in branches:

        def body(it, carry, d=d, pitch=pitch, qr=qr, kr=kr, vr=vr, orf=orf, lrf=lrf):
            _attn_blocks(qr, kr, vr, orf, lrf, bias_ref, hmask, head0, it * A_UNROLL, d, pitch)
            return carry

        lax.fori_loop(zero, zero + seq // (A_BLOCK * A_UNROLL), body, 0)

    def mix(g, carry):
        nat = pl.ds(pl.multiple_of(g * A_GROUP, A_GROUP), A_GROUP)
        pad = pl.ds(pl.multiple_of(g * A_PITCH, 4), A_GROUP)
        o0, o1, o2 = ob_ref[0, nat, :], ob_ref[1, nat, :], o16_ref[pad, :]
        l0, l1, l2 = lb_ref[0, nat, :], lb_ref[1, nat, :], l16_ref[pad, :]
        mx = jnp.maximum(jnp.maximum(l0, l1), l2)
        w0, w1, w2 = jnp.exp2(l0 - mx), jnp.exp2(l1 - mx), jnp.exp2(l2 - mx)
        o_ref[nat, :] = ((w0 * o0 + w1 * o1 + w2 * o2) / (w0 + w1 + w2)).astype(o_ref.dtype)
        return carry

    lax.fori_loop(0, groups, mix, 0, unroll=8)


def _attention(q, k, v):
    batch, hp, seq, _ = q.shape
    assert seq % (2 * A_BLOCK * max(A_DILATIONS)) == 0 and seq % (A_BLOCK * A_UNROLL) == 0
    in_spec = pl.BlockSpec((None, None, seq, LANES), lambda b, p: (b, p, 0, 0))
    padded = seq // A_GROUP * A_PITCH
    return pl.pallas_call(
        _attn_kernel,
        grid=(batch, hp),
        in_specs=[in_spec, in_spec, in_spec],
        out_specs=pl.BlockSpec((None, seq, LANES), lambda b, p: (b, 0, p)),
        out_shape=jax.ShapeDtypeStruct((batch, seq, hp * LANES), BF16),
        scratch_shapes=[pltpu.VMEM((padded, LANES), F32)] * 3
        + [pltpu.VMEM((2, seq, LANES), F32)] * 2
        + [pltpu.VMEM((padded, LANES), F32)] * 2
        + [pltpu.VMEM((2, A_BLOCK, 2 * A_BLOCK), F32)],
        compiler_params=_params("parallel", "parallel"),
        name="dilated_attention",
    )(q, k, v)


def _mixer_dilated(hb, w_in, tabs, batch, seq):
    q, k, v = _proj_a(hb, w_in.astype(BF16), tabs, batch, seq, tm=TILE_PROJ_A)
    o = _attention(q, k, v)
    return o.reshape(batch * seq, D_MODEL)


def _mlstm_kernel(qk_ref, v_ref, o_ref, gc_ref, gb_ref, cw_ref, cb_ref, ng_ref, out_ref,
                  ext_ref, c_ref, n_ref, m_ref):
    L = B_CHUNK
    chunk = pl.program_id(1)

    @pl.when(chunk == 0)
    def _():
        ext_ref[0:8, :] = jnp.zeros((8, D_MODEL), F32)
        c_ref[...] = jnp.zeros(c_ref.shape, F32)
        n_ref[...] = jnp.zeros(n_ref.shape, F32)
        m_ref[...] = jnp.zeros(m_ref.shape, F32)

    u = qk_ref[...].astype(F32)
    ext_ref[8:8 + L, :] = u
    conv = u * cw_ref[B_CONV - 1:B_CONV, :] + cb_ref[...]
    for j in range(1, B_CONV):
        conv = conv + ext_ref[pl.ds(8 - j, L), :] * cw_ref[B_CONV - 1 - j:B_CONV - j, :]
    ext_ref[0:8, :] = u[L - 8:, :]
    qk = conv * jax.nn.sigmoid(conv)
    half = D_MODEL // 2

    gc = gc_ref[...] + gb_ref[...]
    gr = gc.T
    i_col, i_row = gc, gr[:B_HEADS, :]
    lf_col = jax.nn.log_sigmoid(gc)
    lf_row = jax.nn.log_sigmoid(gr[B_HEADS:2 * B_HEADS, :])
    ti = lax.broadcasted_iota(jnp.int32, (L, L), 0)
    si = lax.broadcasted_iota(jnp.int32, (L, L), 1)
    causal = ti >= si
    tri = causal.astype(F32)
    a_col = jnp.dot(tri, lf_col, preferred_element_type=F32, precision=lax.Precision.HIGHEST)
    a_row = lax.dot_general(lf_row, tri, (((1,), (1,)), ((), ())), preferred_element_type=F32,
                            precision=lax.Precision.HIGHEST)
    lane = lax.broadcasted_iota(jnp.int32, (1, LANES), 1)
    lane_h0 = lane < B_QK_DIM
    col_h0 = lax.broadcasted_iota(jnp.int32, (1, 2 * B_V_DIM), 1) < B_V_DIM

    for p in range(B_HEADS // 2):
        qp = qk[:, p * LANES:(p + 1) * LANES]
        kp = qk[:, half + p * LANES:half + (p + 1) * LANES] * (B_QK_DIM ** -0.5)
        vp = v_ref[:, p * 2 * B_V_DIM:(p + 1) * 2 * B_V_DIM]
        kpb = kp.astype(BF16)
        vpb = vp.astype(BF16)
        c_old = c_ref[p]
        c_oldb = c_old.astype(BF16)
        n_old = n_ref[p]
        m_pair = m_ref[p]
        ws_cols, decays, m_news = [], [], []
        for hh in range(2):
            h = 2 * p + hh
            m_old = m_pair[:, hh * B_QK_DIM:hh * B_QK_DIM + 1]
            ac, ar = a_col[:, B_HEADS + h:B_HEADS + h + 1], a_row[h:h + 1, :]
            ic, ir = i_col[:, h:h + 1], i_row[h:h + 1, :]
            dmat = jnp.where(causal, ac + (ir - ar), NEG)
            inter = ac + m_old
            m_t = jnp.maximum(inter, jnp.max(dmat, -1, keepdims=True))
            qm = jnp.where(lane_h0 if hh == 0 else jnp.logical_not(lane_h0), qp, 0.0)
            qmb = qm.astype(BF16)
            sc = lax.dot_general(qmb, kpb, (((1,), (1,)), ((), ())), preferred_element_type=F32)
            sc = sc * jnp.exp(dmat - m_t)
            g_inter = jnp.exp(inter - m_t)
            vh = vpb[:, hh * B_V_DIM:(hh + 1) * B_V_DIM]
            qc = jnp.dot(qmb, c_oldb, preferred_element_type=F32)[:, hh * B_V_DIM:(hh + 1) * B_V_DIM]
            num = jnp.dot(sc.astype(BF16), vh, preferred_element_type=F32) + g_inter * qc
            den = jnp.sum(sc, -1, keepdims=True) + g_inter * jnp.sum(qm * n_old, -1, keepdims=True)
            h_out = num / jnp.maximum(jnp.abs(den), jnp.exp(-m_t))
            mu = jnp.mean(h_out, -1, keepdims=True)
            hc = h_out - mu
            var = jnp.mean(hc * hc, -1, keepdims=True)
            cols = slice(h * B_V_DIM, (h + 1) * B_V_DIM)
            hn = hc * lax.rsqrt(var + LN_EPS) * ng_ref[:, cols]
            out_ref[:, cols] = (hn * jax.nn.sigmoid(o_ref[:, cols].astype(F32))).astype(out_ref.dtype)
            a_end = ac[L - 1:L, :]
            w_col = a_end - ac + ic
            m_new = jnp.maximum(a_end + m_old, jnp.max(w_col, 0, keepdims=True))
            decays.append(jnp.exp(a_end + m_old - m_new))
            ws_cols.append(jnp.exp(w_col - m_new))
            m_news.append(m_new)
        ws = jnp.where(lane_h0, jnp.broadcast_to(ws_cols[0], (L, LANES)), jnp.broadcast_to(ws_cols[1], (L, LANES)))
        kw = kp * ws
        dec_c = jnp.where(col_h0, jnp.broadcast_to(decays[0], (1, 2 * B_V_DIM)),
                          jnp.broadcast_to(decays[1], (1, 2 * B_V_DIM)))
        dec_n = jnp.where(lane_h0, jnp.broadcast_to(decays[0], (1, LANES)), jnp.broadcast_to(decays[1], (1, LANES)))
        c_ref[p] = dec_c * c_old + lax.dot_general(kw.astype(BF16), vpb, (((0,), (0,)), ((), ())),
                                                   preferred_element_type=F32)
        n_ref[p] = dec_n * n_old + jnp.sum(kw, 0, keepdims=True)
        m_ref[p] = jnp.where(lane_h0, jnp.broadcast_to(m_news[0], (1, LANES)), jnp.broadcast_to(m_news[1], (1, LANES)))


def _mlstm(proj, gates, gate_bias, conv_w, conv_b, norm_g, batch, seq):
    L = B_CHUNK
    d = D_MODEL
    slab = lambda c: pl.BlockSpec((None, L, d), lambda b, s, c=c: (b, s, c))
    full = lambda shape: pl.BlockSpec(shape, lambda b, s: (0,) * len(shape))
    return pl.pallas_call(
        _mlstm_kernel,
        grid=(batch, seq // L),
        in_specs=[slab(0), slab(1), slab(2),
                  pl.BlockSpec((None, L, LANES), lambda b, s: (b, s, 0)),
                  full((1, LANES)),
                  full((B_CONV, d)), full((1, d)), full((1, d))],
        out_specs=pl.BlockSpec((None, L, d), lambda b, s: (b, s, 0)),
        out_shape=jax.ShapeDtypeStruct((batch, seq, d), BF16),
        scratch_shapes=[pltpu.VMEM((L + 8, d), F32),
                        pltpu.VMEM((B_HEADS // 2, 2 * B_QK_DIM, 2 * B_V_DIM), F32),
                        pltpu.VMEM((B_HEADS // 2, 1, LANES), F32),
                        pltpu.VMEM((B_HEADS // 2, 1, LANES), F32)],
        compiler_params=_params("parallel", "arbitrary"),
        name="mlstm",
    )(proj, proj, proj, gates, jnp.pad(gate_bias, (0, LANES - 2 * B_HEADS)).reshape(1, LANES),
      conv_w, conv_b.reshape(1, d), norm_g.reshape(1, d))


def _mixer_mlstm(hb, w_in, gate_bias, conv_w, conv_b, norm_g, batch, seq):
    n = batch * seq
    main = 3 * D_MODEL
    tm, tn = TILE_MLSTM_PROJ
    proj = _matmul(hb, w_in[:, :main].astype(BF16), tm=tm, tn=tn, out_dtype=BF16)
    w_g = jnp.pad(w_in[:, main:], ((0, 0), (0, LANES - 2 * B_HEADS))).astype(BF16)
    gates = _matmul(hb, w_g, tm=TILE_MLSTM_GATES[0], tn=TILE_MLSTM_GATES[1])
    out = _mlstm(proj.reshape(batch, seq, main), gates.reshape(batch, seq, LANES), gate_bias,
                 conv_w, conv_b, norm_g, batch, seq)
    return out.reshape(n, D_MODEL)


def _retention_kernel(lg_ref, q_ref, k_ref, v_ref, g_ref, cos_ref, sin_ref, ng_ref, out_ref,
                      r_ref, dm_ref, xi_ref, zeta_ref):
    L = C_CHUNK
    chunk = pl.program_id(1)

    @pl.when(chunk == 0)
    def _():
        r_ref[...] = jnp.zeros(r_ref.shape, F32)
        ti = lax.broadcasted_iota(jnp.int32, (L, L), 0)
        si = lax.broadcasted_iota(jnp.int32, (L, L), 1)
        rel = (ti - si).astype(F32)
        idx = lax.broadcasted_iota(jnp.int32, (L, LANES), 0).astype(F32)
        for h in range(C_HEADS):
            lg = lg_ref[h]
            dm_ref[h] = jnp.where(rel >= 0, jnp.exp(jnp.maximum(rel, 0.0) * lg), 0.0)
            xi_ref[h] = jnp.exp((idx + 1.0) * lg)
            zeta_ref[h] = jnp.exp((L - 1.0 - idx) * lg)

    cos, sin = cos_ref[...], sin_ref[...]
    hd = C_QK_DIM // 2

    def rope(t):
        t1, t2 = t[:, :hd], t[:, hd:]
        return jnp.concatenate([t1 * cos - t2 * sin, t2 * cos + t1 * sin], -1)

    for h in range(C_HEADS):
        qk_cols = slice(h * C_QK_DIM, (h + 1) * C_QK_DIM)
        v_cols = slice(h * C_V_DIM, (h + 1) * C_V_DIM)
        q = rope(q_ref[:, qk_cols].astype(F32))
        k = rope(k_ref[:, qk_cols].astype(F32)) * (C_QK_DIM ** -0.5)
        qb = q.astype(BF16)
        vb = v_ref[:, v_cols]
        r_old = r_ref[h]
        sc = lax.dot_general(qb, k.astype(BF16), (((1,), (1,)), ((), ())), preferred_element_type=F32) * dm_ref[h]
        o = jnp.dot(sc.astype(BF16), vb, preferred_element_type=F32)
        o = o + xi_ref[h, :, 0:1] * jnp.dot(qb, r_old.astype(BF16), preferred_element_type=F32)
        kz = (k * zeta_ref[h, :, 0:1]).astype(BF16)
        cd = jnp.exp(jnp.full((1, 1), float(L), F32) * lg_ref[h])
        r_ref[h] = cd * r_old + lax.dot_general(kz, vb, (((0,), (0,)), ((), ())), preferred_element_type=F32)
        mu = jnp.mean(o, -1, keepdims=True)
        oc = o - mu
        var = jnp.mean(oc * oc, -1, keepdims=True)
        on = oc * lax.rsqrt(var + LN_EPS) * ng_ref[:, v_cols]
        g = g_ref[:, v_cols].astype(F32)
        out_ref[:, v_cols] = (on * (g * jax.nn.sigmoid(g))).astype(out_ref.dtype)


def _retention(proj, cos, sin, norm_g, batch, seq):
    L = C_CHUNK
    d = D_MODEL
    log_gamma = jnp.log(1.0 - 2.0 ** (-5.0 - jnp.arange(C_HEADS, dtype=F32)))
    return pl.pallas_call(
        _retention_kernel,
        grid=(batch, seq // L),
        in_specs=[pl.BlockSpec(memory_space=pltpu.SMEM),
                  pl.BlockSpec((None, L, d), lambda b, c: (b, c, 0)),
                  pl.BlockSpec((None, L, d), lambda b, c: (b, c, 1)),
                  pl.BlockSpec((None, L, 2 * d), lambda b, c: (b, c, 1)),
                  pl.BlockSpec((None, L, 2 * d), lambda b, c: (b, c, 2)),
                  pl.BlockSpec((L, C_QK_DIM // 2), lambda b, c: (c, 0)),
                  pl.BlockSpec((L, C_QK_DIM // 2), lambda b, c: (c, 0)),
                  pl.BlockSpec((1, 2 * d), lambda b, c: (0, 0))],
        out_specs=pl.BlockSpec((None, L, 2 * d), lambda b, c: (b, c, 0)),
        out_shape=jax.ShapeDtypeStruct((batch, seq, 2 * d), BF16),
        scratch_shapes=[pltpu.VMEM((C_HEADS, C_QK_DIM, C_V_DIM), F32),
                        pltpu.VMEM((C_HEADS, L, L), F32),
                        pltpu.VMEM((C_HEADS, L, LANES), F32),
                        pltpu.VMEM((C_HEADS, L, LANES), F32)],
        compiler_params=_params("parallel", "arbitrary"),
        name="retention",
    )(log_gamma, proj, proj, proj, proj, cos, sin, norm_g.reshape(1, 2 * d))


def _mixer_retention(hb, w_in, norm_g, cos, sin, batch, seq):
    tm, tn = TILE_RETENTION_PROJ
    proj = _matmul(hb, w_in.astype(BF16), tm=tm, tn=tn, out_dtype=BF16)
    out = _retention(proj.reshape(batch, seq, 6 * D_MODEL), cos, sin, norm_g, batch, seq)
    return out.reshape(batch * seq, 2 * D_MODEL)


def _expert_kernel(be_ref, bv_ref, x_ref, wg_ref, wu_ref, wd_ref, y_ref, wgb_ref, wub_ref, wdb_ref):
    j = pl.program_id(0)
    changed = jnp.logical_or(j == 0, be_ref[j] != be_ref[jnp.maximum(j - 1, 0)])

    @pl.when(changed)
    def _():
        wgb_ref[...] = wg_ref[...].astype(BF16)
        wub_ref[...] = wu_ref[...].astype(BF16)
        wdb_ref[...] = wd_ref[...].astype(BF16)

    real = lax.broadcasted_iota(jnp.int32, (x_ref.shape[0], 1), 0) < bv_ref[j]
    x = _unpack_pairs(jnp.where(real, x_ref[...], 0.0)).astype(BF16)
    a = jnp.dot(x, wgb_ref[...], preferred_element_type=F32)
    u = jnp.dot(x, wub_ref[...], preferred_element_type=F32)
    act = (a * jax.nn.sigmoid(a) * u).astype(BF16)
    y_ref[...] = _pack_pairs(jnp.dot(act, wdb_ref[...], preferred_element_type=F32))


def _experts(blk_e, blk_rows, xb, w_gate, w_up, w_down, layer):
    p, dw = xb.shape
    d = 2 * dw
    nb = p // MOE_BLOCK
    hid = MOE_HIDDEN
    grid_spec = pltpu.PrefetchScalarGridSpec(
        num_scalar_prefetch=2,
        grid=(nb,),
        in_specs=[pl.BlockSpec((MOE_BLOCK, dw), lambda j, be, bv: (j, 0)),
                  pl.BlockSpec((None, None, d, hid), lambda j, be, bv: (layer, be[j], 0, 0)),
                  pl.BlockSpec((None, None, d, hid), lambda j, be, bv: (layer, be[j], 0, 0)),
                  pl.BlockSpec((None, None, hid, d), lambda j, be, bv: (layer, be[j], 0, 0))],
        out_specs=pl.BlockSpec((MOE_BLOCK, dw), lambda j, be, bv: (j, 0)),
        scratch_shapes=[pltpu.VMEM((d, hid), BF16), pltpu.VMEM((d, hid), BF16), pltpu.VMEM((hid, d), BF16)],
    )
    return pl.pallas_call(
        _expert_kernel,
        grid_spec=grid_spec,
        out_shape=jax.ShapeDtypeStruct((p, dw), F32),
        compiler_params=_params("arbitrary"),
        name="moe_experts",
    )(blk_e, blk_rows, xb, w_gate, w_up, w_down)


def _combine_ln_kernel(h_ref, y0_ref, y1_ref, rt_ref, g_ref, b_ref, o_ref, ob_ref):
    rt = rt_ref[...]
    y = (_unpack_pairs(y0_ref[...]) * rt[:, R_GATE:R_GATE + 1]
         + _unpack_pairs(y1_ref[...]) * rt[:, R_GATE + 1:R_GATE + 2])
    out = _layer_norm_rows(DN_ALPHA * h_ref[...] + y, g_ref[...], b_ref[...])
    o_ref[...] = out
    ob_ref[...] = out.astype(BF16)


def _combine_ln(h, y0, y1, route, g, b, *, tm):
    n, d = h.shape
    row = pl.BlockSpec((tm, d), lambda i: (i, 0))
    words = pl.BlockSpec((tm, d // 2), lambda i: (i, 0))
    vec = pl.BlockSpec((1, d), lambda i: (0, 0))
    return pl.pallas_call(
        _combine_ln_kernel,
        grid=(n // tm,),
        in_specs=[row, words, words, pl.BlockSpec((tm, LANES), lambda i: (i, 0)), vec, vec],
        out_specs=[row, row],
        out_shape=[jax.ShapeDtypeStruct((n, d), F32), jax.ShapeDtypeStruct((n, d), BF16)],
        compiler_params=_params("parallel"),
        name="moe_combine_ln",
    )(h, y0, y1, route, g.reshape(1, d), b.reshape(1, d))


R_EID, R_GATE, R_RANK = 0, 2, 4


def _route_tile(h, w_ref, b_ref, route_ref, cnt_ref, base_ref, tri_ref):
    i = pl.program_id(0)
    tm = h.shape[0]

    @pl.when(i == 0)
    def _():
        base_ref[...] = jnp.zeros(base_ref.shape, F32)
        ti = lax.broadcasted_iota(jnp.int32, (tm, tm), 0)
        si = lax.broadcasted_iota(jnp.int32, (tm, tm), 1)
        tri_ref[...] = jnp.where(si < ti, 1.0, 0.0).astype(BF16)

    h_hi = h.astype(BF16)
    h_lo = (h - h_hi.astype(F32)).astype(BF16)
    hh = jnp.dot(h_hi, w_ref[...], preferred_element_type=F32)
    logits = (hh[:, :LANES] + hh[:, LANES:]
              + jnp.dot(h_lo, w_ref[:, :LANES], preferred_element_type=F32)) + b_ref[...]
    lane = lax.broadcasted_iota(jnp.int32, (tm, LANES), 1).astype(F32)
    neg_inf = -jnp.inf
    big = float(4 * LANES)
    is_g = lane < MOE_GROUPS
    gl = jnp.where(is_g, logits, neg_inf)
    gmax = jnp.max(gl, -1, keepdims=True)
    grp = jnp.min(jnp.where(gl == gmax, lane, big), -1, keepdims=True)
    p_grp = 1.0 / jnp.sum(jnp.where(is_g, jnp.exp(logits - gmax), 0.0), -1, keepdims=True)
    lo = MOE_GROUPS + MOE_PER_GROUP * grp
    el = jnp.where((lane >= lo) & (lane < lo + MOE_PER_GROUP), logits, neg_inf)
    v1 = jnp.max(el, -1, keepdims=True)
    i1 = jnp.min(jnp.where(el == v1, lane, big), -1, keepdims=True)
    el2 = jnp.where(lane == i1, neg_inf, el)
    v2 = jnp.max(el2, -1, keepdims=True)
    i2 = jnp.min(jnp.where(el2 == v2, lane, big), -1, keepdims=True)
    t = jnp.exp(v2 - v1)
    g1 = p_grp / (1.0 + t)
    g2 = g1 * t
    e1 = i1 - MOE_GROUPS
    e2 = i2 - MOE_GROUPS
    oh1 = jnp.where(lane == e1, 1.0, 0.0)
    oh2 = jnp.where(lane == e2, 1.0, 0.0)
    oh = oh1 + oh2
    tot = base_ref[...] + jnp.dot(tri_ref[...], oh.astype(BF16), preferred_element_type=F32)
    r1 = jnp.sum(oh1 * tot, -1, keepdims=True)
    r2 = jnp.sum(oh2 * tot, -1, keepdims=True)
    new_base = base_ref[...] + jnp.sum(oh, 0, keepdims=True)
    base_ref[...] = new_base
    cnt_ref[...] = jnp.broadcast_to(new_base, cnt_ref.shape)
    route = jnp.zeros((tm, LANES), F32)
    for k, val in enumerate((e1, e2, g1, g2, r1, r2)):
        route = jnp.where(lane == float(k), val, route)
    route_ref[...] = route


def _mm_res_ln_route_kernel(x_ref, w_ref, h_ref, g_ref, b_ref, rw_ref, rb_ref,
                            o_ref, op_ref, route_ref, cnt_ref, base_ref, tri_ref):
    y = jnp.dot(x_ref[...], w_ref[...], preferred_element_type=F32)
    out = _layer_norm_rows(DN_ALPHA * h_ref[...] + y, g_ref[...], b_ref[...])
    o_ref[...] = out
    op_ref[...] = _pack_pairs(out)
    _route_tile(out, rw_ref, rb_ref, route_ref, cnt_ref, base_ref, tri_ref)


def _matmul_res_ln_route(x, w, h, g, b, w_r, b_r, *, tm):
    n, k = x.shape
    d = w.shape[1]
    rows = lambda width: pl.BlockSpec((tm, width), lambda i: (i, 0))
    const = lambda shape: pl.BlockSpec(shape, lambda i: (0, 0))
    return pl.pallas_call(
        _mm_res_ln_route_kernel,
        grid=(n // tm,),
        in_specs=[rows(k), const((k, d)), rows(d), const((1, d)), const((1, d)),
                  const((d, 2 * LANES)), const((1, LANES))],
        out_specs=[rows(d), rows(d // 2), rows(LANES), const((8, LANES))],
        out_shape=[jax.ShapeDtypeStruct((n, d), F32), jax.ShapeDtypeStruct((n, d // 2), F32),
                   jax.ShapeDtypeStruct((n, LANES), F32), jax.ShapeDtypeStruct((8, LANES), F32)],
        scratch_shapes=[pltpu.VMEM((1, LANES), F32), pltpu.VMEM((tm, tm), BF16)],
        compiler_params=_params("arbitrary"),
        name="matmul_res_ln_route",
    )(x, w, h, g.reshape(1, d), b.reshape(1, d), w_r, b_r)


def _dest_kernel(route_ref, ps_ref, i0_ref, i1_ref):
    tm = route_ref.shape[0]
    rt = route_ref[...]
    lane = lax.broadcasted_iota(jnp.int32, (tm, LANES), 1)
    row = lax.broadcasted_iota(jnp.int32, (tm, LANES), 0)
    own_lane = lane == row % SC_ROWS
    group = (lax.broadcasted_iota(jnp.int32, (tm // SC_ROWS, tm), 1) // SC_ROWS
             == lax.broadcasted_iota(jnp.int32, (tm // SC_ROWS, tm), 0)).astype(F32)
    ps = ps_ref[0:1, :]
    for slot, out_ref in ((0, i0_ref), (1, i1_ref)):
        e = rt[:, R_EID + slot:R_EID + slot + 1]
        r = rt[:, R_RANK + slot:R_RANK + slot + 1]
        dest = jnp.sum(jnp.where(lane.astype(F32) == e, ps, 0.0), -1, keepdims=True) + r
        spread = jnp.where(own_lane, dest, 0.0)
        out_ref[...] = jnp.dot(group, spread, preferred_element_type=F32,
                               precision=lax.Precision.HIGHEST).astype(jnp.int32)


def _dest_rows(route, pstarts, *, tm):
    n = route.shape[0]
    ps = jnp.zeros((8, LANES), F32).at[0, :MOE_EXPERTS].set(pstarts.astype(F32))
    idx = jax.ShapeDtypeStruct((n // SC_ROWS, SC_INDEX_LANES), jnp.int32)
    idx_spec = pl.BlockSpec((tm // SC_ROWS, SC_INDEX_LANES), lambda i: (i, 0))
    return pl.pallas_call(
        _dest_kernel,
        grid=(n // tm,),
        in_specs=[pl.BlockSpec((tm, LANES), lambda i: (i, 0)), pl.BlockSpec((8, LANES), lambda i: (0, 0))],
        out_specs=[idx_spec, idx_spec],
        out_shape=[idx, idx],
        compiler_params=_params("parallel"),
        name="moe_dest_rows",
    )(route, ps)


def _sc_mesh():
    return plsc.VectorSubcoreMesh(core_axis_name="core", subcore_axis_name="subcore")


def _sc_gather_rows(y, idx_rows):
    n = idx_rows.shape[0] * SC_ROWS
    d = y.shape[1]

    @pl.kernel(out_type=jax.ShapeDtypeStruct((n, d), y.dtype), mesh=_sc_mesh(), scratch_types=[])
    def gather(y_hbm, i_hbm, o_hbm):
        def body(i_vmem, o_vmem):
            pltpu.sync_copy(y_hbm.at[i_vmem.at[0, pl.ds(0, SC_ROWS)]], o_vmem)

        pltpu.emit_pipeline(
            body, grid=(n // SC_ROWS,),
            in_specs=[pl.BlockSpec((1, SC_INDEX_LANES), lambda i: (i, 0))],
            out_specs=[pl.BlockSpec((SC_ROWS, d), lambda i: (i, 0))],
            core_axis_name=("core", "subcore"), dimension_semantics=(pltpu.PARALLEL,),
        )(i_hbm, o_hbm)

    return gather(y, idx_rows)


def _sc_scatter_rows(x, idx0_rows, idx1_rows, p_rows):
    n, d = x.shape

    @pl.kernel(out_type=jax.ShapeDtypeStruct((p_rows, d), x.dtype), mesh=_sc_mesh(), scratch_types=[])
    def scatter(x_hbm, i0_hbm, i1_hbm, o_hbm):
        def body(x_vmem, i0_vmem, i1_vmem):
            pltpu.sync_copy(x_vmem, o_hbm.at[i0_vmem.at[0, pl.ds(0, SC_ROWS)]])
            pltpu.sync_copy(x_vmem, o_hbm.at[i1_vmem.at[0, pl.ds(0, SC_ROWS)]])

        pltpu.emit_pipeline(
            body, grid=(n // SC_ROWS,),
            in_specs=[pl.BlockSpec((SC_ROWS, d), lambda i: (i, 0)),
                      pl.BlockSpec((1, SC_INDEX_LANES), lambda i: (i, 0)),
                      pl.BlockSpec((1, SC_INDEX_LANES), lambda i: (i, 0))],
            out_specs=[],
            core_axis_name=("core", "subcore"), dimension_semantics=(pltpu.PARALLEL,),
        )(x_hbm, i0_hbm, i1_hbm)

    return scatter(x, idx0_rows, idx1_rows)


def _router_params(wg_r, bg_r, we_r, be_r):
    pad = LANES - MOE_GROUPS - MOE_EXPERTS
    w_r = jnp.pad(jnp.concatenate([wg_r, we_r], 1), ((0, 0), (0, pad)))
    w_hi = w_r.astype(BF16)
    w_lo = (w_r - w_hi.astype(F32)).astype(BF16)
    b_r = jnp.pad(jnp.concatenate([bg_r, be_r]), (0, pad)).reshape(1, LANES)
    return jnp.concatenate([w_hi, w_lo], 1), b_r


def _moe(h, hp, route, cnt, w_gate, w_up, w_down, layer, ln_g, ln_b):
    n, d = h.shape
    counts = cnt[0, :MOE_EXPERTS].astype(jnp.int32)
    padded = (counts + MOE_BLOCK - 1) // MOE_BLOCK * MOE_BLOCK
    pends = jnp.cumsum(padded)
    pstarts = pends - padded
    experts = jnp.arange(MOE_EXPERTS, dtype=jnp.int32)
    dest0, dest1 = _dest_rows(route, pstarts, tm=2 * TILE_TOKEN_ROWS)
    p_rows = n * MOE_TOPK + MOE_EXPERTS * MOE_BLOCK
    nb = p_rows // MOE_BLOCK
    blk_start = jnp.arange(nb, dtype=jnp.int32) * MOE_BLOCK
    blk_e = jnp.minimum(jnp.sum((pends[None, :] <= blk_start[:, None]).astype(jnp.int32), -1), MOE_EXPERTS - 1)
    real_ends = pstarts + counts
    blk_end = jnp.sum(jnp.where(blk_e[:, None] == experts[None, :], real_ends[None, :], 0), -1)
    blk_rows = jnp.clip(blk_end - blk_start, 0, MOE_BLOCK).astype(jnp.int32)
    xb = _sc_scatter_rows(hp, dest0, dest1, p_rows)
    yb = _experts(blk_e, blk_rows, xb, w_gate, w_up, w_down, layer)
    return _combine_ln(h, _sc_gather_rows(yb, dest0), _sc_gather_rows(yb, dest1), route, ln_g, ln_b,
                       tm=TILE_TOKEN_ROWS)


def kernel(x, positions, ln1_g, ln1_b, ln2_g, ln2_b, a_w_in, a_w_out, b_w_in, b_gate_bias, b_conv_w, b_conv_b,
           b_norm_g, b_w_out, c_w_in, c_norm_g, c_w_out, r_group_w, r_group_b, r_expert_w, r_expert_b,
           e_w_gate, e_w_up, e_w_down):
    batch, seq, d = x.shape
    n = batch * seq
    tabs_a = _rope_tables_a(positions)
    inv_c = C_THETA ** (-jnp.arange(0, C_QK_DIM, 2, dtype=F32) / C_QK_DIM)
    ang_c = positions.astype(F32)[:, None] * inv_c[None, :]
    cos_c, sin_c = jnp.cos(ang_c), jnp.sin(ang_c)
    h = x.reshape(n, d)
    hb = h
    for i in range(DEPTH):
        kind, j = i % 3, i // 3
        if kind == 0:
            y = _mixer_dilated(hb, a_w_in[j], tabs_a, batch, seq)
            w_out = a_w_out[j]
        elif kind == 1:
            y = _mixer_mlstm(hb, b_w_in[j], b_gate_bias[j], b_conv_w[j], b_conv_b[j], b_norm_g[j], batch, seq)
            w_out = b_w_out[j]
        else:
            y = _mixer_retention(hb, c_w_in[j], c_norm_g[j], cos_c, sin_c, batch, seq)
            w_out = c_w_out[j]
        w_r, b_r = _router_params(r_group_w[i], r_group_b[i], r_expert_w[i], r_expert_b[i])
        h, hp, route, cnt = _matmul_res_ln_route(y, w_out.astype(BF16), h, ln1_g[i], ln1_b[i], w_r, b_r,
                                                 tm=TILE_TOKEN_ROWS)
        h, hb = _moe(h, hp, route, cnt, e_w_gate, e_w_up, e_w_down, i, ln2_g[i], ln2_b[i])
    return h.reshape(batch, seq, d)
```

```python
import jax
import jax.numpy as jnp
from jax import lax
from jax.experimental import pallas as pl
from jax.experimental.pallas import tpu as pltpu
from jax.experimental.pallas import tpu_sc as plsc

F32 = jnp.float32
BF16 = jnp.bfloat16

D_MODEL = 1024
DEPTH = 4
DN_ALPHA = (2.0 * DEPTH) ** 0.25
LN_EPS = 1e-5

A_HEADS = 16
A_HEAD_DIM = 64
A_DILATIONS = (1, 4, 16)
A_BLOCK = 128
A_UNROLL = 32
A_GROUP = 16
A_PITCH = 20
LOG2_E = 1.4426950408889634
A_ROT_DIM = 16
ROPE_THETA = 500000.0

B_HEADS = 8
B_QK_DIM = 64
B_V_DIM = 128
B_CONV = 4
B_CHUNK = 256

C_HEADS = 4
C_QK_DIM = 256
C_V_DIM = 512
C_CHUNK = 256
C_THETA = 10000.0

MOE_GROUPS = 8
MOE_PER_GROUP = 8
MOE_EXPERTS = 64
MOE_TOPK = 2
MOE_HIDDEN = 256
MOE_BLOCK = 512

LANES = 128
SC_ROWS = 64
SC_INDEX_LANES = LANES
NEG = -1e30
V7X_VMEM_BYTES = 64 * 1024 * 1024
VMEM_LIMIT = V7X_VMEM_BYTES * 3 // 4

TILE_PROJ_A = 512
TILE_MLSTM_PROJ = (1024, 1536)
TILE_MLSTM_GATES = (2048, LANES)
TILE_RETENTION_PROJ = (1024, 2048)
TILE_TOKEN_ROWS = 512


def _params(*sem):
    return pltpu.CompilerParams(dimension_semantics=sem, vmem_limit_bytes=VMEM_LIMIT)


def _mm_kernel(x_ref, w_ref, o_ref):
    o_ref[...] = jnp.dot(x_ref[...], w_ref[...], preferred_element_type=F32).astype(o_ref.dtype)


def _matmul(x, w, *, tm, tn, out_dtype=F32):
    n, k = x.shape
    m = w.shape[1]
    return pl.pallas_call(
        _mm_kernel,
        grid=(n // tm, m // tn),
        in_specs=[pl.BlockSpec((tm, k), lambda i, j: (i, 0)),
                  pl.BlockSpec((k, tn), lambda i, j: (0, j))],
        out_specs=pl.BlockSpec((tm, tn), lambda i, j: (i, j)),
        out_shape=jax.ShapeDtypeStruct((n, m), out_dtype),
        compiler_params=_params("parallel", "parallel"),
        name="matmul",
    )(x, w)


def _layer_norm_rows(z, g, b):
    mu = jnp.mean(z, -1, keepdims=True)
    zc = z - mu
    var = jnp.mean(zc * zc, -1, keepdims=True)
    return zc * lax.rsqrt(var + LN_EPS) * g + b


def _pack_pairs(x):
    c = x.shape[1] // 2
    hi = pltpu.bitcast(x[:, :c].astype(BF16).astype(F32), jnp.uint32)
    lo = pltpu.bitcast(x[:, c:].astype(BF16).astype(F32), jnp.uint32)
    return pltpu.bitcast(hi | (lo >> 16), F32)


def _unpack_pairs(w):
    bits = pltpu.bitcast(w, jnp.uint32)
    hi = pltpu.bitcast(bits & jnp.uint32(0xFFFF0000), F32)
    lo = pltpu.bitcast(bits << 16, F32)
    return jnp.concatenate([hi, lo], axis=1)


def _proj_a_kernel(x_ref, w_ref, c_ref, s1_ref, s2_ref, q_ref, k_ref, v_ref):
    x = x_ref[...].astype(BF16)
    width = 2 * LANES
    for c, ref in ((0, q_ref), (1, k_ref), (2, v_ref)):
        for j in range(D_MODEL // width):
            col = c * D_MODEL + j * width
            y = jnp.dot(x, w_ref[:, col:col + width], preferred_element_type=F32)
            if c < 2:
                half = A_ROT_DIM // 2
                y = (y * c_ref[...] + pltpu.roll(y, width - half, 1) * s1_ref[...]
                     + pltpu.roll(y, half, 1) * s2_ref[...])
            if c == 0:
                y = y * (LOG2_E * A_HEAD_DIM ** -0.5)
            ref[0, 2 * j] = y[:, :LANES]
            ref[0, 2 * j + 1] = y[:, LANES:]


def _proj_a(xb, w, tabs, batch, seq, *, tm):
    n, d = xb.shape
    spb = seq // tm
    hp = D_MODEL // LANES
    qkv_shape = jax.ShapeDtypeStruct((batch, hp, seq, LANES), F32)
    out_spec = pl.BlockSpec((1, hp, tm, LANES), lambda i: (i // spb, 0, i % spb, 0))
    tab_spec = pl.BlockSpec((tm, 2 * LANES), lambda i: (i % spb, 0))
    return pl.pallas_call(
        _proj_a_kernel,
        grid=(n // tm,),
        in_specs=[pl.BlockSpec((tm, d), lambda i: (i, 0)),
                  pl.BlockSpec((d, 3 * d), lambda i: (0, 0)),
                  tab_spec, tab_spec, tab_spec],
        out_specs=[out_spec, out_spec, out_spec],
        out_shape=[qkv_shape, qkv_shape, qkv_shape],
        compiler_params=_params("parallel"),
        name="proj_a",
    )(xb, w, *tabs)


def _rope_tables_a(positions):
    half = A_ROT_DIM // 2
    inv = ROPE_THETA ** (-jnp.arange(0, A_ROT_DIM, 2, dtype=F32) / A_ROT_DIM)
    ang = positions.astype(F32)[:, None] * inv[None, :]
    cos, sin = jnp.cos(ang), jnp.sin(ang)
    s = positions.shape[0]
    pad = jnp.zeros((s, A_HEAD_DIM - A_ROT_DIM), F32)
    c_head = jnp.concatenate([cos, cos, pad + 1.0], -1)
    s1_head = jnp.concatenate([-sin, jnp.zeros_like(sin), pad], -1)
    s2_head = jnp.concatenate([jnp.zeros_like(sin), sin, pad], -1)
    reps = 2 * LANES // A_HEAD_DIM
    return tuple(jnp.tile(t, (1, reps)) for t in (c_head, s1_head, s2_head))


def _attn_blocks(q_ref, k_ref, v_ref, o_ref, lse_ref, bias_ref, hmask, head0, first_block, d, pitch):
    nk = 2 * A_BLOCK
    loaded = []
    for u in range(A_UNROLL):
        g = first_block + u
        r = g % d
        n = g // d
        qstart = n * (A_BLOCK * pitch) + r
        kstart = jnp.maximum(qstart - A_BLOCK * pitch, r)
        if pitch == 1:
            qstart = pl.multiple_of(qstart, A_BLOCK)
            kstart = pl.multiple_of(kstart, A_BLOCK)
            qsl, ksl = pl.ds(qstart, A_BLOCK), pl.ds(kstart, nk)
        else:
            qsl, ksl = pl.ds(qstart, A_BLOCK, stride=pitch), pl.ds(kstart, nk, stride=pitch)
        bias = bias_ref[jnp.minimum(n, 1)]
        loaded.append((qsl, q_ref[qsl, :].astype(BF16), k_ref[ksl, :].astype(BF16), v_ref[ksl, :].astype(BF16), bias))
    results = []
    for qsl, qb, kb, vb, bias in loaded:
        pvs, ms, ls = [], [], []
        for h in range(2):
            s = lax.dot_general(qb * hmask[h], kb, (((1,), (1,)), ((), ())), preferred_element_type=F32) + bias
            m = jnp.max(s, -1, keepdims=True)
            p = jnp.exp2(s - m)
            ms.append(jnp.broadcast_to(m, (A_BLOCK, LANES)))
            ls.append(jnp.broadcast_to(jnp.sum(p, -1, keepdims=True), (A_BLOCK, LANES)))
            pvs.append(jnp.dot(p.astype(BF16), vb, preferred_element_type=F32))
        l = jnp.where(head0, ls[0], ls[1])
        out = jnp.where(head0, pvs[0], pvs[1]) * (1.0 / l)
        results.append((qsl, out, jnp.where(head0, ms[0], ms[1]) + jnp.log2(l)))
    for qsl, out, lse in results:
        o_ref[qsl, :] = out
        lse_ref[qsl, :] = lse


def _attn_kernel(q_ref, k_ref, v_ref, o_ref, q16_ref, k16_ref, v16_ref, ob_ref, lb_ref, o16_ref, l16_ref,
                 bias_ref):
    seq = q_ref.shape[0]
    groups = seq // A_GROUP
    head0 = lax.broadcasted_iota(jnp.int32, (A_BLOCK, LANES), 1) < A_HEAD_DIM
    hmask = [jnp.where(head0, 1.0, 0.0).astype(BF16), jnp.where(head0, 0.0, 1.0).astype(BF16)]
    qi = lax.broadcasted_iota(jnp.int32, (A_BLOCK, 2 * A_BLOCK), 0)
    kj = lax.broadcasted_iota(jnp.int32, (A_BLOCK, 2 * A_BLOCK), 1)
    bias_ref[0] = jnp.where(kj <= qi, 0.0, NEG).astype(F32)
    bias_ref[1] = jnp.where((kj >= qi) & (kj <= qi + A_BLOCK), 0.0, NEG).astype(F32)

    def spread(g, carry):
        src = pl.ds(pl.multiple_of(g * A_GROUP, A_GROUP), A_GROUP)
        dst = pl.ds(pl.multiple_of(g * A_PITCH, 4), A_GROUP)
        q16_ref[dst, :] = q_ref[src, :]
        k16_ref[dst, :] = k_ref[src, :]
        v16_ref[dst, :] = v_ref[src, :]
        return carry

    lax.fori_loop(0, groups, spread, 0, unroll=8)

    branches = ((1, 1, q_ref, k_ref, v_ref, ob_ref.at[0], lb_ref.at[0]),
                (4, 4, q_ref, k_ref, v_ref, ob_ref.at[1], lb_ref.at[1]),
                (16, A_PITCH, q16_ref, k16_ref, v16_ref, o16_ref, l16_ref))
    zero = jnp.minimum(pl.program_id(0), 0)
    for d, pitch, qr, kr, vr, orf, lrf in branches:

        def body(it, carry, d=d, pitch=pitch, qr=qr, kr=kr, vr=vr, orf=orf, lrf=lrf):
            _attn_blocks(qr, kr, vr, orf, lrf, bias_ref, hmask, head0, it * A_UNROLL, d, pitch)
            return carry

        lax.fori_loop(zero, zero + seq // (A_BLOCK * A_UNROLL), body, 0)

    def mix(g, carry):
        nat = pl.ds(pl.multiple_of(g * A_GROUP, A_GROUP), A_GROUP)
        pad = pl.ds(pl.multiple_of(g * A_PITCH, 4), A_GROUP)
        o0, o1, o2 = ob_ref[0, nat, :], ob_ref[1, nat, :], o16_ref[pad, :]
        l0, l1, l2 = lb_ref[0, nat, :], lb_ref[1, nat, :], l16_ref[pad, :]
        mx = jnp.maximum(jnp.maximum(l0, l1), l2)
        w0, w1, w2 = jnp.exp2(l0 - mx), jnp.exp2(l1 - mx), jnp.exp2(l2 - mx)
        o_ref[nat, :] = ((w0 * o0 + w1 * o1 + w2 * o2) / (w0 + w1 + w2)).astype(o_ref.dtype)
        return carry

    lax.fori_loop(0, groups, mix, 0, unroll=8)


def _attention(q, k, v):
    batch, hp, seq, _ = q.shape
    assert seq % (2 * A_BLOCK * max(A_DILATIONS)) == 0 and seq % (A_BLOCK * A_UNROLL) == 0
    in_spec = pl.BlockSpec((None, None, seq, LANES), lambda b, p: (b, p, 0, 0))
    padded = seq // A_GROUP * A_PITCH
    return pl.pallas_call(
        _attn_kernel,
        grid=(batch, hp),
        in_specs=[in_spec, in_spec, in_spec],
        out_specs=pl.BlockSpec((None, seq, LANES), lambda b, p: (b, 0, p)),
        out_shape=jax.ShapeDtypeStruct((batch, seq, hp * LANES), BF16),
        scratch_shapes=[pltpu.VMEM((padded, LANES), F32)] * 3
        + [pltpu.VMEM((2, seq, LANES), F32)] * 2
        + [pltpu.VMEM((padded, LANES), F32)] * 2
        + [pltpu.VMEM((2, A_BLOCK, 2 * A_BLOCK), F32)],
        compiler_params=_params("parallel", "parallel"),
        name="dilated_attention",
    )(q, k, v)


def _mixer_dilated(hb, w_in, tabs, batch, seq):
    q, k, v = _proj_a(hb, w_in.astype(BF16), tabs, batch, seq, tm=TILE_PROJ_A)
    o = _attention(q, k, v)
    return o.reshape(batch * seq, D_MODEL)


def _mlstm_kernel(qk_ref, v_ref, o_ref, gc_ref, gb_ref, cw_ref, cb_ref, ng_ref, out_ref,
                  ext_ref, c_ref, n_ref, m_ref):
    L = B_CHUNK
    chunk = pl.program_id(1)

    @pl.when(chunk == 0)
    def _():
        ext_ref[0:8, :] = jnp.zeros((8, D_MODEL), F32)
        c_ref[...] = jnp.zeros(c_ref.shape, F32)
        n_ref[...] = jnp.zeros(n_ref.shape, F32)
        m_ref[...] = jnp.zeros(m_ref.shape, F32)

    u = qk_ref[...].astype(F32)
    ext_ref[8:8 + L, :] = u
    conv = u * cw_ref[B_CONV - 1:B_CONV, :] + cb_ref[...]
    for j in range(1, B_CONV):
        conv = conv + ext_ref[pl.ds(8 - j, L), :] * cw_ref[B_CONV - 1 - j:B_CONV - j, :]
    ext_ref[0:8, :] = u[L - 8:, :]
    qk = conv * jax.nn.sigmoid(conv)
    half = D_MODEL // 2

    gc = gc_ref[...] + gb_ref[...]
    gr = gc.T
    i_col, i_row = gc, gr[:B_HEADS, :]
    lf_col = jax.nn.log_sigmoid(gc)
    lf_row = jax.nn.log_sigmoid(gr[B_HEADS:2 * B_HEADS, :])
    ti = lax.broadcasted_iota(jnp.int32, (L, L), 0)
    si = lax.broadcasted_iota(jnp.int32, (L, L), 1)
    causal = ti >= si
    tri = causal.astype(F32)
    a_col = jnp.dot(tri, lf_col, preferred_element_type=F32, precision=lax.Precision.HIGHEST)
    a_row = lax.dot_general(lf_row, tri, (((1,), (1,)), ((), ())), preferred_element_type=F32,
                            precision=lax.Precision.HIGHEST)
    lane = lax.broadcasted_iota(jnp.int32, (1, LANES), 1)
    lane_h0 = lane < B_QK_DIM
    col_h0 = lax.broadcasted_iota(jnp.int32, (1, 2 * B_V_DIM), 1) < B_V_DIM

    heads = range(B_HEADS)
    pairs = range(B_HEADS // 2)
    qp = [qk[:, p * LANES:(p + 1) * LANES] for p in pairs]
    kp = [qk[:, half + p * LANES:half + (p + 1) * LANES] * (B_QK_DIM ** -0.5) for p in pairs]
    kpb = [k.astype(BF16) for k in kp]
    vpb = [v_ref[:, p * 2 * B_V_DIM:(p + 1) * 2 * B_V_DIM] for p in pairs]
    c_old = [c_ref[p] for p in pairs]
    c_oldb = [c.astype(BF16) for c in c_old]
    n_old = [n_ref[p] for p in pairs]
    m_old = [m_ref[h // 2][:, (h % 2) * B_QK_DIM:(h % 2) * B_QK_DIM + 1] for h in heads]
    ac = [a_col[:, B_HEADS + h:B_HEADS + h + 1] for h in heads]
    ic = [i_col[:, h:h + 1] for h in heads]
    dmat = [jnp.where(causal, ac[h] + (i_row[h:h + 1, :] - a_row[h:h + 1, :]), NEG) for h in heads]
    inter = [ac[h] + m_old[h] for h in heads]
    m_t = [jnp.maximum(inter[h], jnp.max(dmat[h], -1, keepdims=True)) for h in heads]
    qm = [jnp.where(lane_h0 if h % 2 == 0 else jnp.logical_not(lane_h0), qp[h // 2], 0.0) for h in heads]
    qmb = [q.astype(BF16) for q in qm]
    qk_s = [lax.dot_general(qmb[h], kpb[h // 2], (((1,), (1,)), ((), ())), preferred_element_type=F32) for h in heads]
    qc = [jnp.dot(qmb[h], c_oldb[h // 2], preferred_element_type=F32)[:, (h % 2) * B_V_DIM:(h % 2 + 1) * B_V_DIM]
          for h in heads]
    sc = [qk_s[h] * jnp.exp(dmat[h] - m_t[h]) for h in heads]
    g_inter = [jnp.exp(inter[h] - m_t[h]) for h in heads]
    num = [jnp.dot(sc[h].astype(BF16), vpb[h // 2][:, (h % 2) * B_V_DIM:(h % 2 + 1) * B_V_DIM],
                   preferred_element_type=F32) + g_inter[h] * qc[h] for h in heads]
    den = [jnp.sum(sc[h], -1, keepdims=True) + g_inter[h] * jnp.sum(qm[h] * n_old[h // 2], -1, keepdims=True)
           for h in heads]
    h_out = [num[h] / jnp.maximum(jnp.abs(den[h]), jnp.exp(-m_t[h])) for h in heads]
    for h in heads:
        mu = jnp.mean(h_out[h], -1, keepdims=True)
        hc = h_out[h] - mu
        var = jnp.mean(hc * hc, -1, keepdims=True)
        cols = slice(h * B_V_DIM, (h + 1) * B_V_DIM)
        hn = hc * lax.rsqrt(var + LN_EPS) * ng_ref[:, cols]
        out_ref[:, cols] = (hn * jax.nn.sigmoid(o_ref[:, cols].astype(F32))).astype(out_ref.dtype)
    a_end = [ac[h][L - 1:L, :] for h in heads]
    w_col = [a_end[h] - ac[h] + ic[h] for h in heads]
    m_new = [jnp.maximum(a_end[h] + m_old[h], jnp.max(w_col[h], 0, keepdims=True)) for h in heads]
    decay = [jnp.exp(a_end[h] + m_old[h] - m_new[h]) for h in heads]
    ws_col = [jnp.exp(w_col[h] - m_new[h]) for h in heads]
    for p in pairs:
        h0, h1 = 2 * p, 2 * p + 1
        ws = jnp.where(lane_h0, jnp.broadcast_to(ws_col[h0], (L, LANES)), jnp.broadcast_to(ws_col[h1], (L, LANES)))
        kw = kp[p] * ws
        dec_c = jnp.where(col_h0, jnp.broadcast_to(decay[h0], (1, 2 * B_V_DIM)),
                          jnp.broadcast_to(decay[h1], (1, 2 * B_V_DIM)))
        dec_n = jnp.where(lane_h0, jnp.broadcast_to(decay[h0], (1, LANES)), jnp.broadcast_to(decay[h1], (1, LANES)))
        c_ref[p] = dec_c * c_old[p] + lax.dot_general(kw.astype(BF16), vpb[p], (((0,), (0,)), ((), ())),
                                                      preferred_element_type=F32)
        n_ref[p] = dec_n * n_old[p] + jnp.sum(kw, 0, keepdims=True)
        m_ref[p] = jnp.where(lane_h0, jnp.broadcast_to(m_new[h0], (1, LANES)), jnp.broadcast_to(m_new[h1], (1, LANES)))


def _mlstm(proj, gates, gate_bias, conv_w, conv_b, norm_g, batch, seq):
    L = B_CHUNK
    d = D_MODEL
    slab = lambda c: pl.BlockSpec((None, L, d), lambda b, s, c=c: (b, s, c))
    full = lambda shape: pl.BlockSpec(shape, lambda b, s: (0,) * len(shape))
    return pl.pallas_call(
        _mlstm_kernel,
        grid=(batch, seq // L),
        in_specs=[slab(0), slab(1), slab(2),
                  pl.BlockSpec((None, L, LANES), lambda b, s: (b, s, 0)),
                  full((1, LANES)),
                  full((B_CONV, d)), full((1, d)), full((1, d))],
        out_specs=pl.BlockSpec((None, L, d), lambda b, s: (b, s, 0)),
        out_shape=jax.ShapeDtypeStruct((batch, seq, d), BF16),
        scratch_shapes=[pltpu.VMEM((L + 8, d), F32),
                        pltpu.VMEM((B_HEADS // 2, 2 * B_QK_DIM, 2 * B_V_DIM), F32),
                        pltpu.VMEM((B_HEADS // 2, 1, LANES), F32),
                        pltpu.VMEM((B_HEADS // 2, 1, LANES), F32)],
        compiler_params=_params("parallel", "arbitrary"),
        name="mlstm",
    )(proj, proj, proj, gates, jnp.pad(gate_bias, (0, LANES - 2 * B_HEADS)).reshape(1, LANES),
      conv_w, conv_b.reshape(1, d), norm_g.reshape(1, d))


def _mixer_mlstm(hb, w_in, gate_bias, conv_w, conv_b, norm_g, batch, seq):
    n = batch * seq
    main = 3 * D_MODEL
    tm, tn = TILE_MLSTM_PROJ
    proj = _matmul(hb, w_in[:, :main].astype(BF16), tm=tm, tn=tn, out_dtype=BF16)
    w_g = jnp.pad(w_in[:, main:], ((0, 0), (0, LANES - 2 * B_HEADS))).astype(BF16)
    gates = _matmul(hb, w_g, tm=TILE_MLSTM_GATES[0], tn=TILE_MLSTM_GATES[1])
    out = _mlstm(proj.reshape(batch, seq, main), gates.reshape(batch, seq, LANES), gate_bias,
                 conv_w, conv_b, norm_g, batch, seq)
    return out.reshape(n, D_MODEL)


def _retention_kernel(lg_ref, q_ref, k_ref, v_ref, g_ref, cos_ref, sin_ref, ng_ref, out_ref,
                      r_ref, dm_ref, xi_ref, zeta_ref):
    L = C_CHUNK
    chunk = pl.program_id(1)

    @pl.when(chunk == 0)
    def _():
        r_ref[...] = jnp.zeros(r_ref.shape, F32)
        ti = lax.broadcasted_iota(jnp.int32, (L, L), 0)
        si = lax.broadcasted_iota(jnp.int32, (L, L), 1)
        rel = (ti - si).astype(F32)
        idx = lax.broadcasted_iota(jnp.int32, (L, LANES), 0).astype(F32)
        for h in range(C_HEADS):
            lg = lg_ref[h]
            dm_ref[h] = jnp.where(rel >= 0, jnp.exp(jnp.maximum(rel, 0.0) * lg), 0.0)
            xi_ref[h] = jnp.exp((idx + 1.0) * lg)
            zeta_ref[h] = jnp.exp((L - 1.0 - idx) * lg)

    cos, sin = cos_ref[...], sin_ref[...]
    hd = C_QK_DIM // 2

    def rope(t):
        t1, t2 = t[:, :hd], t[:, hd:]
        return jnp.concatenate([t1 * cos - t2 * sin, t2 * cos + t1 * sin], -1)

    for h in range(C_HEADS):
        qk_cols = slice(h * C_QK_DIM, (h + 1) * C_QK_DIM)
        v_cols = slice(h * C_V_DIM, (h + 1) * C_V_DIM)
        q = rope(q_ref[:, qk_cols].astype(F32))
        k = rope(k_ref[:, qk_cols].astype(F32)) * (C_QK_DIM ** -0.5)
        qb = q.astype(BF16)
        vb = v_ref[:, v_cols]
        r_old = r_ref[h]
        sc = lax.dot_general(qb, k.astype(BF16), (((1,), (1,)), ((), ())), preferred_element_type=F32) * dm_ref[h]
        o = jnp.dot(sc.astype(BF16), vb, preferred_element_type=F32)
        o = o + xi_ref[h, :, 0:1] * jnp.dot(qb, r_old.astype(BF16), preferred_element_type=F32)
        kz = (k * zeta_ref[h, :, 0:1]).astype(BF16)
        cd = jnp.exp(jnp.full((1, 1), float(L), F32) * lg_ref[h])
        r_ref[h] = cd * r_old + lax.dot_general(kz, vb, (((0,), (0,)), ((), ())), preferred_element_type=F32)
        mu = jnp.mean(o, -1, keepdims=True)
        oc = o - mu
        var = jnp.mean(oc * oc, -1, keepdims=True)
        on = oc * lax.rsqrt(var + LN_EPS) * ng_ref[:, v_cols]
        g = g_ref[:, v_cols].astype(F32)
        out_ref[:, v_cols] = (on * (g * jax.nn.sigmoid(g))).astype(out_ref.dtype)


def _retention(proj, cos, sin, norm_g, batch, seq):
    L = C_CHUNK
    d = D_MODEL
    log_gamma = jnp.log(1.0 - 2.0 ** (-5.0 - jnp.arange(C_HEADS, dtype=F32)))
    return pl.pallas_call(
        _retention_kernel,
        grid=(batch, seq // L),
        in_specs=[pl.BlockSpec(memory_space=pltpu.SMEM),
                  pl.BlockSpec((None, L, d), lambda b, c: (b, c, 0)),
                  pl.BlockSpec((None, L, d), lambda b, c: (b, c, 1)),
                  pl.BlockSpec((None, L, 2 * d), lambda b, c: (b, c, 1)),
                  pl.BlockSpec((None, L, 2 * d), lambda b, c: (b, c, 2)),
                  pl.BlockSpec((L, C_QK_DIM // 2), lambda b, c: (c, 0)),
                  pl.BlockSpec((L, C_QK_DIM // 2), lambda b, c: (c, 0)),
                  pl.BlockSpec((1, 2 * d), lambda b, c: (0, 0))],
        out_specs=pl.BlockSpec((None, L, 2 * d), lambda b, c: (b, c, 0)),
        out_shape=jax.ShapeDtypeStruct((batch, seq, 2 * d), BF16),
        scratch_shapes=[pltpu.VMEM((C_HEADS, C_QK_DIM, C_V_DIM), F32),
                        pltpu.VMEM((C_HEADS, L, L), F32),
                        pltpu.VMEM((C_HEADS, L, LANES), F32),
                        pltpu.VMEM((C_HEADS, L, LANES), F32)],
        compiler_params=_params("parallel", "arbitrary"),
        name="retention",
    )(log_gamma, proj, proj, proj, proj, cos, sin, norm_g.reshape(1, 2 * d))


def _mixer_retention(hb, w_in, norm_g, cos, sin, batch, seq):
    tm, tn = TILE_RETENTION_PROJ
    proj = _matmul(hb, w_in.astype(BF16), tm=tm, tn=tn, out_dtype=BF16)
    out = _retention(proj.reshape(batch, seq, 6 * D_MODEL), cos, sin, norm_g, batch, seq)
    return out.reshape(batch * seq, 2 * D_MODEL)


def _expert_kernel(be_ref, bv_ref, x_ref, wg_ref, wu_ref, wd_ref, y_ref, wgb_ref, wub_ref, wdb_ref):
    j = pl.program_id(0)
    changed = jnp.logical_or(j == 0, be_ref[j] != be_ref[jnp.maximum(j - 1, 0)])

    @pl.when(changed)
    def _():
        wgb_ref[...] = wg_ref[...].astype(BF16)
        wub_ref[...] = wu_ref[...].astype(BF16)
        wdb_ref[...] = wd_ref[...].astype(BF16)

    real = lax.broadcasted_iota(jnp.int32, (x_ref.shape[0], 1), 0) < bv_ref[j]
    x = _unpack_pairs(jnp.where(real, x_ref[...], 0.0)).astype(BF16)
    a = jnp.dot(x, wgb_ref[...], preferred_element_type=F32)
    u = jnp.dot(x, wub_ref[...], preferred_element_type=F32)
    act = (a * jax.nn.sigmoid(a) * u).astype(BF16)
    y_ref[...] = _pack_pairs(jnp.dot(act, wdb_ref[...], preferred_element_type=F32))


def _experts(blk_e, blk_rows, xb, w_gate, w_up, w_down, layer):
    p, dw = xb.shape
    d = 2 * dw
    nb = p // MOE_BLOCK
    hid = MOE_HIDDEN
    grid_spec = pltpu.PrefetchScalarGridSpec(
        num_scalar_prefetch=2,
        grid=(nb,),
        in_specs=[pl.BlockSpec((MOE_BLOCK, dw), lambda j, be, bv: (j, 0)),
                  pl.BlockSpec((None, None, d, hid), lambda j, be, bv: (layer, be[j], 0, 0)),
                  pl.BlockSpec((None, None, d, hid), lambda j, be, bv: (layer, be[j], 0, 0)),
                  pl.BlockSpec((None, None, hid, d), lambda j, be, bv: (layer, be[j], 0, 0))],
        out_specs=pl.BlockSpec((MOE_BLOCK, dw), lambda j, be, bv: (j, 0)),
        scratch_shapes=[pltpu.VMEM((d, hid), BF16), pltpu.VMEM((d, hid), BF16), pltpu.VMEM((hid, d), BF16)],
    )
    return pl.pallas_call(
        _expert_kernel,
        grid_spec=grid_spec,
        out_shape=jax.ShapeDtypeStruct((p, dw), F32),
        compiler_params=_params("arbitrary"),
        name="moe_experts",
    )(blk_e, blk_rows, xb, w_gate, w_up, w_down)


def _combine_ln_kernel(h_ref, y0_ref, y1_ref, rt_ref, g_ref, b_ref, o_ref, ob_ref):
    rt = rt_ref[...]
    y = (_unpack_pairs(y0_ref[...]) * rt[:, R_GATE:R_GATE + 1]
         + _unpack_pairs(y1_ref[...]) * rt[:, R_GATE + 1:R_GATE + 2])
    out = _layer_norm_rows(DN_ALPHA * h_ref[...] + y, g_ref[...], b_ref[...])
    o_ref[...] = out
    ob_ref[...] = out.astype(BF16)


def _combine_ln(h, y0, y1, route, g, b, *, tm):
    n, d = h.shape
    row = pl.BlockSpec((tm, d), lambda i: (i, 0))
    words = pl.BlockSpec((tm, d // 2), lambda i: (i, 0))
    vec = pl.BlockSpec((1, d), lambda i: (0, 0))
    return pl.pallas_call(
        _combine_ln_kernel,
        grid=(n // tm,),
        in_specs=[row, words, words, pl.BlockSpec((tm, LANES), lambda i: (i, 0)), vec, vec],
        out_specs=[row, row],
        out_shape=[jax.ShapeDtypeStruct((n, d), F32), jax.ShapeDtypeStruct((n, d), BF16)],
        compiler_params=_params("parallel"),
        name="moe_combine_ln",
    )(h, y0, y1, route, g.reshape(1, d), b.reshape(1, d))


R_EID, R_GATE, R_RANK = 0, 2, 4


def _route_tile(h, w_ref, b_ref, route_ref, cnt_ref, base_ref, tri_ref):
    i = pl.program_id(0)
    tm = h.shape[0]

    @pl.when(i == 0)
    def _():
        base_ref[...] = jnp.zeros(base_ref.shape, F32)
        ti = lax.broadcasted_iota(jnp.int32, (tm, tm), 0)
        si = lax.broadcasted_iota(jnp.int32, (tm, tm), 1)
        tri_ref[...] = jnp.where(si < ti, 1.0, 0.0).astype(BF16)

    h_hi = h.astype(BF16)
    h_lo = (h - h_hi.astype(F32)).astype(BF16)
    hh = jnp.dot(h_hi, w_ref[...], preferred_element_type=F32)
    logits = (hh[:, :LANES] + hh[:, LANES:]
              + jnp.dot(h_lo, w_ref[:, :LANES], preferred_element_type=F32)) + b_ref[...]
    lane = lax.broadcasted_iota(jnp.int32, (tm, LANES), 1).astype(F32)
    neg_inf = -jnp.inf
    big = float(4 * LANES)
    is_g = lane < MOE_GROUPS
    gl = jnp.where(is_g, logits, neg_inf)
    gmax = jnp.max(gl, -1, keepdims=True)
    grp = jnp.min(jnp.where(gl == gmax, lane, big), -1, keepdims=True)
    p_grp = 1.0 / jnp.sum(jnp.where(is_g, jnp.exp(logits - gmax), 0.0), -1, keepdims=True)
    lo = MOE_GROUPS + MOE_PER_GROUP * grp
    el = jnp.where((lane >= lo) & (lane < lo + MOE_PER_GROUP), logits, neg_inf)
    v1 = jnp.max(el, -1, keepdims=True)
    i1 = jnp.min(jnp.where(el == v1, lane, big), -1, keepdims=True)
    el2 = jnp.where(lane == i1, neg_inf, el)
    v2 = jnp.max(el2, -1, keepdims=True)
    i2 = jnp.min(jnp.where(el2 == v2, lane, big), -1, keepdims=True)
    t = jnp.exp(v2 - v1)
    g1 = p_grp / (1.0 + t)
    g2 = g1 * t
    e1 = i1 - MOE_GROUPS
    e2 = i2 - MOE_GROUPS
    oh1 = jnp.where(lane == e1, 1.0, 0.0)
    oh2 = jnp.where(lane == e2, 1.0, 0.0)
    oh = oh1 + oh2
    tot = base_ref[...] + jnp.dot(tri_ref[...], oh.astype(BF16), preferred_element_type=F32)
    r1 = jnp.sum(oh1 * tot, -1, keepdims=True)
    r2 = jnp.sum(oh2 * tot, -1, keepdims=True)
    new_base = base_ref[...] + jnp.sum(oh, 0, keepdims=True)
    base_ref[...] = new_base
    cnt_ref[...] = jnp.broadcast_to(new_base, cnt_ref.shape)
    route = jnp.zeros((tm, LANES), F32)
    for k, val in enumerate((e1, e2, g1, g2, r1, r2)):
        route = jnp.where(lane == float(k), val, route)
    route_ref[...] = route


def _mm_res_ln_route_kernel(x_ref, w_ref, h_ref, g_ref, b_ref, rw_ref, rb_ref,
                            o_ref, op_ref, route_ref, cnt_ref, base_ref, tri_ref):
    y = jnp.dot(x_ref[...], w_ref[...], preferred_element_type=F32)
    out = _layer_norm_rows(DN_ALPHA * h_ref[...] + y, g_ref[...], b_ref[...])
    o_ref[...] = out
    op_ref[...] = _pack_pairs(out)
    _route_tile(out, rw_ref, rb_ref, route_ref, cnt_ref, base_ref, tri_ref)


def _matmul_res_ln_route(x, w, h, g, b, w_r, b_r, *, tm):
    n, k = x.shape
    d = w.shape[1]
    rows = lambda width: pl.BlockSpec((tm, width), lambda i: (i, 0))
    const = lambda shape: pl.BlockSpec(shape, lambda i: (0, 0))
    return pl.pallas_call(
        _mm_res_ln_route_kernel,
        grid=(n // tm,),
        in_specs=[rows(k), const((k, d)), rows(d), const((1, d)), const((1, d)),
                  const((d, 2 * LANES)), const((1, LANES))],
        out_specs=[rows(d), rows(d // 2), rows(LANES), const((8, LANES))],
        out_shape=[jax.ShapeDtypeStruct((n, d), F32), jax.ShapeDtypeStruct((n, d // 2), F32),
                   jax.ShapeDtypeStruct((n, LANES), F32), jax.ShapeDtypeStruct((8, LANES), F32)],
        scratch_shapes=[pltpu.VMEM((1, LANES), F32), pltpu.VMEM((tm, tm), BF16)],
        compiler_params=_params("arbitrary"),
        name="matmul_res_ln_route",
    )(x, w, h, g.reshape(1, d), b.reshape(1, d), w_r, b_r)


def _dest_kernel(route_ref, ps_ref, i0_ref, i1_ref):
    tm = route_ref.shape[0]
    rt = route_ref[...]
    lane = lax.broadcasted_iota(jnp.int32, (tm, LANES), 1)
    row = lax.broadcasted_iota(jnp.int32, (tm, LANES), 0)
    own_lane = lane == row % SC_ROWS
    group = (lax.broadcasted_iota(jnp.int32, (tm // SC_ROWS, tm), 1) // SC_ROWS
             == lax.broadcasted_iota(jnp.int32, (tm // SC_ROWS, tm), 0)).astype(F32)
    ps = ps_ref[0:1, :]
    for slot, out_ref in ((0, i0_ref), (1, i1_ref)):
        e = rt[:, R_EID + slot:R_EID + slot + 1]
        r = rt[:, R_RANK + slot:R_RANK + slot + 1]
        dest = jnp.sum(jnp.where(lane.astype(F32) == e, ps, 0.0), -1, keepdims=True) + r
        spread = jnp.where(own_lane, dest, 0.0)
        out_ref[...] = jnp.dot(group, spread, preferred_element_type=F32,
                               precision=lax.Precision.HIGHEST).astype(jnp.int32)


def _dest_rows(route, pstarts, *, tm):
    n = route.shape[0]
    ps = jnp.zeros((8, LANES), F32).at[0, :MOE_EXPERTS].set(pstarts.astype(F32))
    idx = jax.ShapeDtypeStruct((n // SC_ROWS, SC_INDEX_LANES), jnp.int32)
    idx_spec = pl.BlockSpec((tm // SC_ROWS, SC_INDEX_LANES), lambda i: (i, 0))
    return pl.pallas_call(
        _dest_kernel,
        grid=(n // tm,),
        in_specs=[pl.BlockSpec((tm, LANES), lambda i: (i, 0)), pl.BlockSpec((8, LANES), lambda i: (0, 0))],
        out_specs=[idx_spec, idx_spec],
        out_shape=[idx, idx],
        compiler_params=_params("parallel"),
        name="moe_dest_rows",
    )(route, ps)


def _sc_mesh():
    return plsc.VectorSubcoreMesh(core_axis_name="core", subcore_axis_name="subcore")


def _sc_gather_rows(y, idx_rows):
    n = idx_rows.shape[0] * SC_ROWS
    d = y.shape[1]

    @pl.kernel(out_type=jax.ShapeDtypeStruct((n, d), y.dtype), mesh=_sc_mesh(), scratch_types=[])
    def gather(y_hbm, i_hbm, o_hbm):
        def body(i_vmem, o_vmem):
            pltpu.sync_copy(y_hbm.at[i_vmem.at[0, pl.ds(0, SC_ROWS)]], o_vmem)

        pltpu.emit_pipeline(
            body, grid=(n // SC_ROWS,),
            in_specs=[pl.BlockSpec((1, SC_INDEX_LANES), lambda i: (i, 0))],
            out_specs=[pl.BlockSpec((SC_ROWS, d), lambda i: (i, 0))],
            core_axis_name=("core", "subcore"), dimension_semantics=(pltpu.PARALLEL,),
        )(i_hbm, o_hbm)

    return gather(y, idx_rows)


def _sc_scatter_rows(x, idx0_rows, idx1_rows, p_rows):
    n, d = x.shape

    @pl.kernel(out_type=jax.ShapeDtypeStruct((p_rows, d), x.dtype), mesh=_sc_mesh(), scratch_types=[])
    def scatter(x_hbm, i0_hbm, i1_hbm, o_hbm):
        def body(x_vmem, i0_vmem, i1_vmem):
            pltpu.sync_copy(x_vmem, o_hbm.at[i0_vmem.at[0, pl.ds(0, SC_ROWS)]])
            pltpu.sync_copy(x_vmem, o_hbm.at[i1_vmem.at[0, pl.ds(0, SC_ROWS)]])

        pltpu.emit_pipeline(
            body, grid=(n // SC_ROWS,),
            in_specs=[pl.BlockSpec((SC_ROWS, d), lambda i: (i, 0)),
                      pl.BlockSpec((1, SC_INDEX_LANES), lambda i: (i, 0)),
                      pl.BlockSpec((1, SC_INDEX_LANES), lambda i: (i, 0))],
            out_specs=[],
            core_axis_name=("core", "subcore"), dimension_semantics=(pltpu.PARALLEL,),
        )(x_hbm, i0_hbm, i1_hbm)

    return scatter(x, idx0_rows, idx1_rows)


def _router_params(wg_r, bg_r, we_r, be_r):
    pad = LANES - MOE_GROUPS - MOE_EXPERTS
    w_r = jnp.pad(jnp.concatenate([wg_r, we_r], 1), ((0, 0), (0, pad)))
    w_hi = w_r.astype(BF16)
    w_lo = (w_r - w_hi.astype(F32)).astype(BF16)
    b_r = jnp.pad(jnp.concatenate([bg_r, be_r]), (0, pad)).reshape(1, LANES)
    return jnp.concatenate([w_hi, w_lo], 1), b_r


def _moe(h, hp, route, cnt, w_gate, w_up, w_down, layer, ln_g, ln_b):
    n, d = h.shape
    counts = cnt[0, :MOE_EXPERTS].astype(jnp.int32)
    padded = (counts + MOE_BLOCK - 1) // MOE_BLOCK * MOE_BLOCK
    pends = jnp.cumsum(padded)
    pstarts = pends - padded
    experts = jnp.arange(MOE_EXPERTS, dtype=jnp.int32)
    dest0, dest1 = _dest_rows(route, pstarts, tm=2 * TILE_TOKEN_ROWS)
    p_rows = n * MOE_TOPK + MOE_EXPERTS * MOE_BLOCK
    nb = p_rows // MOE_BLOCK
    blk_start = jnp.arange(nb, dtype=jnp.int32) * MOE_BLOCK
    blk_e = jnp.minimum(jnp.sum((pends[None, :] <= blk_start[:, None]).astype(jnp.int32), -1), MOE_EXPERTS - 1)
    real_ends = pstarts + counts
    blk_end = jnp.sum(jnp.where(blk_e[:, None] == experts[None, :], real_ends[None, :], 0), -1)
    blk_rows = jnp.clip(blk_end - blk_start, 0, MOE_BLOCK).astype(jnp.int32)
    xb = _sc_scatter_rows(hp, dest0, dest1, p_rows)
    yb = _experts(blk_e, blk_rows, xb, w_gate, w_up, w_down, layer)
    return _combine_ln(h, _sc_gather_rows(yb, dest0), _sc_gather_rows(yb, dest1), route, ln_g, ln_b,
                       tm=TILE_TOKEN_ROWS)


def kernel(x, positions, ln1_g, ln1_b, ln2_g, ln2_b, a_w_in, a_w_out, b_w_in, b_gate_bias, b_conv_w, b_conv_b,
           b_norm_g, b_w_out, c_w_in, c_norm_g, c_w_out, r_group_w, r_group_b, r_expert_w, r_expert_b,
           e_w_gate, e_w_up, e_w_down):
    batch, seq, d = x.shape
    n = batch * seq
    tabs_a = _rope_tables_a(positions)
    inv_c = C_THETA ** (-jnp.arange(0, C_QK_DIM, 2, dtype=F32) / C_QK_DIM)
    ang_c = positions.astype(F32)[:, None] * inv_c[None, :]
    cos_c, sin_c = jnp.cos(ang_c), jnp.sin(ang_c)
    h = x.reshape(n, d)
    hb = h
    for i in range(DEPTH):
        kind, j = i % 3, i // 3
        if kind == 0:
            y = _mixer_dilated(hb, a_w_in[j], tabs_a, batch, seq)
            w_out = a_w_out[j]
        elif kind == 1:
            y = _mixer_mlstm(hb, b_w_in[j], b_gate_bias[j], b_conv_w[j], b_conv_b[j], b_norm_g[j], batch, seq)
            w_out = b_w_out[j]
        else:
            y = _mixer_retention(hb, c_w_in[j], c_norm_g[j], cos_c, sin_c, batch, seq)
            w_out = c_w_out[j]
        w_r, b_r = _router_params(r_group_w[i], r_group_b[i], r_expert_w[i], r_expert_b[i])
        h, hp, route, cnt = _matmul_res_ln_route(y, w_out.astype(BF16), h, ln1_g[i], ln1_b[i], w_r, b_r,
                                                 tm=TILE_TOKEN_ROWS)
        h, hb = _moe(h, hp, route, cnt, e_w_gate, e_w_up, e_w_down, i, ln2_g[i], ln2_b[i])
    return h.reshape(batch, seq, d)
```

```python
import jax
import jax.numpy as jnp
from jax import lax
from jax.experimental import pallas as pl
from jax.experimental.pallas import tpu as pltpu
from jax.experimental.pallas import tpu_sc as plsc

F32 = jnp.float32
BF16 = jnp.bfloat16

D_MODEL = 1024
DEPTH = 4
DN_ALPHA = (2.0 * DEPTH) ** 0.25
LN_EPS = 1e-5

A_HEADS = 16
A_HEAD_DIM = 64
A_DILATIONS = (1, 4, 16)
A_BLOCK = 128
A_UNROLL = 32
A_GROUP = 16
A_PITCH = 20
LOG2_E = 1.4426950408889634
A_ROT_DIM = 16
ROPE_THETA = 500000.0

B_HEADS = 8
B_QK_DIM = 64
B_V_DIM = 128
B_CONV = 4
B_CHUNK = 256

C_HEADS = 4
C_QK_DIM = 256
C_V_DIM = 512
C_CHUNK = 256
C_THETA = 10000.0

MOE_GROUPS = 8
MOE_PER_GROUP = 8
MOE_EXPERTS = 64
MOE_TOPK = 2
MOE_HIDDEN = 256
MOE_BLOCK = 512

LANES = 128
SC_ROWS = 64
SC_INDEX_LANES = LANES
NEG = -1e30
V7X_VMEM_BYTES = 64 * 1024 * 1024
VMEM_LIMIT = V7X_VMEM_BYTES * 3 // 4

TILE_PROJ_A = 512
TILE_MLSTM_PROJ = (1024, 1536)
TILE_MLSTM_GATES = (2048, LANES)
TILE_RETENTION_PROJ = (1024, 2048)
TILE_TOKEN_ROWS = 512
TILE_COMBINE_ROWS = 1024
TILE_DEST_ROWS = 2048


def _params(*sem):
    return pltpu.CompilerParams(dimension_semantics=sem, vmem_limit_bytes=VMEM_LIMIT)


def _mm_kernel(x_ref, w_ref, o_ref):
    o_ref[...] = jnp.dot(x_ref[...], w_ref[...], preferred_element_type=F32).astype(o_ref.dtype)


def _matmul(x, w, *, tm, tn, out_dtype=F32):
    n, k = x.shape
    m = w.shape[1]
    return pl.pallas_call(
        _mm_kernel,
        grid=(n // tm, m // tn),
        in_specs=[pl.BlockSpec((tm, k), lambda i, j: (i, 0)),
                  pl.BlockSpec((k, tn), lambda i, j: (0, j))],
        out_specs=pl.BlockSpec((tm, tn), lambda i, j: (i, j)),
        out_shape=jax.ShapeDtypeStruct((n, m), out_dtype),
        compiler_params=_params("parallel", "parallel"),
        name="matmul",
    )(x, w)


def _layer_norm_rows(z, g, b):
    mu = jnp.mean(z, -1, keepdims=True)
    zc = z - mu
    var = jnp.mean(zc * zc, -1, keepdims=True)
    return zc * lax.rsqrt(var + LN_EPS) * g + b


def _pack_pairs(x):
    c = x.shape[1] // 2
    hi = pltpu.bitcast(x[:, :c].astype(BF16).astype(F32), jnp.uint32)
    lo = pltpu.bitcast(x[:, c:].astype(BF16).astype(F32), jnp.uint32)
    return pltpu.bitcast(hi | (lo >> 16), F32)


def _unpack_pairs(w):
    bits = pltpu.bitcast(w, jnp.uint32)
    hi = pltpu.bitcast(bits & jnp.uint32(0xFFFF0000), F32)
    lo = pltpu.bitcast(bits << 16, F32)
    return jnp.concatenate([hi, lo], axis=1)


def _proj_a_kernel(x_ref, w_ref, c_ref, s1_ref, s2_ref, q_ref, k_ref, v_ref):
    x = x_ref[...].astype(BF16)
    width = 2 * LANES
    for c, ref in ((0, q_ref), (1, k_ref), (2, v_ref)):
        for j in range(D_MODEL // width):
            col = c * D_MODEL + j * width
            y = jnp.dot(x, w_ref[:, col:col + width], preferred_element_type=F32)
            if c < 2:
                half = A_ROT_DIM // 2
                y = (y * c_ref[...] + pltpu.roll(y, width - half, 1) * s1_ref[...]
                     + pltpu.roll(y, half, 1) * s2_ref[...])
            if c == 0:
                y = y * (LOG2_E * A_HEAD_DIM ** -0.5)
            ref[0, 2 * j] = y[:, :LANES]
            ref[0, 2 * j + 1] = y[:, LANES:]


def _proj_a(xb, w, tabs, batch, seq, *, tm):
    n, d = xb.shape
    spb = seq // tm
    hp = D_MODEL // LANES
    qkv_shape = jax.ShapeDtypeStruct((batch, hp, seq, LANES), F32)
    out_spec = pl.BlockSpec((1, hp, tm, LANES), lambda i: (i // spb, 0, i % spb, 0))
    tab_spec = pl.BlockSpec((tm, 2 * LANES), lambda i: (i % spb, 0))
    return pl.pallas_call(
        _proj_a_kernel,
        grid=(n // tm,),
        in_specs=[pl.BlockSpec((tm, d), lambda i: (i, 0)),
                  pl.BlockSpec((d, 3 * d), lambda i: (0, 0)),
                  tab_spec, tab_spec, tab_spec],
        out_specs=[out_spec, out_spec, out_spec],
        out_shape=[qkv_shape, qkv_shape, qkv_shape],
        compiler_params=_params("parallel"),
        name="proj_a",
    )(xb, w, *tabs)


def _rope_tables_a(positions):
    half = A_ROT_DIM // 2
    inv = ROPE_THETA ** (-jnp.arange(0, A_ROT_DIM, 2, dtype=F32) / A_ROT_DIM)
    ang = positions.astype(F32)[:, None] * inv[None, :]
    cos, sin = jnp.cos(ang), jnp.sin(ang)
    s = positions.shape[0]
    pad = jnp.zeros((s, A_HEAD_DIM - A_ROT_DIM), F32)
    c_head = jnp.concatenate([cos, cos, pad + 1.0], -1)
    s1_head = jnp.concatenate([-sin, jnp.zeros_like(sin), pad], -1)
    s2_head = jnp.concatenate([jnp.zeros_like(sin), sin, pad], -1)
    reps = 2 * LANES // A_HEAD_DIM
    return tuple(jnp.tile(t, (1, reps)) for t in (c_head, s1_head, s2_head))


def _attn_blocks(q_ref, k_ref, v_ref, o_ref, lse_ref, bias_ref, hmask, head0, first_block, d, pitch):
    nk = 2 * A_BLOCK
    loaded = []
    for u in range(A_UNROLL):
        g = first_block + u
        r = g % d
        n = g // d
        qstart = n * (A_BLOCK * pitch) + r
        kstart = jnp.maximum(qstart - A_BLOCK * pitch, r)
        if pitch == 1:
            qstart = pl.multiple_of(qstart, A_BLOCK)
            kstart = pl.multiple_of(kstart, A_BLOCK)
            qsl, ksl = pl.ds(qstart, A_BLOCK), pl.ds(kstart, nk)
        else:
            qsl, ksl = pl.ds(qstart, A_BLOCK, stride=pitch), pl.ds(kstart, nk, stride=pitch)
        bias = bias_ref[jnp.minimum(n, 1)]
        loaded.append((qsl, q_ref[qsl, :].astype(BF16), k_ref[ksl, :].astype(BF16), v_ref[ksl, :].astype(BF16), bias))
    results = []
    for qsl, qb, kb, vb, bias in loaded:
        pvs, ms, ls = [], [], []
        for h in range(2):
            s = lax.dot_general(qb * hmask[h], kb, (((1,), (1,)), ((), ())), preferred_element_type=F32) + bias
            m = jnp.max(s, -1, keepdims=True)
            p = jnp.exp2(s - m)
            ms.append(jnp.broadcast_to(m, (A_BLOCK, LANES)))
            ls.append(jnp.broadcast_to(jnp.sum(p, -1, keepdims=True), (A_BLOCK, LANES)))
            pvs.append(jnp.dot(p.astype(BF16), vb, preferred_element_type=F32))
        l = jnp.where(head0, ls[0], ls[1])
        out = jnp.where(head0, pvs[0], pvs[1]) * (1.0 / l)
        results.append((qsl, out, jnp.where(head0, ms[0], ms[1]) + jnp.log2(l)))
    for qsl, out, lse in results:
        o_ref[qsl, :] = out
        lse_ref[qsl, :] = lse


def _attn_kernel(q_ref, k_ref, v_ref, o_ref, q16_ref, k16_ref, v16_ref, ob_ref, lb_ref, o16_ref, l16_ref,
                 bias_ref):
    seq = q_ref.shape[0]
    groups = seq // A_GROUP
    head0 = lax.broadcasted_iota(jnp.int32, (A_BLOCK, LANES), 1) < A_HEAD_DIM
    hmask = [jnp.where(head0, 1.0, 0.0).astype(BF16), jnp.where(head0, 0.0, 1.0).astype(BF16)]
    qi = lax.broadcasted_iota(jnp.int32, (A_BLOCK, 2 * A_BLOCK), 0)
    kj = lax.broadcasted_iota(jnp.int32, (A_BLOCK, 2 * A_BLOCK), 1)
    bias_ref[0] = jnp.where(kj <= qi, 0.0, NEG).astype(F32)
    bias_ref[1] = jnp.where((kj >= qi) & (kj <= qi + A_BLOCK), 0.0, NEG).astype(F32)

    def spread(g, carry):
        src = pl.ds(pl.multiple_of(g * A_GROUP, A_GROUP), A_GROUP)
        dst = pl.ds(pl.multiple_of(g * A_PITCH, 4), A_GROUP)
        q16_ref[dst, :] = q_ref[src, :]
        k16_ref[dst, :] = k_ref[src, :]
        v16_ref[dst, :] = v_ref[src, :]
        return carry

    lax.fori_loop(0, groups, spread, 0, unroll=8)

    branches = ((1, 1, q_ref, k_ref, v_ref, ob_ref.at[0], lb_ref.at[0]),
                (4, 4, q_ref, k_ref, v_ref, ob_ref.at[1], lb_ref.at[1]),
                (16, A_PITCH, q16_ref, k16_ref, v16_ref, o16_ref, l16_ref))
    zero = jnp.minimum(pl.program_id(0), 0)
    for d, pitch, qr, kr, vr, orf, lrf in branches:

        def body(it, carry, d=d, pitch=pitch, qr=qr, kr=kr, vr=vr, orf=orf, lrf=lrf):
            _attn_blocks(qr, kr, vr, orf, lrf, bias_ref, hmask, head0, it * A_UNROLL, d, pitch)
            return carry

        lax.fori_loop(zero, zero + seq // (A_BLOCK * A_UNROLL), body, 0)

    def mix(g, carry):
        nat = pl.ds(pl.multiple_of(g * A_GROUP, A_GROUP), A_GROUP)
        pad = pl.ds(pl.multiple_of(g * A_PITCH, 4), A_GROUP)
        o0, o1, o2 = ob_ref[0, nat, :], ob_ref[1, nat, :], o16_ref[pad, :]
        l0, l1, l2 = lb_ref[0, nat, :], lb_ref[1, nat, :], l16_ref[pad, :]
        mx = jnp.maximum(jnp.maximum(l0, l1), l2)
        w0, w1, w2 = jnp.exp2(l0 - mx), jnp.exp2(l1 - mx), jnp.exp2(l2 - mx)
        o_ref[nat, :] = ((w0 * o0 + w1 * o1 + w2 * o2) / (w0 + w1 + w2)).astype(o_ref.dtype)
        return carry

    lax.fori_loop(0, groups, mix, 0, unroll=8)


def _attention(q, k, v):
    batch, hp, seq, _ = q.shape
    assert seq % (2 * A_BLOCK * max(A_DILATIONS)) == 0 and seq % (A_BLOCK * A_UNROLL) == 0
    in_spec = pl.BlockSpec((None, None, seq, LANES), lambda b, p: (b, p, 0, 0))
    padded = seq // A_GROUP * A_PITCH
    return pl.pallas_call(
        _attn_kernel,
        grid=(batch, hp),
        in_specs=[in_spec, in_spec, in_spec],
        out_specs=pl.BlockSpec((None, seq, LANES), lambda b, p: (b, 0, p)),
        out_shape=jax.ShapeDtypeStruct((batch, seq, hp * LANES), BF16),
        scratch_shapes=[pltpu.VMEM((padded, LANES), F32)] * 3
        + [pltpu.VMEM((2, seq, LANES), F32)] * 2
        + [pltpu.VMEM((padded, LANES), F32)] * 2
        + [pltpu.VMEM((2, A_BLOCK, 2 * A_BLOCK), F32)],
        compiler_params=_params("parallel", "parallel"),
        name="dilated_attention",
    )(q, k, v)


def _mixer_dilated(hb, w_in, tabs, batch, seq):
    q, k, v = _proj_a(hb, w_in.astype(BF16), tabs, batch, seq, tm=TILE_PROJ_A)
    o = _attention(q, k, v)
    return o.reshape(batch * seq, D_MODEL)


def _mlstm_kernel(qk_ref, v_ref, o_ref, gc_ref, gb_ref, cw_ref, cb_ref, ng_ref, out_ref,
                  ext_ref, c_ref, n_ref, m_ref):
    L = B_CHUNK
    chunk = pl.program_id(1)

    @pl.when(chunk == 0)
    def _():
        ext_ref[0:8, :] = jnp.zeros((8, D_MODEL), F32)
        c_ref[...] = jnp.zeros(c_ref.shape, F32)
        n_ref[...] = jnp.zeros(n_ref.shape, F32)
        m_ref[...] = jnp.zeros(m_ref.shape, F32)

    u = qk_ref[...].astype(F32)
    ext_ref[8:8 + L, :] = u
    conv = u * cw_ref[B_CONV - 1:B_CONV, :] + cb_ref[...]
    for j in range(1, B_CONV):
        conv = conv + ext_ref[pl.ds(8 - j, L), :] * cw_ref[B_CONV - 1 - j:B_CONV - j, :]
    ext_ref[0:8, :] = u[L - 8:, :]
    qk = conv * jax.nn.sigmoid(conv)
    half = D_MODEL // 2

    gc = gc_ref[...] + gb_ref[...]
    gr = gc.T
    i_col, i_row = gc, gr[:B_HEADS, :]
    lf_col = jax.nn.log_sigmoid(gc)
    lf_row = jax.nn.log_sigmoid(gr[B_HEADS:2 * B_HEADS, :])
    ti = lax.broadcasted_iota(jnp.int32, (L, L), 0)
    si = lax.broadcasted_iota(jnp.int32, (L, L), 1)
    causal = ti >= si
    tri = causal.astype(F32)
    a_col = jnp.dot(tri, lf_col, preferred_element_type=F32, precision=lax.Precision.HIGHEST)
    a_row = lax.dot_general(lf_row, tri, (((1,), (1,)), ((), ())), preferred_element_type=F32,
                            precision=lax.Precision.HIGHEST)
    lane = lax.broadcasted_iota(jnp.int32, (1, LANES), 1)
    lane_h0 = lane < B_QK_DIM
    col_h0 = lax.broadcasted_iota(jnp.int32, (1, 2 * B_V_DIM), 1) < B_V_DIM

    heads = range(B_HEADS)
    pairs = range(B_HEADS // 2)
    qp = [qk[:, p * LANES:(p + 1) * LANES] for p in pairs]
    kp = [qk[:, half + p * LANES:half + (p + 1) * LANES] * (B_QK_DIM ** -0.5) for p in pairs]
    kpb = [k.astype(BF16) for k in kp]
    vpb = [v_ref[:, p * 2 * B_V_DIM:(p + 1) * 2 * B_V_DIM] for p in pairs]
    c_old = [c_ref[p] for p in pairs]
    c_oldb = [c.astype(BF16) for c in c_old]
    n_old = [n_ref[p] for p in pairs]
    m_old = [m_ref[h // 2][:, (h % 2) * B_QK_DIM:(h % 2) * B_QK_DIM + 1] for h in heads]
    ac = [a_col[:, B_HEADS + h:B_HEADS + h + 1] for h in heads]
    ic = [i_col[:, h:h + 1] for h in heads]
    dmat = [jnp.where(causal, ac[h] + (i_row[h:h + 1, :] - a_row[h:h + 1, :]), NEG) for h in heads]
    inter = [ac[h] + m_old[h] for h in heads]
    m_t = [jnp.maximum(inter[h], jnp.max(dmat[h], -1, keepdims=True)) for h in heads]
    qm = [jnp.where(lane_h0 if h % 2 == 0 else jnp.logical_not(lane_h0), qp[h // 2], 0.0) for h in heads]
    qmb = [q.astype(BF16) for q in qm]
    qk_s = [lax.dot_general(qmb[h], kpb[h // 2], (((1,), (1,)), ((), ())), preferred_element_type=F32) for h in heads]
    qc = [jnp.dot(qmb[h], c_oldb[h // 2], preferred_element_type=F32)[:, (h % 2) * B_V_DIM:(h % 2 + 1) * B_V_DIM]
          for h in heads]
    sc = [qk_s[h] * jnp.exp(dmat[h] - m_t[h]) for h in heads]
    g_inter = [jnp.exp(inter[h] - m_t[h]) for h in heads]
    num = [jnp.dot(sc[h].astype(BF16), vpb[h // 2][:, (h % 2) * B_V_DIM:(h % 2 + 1) * B_V_DIM],
                   preferred_element_type=F32) + g_inter[h] * qc[h] for h in heads]
    den = [jnp.sum(sc[h], -1, keepdims=True) + g_inter[h] * jnp.sum(qm[h] * n_old[h // 2], -1, keepdims=True)
           for h in heads]
    h_out = [num[h] / jnp.maximum(jnp.abs(den[h]), jnp.exp(-m_t[h])) for h in heads]
    for h in heads:
        mu = jnp.mean(h_out[h], -1, keepdims=True)
        hc = h_out[h] - mu
        var = jnp.mean(hc * hc, -1, keepdims=True)
        cols = slice(h * B_V_DIM, (h + 1) * B_V_DIM)
        hn = hc * lax.rsqrt(var + LN_EPS) * ng_ref[:, cols]
        out_ref[:, cols] = (hn * jax.nn.sigmoid(o_ref[:, cols].astype(F32))).astype(out_ref.dtype)
    a_end = [ac[h][L - 1:L, :] for h in heads]
    w_col = [a_end[h] - ac[h] + ic[h] for h in heads]
    m_new = [jnp.maximum(a_end[h] + m_old[h], jnp.max(w_col[h], 0, keepdims=True)) for h in heads]
    decay = [jnp.exp(a_end[h] + m_old[h] - m_new[h]) for h in heads]
    ws_col = [jnp.exp(w_col[h] - m_new[h]) for h in heads]
    for p in pairs:
        h0, h1 = 2 * p, 2 * p + 1
        ws = jnp.where(lane_h0, jnp.broadcast_to(ws_col[h0], (L, LANES)), jnp.broadcast_to(ws_col[h1], (L, LANES)))
        kw = kp[p] * ws
        dec_c = jnp.where(col_h0, jnp.broadcast_to(decay[h0], (1, 2 * B_V_DIM)),
                          jnp.broadcast_to(decay[h1], (1, 2 * B_V_DIM)))
        dec_n = jnp.where(lane_h0, jnp.broadcast_to(decay[h0], (1, LANES)), jnp.broadcast_to(decay[h1], (1, LANES)))
        c_ref[p] = dec_c * c_old[p] + lax.dot_general(kw.astype(BF16), vpb[p], (((0,), (0,)), ((), ())),
                                                      preferred_element_type=F32)
        n_ref[p] = dec_n * n_old[p] + jnp.sum(kw, 0, keepdims=True)
        m_ref[p] = jnp.where(lane_h0, jnp.broadcast_to(m_new[h0], (1, LANES)), jnp.broadcast_to(m_new[h1], (1, LANES)))


def _mlstm(proj, gates, gate_bias, conv_w, conv_b, norm_g, batch, seq):
    L = B_CHUNK
    d = D_MODEL
    slab = lambda c: pl.BlockSpec((None, L, d), lambda b, s, c=c: (b, s, c))
    full = lambda shape: pl.BlockSpec(shape, lambda b, s: (0,) * len(shape))
    return pl.pallas_call(
        _mlstm_kernel,
        grid=(batch, seq // L),
        in_specs=[slab(0), slab(1), slab(2),
                  pl.BlockSpec((None, L, LANES), lambda b, s: (b, s, 0)),
                  full((1, LANES)),
                  full((B_CONV, d)), full((1, d)), full((1, d))],
        out_specs=pl.BlockSpec((None, L, d), lambda b, s: (b, s, 0)),
        out_shape=jax.ShapeDtypeStruct((batch, seq, d), BF16),
        scratch_shapes=[pltpu.VMEM((L + 8, d), F32),
                        pltpu.VMEM((B_HEADS // 2, 2 * B_QK_DIM, 2 * B_V_DIM), F32),
                        pltpu.VMEM((B_HEADS // 2, 1, LANES), F32),
                        pltpu.VMEM((B_HEADS // 2, 1, LANES), F32)],
        compiler_params=_params("parallel", "arbitrary"),
        name="mlstm",
    )(proj, proj, proj, gates, jnp.pad(gate_bias, (0, LANES - 2 * B_HEADS)).reshape(1, LANES),
      conv_w, conv_b.reshape(1, d), norm_g.reshape(1, d))


def _mixer_mlstm(hb, w_in, gate_bias, conv_w, conv_b, norm_g, batch, seq):
    n = batch * seq
    main = 3 * D_MODEL
    tm, tn = TILE_MLSTM_PROJ
    proj = _matmul(hb, w_in[:, :main].astype(BF16), tm=tm, tn=tn, out_dtype=BF16)
    w_g = jnp.pad(w_in[:, main:], ((0, 0), (0, LANES - 2 * B_HEADS))).astype(BF16)
    gates = _matmul(hb, w_g, tm=TILE_MLSTM_GATES[0], tn=TILE_MLSTM_GATES[1])
    out = _mlstm(proj.reshape(batch, seq, main), gates.reshape(batch, seq, LANES), gate_bias,
                 conv_w, conv_b, norm_g, batch, seq)
    return out.reshape(n, D_MODEL)


def _retention_kernel(lg_ref, q_ref, k_ref, v_ref, g_ref, cos_ref, sin_ref, ng_ref, out_ref,
                      r_ref, dm_ref, xi_ref, zeta_ref):
    L = C_CHUNK
    chunk = pl.program_id(1)

    @pl.when(chunk == 0)
    def _():
        r_ref[...] = jnp.zeros(r_ref.shape, F32)
        ti = lax.broadcasted_iota(jnp.int32, (L, L), 0)
        si = lax.broadcasted_iota(jnp.int32, (L, L), 1)
        rel = (ti - si).astype(F32)
        idx = lax.broadcasted_iota(jnp.int32, (L, LANES), 0).astype(F32)
        for h in range(C_HEADS):
            lg = lg_ref[h]
            dm_ref[h] = jnp.where(rel >= 0, jnp.exp(jnp.maximum(rel, 0.0) * lg), 0.0)
            xi_ref[h] = jnp.exp((idx + 1.0) * lg)
            zeta_ref[h] = jnp.exp((L - 1.0 - idx) * lg)

    cos, sin = cos_ref[...], sin_ref[...]
    hd = C_QK_DIM // 2

    def rope(t):
        t1, t2 = t[:, :hd], t[:, hd:]
        return jnp.concatenate([t1 * cos - t2 * sin, t2 * cos + t1 * sin], -1)

    for h in range(C_HEADS):
        qk_cols = slice(h * C_QK_DIM, (h + 1) * C_QK_DIM)
        v_cols = slice(h * C_V_DIM, (h + 1) * C_V_DIM)
        q = rope(q_ref[:, qk_cols].astype(F32))
        k = rope(k_ref[:, qk_cols].astype(F32)) * (C_QK_DIM ** -0.5)
        qb = q.astype(BF16)
        vb = v_ref[:, v_cols]
        r_old = r_ref[h]
        sc = lax.dot_general(qb, k.astype(BF16), (((1,), (1,)), ((), ())), preferred_element_type=F32) * dm_ref[h]
        o = jnp.dot(sc.astype(BF16), vb, preferred_element_type=F32)
        o = o + xi_ref[h, :, 0:1] * jnp.dot(qb, r_old.astype(BF16), preferred_element_type=F32)
        kz = (k * zeta_ref[h, :, 0:1]).astype(BF16)
        cd = jnp.exp(jnp.full((1, 1), float(L), F32) * lg_ref[h])
        r_ref[h] = cd * r_old + lax.dot_general(kz, vb, (((0,), (0,)), ((), ())), preferred_element_type=F32)
        mu = jnp.mean(o, -1, keepdims=True)
        oc = o - mu
        var = jnp.mean(oc * oc, -1, keepdims=True)
        on = oc * lax.rsqrt(var + LN_EPS) * ng_ref[:, v_cols]
        g = g_ref[:, v_cols].astype(F32)
        out_ref[:, v_cols] = (on * (g * jax.nn.sigmoid(g))).astype(out_ref.dtype)


def _retention(proj, cos, sin, norm_g, batch, seq):
    L = C_CHUNK
    d = D_MODEL
    log_gamma = jnp.log(1.0 - 2.0 ** (-5.0 - jnp.arange(C_HEADS, dtype=F32)))
    return pl.pallas_call(
        _retention_kernel,
        grid=(batch, seq // L),
        in_specs=[pl.BlockSpec(memory_space=pltpu.SMEM),
                  pl.BlockSpec((None, L, d), lambda b, c: (b, c, 0)),
                  pl.BlockSpec((None, L, d), lambda b, c: (b, c, 1)),
                  pl.BlockSpec((None, L, 2 * d), lambda b, c: (b, c, 1)),
                  pl.BlockSpec((None, L, 2 * d), lambda b, c: (b, c, 2)),
                  pl.BlockSpec((L, C_QK_DIM // 2), lambda b, c: (c, 0)),
                  pl.BlockSpec((L, C_QK_DIM // 2), lambda b, c: (c, 0)),
                  pl.BlockSpec((1, 2 * d), lambda b, c: (0, 0))],
        out_specs=pl.BlockSpec((None, L, 2 * d), lambda b, c: (b, c, 0)),
        out_shape=jax.ShapeDtypeStruct((batch, seq, 2 * d), BF16),
        scratch_shapes=[pltpu.VMEM((C_HEADS, C_QK_DIM, C_V_DIM), F32),
                        pltpu.VMEM((C_HEADS, L, L), F32),
                        pltpu.VMEM((C_HEADS, L, LANES), F32),
                        pltpu.VMEM((C_HEADS, L, LANES), F32)],
        compiler_params=_params("parallel", "arbitrary"),
        name="retention",
    )(log_gamma, proj, proj, proj, proj, cos, sin, norm_g.reshape(1, 2 * d))


def _mixer_retention(hb, w_in, norm_g, cos, sin, batch, seq):
    tm, tn = TILE_RETENTION_PROJ
    proj = _matmul(hb, w_in.astype(BF16), tm=tm, tn=tn, out_dtype=BF16)
    out = _retention(proj.reshape(batch, seq, 6 * D_MODEL), cos, sin, norm_g, batch, seq)
    return out.reshape(batch * seq, 2 * D_MODEL)


def _expert_kernel(be_ref, bv_ref, x_ref, wg_ref, wu_ref, wd_ref, y_ref, wgb_ref, wub_ref, wdb_ref):
    j = pl.program_id(0)
    changed = jnp.logical_or(j == 0, be_ref[j] != be_ref[jnp.maximum(j - 1, 0)])

    @pl.when(changed)
    def _():
        wgb_ref[...] = wg_ref[...].astype(BF16)
        wub_ref[...] = wu_ref[...].astype(BF16)
        wdb_ref[...] = wd_ref[...].astype(BF16)

    real = lax.broadcasted_iota(jnp.int32, (x_ref.shape[0], 1), 0) < bv_ref[j]
    x = _unpack_pairs(jnp.where(real, x_ref[...], 0.0)).astype(BF16)
    a = jnp.dot(x, wgb_ref[...], preferred_element_type=F32)
    u = jnp.dot(x, wub_ref[...], preferred_element_type=F32)
    act = (a * jax.nn.sigmoid(a) * u).astype(BF16)
    y_ref[...] = _pack_pairs(jnp.dot(act, wdb_ref[...], preferred_element_type=F32))


def _experts(blk_e, blk_rows, xb, w_gate, w_up, w_down, layer):
    p, dw = xb.shape
    d = 2 * dw
    nb = p // MOE_BLOCK
    hid = MOE_HIDDEN
    grid_spec = pltpu.PrefetchScalarGridSpec(
        num_scalar_prefetch=2,
        grid=(nb,),
        in_specs=[pl.BlockSpec((MOE_BLOCK, dw), lambda j, be, bv: (j, 0)),
                  pl.BlockSpec((None, None, d, hid), lambda j, be, bv: (layer, be[j], 0, 0)),
                  pl.BlockSpec((None, None, d, hid), lambda j, be, bv: (layer, be[j], 0, 0)),
                  pl.BlockSpec((None, None, hid, d), lambda j, be, bv: (layer, be[j], 0, 0))],
        out_specs=pl.BlockSpec((MOE_BLOCK, dw), lambda j, be, bv: (j, 0)),
        scratch_shapes=[pltpu.VMEM((d, hid), BF16), pltpu.VMEM((d, hid), BF16), pltpu.VMEM((hid, d), BF16)],
    )
    return pl.pallas_call(
        _expert_kernel,
        grid_spec=grid_spec,
        out_shape=jax.ShapeDtypeStruct((p, dw), F32),
        compiler_params=_params("arbitrary"),
        name="moe_experts",
    )(blk_e, blk_rows, xb, w_gate, w_up, w_down)


def _combine_ln_kernel(h_ref, y0_ref, y1_ref, rt_ref, g_ref, b_ref, o_ref, ob_ref):
    rt = rt_ref[...]
    y = (_unpack_pairs(y0_ref[...]) * rt[:, R_GATE:R_GATE + 1]
         + _unpack_pairs(y1_ref[...]) * rt[:, R_GATE + 1:R_GATE + 2])
    out = _layer_norm_rows(DN_ALPHA * h_ref[...] + y, g_ref[...], b_ref[...])
    o_ref[...] = out
    ob_ref[...] = out.astype(BF16)


def _combine_ln(h, y0, y1, route, g, b, *, tm):
    n, d = h.shape
    row = pl.BlockSpec((tm, d), lambda i: (i, 0))
    words = pl.BlockSpec((tm, d // 2), lambda i: (i, 0))
    vec = pl.BlockSpec((1, d), lambda i: (0, 0))
    return pl.pallas_call(
        _combine_ln_kernel,
        grid=(n // tm,),
        in_specs=[row, words, words, pl.BlockSpec((tm, LANES), lambda i: (i, 0)), vec, vec],
        out_specs=[row, row],
        out_shape=[jax.ShapeDtypeStruct((n, d), F32), jax.ShapeDtypeStruct((n, d), BF16)],
        compiler_params=_params("parallel"),
        name="moe_combine_ln",
    )(h, y0, y1, route, g.reshape(1, d), b.reshape(1, d))


R_EID, R_GATE, R_RANK = 0, 2, 4


def _route_tile(h, w_ref, b_ref, route_ref, cnt_ref, base_ref, tri_ref):
    i = pl.program_id(0)
    tm = h.shape[0]

    @pl.when(i == 0)
    def _():
        base_ref[...] = jnp.zeros(base_ref.shape, F32)
        ti = lax.broadcasted_iota(jnp.int32, (tm, tm), 0)
        si = lax.broadcasted_iota(jnp.int32, (tm, tm), 1)
        tri_ref[...] = jnp.where(si < ti, 1.0, 0.0).astype(BF16)

    h_hi = h.astype(BF16)
    h_lo = (h - h_hi.astype(F32)).astype(BF16)
    hh = jnp.dot(h_hi, w_ref[...], preferred_element_type=F32)
    logits = (hh[:, :LANES] + hh[:, LANES:]
              + jnp.dot(h_lo, w_ref[:, :LANES], preferred_element_type=F32)) + b_ref[...]
    lane = lax.broadcasted_iota(jnp.int32, (tm, LANES), 1).astype(F32)
    neg_inf = -jnp.inf
    big = float(4 * LANES)
    is_g = lane < MOE_GROUPS
    gl = jnp.where(is_g, logits, neg_inf)
    gmax = jnp.max(gl, -1, keepdims=True)
    grp = jnp.min(jnp.where(gl == gmax, lane, big), -1, keepdims=True)
    p_grp = 1.0 / jnp.sum(jnp.where(is_g, jnp.exp(logits - gmax), 0.0), -1, keepdims=True)
    lo = MOE_GROUPS + MOE_PER_GROUP * grp
    el = jnp.where((lane >= lo) & (lane < lo + MOE_PER_GROUP), logits, neg_inf)
    v1 = jnp.max(el, -1, keepdims=True)
    i1 = jnp.min(jnp.where(el == v1, lane, big), -1, keepdims=True)
    el2 = jnp.where(lane == i1, neg_inf, el)
    v2 = jnp.max(el2, -1, keepdims=True)
    i2 = jnp.min(jnp.where(el2 == v2, lane, big), -1, keepdims=True)
    t = jnp.exp(v2 - v1)
    g1 = p_grp / (1.0 + t)
    g2 = g1 * t
    e1 = i1 - MOE_GROUPS
    e2 = i2 - MOE_GROUPS
    oh1 = jnp.where(lane == e1, 1.0, 0.0)
    oh2 = jnp.where(lane == e2, 1.0, 0.0)
    oh = oh1 + oh2
    tot = base_ref[...] + jnp.dot(tri_ref[...], oh.astype(BF16), preferred_element_type=F32)
    r1 = jnp.sum(oh1 * tot, -1, keepdims=True)
    r2 = jnp.sum(oh2 * tot, -1, keepdims=True)
    new_base = base_ref[...] + jnp.sum(oh, 0, keepdims=True)
    base_ref[...] = new_base
    cnt_ref[...] = jnp.broadcast_to(new_base, cnt_ref.shape)
    route = jnp.zeros((tm, LANES), F32)
    for k, val in enumerate((e1, e2, g1, g2, r1, r2)):
        route = jnp.where(lane == float(k), val, route)
    route_ref[...] = route


def _mm_res_ln_route_kernel(x_ref, w_ref, h_ref, g_ref, b_ref, rw_ref, rb_ref,
                            o_ref, op_ref, route_ref, cnt_ref, base_ref, tri_ref):
    y = jnp.dot(x_ref[...], w_ref[...], preferred_element_type=F32)
    out = _layer_norm_rows(DN_ALPHA * h_ref[...] + y, g_ref[...], b_ref[...])
    o_ref[...] = out
    op_ref[...] = _pack_pairs(out)
    _route_tile(out, rw_ref, rb_ref, route_ref, cnt_ref, base_ref, tri_ref)


def _matmul_res_ln_route(x, w, h, g, b, w_r, b_r, *, tm):
    n, k = x.shape
    d = w.shape[1]
    rows = lambda width: pl.BlockSpec((tm, width), lambda i: (i, 0))
    const = lambda shape: pl.BlockSpec(shape, lambda i: (0, 0))
    return pl.pallas_call(
        _mm_res_ln_route_kernel,
        grid=(n // tm,),
        in_specs=[rows(k), const((k, d)), rows(d), const((1, d)), const((1, d)),
                  const((d, 2 * LANES)), const((1, LANES))],
        out_specs=[rows(d), rows(d // 2), rows(LANES), const((8, LANES))],
        out_shape=[jax.ShapeDtypeStruct((n, d), F32), jax.ShapeDtypeStruct((n, d // 2), F32),
                   jax.ShapeDtypeStruct((n, LANES), F32), jax.ShapeDtypeStruct((8, LANES), F32)],
        scratch_shapes=[pltpu.VMEM((1, LANES), F32), pltpu.VMEM((tm, tm), BF16)],
        compiler_params=_params("arbitrary"),
        name="matmul_res_ln_route",
    )(x, w, h, g.reshape(1, d), b.reshape(1, d), w_r, b_r)


def _dest_kernel(route_ref, ps_ref, i0_ref, i1_ref):
    tm = route_ref.shape[0]
    rt = route_ref[...]
    lane = lax.broadcasted_iota(jnp.int32, (tm, LANES), 1)
    row = lax.broadcasted_iota(jnp.int32, (tm, LANES), 0)
    own_lane = lane == row % SC_ROWS
    group = (lax.broadcasted_iota(jnp.int32, (tm // SC_ROWS, tm), 1) // SC_ROWS
             == lax.broadcasted_iota(jnp.int32, (tm // SC_ROWS, tm), 0)).astype(F32)
    ps = ps_ref[0:1, :]
    for slot, out_ref in ((0, i0_ref), (1, i1_ref)):
        e = rt[:, R_EID + slot:R_EID + slot + 1]
        r = rt[:, R_RANK + slot:R_RANK + slot + 1]
        dest = jnp.sum(jnp.where(lane.astype(F32) == e, ps, 0.0), -1, keepdims=True) + r
        spread = jnp.where(own_lane, dest, 0.0)
        out_ref[...] = jnp.dot(group, spread, preferred_element_type=F32,
                               precision=lax.Precision.HIGHEST).astype(jnp.int32)


def _dest_rows(route, pstarts, *, tm):
    n = route.shape[0]
    ps = jnp.zeros((8, LANES), F32).at[0, :MOE_EXPERTS].set(pstarts.astype(F32))
    idx = jax.ShapeDtypeStruct((n // SC_ROWS, SC_INDEX_LANES), jnp.int32)
    idx_spec = pl.BlockSpec((tm // SC_ROWS, SC_INDEX_LANES), lambda i: (i, 0))
    return pl.pallas_call(
        _dest_kernel,
        grid=(n // tm,),
        in_specs=[pl.BlockSpec((tm, LANES), lambda i: (i, 0)), pl.BlockSpec((8, LANES), lambda i: (0, 0))],
        out_specs=[idx_spec, idx_spec],
        out_shape=[idx, idx],
        compiler_params=_params("parallel"),
        name="moe_dest_rows",
    )(route, ps)


def _sc_mesh():
    return plsc.VectorSubcoreMesh(core_axis_name="core", subcore_axis_name="subcore")


def _sc_gather_rows(y, idx_rows):
    n = idx_rows.shape[0] * SC_ROWS
    d = y.shape[1]

    @pl.kernel(out_type=jax.ShapeDtypeStruct((n, d), y.dtype), mesh=_sc_mesh(), scratch_types=[])
    def gather(y_hbm, i_hbm, o_hbm):
        def body(i_vmem, o_vmem):
            pltpu.sync_copy(y_hbm.at[i_vmem.at[0, pl.ds(0, SC_ROWS)]], o_vmem)

        pltpu.emit_pipeline(
            body, grid=(n // SC_ROWS,),
            in_specs=[pl.BlockSpec((1, SC_INDEX_LANES), lambda i: (i, 0))],
            out_specs=[pl.BlockSpec((SC_ROWS, d), lambda i: (i, 0))],
            core_axis_name=("core", "subcore"), dimension_semantics=(pltpu.PARALLEL,),
        )(i_hbm, o_hbm)

    return gather(y, idx_rows)


def _sc_scatter_rows(x, idx0_rows, idx1_rows, p_rows):
    n, d = x.shape

    @pl.kernel(out_type=jax.ShapeDtypeStruct((p_rows, d), x.dtype), mesh=_sc_mesh(), scratch_types=[])
    def scatter(x_hbm, i0_hbm, i1_hbm, o_hbm):
        def body(x_vmem, i0_vmem, i1_vmem):
            pltpu.sync_copy(x_vmem, o_hbm.at[i0_vmem.at[0, pl.ds(0, SC_ROWS)]])
            pltpu.sync_copy(x_vmem, o_hbm.at[i1_vmem.at[0, pl.ds(0, SC_ROWS)]])

        pltpu.emit_pipeline(
            body, grid=(n // SC_ROWS,),
            in_specs=[pl.BlockSpec((SC_ROWS, d), lambda i: (i, 0)),
                      pl.BlockSpec((1, SC_INDEX_LANES), lambda i: (i, 0)),
                      pl.BlockSpec((1, SC_INDEX_LANES), lambda i: (i, 0))],
            out_specs=[],
            core_axis_name=("core", "subcore"), dimension_semantics=(pltpu.PARALLEL,),
        )(x_hbm, i0_hbm, i1_hbm)

    return scatter(x, idx0_rows, idx1_rows)


def _router_params(wg_r, bg_r, we_r, be_r):
    pad = LANES - MOE_GROUPS - MOE_EXPERTS
    w_r = jnp.pad(jnp.concatenate([wg_r, we_r], 1), ((0, 0), (0, pad)))
    w_hi = w_r.astype(BF16)
    w_lo = (w_r - w_hi.astype(F32)).astype(BF16)
    b_r = jnp.pad(jnp.concatenate([bg_r, be_r]), (0, pad)).reshape(1, LANES)
    return jnp.concatenate([w_hi, w_lo], 1), b_r


def _moe(h, hp, route, cnt, w_gate, w_up, w_down, layer, ln_g, ln_b):
    n, d = h.shape
    counts = cnt[0, :MOE_EXPERTS].astype(jnp.int32)
    padded = (counts + MOE_BLOCK - 1) // MOE_BLOCK * MOE_BLOCK
    pends = jnp.cumsum(padded)
    pstarts = pends - padded
    experts = jnp.arange(MOE_EXPERTS, dtype=jnp.int32)
    dest0, dest1 = _dest_rows(route, pstarts, tm=TILE_DEST_ROWS)
    p_rows = n * MOE_TOPK + MOE_EXPERTS * MOE_BLOCK
    nb = p_rows // MOE_BLOCK
    blk_start = jnp.arange(nb, dtype=jnp.int32) * MOE_BLOCK
    blk_e = jnp.minimum(jnp.sum((pends[None, :] <= blk_start[:, None]).astype(jnp.int32), -1), MOE_EXPERTS - 1)
    real_ends = pstarts + counts
    blk_end = jnp.sum(jnp.where(blk_e[:, None] == experts[None, :], real_ends[None, :], 0), -1)
    blk_rows = jnp.clip(blk_end - blk_start, 0, MOE_BLOCK).astype(jnp.int32)
    xb = _sc_scatter_rows(hp, dest0, dest1, p_rows)
    yb = _experts(blk_e, blk_rows, xb, w_gate, w_up, w_down, layer)
    return _combine_ln(h, _sc_gather_rows(yb, dest0), _sc_gather_rows(yb, dest1), route, ln_g, ln_b,
                       tm=TILE_COMBINE_ROWS)


def kernel(x, positions, ln1_g, ln1_b, ln2_g, ln2_b, a_w_in, a_w_out, b_w_in, b_gate_bias, b_conv_w, b_conv_b,
           b_norm_g, b_w_out, c_w_in, c_norm_g, c_w_out, r_group_w, r_group_b, r_expert_w, r_expert_b,
           e_w_gate, e_w_up, e_w_down):
    batch, seq, d = x.shape
    n = batch * seq
    tabs_a = _rope_tables_a(positions)
    inv_c = C_THETA ** (-jnp.arange(0, C_QK_DIM, 2, dtype=F32) / C_QK_DIM)
    ang_c = positions.astype(F32)[:, None] * inv_c[None, :]
    cos_c, sin_c = jnp.cos(ang_c), jnp.sin(ang_c)
    h = x.reshape(n, d)
    hb = h
    for i in range(DEPTH):
        kind, j = i % 3, i // 3
        if kind == 0:
            y = _mixer_dilated(hb, a_w_in[j], tabs_a, batch, seq)
            w_out = a_w_out[j]
        elif kind == 1:
            y = _mixer_mlstm(hb, b_w_in[j], b_gate_bias[j], b_conv_w[j], b_conv_b[j], b_norm_g[j], batch, seq)
            w_out = b_w_out[j]
        else:
            y = _mixer_retention(hb, c_w_in[j], c_norm_g[j], cos_c, sin_c, batch, seq)
            w_out = c_w_out[j]
        w_r, b_r = _router_params(r_group_w[i], r_group_b[i], r_expert_w[i], r_expert_b[i])
        h, hp, route, cnt = _matmul_res_ln_route(y, w_out.astype(BF16), h, ln1_g[i], ln1_b[i], w_r, b_r,
                                                 tm=TILE_TOKEN_ROWS)
        h, hb = _moe(h, hp, route, cnt, e_w_gate, e_w_up, e_w_down, i, ln2_g[i], ln2_b[i])
    return h.reshape(batch, seq, d)
```
